```python
import math
import jax
import jax.numpy as jnp
from jax import lax
import numpy as np

D_MODEL = 1024
BATCH = 8
SEQ = 2048
DEPTH = 4
DEC_BATCH = 1
DEC_SEQ = 16384
PAST_LEN = 128

HEAD_DIM = 64
EPS = 1e-6
GLA_HEADS = 4
GLA_DK = 32
GLA_DV = 64
GLA_LOWRANK = 16
GLA_GATE_NORMALIZER = 16.0
GLA_CHUNK = 32
SWA_HEADS = 8
SWA_KV_HEADS = 2
SWA_WINDOW = 128
SWA_BLOCK = 128
NUM_BUCKETS = 32
MAX_DISTANCE = 128
DN_HEADS = 4
DN_DK = 64
DN_DV = 64
DN_CONV = 5
DN_CHUNK = 64
GLA_WIDTH = GLA_HEADS * GLA_DV
SWA_WIDTH = SWA_HEADS * HEAD_DIM
DN_WIDTH = DN_HEADS * DN_DV
D_MIX = GLA_WIDTH + SWA_WIDTH + DN_WIDTH
D_FF = 4 * D_MODEL
DN_QKV = DN_HEADS * (2 * DN_DK + DN_DV)
IN_SIZES = (GLA_HEADS * GLA_DK, GLA_HEADS * GLA_DK, GLA_WIDTH, GLA_WIDTH, 2 * GLA_LOWRANK,
            SWA_WIDTH, SWA_KV_HEADS * HEAD_DIM, SWA_KV_HEADS * HEAD_DIM,
            DN_QKV, DN_WIDTH, 2 * DN_HEADS, 2 * DN_HEADS)
D_IN = sum(IN_SIZES)

kernel_name = "hybrid_gla_swa_deltanet_encoder"


def rmsnorm(x, g):
    xf = x.astype(jnp.float32)
    y = xf * lax.rsqrt(jnp.mean(xf * xf, axis=-1, keepdims=True) + EPS)
    return (y * g.astype(jnp.float32)).astype(x.dtype)


def l2norm(x):
    return x * lax.rsqrt(jnp.sum(x * x, axis=-1, keepdims=True) + EPS)


def split_heads(x, h):
    B, T, _ = x.shape
    return x.reshape(B, T, h, -1).transpose(0, 2, 1, 3)


def merge_heads(x):
    B, H, T, d = x.shape
    return x.transpose(0, 2, 1, 3).reshape(B, T, H * d)


def split_columns(proj):
    idx, acc = [], 0
    for s in IN_SIZES[:-1]:
        acc += s
        idx.append(acc)
    return jnp.split(proj, idx, axis=-1)


def gla_chunk_scan(q, k, v, gk):
    B, H, T, dk = q.shape
    dv = v.shape[-1]
    C = GLA_CHUNK
    N = T // C
    q, k, v, gk = [a.reshape(B, H, N, C, a.shape[-1]) for a in (q, k, v, gk)]
    b = jnp.cumsum(gk, axis=3)
    causal = jnp.tril(jnp.ones((C, C), dtype=bool))
    diff = jnp.where(causal[:, :, None], b[..., :, None, :] - b[..., None, :, :], -jnp.inf)
    A = jnp.einsum('bhnid,bhnjd,bhnijd->bhnij', q, k, jnp.exp(diff))
    o_intra = jnp.einsum('bhnij,bhnjv->bhniv', A, v)
    b_last = b[..., -1, :]
    k_dec = k * jnp.exp(b_last[..., None, :] - b)
    upd = jnp.einsum('bhncd,bhncv->nbhdv', k_dec, v)
    dec = jnp.moveaxis(jnp.exp(b_last), 2, 0)

    def step(S, inp):
        d, u = inp
        return d[..., None] * S + u, S

    _, S_prev = lax.scan(step, jnp.zeros((B, H, dk, dv), q.dtype), (dec, upd))
    o_inter = jnp.einsum('bhncd,nbhdv->bhncv', q * jnp.exp(b), S_prev)
    return (o_intra + o_inter).reshape(B, H, T, dv)


def gla_mixer(q, k, v, g, lr, w_lr2, b_lr, norm_g):
    out_dtype = q.dtype
    q, k, v, g, lr = [a.astype(jnp.float32) for a in (q, k, v, g, lr)]
    B, T, _ = q.shape
    lr = lr.reshape(B, T, 2, GLA_LOWRANK)
    logits = jnp.einsum('btzr,zrk->zbtk', lr, w_lr2.astype(jnp.float32)) + b_lr.astype(jnp.float32)[:, None, None, :]
    gk = jax.nn.log_sigmoid(logits) / GLA_GATE_NORMALIZER
    gk = gk.reshape(2, B, T, GLA_HEADS, GLA_DK).transpose(0, 1, 3, 2, 4)
    qh = split_heads(q, GLA_HEADS) * (GLA_DK ** -0.5)
    kh = split_heads(k, GLA_HEADS)
    vh = split_heads(v, GLA_HEADS)
    o_f = gla_chunk_scan(qh, kh, vh, gk[0])
    fl = lambda a: jnp.flip(a, axis=2)
    o_b = fl(gla_chunk_scan(fl(qh), fl(kh), fl(vh), fl(gk[1])))
    o = rmsnorm(o_f + o_b, norm_g)
    return (merge_heads(o) * jax.nn.silu(g)).astype(out_dtype)


def t5_bucket(rel):
    nb = NUM_BUCKETS // 2
    max_exact = nb // 2
    base = jnp.where(rel > 0, nb, 0)
    n = jnp.abs(rel)
    nf = jnp.maximum(n, 1).astype(jnp.float32)
    large = max_exact + (jnp.log(nf / max_exact) / math.log(MAX_DISTANCE / max_exact)
                         * (nb - max_exact)).astype(jnp.int32)
    large = jnp.minimum(large, nb - 1)
    return base + jnp.where(n < max_exact, n, large)


def swa_band_bias(table):
    W = SWA_BLOCK
    rel = jnp.arange(3 * W)[None, :] - W - jnp.arange(W)[:, None]
    bias = table[t5_bucket(rel)].astype(jnp.float32)
    return bias.transpose(2, 0, 1).reshape(SWA_KV_HEADS, SWA_HEADS // SWA_KV_HEADS, W, 3 * W)


def swa_band_mask(T):
    W = SWA_BLOCK
    n = jnp.arange(T // W)[:, None, None]
    i = jnp.arange(W)[None, :, None]
    j = jnp.arange(3 * W)[None, None, :]
    kpos = n * W + j - W
    return (jnp.abs(j - W - i) <= SWA_WINDOW) & (kpos >= 0) & (kpos < T)


def swa_mixer(q, k, v, sink, band_bias, band_mask):
    B, T, _ = q.shape
    W = SWA_BLOCK
    N = T // W
    G = SWA_HEADS // SWA_KV_HEADS
    qb = q.reshape(B, N, W, SWA_KV_HEADS, G, HEAD_DIM)
    pad = ((0, 0), (W, W), (0, 0), (0, 0))

    def band(a):
        ap = jnp.pad(a.reshape(B, T, SWA_KV_HEADS, HEAD_DIM), pad).reshape(B, N + 2, W, SWA_KV_HEADS, HEAD_DIM)
        return jnp.concatenate([ap[:, :-2], ap[:, 1:-1], ap[:, 2:]], axis=2)

    kb, vb = band(k), band(v)
    s = jnp.einsum('bnqhgd,bnkhd->bnhgqk', qb, kb).astype(jnp.float32) * (HEAD_DIM ** -0.5)
    s = s + band_bias[None, None]
    s = jnp.where(band_mask[None, :, None, None], s, -jnp.inf)
    sink_b = sink.astype(jnp.float32).reshape(SWA_KV_HEADS, G)[None, None, :, :, None, None]
    m = jnp.maximum(jnp.max(s, axis=-1, keepdims=True), sink_b)
    p = jnp.exp(s - m)
    probs = p / (jnp.sum(p, axis=-1, keepdims=True) + jnp.exp(sink_b - m))
    o = jnp.einsum('bnhgqk,bnkhd->bnqhgd', probs.astype(vb.dtype), vb)
    return o.reshape(B, T, SWA_WIDTH)


def short_conv(x, w):
    K = w.shape[0]
    p = K // 2
    T = x.shape[1]
    xp = jnp.pad(x, ((0, 0), (p, p), (0, 0)))
    acc = xp[:, 0:T] * w[0]
    for j in range(1, K):
        acc = acc + xp[:, j:j + T] * w[j]
    return acc


def gated_delta_chunk(q, k, v, g, beta):
    B, H, T, dk = q.shape
    dv = v.shape[-1]
    C = DN_CHUNK
    N = T // C
    q, k, v = [a.reshape(B, H, N, C, a.shape[-1]) for a in (q, k, v)]
    g = g.reshape(B, H, N, C)
    beta = beta.reshape(B, H, N, C)
    b = jnp.cumsum(g, axis=-1)
    tri = jnp.tril(jnp.ones((C, C), dtype=bool))
    strict = jnp.tril(jnp.ones((C, C), dtype=bool), -1)
    L = jnp.exp(jnp.where(tri, b[..., :, None] - b[..., None, :], -jnp.inf))
    kb = k * beta[..., None]
    M = jnp.where(strict, jnp.einsum('bhnid,bhnjd->bhnij', kb, k) * L, 0.0)
    Amat = M + jnp.eye(C, dtype=M.dtype)
    rhs = jnp.concatenate([v * beta[..., None], kb * jnp.exp(b)[..., None]], axis=-1)
    sol = lax.linalg.triangular_solve(Amat, rhs, left_side=True, lower=True, unit_diagonal=True)
    u, w = sol[..., :dv], sol[..., dv:]
    Aqk = jnp.where(tri, jnp.einsum('bhnid,bhnjd->bhnij', q, k) * L, 0.0)
    q_dec = q * jnp.exp(b)[..., None]
    k_dec = k * jnp.exp(b[..., -1:] - b)[..., None]
    dec_last = jnp.exp(b[..., -1])
    xs = tuple(jnp.moveaxis(a, 2, 0) for a in (u, w, Aqk, q_dec, k_dec, dec_last))

    def step(S, inp):
        uc, wc, Ac, qdc, kdc, dl = inp
        v_new = uc - jnp.einsum('bhcd,bhdv->bhcv', wc, S)
        o = jnp.einsum('bhcd,bhdv->bhcv', qdc, S) + jnp.einsum('bhij,bhjv->bhiv', Ac, v_new)
        S = dl[..., None, None] * S + jnp.einsum('bhcd,bhcv->bhdv', kdc, v_new)
        return S, o

    _, o = lax.scan(step, jnp.zeros((B, H, dk, dv), q.dtype), xs)
    return jnp.moveaxis(o, 0, 2).reshape(B, H, T, dv)


def deltanet_mixer(qkv, z, beta_raw, a_raw, conv_w, a_log, dt_bias, norm_g):
    out_dtype = qkv.dtype
    qkv, z, beta_raw, a_raw = [a.astype(jnp.float32) for a in (qkv, z, beta_raw, a_raw)]
    B, T, _ = qkv.shape
    qkv = jax.nn.silu(short_conv(qkv, conv_w.astype(jnp.float32)))
    q, k, v = jnp.split(qkv, [DN_HEADS * DN_DK, 2 * DN_HEADS * DN_DK], axis=-1)
    qh = l2norm(split_heads(q, DN_HEADS)) * (DN_DK ** -0.5)
    kh = l2norm(split_heads(k, DN_HEADS))
    vh = split_heads(v, DN_HEADS)
    beta = jax.nn.sigmoid(beta_raw.reshape(B, T, 2, DN_HEADS)).transpose(2, 0, 3, 1)
    a = a_raw.reshape(B, T, 2, DN_HEADS).transpose(2, 0, 1, 3)
    g = -jnp.exp(a_log.astype(jnp.float32))[:, None, None, :] * jax.nn.softplus(
        a + dt_bias.astype(jnp.float32)[:, None, None, :])
    g = g.transpose(0, 1, 3, 2)
    o_f = gated_delta_chunk(qh, kh, vh, g[0], beta[0])
    fl = lambda x: jnp.flip(x, axis=2)
    o_b = fl(gated_delta_chunk(fl(qh), fl(kh), fl(vh), fl(g[1]), fl(beta[1])))
    o = rmsnorm(o_f + o_b, norm_g)
    return (merge_heads(o) * jax.nn.silu(z)).astype(out_dtype)


def trunk(x, rel_bias_table, mix_pre_g, w_in, gla_w_lr2, gla_b_lr, gla_norm_g, swa_sink,
          dn_conv_w, dn_a_log, dn_dt_bias, dn_norm_g, w_out, mix_post_g, mlp_pre_g, mlp_w1,
          mlp_w2, mlp_post_g):
    T = x.shape[1]
    band_bias = swa_band_bias(rel_bias_table)
    band_mask = swa_band_mask(T)
    for l in range(DEPTH):
        h = rmsnorm(x, mix_pre_g[l])
        proj = h @ w_in[l]
        (gq, gk, gv, gg, glr, sq, sk, sv, dqkv, dz, dbeta, da) = split_columns(proj)
        o_gla = gla_mixer(gq, gk, gv, gg, glr, gla_w_lr2[l], gla_b_lr[l], gla_norm_g[l])
        o_swa = swa_mixer(sq, sk, sv, swa_sink[l], band_bias, band_mask)
        o_dn = deltanet_mixer(dqkv, dz, dbeta, da, dn_conv_w[l], dn_a_log[l], dn_dt_bias[l], dn_norm_g[l])
        mix = jnp.concatenate([o_gla, o_swa.astype(x.dtype), o_dn], axis=-1) @ w_out[l]
        x = x + rmsnorm(mix, mix_post_g[l])
        h = rmsnorm(x, mlp_pre_g[l])
        f = jnp.square(jax.nn.relu(h @ mlp_w1[l])) @ mlp_w2[l]
        x = x + rmsnorm(f, mlp_post_g[l])
    return x


def setup_inputs(seed: int = 0) -> dict:
    key = jax.random.key(seed)
    ks = jax.random.split(key, 24)
    nrm = lambda k, shape, s: jax.random.normal(k, shape, jnp.float32) * s
    gain = lambda k, shape: 1.0 + nrm(k, shape, 0.05)
    dt = jnp.exp(jax.random.uniform(ks[12], (DEPTH, 2, DN_HEADS), jnp.float32,
                                    minval=math.log(1e-3), maxval=math.log(1e-1)))
    return {
        "x_prompt": nrm(ks[0], (BATCH, SEQ, D_MODEL), 1.0),
        "x_sample": nrm(ks[1], (DEC_BATCH, DEC_SEQ, D_MODEL), 1.0),
        "rel_bias_table": nrm(ks[2], (NUM_BUCKETS, SWA_HEADS), 0.5),
        "mix_pre_g": gain(ks[3], (DEPTH, D_MODEL)),
        "w_in": nrm(ks[4], (DEPTH, D_MODEL, D_IN), D_MODEL ** -0.5),
        "gla_w_lr2": nrm(ks[5], (DEPTH, 2, GLA_LOWRANK, GLA_HEADS * GLA_DK), GLA_LOWRANK ** -0.5),
        "gla_b_lr": nrm(ks[6], (DEPTH, 2, GLA_HEADS * GLA_DK), 0.1),
        "gla_norm_g": gain(ks[7], (DEPTH, GLA_DV)),
        "swa_sink": nrm(ks[8], (DEPTH, SWA_HEADS), 0.5),
        "dn_conv_w": nrm(ks[9], (DEPTH, DN_CONV, DN_QKV), DN_CONV ** -0.5),
        "dn_a_log": jnp.log(jax.random.uniform(ks[10], (DEPTH, 2, DN_HEADS), jnp.float32, minval=1.0, maxval=16.0)),
        "dn_dt_bias": dt + jnp.log(-jnp.expm1(-dt)),
        "dn_norm_g": gain(ks[11], (DEPTH, DN_DV)),
        "w_out": nrm(ks[13], (DEPTH, D_MIX, D_MODEL), D_MIX ** -0.5),
        "mix_post_g": gain(ks[14], (DEPTH, D_MODEL)),
        "mlp_pre_g": gain(ks[15], (DEPTH, D_MODEL)),
        "mlp_w1": nrm(ks[16], (DEPTH, D_MODEL, D_FF), D_MODEL ** -0.5),
        "mlp_w2": nrm(ks[17], (DEPTH, D_FF, D_MODEL), D_FF ** -0.5),
        "mlp_post_g": gain(ks[18], (DEPTH, D_MODEL)),
    }


def reference(x_prompt, x_sample, rel_bias_table, mix_pre_g, w_in, gla_w_lr2, gla_b_lr, gla_norm_g,
              swa_sink, dn_conv_w, dn_a_log, dn_dt_bias, dn_norm_g, w_out, mix_post_g, mlp_pre_g,
              mlp_w1, mlp_w2, mlp_post_g):
    y_prompt = trunk(x_prompt, rel_bias_table, mix_pre_g, w_in, gla_w_lr2, gla_b_lr, gla_norm_g,
                     swa_sink, dn_conv_w, dn_a_log, dn_dt_bias, dn_norm_g, w_out, mix_post_g,
                     mlp_pre_g, mlp_w1, mlp_w2, mlp_post_g)
    y_sample = trunk(x_sample, rel_bias_table, mix_pre_g, w_in, gla_w_lr2, gla_b_lr, gla_norm_g,
                     swa_sink, dn_conv_w, dn_a_log, dn_dt_bias, dn_norm_g, w_out, mix_post_g,
                     mlp_pre_g, mlp_w1, mlp_w2, mlp_post_g)
    return (y_prompt, y_sample)
```

```python
import functools
import math

import jax
import jax.numpy as jnp
from jax import lax
from jax.experimental import pallas as pl
from jax.experimental.pallas import tpu as pltpu

D_MODEL = 1024
DEPTH = 4
HEAD_DIM = 64
EPS = 1e-6
GLA_HEADS = 4
GLA_DK = 32
GLA_DV = 64
GLA_LOWRANK = 16
GLA_GATE_NORMALIZER = 16.0
GLA_CHUNK = 32
SWA_HEADS = 8
SWA_KV_HEADS = 2
SWA_WINDOW = 128
SWA_BLOCK = 128
NUM_BUCKETS = 32
MAX_DISTANCE = 128
DN_HEADS = 4
DN_DK = 64
DN_DV = 64
DN_CONV = 5
DN_CHUNK = 64
GLA_WIDTH = GLA_HEADS * GLA_DV
SWA_WIDTH = SWA_HEADS * HEAD_DIM
DN_WIDTH = DN_HEADS * DN_DV
D_MIX = GLA_WIDTH + SWA_WIDTH + DN_WIDTH
D_FF = 4 * D_MODEL
DN_QKV = DN_HEADS * (2 * DN_DK + DN_DV)
IN_SIZES = (GLA_HEADS * GLA_DK, GLA_HEADS * GLA_DK, GLA_WIDTH, GLA_WIDTH, 2 * GLA_LOWRANK,
            SWA_WIDTH, SWA_KV_HEADS * HEAD_DIM, SWA_KV_HEADS * HEAD_DIM,
            DN_QKV, DN_WIDTH, 2 * DN_HEADS, 2 * DN_HEADS)
D_IN = sum(IN_SIZES)

VMEM_LIMIT_BYTES = 56 * 1024 * 1024
ROW_TILE = 512
FF_CHUNK = 1024


def _rms(xf, g):
    return xf * lax.rsqrt(jnp.mean(xf * xf, axis=-1, keepdims=True) + EPS) * g


def _const_spec(shape):
    return pl.BlockSpec(shape, lambda i: (0,) * len(shape), pipeline_mode=pl.Buffered(1))


def _in_proj_kernel(x_ref, g_ref, w_ref, o_ref):
    h = _rms(x_ref[...], g_ref[...]).astype(jnp.bfloat16)
    o_ref[...] = jnp.dot(h, w_ref[...], preferred_element_type=jnp.float32)


def in_proj(x2d, g, w_bf16):
    n, d = x2d.shape
    d_out = w_bf16.shape[1]
    tm = min(ROW_TILE, n)
    return pl.pallas_call(
        _in_proj_kernel,
        grid=(n // tm,),
        in_specs=[pl.BlockSpec((tm, d), lambda i: (i, 0)),
                  _const_spec((1, d)),
                  _const_spec((d, d_out))],
        out_specs=pl.BlockSpec((tm, d_out), lambda i: (i, 0)),
        out_shape=jax.ShapeDtypeStruct((n, d_out), jnp.float32),
        compiler_params=pltpu.CompilerParams(dimension_semantics=("arbitrary",),
                                             vmem_limit_bytes=VMEM_LIMIT_BYTES),
        name="in_proj",
    )(x2d, g.reshape(1, d), w_bf16)


def _out_mlp_kernel(mix_ref, x_ref, wo_ref, g1_ref, g2_ref, w1_ref, w2_ref, g3_ref, o_ref):
    m = jnp.dot(mix_ref[...].astype(jnp.bfloat16), wo_ref[...], preferred_element_type=jnp.float32)
    x1 = x_ref[...] + _rms(m, g1_ref[...])
    h = _rms(x1, g2_ref[...]).astype(jnp.bfloat16)
    acc = jnp.zeros_like(x1)
    for c in range(D_FF // FF_CHUNK):
        f = jnp.dot(h, w1_ref[:, c * FF_CHUNK:(c + 1) * FF_CHUNK], preferred_element_type=jnp.float32)
        f = jnp.square(jnp.maximum(f, 0.0)).astype(jnp.bfloat16)
        acc = acc + jnp.dot(f, w2_ref[c * FF_CHUNK:(c + 1) * FF_CHUNK, :], preferred_element_type=jnp.float32)
    o_ref[...] = x1 + _rms(acc, g3_ref[...])


def out_mlp(mix2d, x2d, wo, g1, g2, w1, w2, g3):
    n, d = x2d.shape
    tm = min(ROW_TILE, n)
    row = pl.BlockSpec((tm, d), lambda i: (i, 0))
    vec = _const_spec((1, d))
    return pl.pallas_call(
        _out_mlp_kernel,
        grid=(n // tm,),
        in_specs=[row, row, _const_spec((d, d)), vec, vec,
                  _const_spec((d, D_FF)), _const_spec((D_FF, d)), vec],
        out_specs=row,
        out_shape=jax.ShapeDtypeStruct((n, d), jnp.float32),
        compiler_params=pltpu.CompilerParams(dimension_semantics=("arbitrary",),
                                             vmem_limit_bytes=VMEM_LIMIT_BYTES),
        name="out_mlp",
    )(mix2d, x2d, wo, g1.reshape(1, d), g2.reshape(1, d), w1, w2, g3.reshape(1, d))


def _rmsnorm(x, g):
    xf = x.astype(jnp.float32)
    y = xf * lax.rsqrt(jnp.mean(xf * xf, axis=-1, keepdims=True) + EPS)
    return (y * g.astype(jnp.float32)).astype(x.dtype)


def _l2norm(x):
    return x * lax.rsqrt(jnp.sum(x * x, axis=-1, keepdims=True) + EPS)


def _split_heads(x, h):
    B, T, _ = x.shape
    return x.reshape(B, T, h, -1).transpose(0, 2, 1, 3)


def _merge_heads(x):
    B, H, T, d = x.shape
    return x.transpose(0, 2, 1, 3).reshape(B, T, H * d)


def _split_columns(proj):
    idx, acc = [], 0
    for s in IN_SIZES[:-1]:
        acc += s
        idx.append(acc)
    return jnp.split(proj, idx, axis=-1)


def _gla_chunk_scan(q, k, v, gk):
    B, H, T, dk = q.shape
    dv = v.shape[-1]
    C = GLA_CHUNK
    N = T // C
    q, k, v, gk = [a.reshape(B, H, N, C, a.shape[-1]) for a in (q, k, v, gk)]
    b = jnp.cumsum(gk, axis=3)
    causal = jnp.tril(jnp.ones((C, C), dtype=bool))
    diff = jnp.where(causal[:, :, None], b[..., :, None, :] - b[..., None, :, :], -jnp.inf)
    A = jnp.einsum('bhnid,bhnjd,bhnijd->bhnij', q, k, jnp.exp(diff))
    o_intra = jnp.einsum('bhnij,bhnjv->bhniv', A, v)
    b_last = b[..., -1, :]
    k_dec = k * jnp.exp(b_last[..., None, :] - b)
    upd = jnp.einsum('bhncd,bhncv->nbhdv', k_dec, v)
    dec = jnp.moveaxis(jnp.exp(b_last), 2, 0)

    def step(S, inp):
        d, u = inp
        return d[..., None] * S + u, S

    _, S_prev = lax.scan(step, jnp.zeros((B, H, dk, dv), q.dtype), (dec, upd))
    o_inter = jnp.einsum('bhncd,nbhdv->bhncv', q * jnp.exp(b), S_prev)
    return (o_intra + o_inter).reshape(B, H, T, dv)


def _gla_mixer(q, k, v, g, lr, w_lr2, b_lr, norm_g):
    B, T, _ = q.shape
    lr = lr.reshape(B, T, 2, GLA_LOWRANK)
    logits = jnp.einsum('btzr,zrk->zbtk', lr, w_lr2) + b_lr[:, None, None, :]
    gk = jax.nn.log_sigmoid(logits) / GLA_GATE_NORMALIZER
    gk = gk.reshape(2, B, T, GLA_HEADS, GLA_DK).transpose(0, 1, 3, 2, 4)
    qh = _split_heads(q, GLA_HEADS) * (GLA_DK ** -0.5)
    kh = _split_heads(k, GLA_HEADS)
    vh = _split_heads(v, GLA_HEADS)
    o_f = _gla_chunk_scan(qh, kh, vh, gk[0])
    fl = lambda a: jnp.flip(a, axis=2)
    o_b = fl(_gla_chunk_scan(fl(qh), fl(kh), fl(vh), fl(gk[1])))
    o = _rmsnorm(o_f + o_b, norm_g)
    return _merge_heads(o) * jax.nn.silu(g)


def _t5_bucket(rel):
    nb = NUM_BUCKETS // 2
    max_exact = nb // 2
    base = jnp.where(rel > 0, nb, 0)
    n = jnp.abs(rel)
    nf = jnp.maximum(n, 1).astype(jnp.float32)
    large = max_exact + (jnp.log(nf / max_exact) / math.log(MAX_DISTANCE / max_exact)
                         * (nb - max_exact)).astype(jnp.int32)
    large = jnp.minimum(large, nb - 1)
    return base + jnp.where(n < max_exact, n, large)


def _swa_band_bias(table):
    W = SWA_BLOCK
    rel = jnp.arange(3 * W)[None, :] - W - jnp.arange(W)[:, None]
    bias = table[_t5_bucket(rel)].astype(jnp.float32)
    return bias.transpose(2, 0, 1).reshape(SWA_KV_HEADS, SWA_HEADS // SWA_KV_HEADS, W, 3 * W)


def _swa_band_mask(T):
    W = SWA_BLOCK
    n = jnp.arange(T // W)[:, None, None]
    i = jnp.arange(W)[None, :, None]
    j = jnp.arange(3 * W)[None, None, :]
    kpos = n * W + j - W
    return (jnp.abs(j - W - i) <= SWA_WINDOW) & (kpos >= 0) & (kpos < T)


def _swa_mixer(q, k, v, sink, band_bias, band_mask):
    B, T, _ = q.shape
    W = SWA_BLOCK
    N = T // W
    G = SWA_HEADS // SWA_KV_HEADS
    qb = q.reshape(B, N, W, SWA_KV_HEADS, G, HEAD_DIM)
    pad = ((0, 0), (W, W), (0, 0), (0, 0))

    def band(a):
        ap = jnp.pad(a.reshape(B, T, SWA_KV_HEADS, HEAD_DIM), pad).reshape(B, N + 2, W, SWA_KV_HEADS, HEAD_DIM)
        return jnp.concatenate([ap[:, :-2], ap[:, 1:-1], ap[:, 2:]], axis=2)

    kb, vb = band(k), band(v)
    s = jnp.einsum('bnqhgd,bnkhd->bnhgqk', qb, kb).astype(jnp.float32) * (HEAD_DIM ** -0.5)
    s = s + band_bias[None, None]
    s = jnp.where(band_mask[None, :, None, None], s, -jnp.inf)
    sink_b = sink.astype(jnp.float32).reshape(SWA_KV_HEADS, G)[None, None, :, :, None, None]
    m = jnp.maximum(jnp.max(s, axis=-1, keepdims=True), sink_b)
    p = jnp.exp(s - m)
    probs = p / (jnp.sum(p, axis=-1, keepdims=True) + jnp.exp(sink_b - m))
    o = jnp.einsum('bnhgqk,bnkhd->bnqhgd', probs.astype(vb.dtype), vb)
    return o.reshape(B, T, SWA_WIDTH)


def _short_conv(x, w):
    K = w.shape[0]
    p = K // 2
    T = x.shape[1]
    xp = jnp.pad(x, ((0, 0), (p, p), (0, 0)))
    acc = xp[:, 0:T] * w[0]
    for j in range(1, K):
        acc = acc + xp[:, j:j + T] * w[j]
    return acc


def _gated_delta_chunk(q, k, v, g, beta):
    B, H, T, dk = q.shape
    dv = v.shape[-1]
    C = DN_CHUNK
    N = T // C
    q, k, v = [a.reshape(B, H, N, C, a.shape[-1]) for a in (q, k, v)]
    g = g.reshape(B, H, N, C)
    beta = beta.reshape(B, H, N, C)
    b = jnp.cumsum(g, axis=-1)
    tri = jnp.tril(jnp.ones((C, C), dtype=bool))
    strict = jnp.tril(jnp.ones((C, C), dtype=bool), -1)
    L = jnp.exp(jnp.where(tri, b[..., :, None] - b[..., None, :], -jnp.inf))
    kb = k * beta[..., None]
    M = jnp.where(strict, jnp.einsum('bhnid,bhnjd->bhnij', kb, k) * L, 0.0)
    Amat = M + jnp.eye(C, dtype=M.dtype)
    rhs = jnp.concatenate([v * beta[..., None], kb * jnp.exp(b)[..., None]], axis=-1)
    sol = lax.linalg.triangular_solve(Amat, rhs, left_side=True, lower=True, unit_diagonal=True)
    u, w = sol[..., :dv], sol[..., dv:]
    Aqk = jnp.where(tri, jnp.einsum('bhnid,bhnjd->bhnij', q, k) * L, 0.0)
    q_dec = q * jnp.exp(b)[..., None]
    k_dec = k * jnp.exp(b[..., -1:] - b)[..., None]
    dec_last = jnp.exp(b[..., -1])
    xs = tuple(jnp.moveaxis(a, 2, 0) for a in (u, w, Aqk, q_dec, k_dec, dec_last))

    def step(S, inp):
        uc, wc, Ac, qdc, kdc, dl = inp
        v_new = uc - jnp.einsum('bhcd,bhdv->bhcv', wc, S)
        o = jnp.einsum('bhcd,bhdv->bhcv', qdc, S) + jnp.einsum('bhij,bhjv->bhiv', Ac, v_new)
        S = dl[..., None, None] * S + jnp.einsum('bhcd,bhcv->bhdv', kdc, v_new)
        return S, o

    _, o = lax.scan(step, jnp.zeros((B, H, dk, dv), q.dtype), xs)
    return jnp.moveaxis(o, 0, 2).reshape(B, H, T, dv)


def _deltanet_mixer(qkv, z, beta_raw, a_raw, conv_w, a_log, dt_bias, norm_g):
    B, T, _ = qkv.shape
    qkv = jax.nn.silu(_short_conv(qkv, conv_w))
    q, k, v = jnp.split(qkv, [DN_HEADS * DN_DK, 2 * DN_HEADS * DN_DK], axis=-1)
    qh = _l2norm(_split_heads(q, DN_HEADS)) * (DN_DK ** -0.5)
    kh = _l2norm(_split_heads(k, DN_HEADS))
    vh = _split_heads(v, DN_HEADS)
    beta = jax.nn.sigmoid(beta_raw.reshape(B, T, 2, DN_HEADS)).transpose(2, 0, 3, 1)
    a = a_raw.reshape(B, T, 2, DN_HEADS).transpose(2, 0, 1, 3)
    g = -jnp.exp(a_log)[:, None, None, :] * jax.nn.softplus(a + dt_bias[:, None, None, :])
    g = g.transpose(0, 1, 3, 2)
    o_f = _gated_delta_chunk(qh, kh, vh, g[0], beta[0])
    fl = lambda x: jnp.flip(x, axis=2)
    o_b = fl(_gated_delta_chunk(fl(qh), fl(kh), fl(vh), fl(g[1]), fl(beta[1])))
    o = _rmsnorm(o_f + o_b, norm_g)
    return _merge_heads(o) * jax.nn.silu(z)


def _trunk(x, p):
    B, T, D = x.shape
    band_bias = _swa_band_bias(p["rel_bias_table"])
    band_mask = _swa_band_mask(T)
    x2 = x.reshape(B * T, D)
    for l in range(DEPTH):
        proj = in_proj(x2, p["mix_pre_g"][l], p["w_in_bf16"][l]).reshape(B, T, D_IN)
        (gq, gk, gv, gg, glr, sq, sk, sv, dqkv, dz, dbeta, da) = _split_columns(proj)
        o_gla = _gla_mixer(gq, gk, gv, gg, glr, p["gla_w_lr2"][l], p["gla_b_lr"][l], p["gla_norm_g"][l])
        o_swa = _swa_mixer(sq, sk, sv, p["swa_sink"][l], band_bias, band_mask)
        o_dn = _deltanet_mixer(dqkv, dz, dbeta, da, p["dn_conv_w"][l], p["dn_a_log"][l],
                               p["dn_dt_bias"][l], p["dn_norm_g"][l])
        mix = jnp.concatenate([o_gla, o_swa, o_dn], axis=-1).reshape(B * T, D)
        x2 = out_mlp(mix, x2, p["w_out_bf16"][l], p["mix_post_g"][l], p["mlp_pre_g"][l],
                     p["mlp_w1_bf16"][l], p["mlp_w2_bf16"][l], p["mlp_post_g"][l])
    return x2.reshape(B, T, D)


def kernel(x_prompt, x_sample, rel_bias_table, mix_pre_g, w_in, gla_w_lr2, gla_b_lr, gla_norm_g, swa_sink, dn_conv_w, dn_a_log, dn_dt_bias, dn_norm_g, w_out, mix_post_g, mlp_pre_g, mlp_w1, mlp_w2, mlp_post_g):
    p = dict(rel_bias_table=rel_bias_table, mix_pre_g=mix_pre_g, gla_w_lr2=gla_w_lr2, gla_b_lr=gla_b_lr,
             gla_norm_g=gla_norm_g, swa_sink=swa_sink, dn_conv_w=dn_conv_w, dn_a_log=dn_a_log,
             dn_dt_bias=dn_dt_bias, dn_norm_g=dn_norm_g, mix_post_g=mix_post_g, mlp_pre_g=mlp_pre_g,
             mlp_post_g=mlp_post_g,
             w_in_bf16=w_in.astype(jnp.bfloat16), w_out_bf16=w_out.astype(jnp.bfloat16),
             mlp_w1_bf16=mlp_w1.astype(jnp.bfloat16), mlp_w2_bf16=mlp_w2.astype(jnp.bfloat16))
    return (_trunk(x_prompt, p), _trunk(x_sample, p))
```

```python
import functools
import math

import jax
import jax.numpy as jnp
from jax import lax
from jax.experimental import pallas as pl
from jax.experimental.pallas import tpu as pltpu

D_MODEL = 1024
DEPTH = 4
HEAD_DIM = 64
EPS = 1e-6
GLA_HEADS = 4
GLA_DK = 32
GLA_DV = 64
GLA_LOWRANK = 16
GLA_GATE_NORMALIZER = 16.0
GLA_CHUNK = 32
SWA_HEADS = 8
SWA_KV_HEADS = 2
SWA_WINDOW = 128
SWA_BLOCK = 128
NUM_BUCKETS = 32
MAX_DISTANCE = 128
DN_HEADS = 4
DN_DK = 64
DN_DV = 64
DN_CONV = 5
DN_CHUNK = 64
GLA_WIDTH = GLA_HEADS * GLA_DV
SWA_WIDTH = SWA_HEADS * HEAD_DIM
DN_WIDTH = DN_HEADS * DN_DV
D_MIX = GLA_WIDTH + SWA_WIDTH + DN_WIDTH
D_FF = 4 * D_MODEL
DN_QKV = DN_HEADS * (2 * DN_DK + DN_DV)
IN_SIZES = (GLA_HEADS * GLA_DK, GLA_HEADS * GLA_DK, GLA_WIDTH, GLA_WIDTH, 2 * GLA_LOWRANK,
            SWA_WIDTH, SWA_KV_HEADS * HEAD_DIM, SWA_KV_HEADS * HEAD_DIM,
            DN_QKV, DN_WIDTH, 2 * DN_HEADS, 2 * DN_HEADS)
D_IN = sum(IN_SIZES)

VMEM_LIMIT_BYTES = 56 * 1024 * 1024
ROW_TILE = 512
FF_CHUNK = 1024


def _rms(xf, g):
    return xf * lax.rsqrt(jnp.mean(xf * xf, axis=-1, keepdims=True) + EPS) * g


def _const_spec(shape):
    return pl.BlockSpec(shape, lambda i: (0,) * len(shape), pipeline_mode=pl.Buffered(1))


_O = {}
_acc = 0
for _name, _width in (("swa_q", SWA_WIDTH), ("swa_kv", 4 * 2 * HEAD_DIM), ("gla_qk", 2 * GLA_HEADS * GLA_DK),
                      ("gla_v", GLA_WIDTH), ("gla_g", GLA_WIDTH), ("dn_qkv", DN_QKV), ("dn_z", DN_WIDTH),
                      ("small", 128)):
    _O[_name] = (_acc, _width)
    _acc += _width
D_IN_PACKED = _acc
_OUT_DTYPES = {"swa_q": jnp.bfloat16, "swa_kv": jnp.bfloat16}


def pack_w_in(w_in):
    offs, acc = [], 0
    for s in IN_SIZES:
        offs.append(acc)
        acc += s
    gq, gk, gv, gg, glr, sq, sk, sv, dqkv, dz, dbeta, da = [w_in[:, o:o + s] for o, s in zip(offs, IN_SIZES)]
    hd = HEAD_DIM
    kv = [sk[:, 0:hd], sk[:, 0:hd], sk[:, hd:], sk[:, hd:], sv[:, 0:hd], sv[:, 0:hd], sv[:, hd:], sv[:, hd:]]
    small = jnp.concatenate([glr, dbeta, da], axis=1)
    small = jnp.pad(small, ((0, 0), (0, 128 - small.shape[1])))
    return jnp.concatenate([sq] + kv + [gq, gk, gv, gg, dqkv, dz, small], axis=1).astype(jnp.bfloat16)


def _in_proj_kernel(x_ref, g_ref, w_ref, *o_refs):
    h = _rms(x_ref[...], g_ref[...]).astype(jnp.bfloat16)
    y = jnp.dot(h, w_ref[...], preferred_element_type=jnp.float32)
    for (off, width), o_ref in zip(_O.values(), o_refs):
        o_ref[...] = y[:, off:off + width].astype(o_ref.dtype)


def in_proj(x2d, g, w_packed):
    n, d = x2d.shape
    tm = min(ROW_TILE, n)
    return pl.pallas_call(
        _in_proj_kernel,
        grid=(n // tm,),
        in_specs=[pl.BlockSpec((tm, d), lambda i: (i, 0)),
                  _const_spec((1, d)),
                  _const_spec((d, D_IN_PACKED))],
        out_specs=[pl.BlockSpec((tm, w), lambda i: (i, 0)) for _, w in _O.values()],
        out_shape=[jax.ShapeDtypeStruct((n, w), _OUT_DTYPES.get(name, jnp.float32))
                   for name, (_, w) in _O.items()],
        compiler_params=pltpu.CompilerParams(dimension_semantics=("arbitrary",),
                                             vmem_limit_bytes=VMEM_LIMIT_BYTES),
        name="in_proj",
    )(x2d, g.reshape(1, d), w_packed)


SWA_Q_TILE = 512
SWA_GROUP = SWA_HEADS // SWA_KV_HEADS


def _t5_bucket(rel):
    nb = NUM_BUCKETS // 2
    max_exact = nb // 2
    base = jnp.where(rel > 0, nb, 0)
    n = jnp.abs(rel)
    nf = jnp.maximum(n, 1).astype(jnp.float32)
    large = max_exact + (jnp.log(nf / max_exact) / math.log(MAX_DISTANCE / max_exact)
                         * (nb - max_exact)).astype(jnp.int32)
    large = jnp.minimum(large, nb - 1)
    return base + jnp.where(n < max_exact, n, large)


def _band_bias_kernel(tab_ref, idx_ref, o_ref):
    W = SWA_BLOCK
    idx = idx_ref[...]
    i = lax.broadcasted_iota(jnp.int32, idx.shape, 0)
    j = lax.broadcasted_iota(jnp.int32, idx.shape, 1)
    in_band = jnp.abs(j - W - i) <= SWA_WINDOW
    for h in range(SWA_HEADS):
        acc = jnp.zeros(idx.shape, jnp.float32)
        for b in range(NUM_BUCKETS):
            acc = jnp.where(idx == b, tab_ref[b, h], acc)
        o_ref[h] = jnp.where(in_band, acc, -jnp.inf)


def band_bias(table):
    W = SWA_BLOCK
    rel = jnp.arange(3 * W)[None, :] - W - jnp.arange(W)[:, None]
    idx = _t5_bucket(rel).astype(jnp.int32)
    return pl.pallas_call(
        _band_bias_kernel,
        in_specs=[pl.BlockSpec(memory_space=pltpu.SMEM), pl.BlockSpec(memory_space=pltpu.VMEM)],
        out_specs=pl.BlockSpec(memory_space=pltpu.VMEM),
        out_shape=jax.ShapeDtypeStruct((SWA_HEADS, W, 3 * W), jnp.float32),
        name="band_bias",
    )(table.astype(jnp.float32), idx)


def _swa_kernel(sink_ref, q_ref, kvp_ref, kvc_ref, kvn_ref, bias_ref, o_ref, kv_buf):
    W = SWA_BLOCK
    n_sub = q_ref.shape[0] // W
    n = pl.program_id(1)
    last = pl.num_programs(1) - 1
    kv_buf[0:W, :] = kvp_ref[...]
    kv_buf[W:W + n_sub * W, :] = kvc_ref[...]
    kv_buf[W + n_sub * W:, :] = kvn_ref[...]
    lane = lax.broadcasted_iota(jnp.int32, (W, 2 * HEAD_DIM), 1)
    lo = lane < HEAD_DIM
    key_blk = lax.broadcasted_iota(jnp.int32, (1, 3 * W), 1) // W
    neg = jnp.float32(-jnp.inf)
    for s in range(n_sub):
        edge = jnp.zeros((1, 3 * W), jnp.float32)
        if s == 0:
            edge = jnp.where((key_blk == 0) & (n == 0), neg, edge)
        if s == n_sub - 1:
            edge = jnp.where((key_blk == 2) & (n == last), neg, edge)
        rows = slice(s * W, (s + 1) * W)
        for g in range(SWA_KV_HEADS):
            kd = kv_buf[s * W:(s + 3) * W, g * 128:(g + 1) * 128]
            vd = kv_buf[s * W:(s + 3) * W, 256 + g * 128:256 + (g + 1) * 128]
            qs = []
            for pair in range(SWA_GROUP // 2):
                col = (g * SWA_GROUP // 2 + pair) * 128
                qp = q_ref[rows, col:col + 128]
                qs += [jnp.where(lo, qp, 0), jnp.where(lo, 0, qp)]
            q_stack = jnp.concatenate(qs, axis=0)
            s_all = lax.dot_general(q_stack, kd, (((1,), (1,)), ((), ())),
                                    preferred_element_type=jnp.float32)
            ps, rinv = [], []
            for hh in range(SWA_GROUP):
                h = g * SWA_GROUP + hh
                sc = s_all[hh * W:(hh + 1) * W] * (HEAD_DIM ** -0.5) + bias_ref[h]
                if s == 0 or s == n_sub - 1:
                    sc = sc + edge
                sink = sink_ref[h]
                m = jnp.maximum(jnp.max(sc, axis=-1, keepdims=True), sink)
                p = jnp.exp(sc - m)
                denom = jnp.sum(p, axis=-1, keepdims=True) + jnp.exp(sink - m)
                ps.append(p.astype(jnp.bfloat16))
                rinv.append(1.0 / denom)
            o_all = jnp.dot(jnp.concatenate(ps, axis=0), vd, preferred_element_type=jnp.float32)
            for pair in range(SWA_GROUP // 2):
                e, o = 2 * pair, 2 * pair + 1
                col = (g * SWA_GROUP // 2 + pair) * 128
                o_ref[rows, col:col + 128] = jnp.where(lo, o_all[e * W:(e + 1) * W] * rinv[e],
                                                       o_all[o * W:(o + 1) * W] * rinv[o]).astype(o_ref.dtype)


def swa_attention(q, kv, sink, bias, batch, seq):
    W = SWA_BLOCK
    tq = min(SWA_Q_TILE, seq)
    n_sub = tq // W
    nq = seq // tq
    nb = seq // W
    return pl.pallas_call(
        _swa_kernel,
        grid=(batch, nq),
        in_specs=[pl.BlockSpec(memory_space=pltpu.SMEM),
                  pl.BlockSpec((tq, SWA_WIDTH), lambda b, i: (b * nq + i, 0)),
                  pl.BlockSpec((W, 512), lambda b, i: (b * nb + jnp.maximum(i * n_sub - 1, 0), 0)),
                  pl.BlockSpec((tq, 512), lambda b, i: (b * nq + i, 0)),
                  pl.BlockSpec((W, 512), lambda b, i: (b * nb + jnp.minimum((i + 1) * n_sub, nb - 1), 0)),
                  pl.BlockSpec((SWA_HEADS, W, 3 * W), lambda b, i: (0, 0, 0), pipeline_mode=pl.Buffered(1))],
        out_specs=pl.BlockSpec((tq, SWA_WIDTH), lambda b, i: (b * nq + i, 0)),
        out_shape=jax.ShapeDtypeStruct((batch * seq, SWA_WIDTH), jnp.bfloat16),
        scratch_shapes=[pltpu.VMEM((tq + 2 * W, 512), jnp.bfloat16)],
        compiler_params=pltpu.CompilerParams(dimension_semantics=("arbitrary", "arbitrary"),
                                             vmem_limit_bytes=VMEM_LIMIT_BYTES),
        name="swa",
    )(sink.astype(jnp.float32), q, kv, kv, kv, bias)


GLA_TILE = 256
GLA_SUB = 16
_NT = (((1,), (1,)), ((), ()))
_TN = (((0,), (0,)), ((), ()))


def _seg_cumsum(x, seg, reverse):
    n = x.shape[0]
    rowmod = lax.broadcasted_iota(jnp.int32, x.shape, 0) % seg
    sh = 1
    while sh < seg:
        if reverse:
            x = x + jnp.where(rowmod < seg - sh, pltpu.roll(x, n - sh, 0), 0.0)
        else:
            x = x + jnp.where(rowmod >= sh, pltpu.roll(x, sh, 0), 0.0)
        sh *= 2
    return x


def _log_sigmoid(x):
    return jnp.minimum(x, 0.0) - jnp.log1p(jnp.exp(-jnp.abs(x)))


def _gla_direction(qk_ref, v_ref, sm_ref, wg_ref, bg_ref, st_ref, o_ref, reverse):
    tb, S = qk_ref.shape[0], GLA_SUB
    hk = GLA_HEADS * GLA_DK
    logits = jnp.dot(sm_ref[...].astype(jnp.bfloat16), wg_ref[...], preferred_element_type=jnp.float32)
    g = _log_sigmoid(logits + bg_ref[...]) * (1.0 / GLA_GATE_NORMALIZER)
    cs = _seg_cumsum(g, S, reverse)
    q = qk_ref[:, :hk] * (GLA_DK ** -0.5)
    k = qk_ref[:, hk:]
    v = v_ref[...]
    rowmod = lax.broadcasted_iota(jnp.int32, (tb, hk), 0) % S
    expand = (lax.broadcasted_iota(jnp.int32, (hk, GLA_WIDTH), 0) // GLA_DK ==
              lax.broadcasted_iota(jnp.int32, (hk, GLA_WIDTH), 1) // GLA_DV).astype(jnp.bfloat16)
    acc = jnp.zeros((tb, GLA_WIDTH), jnp.float32)
    for d in range(S):
        if d == 0:
            p, vs = q * k, v
        else:
            sh = tb - d if reverse else d
            valid = (rowmod < S - d) if reverse else (rowmod >= d)
            p = q * pltpu.roll(k, sh, 0) * jnp.exp(cs - pltpu.roll(cs, sh, 0))
            p = jnp.where(valid, p, 0.0)
            vs = pltpu.roll(v, sh, 0)
        acc = acc + jnp.dot(p.astype(jnp.bfloat16), expand, preferred_element_type=jnp.float32) * vs
    qd = (q * jnp.exp(cs)).astype(jnp.bfloat16)
    blockdiag = (lax.broadcasted_iota(jnp.int32, (GLA_WIDTH, hk), 0) // GLA_DV ==
                 lax.broadcasted_iota(jnp.int32, (GLA_WIDTH, hk), 1) // GLA_DK)
    st = st_ref[...]
    n_blk = tb // S
    for s in (range(n_blk - 1, -1, -1) if reverse else range(n_blk)):
        rows = slice(s * S, (s + 1) * S)
        cs_s = cs[rows]
        edge = cs_s[0:1] if reverse else cs_s[S - 1:S]
        kt = (k[rows] * jnp.exp(edge - cs_s)).astype(jnp.bfloat16)
        o_ref[rows, :] = acc[rows] + lax.dot_general(qd[rows], st.astype(jnp.bfloat16), _NT,
                                                     preferred_element_type=jnp.float32)
        upd = lax.dot_general(v[rows].astype(jnp.bfloat16), kt, _TN, preferred_element_type=jnp.float32)
        st = st * jnp.exp(edge) + jnp.where(blockdiag, upd, 0.0)
    st_ref[...] = st


def _gla_kernel(qkf_ref, vf_ref, smf_ref, qkb_ref, vb_ref, smb_ref, wg_ref, bg_ref, of_ref, ob_ref, stf_ref, stb_ref):
    @pl.when(pl.program_id(1) == 0)
    def _():
        stf_ref[...] = jnp.zeros_like(stf_ref)
        stb_ref[...] = jnp.zeros_like(stb_ref)

    _gla_direction(qkf_ref, vf_ref, smf_ref, wg_ref.at[0], bg_ref.at[0], stf_ref, of_ref, False)
    _gla_direction(qkb_ref, vb_ref, smb_ref, wg_ref.at[1], bg_ref.at[1], stb_ref, ob_ref, True)


def gla_scan(gla_qk, gla_v, small, w_lr2, b_lr, batch, seq):
    hk = GLA_HEADS * GLA_DK
    tb = min(GLA_TILE, seq)
    nt = seq // tb
    wg = jnp.zeros((2, 128, hk), jnp.float32)
    for z in range(2):
        wg = wg.at[z, z * GLA_LOWRANK:(z + 1) * GLA_LOWRANK].set(w_lr2[z])
    fwd = lambda w: pl.BlockSpec((tb, w), lambda b, i: (b * nt + i, 0))
    bwd = lambda w: pl.BlockSpec((tb, w), lambda b, i: (b * nt + nt - 1 - i, 0))
    out = jax.ShapeDtypeStruct((batch * seq, GLA_WIDTH), jnp.float32)
    return pl.pallas_call(
        _gla_kernel,
        grid=(batch, nt),
        in_specs=[fwd(2 * hk), fwd(GLA_WIDTH), fwd(128), bwd(2 * hk), bwd(GLA_WIDTH), bwd(128),
                  pl.BlockSpec((2, 128, hk), lambda b, i: (0, 0, 0)),
                  pl.BlockSpec((2, 1, hk), lambda b, i: (0, 0, 0))],
        out_specs=[fwd(GLA_WIDTH), bwd(GLA_WIDTH)],
        out_shape=[out, out],
        scratch_shapes=[pltpu.VMEM((GLA_WIDTH, hk), jnp.float32), pltpu.VMEM((GLA_WIDTH, hk), jnp.float32)],
        compiler_params=pltpu.CompilerParams(dimension_semantics=("arbitrary", "arbitrary"),
                                             vmem_limit_bytes=VMEM_LIMIT_BYTES),
        name="gla",
    )(gla_qk, gla_v, small, gla_qk, gla_v, small, wg.astype(jnp.bfloat16), b_lr.reshape(2, 1, hk).astype(jnp.float32))


DN_PREP_TILE = 512
DN_HALO = 8
DN_TILE = 64
DN_BETA_LANE = 2 * GLA_LOWRANK
DN_A_LANE = DN_BETA_LANE + 2 * DN_HEADS


def _split_dot(x, w_bf16):
    hi = x.astype(jnp.bfloat16)
    lo = (x - hi.astype(jnp.float32)).astype(jnp.bfloat16)
    return (jnp.dot(hi, w_bf16, preferred_element_type=jnp.float32) +
            jnp.dot(lo, w_bf16, preferred_element_type=jnp.float32))


def _group_indicator(n, group):
    r = lax.broadcasted_iota(jnp.int32, (n, n), 0) // group
    c = lax.broadcasted_iota(jnp.int32, (n, n), 1) // group
    return (r == c).astype(jnp.bfloat16)


def _dn_prep_kernel(xp_ref, xc_ref, xn_ref, sm_ref, cw_ref, na_ref, dt_ref, qkv_ref, gate_ref, ext_ref):
    tm = xc_ref.shape[0]
    i = pl.program_id(1)
    last = pl.num_programs(1) - 1
    H = DN_HALO
    ext_ref[0:H, :] = jnp.where(i == 0, 0.0, xp_ref[...])
    ext_ref[H:H + tm, :] = xc_ref[...]
    ext_ref[H + tm:, :] = jnp.where(i == last, 0.0, xn_ref[...])
    half = DN_CONV // 2
    acc = ext_ref[H - half:H - half + tm, :] * cw_ref[0:1, :]
    for j in range(1, DN_CONV):
        acc = acc + ext_ref[H - half + j:H - half + j + tm, :] * cw_ref[j:j + 1, :]
    y = acc * (1.0 / (1.0 + jnp.exp(-acc)))
    ind = _group_indicator(DN_HEADS * DN_DK, DN_DK)
    w = DN_HEADS * DN_DK
    q, k = y[:, :w], y[:, w:2 * w]
    qkv_ref[:, :w] = q * lax.rsqrt(_split_dot(q * q, ind) + EPS) * (DN_DK ** -0.5)
    qkv_ref[:, w:2 * w] = k * lax.rsqrt(_split_dot(k * k, ind) + EPS)
    qkv_ref[:, 2 * w:] = y[:, 2 * w:]
    s = sm_ref[...]
    lane = lax.broadcasted_iota(jnp.int32, s.shape, 1)
    beta = 1.0 / (1.0 + jnp.exp(-s))
    a = s + dt_ref[...]
    g = na_ref[...] * (jnp.maximum(a, 0.0) + jnp.log1p(jnp.exp(-jnp.abs(a))))
    gate_ref[...] = jnp.where((lane >= DN_BETA_LANE) & (lane < DN_A_LANE), beta,
                              jnp.where((lane >= DN_A_LANE) & (lane < DN_A_LANE + 2 * DN_HEADS), g, 0.0))


def dn_prep(dn_qkv, small, conv_w, a_log, dt_bias, batch, seq):
    tm = min(DN_PREP_TILE, seq)
    nt = seq // tm
    hb = tm // DN_HALO
    nh = seq // DN_HALO
    pad = lambda x: jnp.zeros((1, 128), jnp.float32).at[0, DN_A_LANE:DN_A_LANE + 2 * DN_HEADS].set(x.reshape(-1))
    row = lambda w: pl.BlockSpec((tm, w), lambda b, i: (b * nt + i, 0))
    return pl.pallas_call(
        _dn_prep_kernel,
        grid=(batch, nt),
        in_specs=[pl.BlockSpec((DN_HALO, DN_QKV), lambda b, i: (b * nh + jnp.maximum(i * hb - 1, 0), 0)),
                  row(DN_QKV),
                  pl.BlockSpec((DN_HALO, DN_QKV), lambda b, i: (b * nh + jnp.minimum((i + 1) * hb, nh - 1), 0)),
                  row(128),
                  pl.BlockSpec((DN_CONV, DN_QKV), lambda b, i: (0, 0)),
                  pl.BlockSpec((1, 128), lambda b, i: (0, 0)),
                  pl.BlockSpec((1, 128), lambda b, i: (0, 0))],
        out_specs=[row(DN_QKV), row(128)],
        out_shape=[jax.ShapeDtypeStruct((batch * seq, DN_QKV), jnp.float32),
                   jax.ShapeDtypeStruct((batch * seq, 128), jnp.float32)],
        scratch_shapes=[pltpu.VMEM((tm + 2 * DN_HALO, DN_QKV), jnp.float32)],
        compiler_params=pltpu.CompilerParams(dimension_semantics=("arbitrary", "arbitrary"),
                                             vmem_limit_bytes=VMEM_LIMIT_BYTES),
        name="dn_prep",
    )(dn_qkv, dn_qkv, dn_qkv, small, conv_w.astype(jnp.float32), pad(-jnp.exp(a_log.astype(jnp.float32))),
      pad(dt_bias.astype(jnp.float32)))


def _mm(a, b, dims=None):
    a, b = a.astype(jnp.bfloat16), b.astype(jnp.bfloat16)
    if dims is None:
        return jnp.dot(a, b, preferred_element_type=jnp.float32)
    return lax.dot_general(a, b, dims, preferred_element_type=jnp.float32)


def _head_blockdiag(x):
    c, hw = x.shape
    w = hw // DN_HEADS
    stacked = jnp.concatenate([x] * DN_HEADS, axis=0)
    r = lax.broadcasted_iota(jnp.int32, stacked.shape, 0) // c
    l = lax.broadcasted_iota(jnp.int32, stacked.shape, 1) // w
    return jnp.where(r == l, stacked, 0.0)


def _dn_chunk_prepare(qkv, gate, direction):
    C, W = DN_CHUNK, DN_HEADS * DN_DK
    reverse = direction == 1
    q, k, v = qkv[:, :W], qkv[:, W:2 * W], qkv[:, 2 * W:]
    lane = lax.broadcasted_iota(jnp.int32, (C, 128), 1)
    g_lo = DN_A_LANE + direction * DN_HEADS
    b_lo = DN_BETA_LANE + direction * DN_HEADS
    cs = _seg_cumsum(jnp.where((lane >= g_lo) & (lane < g_lo + DN_HEADS), gate, 0.0), C, reverse)
    beta = jnp.where((lane >= b_lo) & (lane < b_lo + DN_HEADS), gate, 0.0)
    src = lax.broadcasted_iota(jnp.int32, (128, W), 0)
    dst = lax.broadcasted_iota(jnp.int32, (128, W), 1) // DN_DK
    expand = ((src - g_lo == dst) | (src - b_lo == dst)).astype(jnp.bfloat16)
    both = _split_dot(jnp.concatenate([cs, beta], axis=0), expand)
    bcol, bexp = both[:C], both[C:]
    row = lax.broadcasted_iota(jnp.int32, (C, W), 0)
    col = lax.broadcasted_iota(jnp.int32, (C, W), 1) % C
    diag = jnp.where(row == col, bcol, 0.0)
    diag_hi = diag.astype(jnp.bfloat16)
    diag_lo = (diag - diag_hi.astype(jnp.float32)).astype(jnp.bfloat16)
    ones = jnp.ones((C, C), jnp.bfloat16)
    brow = (jnp.dot(ones, diag_hi, preferred_element_type=jnp.float32) +
            jnp.dot(ones, diag_lo, preferred_element_type=jnp.float32))
    incl = (col >= row) if reverse else (col <= row)
    strict = (col > row) if reverse else (col < row)
    lmat = jnp.exp(jnp.where(incl, bcol - brow, -jnp.inf))
    kb = k * bexp
    kstack = _head_blockdiag(k)
    m = jnp.where(strict, _mm(kb, kstack, _NT) * lmat, 0.0)
    aqk = jnp.where(incl, _mm(q, kstack, _NT) * lmat, 0.0)
    t = jnp.where(row == col, 1.0, 0.0)
    s = 1
    while s < C:
        same = (row // (2 * s)) == (col // (2 * s))
        r_hi, c_hi = (row % (2 * s)) >= s, (col % (2 * s)) >= s
        off = same & ((c_hi & ~r_hi) if reverse else (r_hi & ~c_hi))
        m_off = jnp.where(off, m, 0.0)
        if s == 1:
            t = t - m_off
        else:
            t = t - _mm(t, _head_blockdiag(_mm(m_off, _head_blockdiag(t))))
        s *= 2
    ecs = jnp.exp(bcol)
    u = _mm(t, _head_blockdiag(v * bexp))
    w = _mm(t, _head_blockdiag(kb * ecs))
    edge = bcol[0:1] if reverse else bcol[C - 1:C]
    return dict(u=u, w=w, aqk=aqk, q_dec=q * ecs, k_dec=k * jnp.exp(edge - bcol), dl=jnp.exp(edge))


def _dn_direction(qkv_ref, gate_ref, s_ref, o_ref, direction):
    C, W = DN_CHUNK, DN_HEADS * DN_DK
    n_chunks = qkv_ref.shape[0] // C
    reverse = direction == 1
    prepared = [_dn_chunk_prepare(qkv_ref[c * C:(c + 1) * C, :], gate_ref[c * C:(c + 1) * C, :], direction)
                for c in range(n_chunks)]
    bd = (lax.broadcasted_iota(jnp.int32, (W, W), 0) // DN_DK ==
          lax.broadcasted_iota(jnp.int32, (W, W), 1) // DN_DV)
    s = s_ref[...]
    for c in (range(n_chunks - 1, -1, -1) if reverse else range(n_chunks)):
        pc = prepared[c]
        v_new = pc["u"] - _mm(pc["w"], s)
        o_ref[c * C:(c + 1) * C, :] = _mm(pc["q_dec"], s) + _mm(pc["aqk"], _head_blockdiag(v_new))
        s = s * pc["dl"] + jnp.where(bd, _mm(pc["k_dec"], v_new, _TN), 0.0)
    s_ref[...] = s


def _dn_kernel(qkvf_ref, gf_ref, qkvb_ref, gb_ref, of_ref, ob_ref, sf_ref, sb_ref):
    @pl.when(pl.program_id(1) == 0)
    def _():
        sf_ref[...] = jnp.zeros_like(sf_ref)
        sb_ref[...] = jnp.zeros_like(sb_ref)

    _dn_direction(qkvf_ref, gf_ref, sf_ref, of_ref, 0)
    _dn_direction(qkvb_ref, gb_ref, sb_ref, ob_ref, 1)


def dn_scan(qkv, gates, batch, seq):
    tb = min(DN_TILE, seq)
    nt = seq // tb
    W = DN_HEADS * DN_DK
    fwd = lambda w: pl.BlockSpec((tb, w), lambda b, i: (b * nt + i, 0))
    bwd = lambda w: pl.BlockSpec((tb, w), lambda b, i: (b * nt + nt - 1 - i, 0))
    out = jax.ShapeDtypeStruct((batch * seq, DN_WIDTH), jnp.float32)
    return pl.pallas_call(
        _dn_kernel,
        grid=(batch, nt),
        in_specs=[fwd(DN_QKV), fwd(128), bwd(DN_QKV), bwd(128)],
        out_specs=[fwd(DN_WIDTH), bwd(DN_WIDTH)],
        out_shape=[out, out],
        scratch_shapes=[pltpu.VMEM((W, DN_WIDTH), jnp.float32), pltpu.VMEM((W, DN_WIDTH), jnp.float32)],
        compiler_params=pltpu.CompilerParams(dimension_semantics=("arbitrary", "arbitrary"),
                                             vmem_limit_bytes=VMEM_LIMIT_BYTES),
        name="dn_scan",
    )(qkv, gates, qkv, gates)


def _gated_group_norm(of_ref, ob_ref, gate_ref, gn_ref, ind):
    o = of_ref[...] + ob_ref[...]
    ms = _split_dot(o * o, ind) * (1.0 / HEAD_DIM)
    gate = gate_ref[...]
    return (o * lax.rsqrt(ms + EPS) * gn_ref[...] * (gate * (1.0 / (1.0 + jnp.exp(-gate))))).astype(jnp.bfloat16)


def _out_mlp_kernel(glaf_ref, glab_ref, glag_ref, swa_ref, dnf_ref, dnb_ref, dnz_ref, x_ref, wo_ref,
                    gng_ref, gnd_ref, g1_ref, g2_ref, w1_ref, w2_ref, g3_ref, o_ref):
    ind = _group_indicator(GLA_WIDTH, HEAD_DIM)
    y_gla = _gated_group_norm(glaf_ref, glab_ref, glag_ref, gng_ref, ind)
    y_dn = _gated_group_norm(dnf_ref, dnb_ref, dnz_ref, gnd_ref, ind)
    a, b = GLA_WIDTH, GLA_WIDTH + SWA_WIDTH
    m = (jnp.dot(y_gla, wo_ref[0:a, :], preferred_element_type=jnp.float32) +
         jnp.dot(swa_ref[...], wo_ref[a:b, :], preferred_element_type=jnp.float32) +
         jnp.dot(y_dn, wo_ref[b:, :], preferred_element_type=jnp.float32))
    x1 = x_ref[...] + _rms(m, g1_ref[...])
    h = _rms(x1, g2_ref[...]).astype(jnp.bfloat16)
    acc = jnp.zeros_like(x1)
    for c in range(D_FF // FF_CHUNK):
        f = jnp.dot(h, w1_ref[:, c * FF_CHUNK:(c + 1) * FF_CHUNK], preferred_element_type=jnp.float32)
        f = jnp.square(jnp.maximum(f, 0.0)).astype(jnp.bfloat16)
        acc = acc + jnp.dot(f, w2_ref[c * FF_CHUNK:(c + 1) * FF_CHUNK, :], preferred_element_type=jnp.float32)
    o_ref[...] = x1 + _rms(acc, g3_ref[...])


def out_mlp(gla_f, gla_b, gla_g, swa_o, dn_f, dn_b, dn_z, x2d, wo, gla_norm_g, dn_norm_g, g1, g2, w1, w2, g3):
    n, d = x2d.shape
    tm = min(ROW_TILE, n)
    row = lambda w: pl.BlockSpec((tm, w), lambda i: (i, 0))
    vec = _const_spec((1, d))
    hvec = _const_spec((1, GLA_WIDTH))
    tile_gain = lambda g, h: jnp.tile(g.astype(jnp.float32), h).reshape(1, -1)
    return pl.pallas_call(
        _out_mlp_kernel,
        grid=(n // tm,),
        in_specs=[row(GLA_WIDTH), row(GLA_WIDTH), row(GLA_WIDTH), row(SWA_WIDTH), row(DN_WIDTH), row(DN_WIDTH),
                  row(DN_WIDTH), row(d), _const_spec((d, d)), hvec, hvec, vec, vec,
                  _const_spec((d, D_FF)), _const_spec((D_FF, d)), vec],
        out_specs=row(d),
        out_shape=jax.ShapeDtypeStruct((n, d), jnp.float32),
        compiler_params=pltpu.CompilerParams(dimension_semantics=("arbitrary",),
                                             vmem_limit_bytes=VMEM_LIMIT_BYTES),
        name="out_mlp",
    )(gla_f, gla_b, gla_g, swa_o, dn_f, dn_b, dn_z, x2d, wo, tile_gain(gla_norm_g, GLA_HEADS),
      tile_gain(dn_norm_g, DN_HEADS), g1.reshape(1, d), g2.reshape(1, d), w1, w2, g3.reshape(1, d))


def _rmsnorm(x, g):
    xf = x.astype(jnp.float32)
    y = xf * lax.rsqrt(jnp.mean(xf * xf, axis=-1, keepdims=True) + EPS)
    return (y * g.astype(jnp.float32)).astype(x.dtype)


def _l2norm(x):
    return x * lax.rsqrt(jnp.sum(x * x, axis=-1, keepdims=True) + EPS)


def _split_heads(x, h):
    B, T, _ = x.shape
    return x.reshape(B, T, h, -1).transpose(0, 2, 1, 3)


def _merge_heads(x):
    B, H, T, d = x.shape
    return x.transpose(0, 2, 1, 3).reshape(B, T, H * d)


def _split_columns(proj):
    idx, acc = [], 0
    for s in IN_SIZES[:-1]:
        acc += s
        idx.append(acc)
    return jnp.split(proj, idx, axis=-1)


def _gla_chunk_scan(q, k, v, gk):
    B, H, T, dk = q.shape
    dv = v.shape[-1]
    C = GLA_CHUNK
    N = T // C
    q, k, v, gk = [a.reshape(B, H, N, C, a.shape[-1]) for a in (q, k, v, gk)]
    b = jnp.cumsum(gk, axis=3)
    causal = jnp.tril(jnp.ones((C, C), dtype=bool))
    diff = jnp.where(causal[:, :, None], b[..., :, None, :] - b[..., None, :, :], -jnp.inf)
    A = jnp.einsum('bhnid,bhnjd,bhnijd->bhnij', q, k, jnp.exp(diff))
    o_intra = jnp.einsum('bhnij,bhnjv->bhniv', A, v)
    b_last = b[..., -1, :]
    k_dec = k * jnp.exp(b_last[..., None, :] - b)
    upd = jnp.einsum('bhncd,bhncv->nbhdv', k_dec, v)
    dec = jnp.moveaxis(jnp.exp(b_last), 2, 0)

    def step(S, inp):
        d, u = inp
        return d[..., None] * S + u, S

    _, S_prev = lax.scan(step, jnp.zeros((B, H, dk, dv), q.dtype), (dec, upd))
    o_inter = jnp.einsum('bhncd,nbhdv->bhncv', q * jnp.exp(b), S_prev)
    return (o_intra + o_inter).reshape(B, H, T, dv)


def _gla_mixer(q, k, v, g, lr, w_lr2, b_lr, norm_g):
    B, T, _ = q.shape
    lr = lr.reshape(B, T, 2, GLA_LOWRANK)
    logits = jnp.einsum('btzr,zrk->zbtk', lr, w_lr2) + b_lr[:, None, None, :]
    gk = jax.nn.log_sigmoid(logits) / GLA_GATE_NORMALIZER
    gk = gk.reshape(2, B, T, GLA_HEADS, GLA_DK).transpose(0, 1, 3, 2, 4)
    qh = _split_heads(q, GLA_HEADS) * (GLA_DK ** -0.5)
    kh = _split_heads(k, GLA_HEADS)
    vh = _split_heads(v, GLA_HEADS)
    o_f = _gla_chunk_scan(qh, kh, vh, gk[0])
    fl = lambda a: jnp.flip(a, axis=2)
    o_b = fl(_gla_chunk_scan(fl(qh), fl(kh), fl(vh), fl(gk[1])))
    o = _rmsnorm(o_f + o_b, norm_g)
    return _merge_heads(o) * jax.nn.silu(g)


def _short_conv(x, w):
    K = w.shape[0]
    p = K // 2
    T = x.shape[1]
    xp = jnp.pad(x, ((0, 0), (p, p), (0, 0)))
    acc = xp[:, 0:T] * w[0]
    for j in range(1, K):
        acc = acc + xp[:, j:j + T] * w[j]
    return acc


def _gated_delta_chunk(q, k, v, g, beta):
    B, H, T, dk = q.shape
    dv = v.shape[-1]
    C = DN_CHUNK
    N = T // C
    q, k, v = [a.reshape(B, H, N, C, a.shape[-1]) for a in (q, k, v)]
    g = g.reshape(B, H, N, C)
    beta = beta.reshape(B, H, N, C)
    b = jnp.cumsum(g, axis=-1)
    tri = jnp.tril(jnp.ones((C, C), dtype=bool))
    strict = jnp.tril(jnp.ones((C, C), dtype=bool), -1)
    L = jnp.exp(jnp.where(tri, b[..., :, None] - b[..., None, :], -jnp.inf))
    kb = k * beta[..., None]
    M = jnp.where(strict, jnp.einsum('bhnid,bhnjd->bhnij', kb, k) * L, 0.0)
    Amat = M + jnp.eye(C, dtype=M.dtype)
    rhs = jnp.concatenate([v * beta[..., None], kb * jnp.exp(b)[..., None]], axis=-1)
    sol = lax.linalg.triangular_solve(Amat, rhs, left_side=True, lower=True, unit_diagonal=True)
    u, w = sol[..., :dv], sol[..., dv:]
    Aqk = jnp.where(tri, jnp.einsum('bhnid,bhnjd->bhnij', q, k) * L, 0.0)
    q_dec = q * jnp.exp(b)[..., None]
    k_dec = k * jnp.exp(b[..., -1:] - b)[..., None]
    dec_last = jnp.exp(b[..., -1])
    xs = tuple(jnp.moveaxis(a, 2, 0) for a in (u, w, Aqk, q_dec, k_dec, dec_last))

    def step(S, inp):
        uc, wc, Ac, qdc, kdc, dl = inp
        v_new = uc - jnp.einsum('bhcd,bhdv->bhcv', wc, S)
        o = jnp.einsum('bhcd,bhdv->bhcv', qdc, S) + jnp.einsum('bhij,bhjv->bhiv', Ac, v_new)
        S = dl[..., None, None] * S + jnp.einsum('bhcd,bhcv->bhdv', kdc, v_new)
        return S, o

    _, o = lax.scan(step, jnp.zeros((B, H, dk, dv), q.dtype), xs)
    return jnp.moveaxis(o, 0, 2).reshape(B, H, T, dv)


def _deltanet_mixer(qkv, z, beta_raw, a_raw, conv_w, a_log, dt_bias, norm_g):
    B, T, _ = qkv.shape
    qkv = jax.nn.silu(_short_conv(qkv, conv_w))
    q, k, v = jnp.split(qkv, [DN_HEADS * DN_DK, 2 * DN_HEADS * DN_DK], axis=-1)
    qh = _l2norm(_split_heads(q, DN_HEADS)) * (DN_DK ** -0.5)
    kh = _l2norm(_split_heads(k, DN_HEADS))
    vh = _split_heads(v, DN_HEADS)
    beta = jax.nn.sigmoid(beta_raw.reshape(B, T, 2, DN_HEADS)).transpose(2, 0, 3, 1)
    a = a_raw.reshape(B, T, 2, DN_HEADS).transpose(2, 0, 1, 3)
    g = -jnp.exp(a_log)[:, None, None, :] * jax.nn.softplus(a + dt_bias[:, None, None, :])
    g = g.transpose(0, 1, 3, 2)
    o_f = _gated_delta_chunk(qh, kh, vh, g[0], beta[0])
    fl = lambda x: jnp.flip(x, axis=2)
    o_b = fl(_gated_delta_chunk(fl(qh), fl(kh), fl(vh), fl(g[1]), fl(beta[1])))
    o = _rmsnorm(o_f + o_b, norm_g)
    return _merge_heads(o) * jax.nn.silu(z)


def _trunk(x, p):
    B, T, D = x.shape
    bias = p["band_bias"]
    x2 = x.reshape(B * T, D)
    for l in range(DEPTH):
        swa_q, swa_kv, gla_qk, gla_v, gla_g, dn_qkv, dn_z, small = in_proj(x2, p["mix_pre_g"][l], p["w_in_packed"][l])
        gla_f, gla_b = gla_scan(gla_qk, gla_v, small, p["gla_w_lr2"][l], p["gla_b_lr"][l], B, T)
        o_swa = swa_attention(swa_q, swa_kv, p["swa_sink"][l], bias, B, T)
        dn_qkvn, dn_gates = dn_prep(dn_qkv, small, p["dn_conv_w"][l], p["dn_a_log"][l], p["dn_dt_bias"][l], B, T)
        dn_f, dn_b = dn_scan(dn_qkvn, dn_gates, B, T)
        x2 = out_mlp(gla_f, gla_b, gla_g, o_swa, dn_f, dn_b, dn_z, x2, p["w_out_bf16"][l], p["gla_norm_g"][l],
                     p["dn_norm_g"][l], p["mix_post_g"][l], p["mlp_pre_g"][l],
                     p["mlp_w1_bf16"][l], p["mlp_w2_bf16"][l], p["mlp_post_g"][l])
    return x2.reshape(B, T, D)


def _prepare_params(rel_bias_table, mix_pre_g, w_in, gla_w_lr2, gla_b_lr, gla_norm_g, swa_sink, dn_conv_w, dn_a_log,
                    dn_dt_bias, dn_norm_g, w_out, mix_post_g, mlp_pre_g, mlp_w1, mlp_w2, mlp_post_g):
    return dict(mix_pre_g=mix_pre_g, gla_w_lr2=gla_w_lr2, gla_b_lr=gla_b_lr,
                gla_norm_g=gla_norm_g, swa_sink=swa_sink, dn_conv_w=dn_conv_w, dn_a_log=dn_a_log,
                dn_dt_bias=dn_dt_bias, dn_norm_g=dn_norm_g, mix_post_g=mix_post_g, mlp_pre_g=mlp_pre_g,
                mlp_post_g=mlp_post_g,
                band_bias=band_bias(rel_bias_table),
                w_in_packed=jax.vmap(pack_w_in)(w_in), w_out_bf16=w_out.astype(jnp.bfloat16),
                mlp_w1_bf16=mlp_w1.astype(jnp.bfloat16), mlp_w2_bf16=mlp_w2.astype(jnp.bfloat16))


def kernel(x_prompt, x_sample, rel_bias_table, mix_pre_g, w_in, gla_w_lr2, gla_b_lr, gla_norm_g, swa_sink, dn_conv_w, dn_a_log, dn_dt_bias, dn_norm_g, w_out, mix_post_g, mlp_pre_g, mlp_w1, mlp_w2, mlp_post_g):
    p = _prepare_params(rel_bias_table, mix_pre_g, w_in, gla_w_lr2, gla_b_lr, gla_norm_g, swa_sink, dn_conv_w,
                        dn_a_log, dn_dt_bias, dn_norm_g, w_out, mix_post_g, mlp_pre_g, mlp_w1, mlp_w2, mlp_post_g)
    return (_trunk(x_prompt, p), _trunk(x_sample, p))
```

```python
import functools
import math

import numpy as np

import jax
import jax.numpy as jnp
from jax import lax
from jax.experimental import pallas as pl
from jax.experimental.pallas import tpu as pltpu

D_MODEL = 1024
DEPTH = 4
HEAD_DIM = 64
EPS = 1e-6
GLA_HEADS = 4
GLA_DK = 32
GLA_DV = 64
GLA_LOWRANK = 16
GLA_GATE_NORMALIZER = 16.0
GLA_CHUNK = 32
SWA_HEADS = 8
SWA_KV_HEADS = 2
SWA_WINDOW = 128
SWA_BLOCK = 128
NUM_BUCKETS = 32
MAX_DISTANCE = 128
DN_HEADS = 4
DN_DK = 64
DN_DV = 64
DN_CONV = 5
DN_CHUNK = 64
GLA_WIDTH = GLA_HEADS * GLA_DV
SWA_WIDTH = SWA_HEADS * HEAD_DIM
DN_WIDTH = DN_HEADS * DN_DV
D_MIX = GLA_WIDTH + SWA_WIDTH + DN_WIDTH
D_FF = 4 * D_MODEL
DN_QKV = DN_HEADS * (2 * DN_DK + DN_DV)
IN_SIZES = (GLA_HEADS * GLA_DK, GLA_HEADS * GLA_DK, GLA_WIDTH, GLA_WIDTH, 2 * GLA_LOWRANK,
            SWA_WIDTH, SWA_KV_HEADS * HEAD_DIM, SWA_KV_HEADS * HEAD_DIM,
            DN_QKV, DN_WIDTH, 2 * DN_HEADS, 2 * DN_HEADS)
D_IN = sum(IN_SIZES)

VMEM_LIMIT_BYTES = 56 * 1024 * 1024
ROW_TILE = 512
FF_CHUNK = 1024


def _rms(xf, g):
    return xf * lax.rsqrt(jnp.mean(xf * xf, axis=-1, keepdims=True) + EPS) * g


def _const_spec(shape):
    return pl.BlockSpec(shape, lambda i: (0,) * len(shape), pipeline_mode=pl.Buffered(1))


_O = {}
_acc = 0
for _name, _width in (("swa_q", SWA_WIDTH), ("swa_kv", 4 * 2 * HEAD_DIM), ("gla_qk", 2 * GLA_HEADS * GLA_DK),
                      ("gla_v", GLA_WIDTH), ("gla_g", GLA_WIDTH), ("dn_qkv", DN_QKV), ("dn_z", DN_WIDTH),
                      ("small", 128)):
    _O[_name] = (_acc, _width)
    _acc += _width
D_IN_PACKED = _acc
_OUT_DTYPES = {"swa_q": jnp.bfloat16, "swa_kv": jnp.bfloat16}


def pack_w_in(w_in):
    offs, acc = [], 0
    for s in IN_SIZES:
        offs.append(acc)
        acc += s
    gq, gk, gv, gg, glr, sq, sk, sv, dqkv, dz, dbeta, da = [w_in[:, o:o + s] for o, s in zip(offs, IN_SIZES)]
    hd = HEAD_DIM
    kv = [sk[:, 0:hd], sk[:, 0:hd], sk[:, hd:], sk[:, hd:], sv[:, 0:hd], sv[:, 0:hd], sv[:, hd:], sv[:, hd:]]
    small = jnp.concatenate([glr, dbeta, da], axis=1)
    small = jnp.pad(small, ((0, 0), (0, 128 - small.shape[1])))
    return jnp.concatenate([sq] + kv + [gq, gk, gv, gg, dqkv, dz, small], axis=1).astype(jnp.bfloat16)


def _in_proj_kernel(x_ref, g_ref, w_ref, *o_refs):
    h = _rms(x_ref[...], g_ref[...]).astype(jnp.bfloat16)
    y = jnp.dot(h, w_ref[...], preferred_element_type=jnp.float32)
    for (off, width), o_ref in zip(_O.values(), o_refs):
        o_ref[...] = y[:, off:off + width].astype(o_ref.dtype)


def in_proj(x2d, g, w_packed):
    n, d = x2d.shape
    tm = min(ROW_TILE, n)
    return pl.pallas_call(
        _in_proj_kernel,
        grid=(n // tm,),
        in_specs=[pl.BlockSpec((tm, d), lambda i: (i, 0)),
                  _const_spec((1, d)),
                  _const_spec((d, D_IN_PACKED))],
        out_specs=[pl.BlockSpec((tm, w), lambda i: (i, 0)) for _, w in _O.values()],
        out_shape=[jax.ShapeDtypeStruct((n, w), _OUT_DTYPES.get(name, jnp.float32))
                   for name, (_, w) in _O.items()],
        compiler_params=pltpu.CompilerParams(dimension_semantics=("arbitrary",),
                                             vmem_limit_bytes=VMEM_LIMIT_BYTES),
        name="in_proj",
    )(x2d, g.reshape(1, d), w_packed)


SWA_Q_TILE = 512
SWA_GROUP = SWA_HEADS // SWA_KV_HEADS


def _t5_bucket(rel):
    nb = NUM_BUCKETS // 2
    max_exact = nb // 2
    base = jnp.where(rel > 0, nb, 0)
    n = jnp.abs(rel)
    nf = jnp.maximum(n, 1).astype(jnp.float32)
    large = max_exact + (jnp.log(nf / max_exact) / math.log(MAX_DISTANCE / max_exact)
                         * (nb - max_exact)).astype(jnp.int32)
    large = jnp.minimum(large, nb - 1)
    return base + jnp.where(n < max_exact, n, large)


def _band_bias_kernel(tab_ref, idx_ref, o_ref):
    W = SWA_BLOCK
    idx = idx_ref[...]
    i = lax.broadcasted_iota(jnp.int32, idx.shape, 0)
    j = lax.broadcasted_iota(jnp.int32, idx.shape, 1)
    in_band = jnp.abs(j - W - i) <= SWA_WINDOW
    for h in range(SWA_HEADS):
        acc = jnp.zeros(idx.shape, jnp.float32)
        for b in range(NUM_BUCKETS):
            acc = jnp.where(idx == b, tab_ref[b, h], acc)
        o_ref[h] = jnp.where(in_band, acc, -jnp.inf)


def band_bias(table):
    W = SWA_BLOCK
    rel = jnp.arange(3 * W)[None, :] - W - jnp.arange(W)[:, None]
    idx = _t5_bucket(rel).astype(jnp.int32)
    return pl.pallas_call(
        _band_bias_kernel,
        in_specs=[pl.BlockSpec(memory_space=pltpu.SMEM), pl.BlockSpec(memory_space=pltpu.VMEM)],
        out_specs=pl.BlockSpec(memory_space=pltpu.VMEM),
        out_shape=jax.ShapeDtypeStruct((SWA_HEADS, W, 3 * W), jnp.float32),
        name="band_bias",
    )(table.astype(jnp.float32), idx)


def _swa_kernel(sink_ref, q_ref, kvp_ref, kvc_ref, kvn_ref, bias_ref, o_ref, kv_buf):
    W = SWA_BLOCK
    n_sub = q_ref.shape[0] // W
    n = pl.program_id(1)
    last = pl.num_programs(1) - 1
    kv_buf[0:W, :] = kvp_ref[...]
    kv_buf[W:W + n_sub * W, :] = kvc_ref[...]
    kv_buf[W + n_sub * W:, :] = kvn_ref[...]
    lane = lax.broadcasted_iota(jnp.int32, (W, 2 * HEAD_DIM), 1)
    lo = lane < HEAD_DIM
    key_blk = lax.broadcasted_iota(jnp.int32, (1, 3 * W), 1) // W
    neg = jnp.float32(-jnp.inf)
    for s in range(n_sub):
        edge = jnp.zeros((1, 3 * W), jnp.float32)
        if s == 0:
            edge = jnp.where((key_blk == 0) & (n == 0), neg, edge)
        if s == n_sub - 1:
            edge = jnp.where((key_blk == 2) & (n == last), neg, edge)
        rows = slice(s * W, (s + 1) * W)
        for g in range(SWA_KV_HEADS):
            kd = kv_buf[s * W:(s + 3) * W, g * 128:(g + 1) * 128]
            vd = kv_buf[s * W:(s + 3) * W, 256 + g * 128:256 + (g + 1) * 128]
            qs = []
            for pair in range(SWA_GROUP // 2):
                col = (g * SWA_GROUP // 2 + pair) * 128
                qp = q_ref[rows, col:col + 128]
                qs += [jnp.where(lo, qp, 0), jnp.where(lo, 0, qp)]
            q_stack = jnp.concatenate(qs, axis=0)
            s_all = lax.dot_general(q_stack, kd, (((1,), (1,)), ((), ())),
                                    preferred_element_type=jnp.float32)
            ps, rinv = [], []
            for hh in range(SWA_GROUP):
                h = g * SWA_GROUP + hh
                sc = s_all[hh * W:(hh + 1) * W] * (HEAD_DIM ** -0.5) + bias_ref[h]
                if s == 0 or s == n_sub - 1:
                    sc = sc + edge
                sink = sink_ref[h]
                m = jnp.maximum(jnp.max(sc, axis=-1, keepdims=True), sink)
                p = jnp.exp(sc - m)
                denom = jnp.sum(p, axis=-1, keepdims=True) + jnp.exp(sink - m)
                ps.append(p.astype(jnp.bfloat16))
                rinv.append(1.0 / denom)
            o_all = jnp.dot(jnp.concatenate(ps, axis=0), vd, preferred_element_type=jnp.float32)
            for pair in range(SWA_GROUP // 2):
                e, o = 2 * pair, 2 * pair + 1
                col = (g * SWA_GROUP // 2 + pair) * 128
                o_ref[rows, col:col + 128] = jnp.where(lo, o_all[e * W:(e + 1) * W] * rinv[e],
                                                       o_all[o * W:(o + 1) * W] * rinv[o]).astype(o_ref.dtype)


def swa_attention(q, kv, sink, bias, batch, seq):
    W = SWA_BLOCK
    tq = min(SWA_Q_TILE, seq)
    n_sub = tq // W
    nq = seq // tq
    nb = seq // W
    return pl.pallas_call(
        _swa_kernel,
        grid=(batch, nq),
        in_specs=[pl.BlockSpec(memory_space=pltpu.SMEM),
                  pl.BlockSpec((tq, SWA_WIDTH), lambda b, i: (b * nq + i, 0)),
                  pl.BlockSpec((W, 512), lambda b, i: (b * nb + jnp.maximum(i * n_sub - 1, 0), 0)),
                  pl.BlockSpec((tq, 512), lambda b, i: (b * nq + i, 0)),
                  pl.BlockSpec((W, 512), lambda b, i: (b * nb + jnp.minimum((i + 1) * n_sub, nb - 1), 0)),
                  pl.BlockSpec((SWA_HEADS, W, 3 * W), lambda b, i: (0, 0, 0), pipeline_mode=pl.Buffered(1))],
        out_specs=pl.BlockSpec((tq, SWA_WIDTH), lambda b, i: (b * nq + i, 0)),
        out_shape=jax.ShapeDtypeStruct((batch * seq, SWA_WIDTH), jnp.bfloat16),
        scratch_shapes=[pltpu.VMEM((tq + 2 * W, 512), jnp.bfloat16)],
        compiler_params=pltpu.CompilerParams(dimension_semantics=("arbitrary", "arbitrary"),
                                             vmem_limit_bytes=VMEM_LIMIT_BYTES),
        name="swa",
    )(sink.astype(jnp.float32), q, kv, kv, kv, bias)


GLA_TILE = 256
GLA_SUB = 16
_NT = (((1,), (1,)), ((), ()))
_TN = (((0,), (0,)), ((), ()))


def _seg_cumsum(x, seg, reverse):
    n = x.shape[0]
    rowmod = lax.broadcasted_iota(jnp.int32, x.shape, 0) % seg
    sh = 1
    while sh < seg:
        if reverse:
            x = x + jnp.where(rowmod < seg - sh, pltpu.roll(x, n - sh, 0), 0.0)
        else:
            x = x + jnp.where(rowmod >= sh, pltpu.roll(x, sh, 0), 0.0)
        sh *= 2
    return x


def _log_sigmoid(x):
    return jnp.minimum(x, 0.0) - jnp.log1p(jnp.exp(-jnp.abs(x)))


def _gla_constants(tb):
    S, hk = GLA_SUB, GLA_HEADS * GLA_DK
    r = np.arange(tb)
    same = (r[:, None] // S) == (r[None, :] // S)
    tri = np.stack([same & (r[None, :] <= r[:, None]), same & (r[None, :] >= r[:, None])])
    src = np.arange(S * hk)
    d, h = src // hk, (src % hk) // GLA_DK
    lane = np.arange(128)
    place = np.stack([lane[None, :] == (32 * h + S - d)[:, None], lane[None, :] == (32 * h + d)[:, None]])
    return jnp.asarray(tri, jnp.bfloat16), jnp.asarray(place, jnp.bfloat16)


def _gla_direction(qk_ref, v_ref, sm_ref, wg_ref, bg_ref, tri_ref, place_ref, st_ref, o_ref, reverse):
    tb, S = qk_ref.shape[0], GLA_SUB
    hk = GLA_HEADS * GLA_DK
    logits = jnp.dot(sm_ref[...].astype(jnp.bfloat16), wg_ref[...], preferred_element_type=jnp.float32)
    g = _log_sigmoid(logits + bg_ref[...]) * (1.0 / GLA_GATE_NORMALIZER)
    g_hi = g.astype(jnp.bfloat16)
    g_lo = (g - g_hi.astype(jnp.float32)).astype(jnp.bfloat16)
    cs = (jnp.dot(tri_ref[...], g_hi, preferred_element_type=jnp.float32) +
          jnp.dot(tri_ref[...], g_lo, preferred_element_type=jnp.float32))
    yield
    a = jnp.exp(g)
    q = qk_ref[:, :hk] * (GLA_DK ** -0.5)
    k = qk_ref[:, hk:]
    v = v_ref[...]
    rowmod = lax.broadcasted_iota(jnp.int32, (tb, hk), 0) % S
    has_prev = (rowmod < S - 1) if reverse else (rowmod >= 1)
    kd = k
    ps = [(q * kd).astype(jnp.bfloat16)]
    for d in range(1, S):
        kd = a * jnp.where(has_prev, pltpu.roll(kd, tb - 1 if reverse else 1, 0), 0.0)
        ps.append((q * kd).astype(jnp.bfloat16))
    scores = jnp.dot(jnp.concatenate(ps, axis=1), place_ref[...], preferred_element_type=jnp.float32)
    yield
    head_of_lane = lax.broadcasted_iota(jnp.int32, (S, GLA_WIDTH), 1) // GLA_DV
    zeros = jnp.zeros((S, GLA_WIDTH), jnp.float32)
    acc = {}
    for s in range(tb // S):
        rows = slice(s * S, (s + 1) * S)
        band = pltpu.roll(scores[rows], 0, 1, stride=1, stride_axis=0).astype(jnp.bfloat16)
        v_blk = v[rows]
        pieces = []
        for h in range(GLA_HEADS):
            vh = jnp.where(head_of_lane == h, v_blk, 0.0)
            pieces += [vh, zeros] if reverse else [zeros, vh]
        acc[s] = jnp.dot(band, jnp.concatenate(pieces, axis=0).astype(jnp.bfloat16),
                         preferred_element_type=jnp.float32)
        if s % 2 == 1:
            yield
    qd = (q * jnp.exp(cs)).astype(jnp.bfloat16)
    blockdiag = (lax.broadcasted_iota(jnp.int32, (GLA_WIDTH, hk), 0) // GLA_DV ==
                 lax.broadcasted_iota(jnp.int32, (GLA_WIDTH, hk), 1) // GLA_DK)
    n_blk = tb // S
    order = range(n_blk - 1, -1, -1) if reverse else range(n_blk)
    dec, upd = {}, {}
    for s in order:
        rows = slice(s * S, (s + 1) * S)
        cs_s = cs[rows]
        edge = cs_s[0:1] if reverse else cs_s[S - 1:S]
        kt = (k[rows] * jnp.exp(edge - cs_s)).astype(jnp.bfloat16)
        dec[s] = jnp.exp(edge)
        upd[s] = jnp.where(blockdiag, lax.dot_general(v[rows].astype(jnp.bfloat16), kt, _TN,
                                                      preferred_element_type=jnp.float32), 0.0)
        yield
    st = st_ref[...]
    for s in order:
        rows = slice(s * S, (s + 1) * S)
        o_ref[rows, :] = acc[s] + lax.dot_general(qd[rows], st.astype(jnp.bfloat16), _NT,
                                                  preferred_element_type=jnp.float32)
        st = st * dec[s] + upd[s]
        yield
    st_ref[...] = st


def _gla_kernel(qkf_ref, vf_ref, smf_ref, qkb_ref, vb_ref, smb_ref, wg_ref, bg_ref, tri_ref, place_ref,
                of_ref, ob_ref, stf_ref, stb_ref):
    @pl.when(pl.program_id(1) == 0)
    def _():
        stf_ref[...] = jnp.zeros_like(stf_ref)
        stb_ref[...] = jnp.zeros_like(stb_ref)

    _lockstep([_gla_direction(qkf_ref, vf_ref, smf_ref, wg_ref.at[0], bg_ref.at[0], tri_ref.at[0], place_ref.at[0],
                              stf_ref, of_ref, False),
               _gla_direction(qkb_ref, vb_ref, smb_ref, wg_ref.at[1], bg_ref.at[1], tri_ref.at[1], place_ref.at[1],
                              stb_ref, ob_ref, True)])


def gla_scan(gla_qk, gla_v, small, w_lr2, b_lr, batch, seq):
    hk = GLA_HEADS * GLA_DK
    tb = min(GLA_TILE, seq)
    nt = seq // tb
    wg = jnp.zeros((2, 128, hk), jnp.float32)
    for z in range(2):
        wg = wg.at[z, z * GLA_LOWRANK:(z + 1) * GLA_LOWRANK].set(w_lr2[z])
    tri, place = _gla_constants(tb)
    fwd = lambda w: pl.BlockSpec((tb, w), lambda b, i: (b * nt + i, 0))
    bwd = lambda w: pl.BlockSpec((tb, w), lambda b, i: (b * nt + nt - 1 - i, 0))
    const = lambda a: pl.BlockSpec(a.shape, lambda b, i: (0,) * a.ndim)
    out = jax.ShapeDtypeStruct((batch * seq, GLA_WIDTH), jnp.float32)
    return pl.pallas_call(
        _gla_kernel,
        grid=(batch, nt),
        in_specs=[fwd(2 * hk), fwd(GLA_WIDTH), fwd(128), bwd(2 * hk), bwd(GLA_WIDTH), bwd(128),
                  pl.BlockSpec((2, 128, hk), lambda b, i: (0, 0, 0)),
                  pl.BlockSpec((2, 1, hk), lambda b, i: (0, 0, 0)), const(tri), const(place)],
        out_specs=[fwd(GLA_WIDTH), bwd(GLA_WIDTH)],
        out_shape=[out, out],
        scratch_shapes=[pltpu.VMEM((GLA_WIDTH, hk), jnp.float32), pltpu.VMEM((GLA_WIDTH, hk), jnp.float32)],
        compiler_params=pltpu.CompilerParams(dimension_semantics=("arbitrary", "arbitrary"),
                                             vmem_limit_bytes=VMEM_LIMIT_BYTES),
        name="gla",
    )(gla_qk, gla_v, small, gla_qk, gla_v, small, wg.astype(jnp.bfloat16), b_lr.reshape(2, 1, hk).astype(jnp.float32),
      tri, place)


DN_PREP_TILE = 512
DN_HALO = 8
DN_TILE = 256
DN_BETA_LANE = 2 * GLA_LOWRANK
DN_A_LANE = DN_BETA_LANE + 2 * DN_HEADS


def _split_dot(x, w_bf16):
    hi = x.astype(jnp.bfloat16)
    lo = (x - hi.astype(jnp.float32)).astype(jnp.bfloat16)
    return (jnp.dot(hi, w_bf16, preferred_element_type=jnp.float32) +
            jnp.dot(lo, w_bf16, preferred_element_type=jnp.float32))


def _group_indicator(n, group):
    r = lax.broadcasted_iota(jnp.int32, (n, n), 0) // group
    c = lax.broadcasted_iota(jnp.int32, (n, n), 1) // group
    return (r == c).astype(jnp.bfloat16)


def _dn_prep_kernel(xp_ref, xc_ref, xn_ref, sm_ref, cw_ref, na_ref, dt_ref, qkv_ref, gate_ref, ext_ref):
    tm = xc_ref.shape[0]
    i = pl.program_id(1)
    last = pl.num_programs(1) - 1
    H = DN_HALO
    ext_ref[0:H, :] = jnp.where(i == 0, 0.0, xp_ref[...])
    ext_ref[H:H + tm, :] = xc_ref[...]
    ext_ref[H + tm:, :] = jnp.where(i == last, 0.0, xn_ref[...])
    half = DN_CONV // 2
    acc = ext_ref[H - half:H - half + tm, :] * cw_ref[0:1, :]
    for j in range(1, DN_CONV):
        acc = acc + ext_ref[H - half + j:H - half + j + tm, :] * cw_ref[j:j + 1, :]
    y = acc * (1.0 / (1.0 + jnp.exp(-acc)))
    ind = _group_indicator(DN_HEADS * DN_DK, DN_DK)
    w = DN_HEADS * DN_DK
    q, k = y[:, :w], y[:, w:2 * w]
    qkv_ref[:, :w] = q * lax.rsqrt(_split_dot(q * q, ind) + EPS) * (DN_DK ** -0.5)
    qkv_ref[:, w:2 * w] = k * lax.rsqrt(_split_dot(k * k, ind) + EPS)
    qkv_ref[:, 2 * w:] = y[:, 2 * w:]
    s = sm_ref[...]
    lane = lax.broadcasted_iota(jnp.int32, s.shape, 1)
    beta = 1.0 / (1.0 + jnp.exp(-s))
    a = s + dt_ref[...]
    g = na_ref[...] * (jnp.maximum(a, 0.0) + jnp.log1p(jnp.exp(-jnp.abs(a))))
    gate_ref[...] = jnp.where((lane >= DN_BETA_LANE) & (lane < DN_A_LANE), beta,
                              jnp.where((lane >= DN_A_LANE) & (lane < DN_A_LANE + 2 * DN_HEADS), g, 0.0))


def dn_prep(dn_qkv, small, conv_w, a_log, dt_bias, batch, seq):
    tm = min(DN_PREP_TILE, seq)
    nt = seq // tm
    hb = tm // DN_HALO
    nh = seq // DN_HALO
    pad = lambda x: jnp.zeros((1, 128), jnp.float32).at[0, DN_A_LANE:DN_A_LANE + 2 * DN_HEADS].set(x.reshape(-1))
    row = lambda w: pl.BlockSpec((tm, w), lambda b, i: (b * nt + i, 0))
    return pl.pallas_call(
        _dn_prep_kernel,
        grid=(batch, nt),
        in_specs=[pl.BlockSpec((DN_HALO, DN_QKV), lambda b, i: (b * nh + jnp.maximum(i * hb - 1, 0), 0)),
                  row(DN_QKV),
                  pl.BlockSpec((DN_HALO, DN_QKV), lambda b, i: (b * nh + jnp.minimum((i + 1) * hb, nh - 1), 0)),
                  row(128),
                  pl.BlockSpec((DN_CONV, DN_QKV), lambda b, i: (0, 0)),
                  pl.BlockSpec((1, 128), lambda b, i: (0, 0)),
                  pl.BlockSpec((1, 128), lambda b, i: (0, 0))],
        out_specs=[row(DN_QKV), row(128)],
        out_shape=[jax.ShapeDtypeStruct((batch * seq, DN_QKV), jnp.float32),
                   jax.ShapeDtypeStruct((batch * seq, 128), jnp.float32)],
        scratch_shapes=[pltpu.VMEM((tm + 2 * DN_HALO, DN_QKV), jnp.float32)],
        compiler_params=pltpu.CompilerParams(dimension_semantics=("arbitrary", "arbitrary"),
                                             vmem_limit_bytes=VMEM_LIMIT_BYTES),
        name="dn_prep",
    )(dn_qkv, dn_qkv, dn_qkv, small, conv_w.astype(jnp.float32), pad(-jnp.exp(a_log.astype(jnp.float32))),
      pad(dt_bias.astype(jnp.float32)))


def _mm(a, b, dims=None):
    a, b = a.astype(jnp.bfloat16), b.astype(jnp.bfloat16)
    if dims is None:
        return jnp.dot(a, b, preferred_element_type=jnp.float32)
    return lax.dot_general(a, b, dims, preferred_element_type=jnp.float32)


def _head_blockdiag(x):
    c, hw = x.shape
    w = hw // DN_HEADS
    stacked = jnp.concatenate([x] * DN_HEADS, axis=0)
    r = lax.broadcasted_iota(jnp.int32, stacked.shape, 0) // c
    l = lax.broadcasted_iota(jnp.int32, stacked.shape, 1) // w
    return jnp.where(r == l, stacked, 0.0)


def _dn_chunk_prepare(qkv, gate, direction):
    C, W = DN_CHUNK, DN_HEADS * DN_DK
    reverse = direction == 1
    q, k, v = qkv[:, :W], qkv[:, W:2 * W], qkv[:, 2 * W:]
    lane = lax.broadcasted_iota(jnp.int32, (C, 128), 1)
    g_lo = DN_A_LANE + direction * DN_HEADS
    b_lo = DN_BETA_LANE + direction * DN_HEADS
    cs = _seg_cumsum(jnp.where((lane >= g_lo) & (lane < g_lo + DN_HEADS), gate, 0.0), C, reverse)
    beta = jnp.where((lane >= b_lo) & (lane < b_lo + DN_HEADS), gate, 0.0)
    src = lax.broadcasted_iota(jnp.int32, (128, W), 0)
    dst = lax.broadcasted_iota(jnp.int32, (128, W), 1) // DN_DK
    expand = ((src - g_lo == dst) | (src - b_lo == dst)).astype(jnp.bfloat16)
    both = _split_dot(jnp.concatenate([cs, beta], axis=0), expand)
    bcol, bexp = both[:C], both[C:]
    yield
    row =lax.broadcasted_iota(jnp.int32, (C, W), 0)
    col = lax.broadcasted_iota(jnp.int32, (C, W), 1) % C
    diag = jnp.where(row == col, bcol, 0.0)
    diag_hi = diag.astype(jnp.bfloat16)
    diag_lo = (diag - diag_hi.astype(jnp.float32)).astype(jnp.bfloat16)
    ones = jnp.ones((C, C), jnp.bfloat16)
    brow = (jnp.dot(ones, diag_hi, preferred_element_type=jnp.float32) +
            jnp.dot(ones, diag_lo, preferred_element_type=jnp.float32))
    incl = (col >= row) if reverse else (col <= row)
    strict = (col > row) if reverse else (col < row)
    lmat = jnp.exp(jnp.where(incl, bcol - brow, -jnp.inf))
    yield
    kb = k * bexp
    kstack = _head_blockdiag(k)
    m = jnp.where(strict, _mm(kb, kstack, _NT) * lmat, 0.0)
    aqk = jnp.where(incl, _mm(q, kstack, _NT) * lmat, 0.0)
    yield
    t = jnp.where(row == col, 1.0, 0.0)
    s = 1
    while s < C:
        same = (row // (2 * s)) == (col // (2 * s))
        r_hi, c_hi = (row % (2 * s)) >= s, (col % (2 * s)) >= s
        off = same & ((c_hi & ~r_hi) if reverse else (r_hi & ~c_hi))
        m_off = jnp.where(off, m, 0.0)
        if s == 1:
            t = t - m_off
        else:
            x = _mm(m_off, _head_blockdiag(t))
            yield
            t = t - _mm(t, _head_blockdiag(x))
            yield
        s *= 2
    ecs = jnp.exp(bcol)
    u = _mm(t, _head_blockdiag(v * bexp))
    w = _mm(t, _head_blockdiag(kb * ecs))
    yield
    edge = bcol[0:1] if reverse else bcol[C - 1:C]
    k_dec = k * jnp.exp(edge - bcol)
    bd = (lax.broadcasted_iota(jnp.int32, (W, W), 0) // DN_DK ==
          lax.broadcasted_iota(jnp.int32, (W, W), 1) // DN_DV)
    kw = jnp.where(bd, _mm(k_dec, w, _TN), 0.0)
    ku = jnp.where(bd, _mm(k_dec, u, _TN), 0.0)
    yield
    q_eff = q * ecs - _mm(aqk, _head_blockdiag(w))
    o_own = _mm(aqk, _head_blockdiag(u))
    yield
    return dict(kw=kw, ku=ku, q_eff=q_eff, o_own=o_own, dl=jnp.exp(edge))


def _lockstep(generators):
    results = [None] * len(generators)
    live = list(range(len(generators)))
    while live:
        still = []
        for i in live:
            try:
                next(generators[i])
                still.append(i)
            except StopIteration as stop:
                results[i] = stop.value
        live = still
    return results


def _dn_state_chain(prepared, s_ref, o_ref, reverse):
    C = DN_CHUNK
    n_chunks = len(prepared)
    s = s_ref[...]
    for c in (range(n_chunks - 1, -1, -1) if reverse else range(n_chunks)):
        pc = prepared[c]
        o_ref[c * C:(c + 1) * C, :] = _mm(pc["q_eff"], s) + pc["o_own"]
        s = s * pc["dl"] - _mm(pc["kw"], s) + pc["ku"]
        yield
    s_ref[...] = s


def _dn_kernel(qkvf_ref, gf_ref, qkvb_ref, gb_ref, of_ref, ob_ref, sf_ref, sb_ref):
    @pl.when(pl.program_id(1) == 0)
    def _():
        sf_ref[...] = jnp.zeros_like(sf_ref)
        sb_ref[...] = jnp.zeros_like(sb_ref)

    C = DN_CHUNK
    n_chunks = qkvf_ref.shape[0] // C
    rows = lambda c: slice(c * C, (c + 1) * C)
    prepared = _lockstep([_dn_chunk_prepare(qkvf_ref[rows(c), :], gf_ref[rows(c), :], 0) for c in range(n_chunks)] +
                         [_dn_chunk_prepare(qkvb_ref[rows(c), :], gb_ref[rows(c), :], 1) for c in range(n_chunks)])
    _lockstep([_dn_state_chain(prepared[:n_chunks], sf_ref, of_ref, False),
               _dn_state_chain(prepared[n_chunks:], sb_ref, ob_ref, True)])


def dn_scan(qkv, gates, batch, seq):
    tb = min(DN_TILE, seq)
    nt = seq // tb
    W = DN_HEADS * DN_DK
    fwd = lambda w: pl.BlockSpec((tb, w), lambda b, i: (b * nt + i, 0))
    bwd = lambda w: pl.BlockSpec((tb, w), lambda b, i: (b * nt + nt - 1 - i, 0))
    out = jax.ShapeDtypeStruct((batch * seq, DN_WIDTH), jnp.float32)
    return pl.pallas_call(
        _dn_kernel,
        grid=(batch, nt),
        in_specs=[fwd(DN_QKV), fwd(128), bwd(DN_QKV), bwd(128)],
        out_specs=[fwd(DN_WIDTH), bwd(DN_WIDTH)],
        out_shape=[out, out],
        scratch_shapes=[pltpu.VMEM((W, DN_WIDTH), jnp.float32), pltpu.VMEM((W, DN_WIDTH), jnp.float32)],
        compiler_params=pltpu.CompilerParams(dimension_semantics=("arbitrary", "arbitrary"),
                                             vmem_limit_bytes=VMEM_LIMIT_BYTES),
        name="dn_scan",
    )(qkv, gates, qkv, gates)


def _gated_group_norm(of_ref, ob_ref, gate_ref, gn_ref, ind):
    o = of_ref[...] + ob_ref[...]
    ms = _split_dot(o * o, ind) * (1.0 / HEAD_DIM)
    gate = gate_ref[...]
    return (o * lax.rsqrt(ms + EPS) * gn_ref[...] * (gate * (1.0 / (1.0 + jnp.exp(-gate))))).astype(jnp.bfloat16)


def _out_mlp_kernel(glaf_ref, glab_ref, glag_ref, swa_ref, dnf_ref, dnb_ref, dnz_ref, x_ref, wo_ref,
                    gng_ref, gnd_ref, g1_ref, g2_ref, w1_ref, w2_ref, g3_ref, o_ref):
    ind = _group_indicator(GLA_WIDTH, HEAD_DIM)
    y_gla = _gated_group_norm(glaf_ref, glab_ref, glag_ref, gng_ref, ind)
    y_dn = _gated_group_norm(dnf_ref, dnb_ref, dnz_ref, gnd_ref, ind)
    a, b = GLA_WIDTH, GLA_WIDTH + SWA_WIDTH
    m = (jnp.dot(y_gla, wo_ref[0:a, :], preferred_element_type=jnp.float32) +
         jnp.dot(swa_ref[...], wo_ref[a:b, :], preferred_element_type=jnp.float32) +
         jnp.dot(y_dn, wo_ref[b:, :], preferred_element_type=jnp.float32))
    x1 = x_ref[...] + _rms(m, g1_ref[...])
    h = _rms(x1, g2_ref[...]).astype(jnp.bfloat16)
    acc = jnp.zeros_like(x1)
    for c in range(D_FF // FF_CHUNK):
        f = jnp.dot(h, w1_ref[:, c * FF_CHUNK:(c + 1) * FF_CHUNK], preferred_element_type=jnp.float32)
        f = jnp.square(jnp.maximum(f, 0.0)).astype(jnp.bfloat16)
        acc = acc + jnp.dot(f, w2_ref[c * FF_CHUNK:(c + 1) * FF_CHUNK, :], preferred_element_type=jnp.float32)
    o_ref[...] = x1 + _rms(acc, g3_ref[...])


def out_mlp(gla_f, gla_b, gla_g, swa_o, dn_f, dn_b, dn_z, x2d, wo, gla_norm_g, dn_norm_g, g1, g2, w1, w2, g3):
    n, d = x2d.shape
    tm = min(ROW_TILE, n)
    row = lambda w: pl.BlockSpec((tm, w), lambda i: (i, 0))
    vec = _const_spec((1, d))
    hvec = _const_spec((1, GLA_WIDTH))
    tile_gain = lambda g, h: jnp.tile(g.astype(jnp.float32), h).reshape(1, -1)
    return pl.pallas_call(
        _out_mlp_kernel,
        grid=(n // tm,),
        in_specs=[row(GLA_WIDTH), row(GLA_WIDTH), row(GLA_WIDTH), row(SWA_WIDTH), row(DN_WIDTH), row(DN_WIDTH),
                  row(DN_WIDTH), row(d), _const_spec((d, d)), hvec, hvec, vec, vec,
                  _const_spec((d, D_FF)), _const_spec((D_FF, d)), vec],
        out_specs=row(d),
        out_shape=jax.ShapeDtypeStruct((n, d), jnp.float32),
        compiler_params=pltpu.CompilerParams(dimension_semantics=("arbitrary",),
                                             vmem_limit_bytes=VMEM_LIMIT_BYTES),
        name="out_mlp",
    )(gla_f, gla_b, gla_g, swa_o, dn_f, dn_b, dn_z, x2d, wo, tile_gain(gla_norm_g, GLA_HEADS),
      tile_gain(dn_norm_g, DN_HEADS), g1.reshape(1, d), g2.reshape(1, d), w1, w2, g3.reshape(1, d))


def _rmsnorm(x, g):
    xf = x.astype(jnp.float32)
    y = xf * lax.rsqrt(jnp.mean(xf * xf, axis=-1, keepdims=True) + EPS)
    return (y * g.astype(jnp.float32)).astype(x.dtype)


def _l2norm(x):
    return x * lax.rsqrt(jnp.sum(x * x, axis=-1, keepdims=True) + EPS)


def _split_heads(x, h):
    B, T, _ = x.shape
    return x.reshape(B, T, h, -1).transpose(0, 2, 1, 3)


def _merge_heads(x):
    B, H, T, d = x.shape
    return x.transpose(0, 2, 1, 3).reshape(B, T, H * d)


def _split_columns(proj):
    idx, acc = [], 0
    for s in IN_SIZES[:-1]:
        acc += s
        idx.append(acc)
    return jnp.split(proj, idx, axis=-1)


def _gla_chunk_scan(q, k, v, gk):
    B, H, T, dk = q.shape
    dv = v.shape[-1]
    C = GLA_CHUNK
    N = T // C
    q, k, v, gk = [a.reshape(B, H, N, C, a.shape[-1]) for a in (q, k, v, gk)]
    b = jnp.cumsum(gk, axis=3)
    causal = jnp.tril(jnp.ones((C, C), dtype=bool))
    diff = jnp.where(causal[:, :, None], b[..., :, None, :] - b[..., None, :, :], -jnp.inf)
    A = jnp.einsum('bhnid,bhnjd,bhnijd->bhnij', q, k, jnp.exp(diff))
    o_intra = jnp.einsum('bhnij,bhnjv->bhniv', A, v)
    b_last = b[..., -1, :]
    k_dec = k * jnp.exp(b_last[..., None, :] - b)
    upd = jnp.einsum('bhncd,bhncv->nbhdv', k_dec, v)
    dec = jnp.moveaxis(jnp.exp(b_last), 2, 0)

    def step(S, inp):
        d, u = inp
        return d[..., None] * S + u, S

    _, S_prev = lax.scan(step, jnp.zeros((B, H, dk, dv), q.dtype), (dec, upd))
    o_inter = jnp.einsum('bhncd,nbhdv->bhncv', q * jnp.exp(b), S_prev)
    return (o_intra + o_inter).reshape(B, H, T, dv)


def _gla_mixer(q, k, v, g, lr, w_lr2, b_lr, norm_g):
    B, T, _ = q.shape
    lr = lr.reshape(B, T, 2, GLA_LOWRANK)
    logits = jnp.einsum('btzr,zrk->zbtk', lr, w_lr2) + b_lr[:, None, None, :]
    gk = jax.nn.log_sigmoid(logits) / GLA_GATE_NORMALIZER
    gk = gk.reshape(2, B, T, GLA_HEADS, GLA_DK).transpose(0, 1, 3, 2, 4)
    qh = _split_heads(q, GLA_HEADS) * (GLA_DK ** -0.5)
    kh = _split_heads(k, GLA_HEADS)
    vh = _split_heads(v, GLA_HEADS)
    o_f = _gla_chunk_scan(qh, kh, vh, gk[0])
    fl = lambda a: jnp.flip(a, axis=2)
    o_b = fl(_gla_chunk_scan(fl(qh), fl(kh), fl(vh), fl(gk[1])))
    o = _rmsnorm(o_f + o_b, norm_g)
    return _merge_heads(o) * jax.nn.silu(g)


def _short_conv(x, w):
    K = w.shape[0]
    p = K // 2
    T = x.shape[1]
    xp = jnp.pad(x, ((0, 0), (p, p), (0, 0)))
    acc = xp[:, 0:T] * w[0]
    for j in range(1, K):
        acc = acc + xp[:, j:j + T] * w[j]
    return acc


def _gated_delta_chunk(q, k, v, g, beta):
    B, H, T, dk = q.shape
    dv = v.shape[-1]
    C = DN_CHUNK
    N = T // C
    q, k, v = [a.reshape(B, H, N, C, a.shape[-1]) for a in (q, k, v)]
    g = g.reshape(B, H, N, C)
    beta = beta.reshape(B, H, N, C)
    b = jnp.cumsum(g, axis=-1)
    tri = jnp.tril(jnp.ones((C, C), dtype=bool))
    strict = jnp.tril(jnp.ones((C, C), dtype=bool), -1)
    L = jnp.exp(jnp.where(tri, b[..., :, None] - b[..., None, :], -jnp.inf))
    kb = k * beta[..., None]
    M = jnp.where(strict, jnp.einsum('bhnid,bhnjd->bhnij', kb, k) * L, 0.0)
    Amat = M + jnp.eye(C, dtype=M.dtype)
    rhs = jnp.concatenate([v * beta[..., None], kb * jnp.exp(b)[..., None]], axis=-1)
    sol = lax.linalg.triangular_solve(Amat, rhs, left_side=True, lower=True, unit_diagonal=True)
    u, w = sol[..., :dv], sol[..., dv:]
    Aqk = jnp.where(tri, jnp.einsum('bhnid,bhnjd->bhnij', q, k) * L, 0.0)
    q_dec = q * jnp.exp(b)[..., None]
    k_dec = k * jnp.exp(b[..., -1:] - b)[..., None]
    dec_last = jnp.exp(b[..., -1])
    xs = tuple(jnp.moveaxis(a, 2, 0) for a in (u, w, Aqk, q_dec, k_dec, dec_last))

    def step(S, inp):
        uc, wc, Ac, qdc, kdc, dl = inp
        v_new = uc - jnp.einsum('bhcd,bhdv->bhcv', wc, S)
        o = jnp.einsum('bhcd,bhdv->bhcv', qdc, S) + jnp.einsum('bhij,bhjv->bhiv', Ac, v_new)
        S = dl[..., None, None] * S + jnp.einsum('bhcd,bhcv->bhdv', kdc, v_new)
        return S, o

    _, o = lax.scan(step, jnp.zeros((B, H, dk, dv), q.dtype), xs)
    return jnp.moveaxis(o, 0, 2).reshape(B, H, T, dv)


def _deltanet_mixer(qkv, z, beta_raw, a_raw, conv_w, a_log, dt_bias, norm_g):
    B, T, _ = qkv.shape
    qkv = jax.nn.silu(_short_conv(qkv, conv_w))
    q, k, v = jnp.split(qkv, [DN_HEADS * DN_DK, 2 * DN_HEADS * DN_DK], axis=-1)
    qh = _l2norm(_split_heads(q, DN_HEADS)) * (DN_DK ** -0.5)
    kh = _l2norm(_split_heads(k, DN_HEADS))
    vh = _split_heads(v, DN_HEADS)
    beta = jax.nn.sigmoid(beta_raw.reshape(B, T, 2, DN_HEADS)).transpose(2, 0, 3, 1)
    a = a_raw.reshape(B, T, 2, DN_HEADS).transpose(2, 0, 1, 3)
    g = -jnp.exp(a_log)[:, None, None, :] * jax.nn.softplus(a + dt_bias[:, None, None, :])
    g = g.transpose(0, 1, 3, 2)
    o_f = _gated_delta_chunk(qh, kh, vh, g[0], beta[0])
    fl = lambda x: jnp.flip(x, axis=2)
    o_b = fl(_gated_delta_chunk(fl(qh), fl(kh), fl(vh), fl(g[1]), fl(beta[1])))
    o = _rmsnorm(o_f + o_b, norm_g)
    return _merge_heads(o) * jax.nn.silu(z)


def _trunk(x, p):
    B, T, D = x.shape
    bias = p["band_bias"]
    x2 = x.reshape(B * T, D)
    for l in range(DEPTH):
        swa_q, swa_kv, gla_qk, gla_v, gla_g, dn_qkv, dn_z, small = in_proj(x2, p["mix_pre_g"][l], p["w_in_packed"][l])
        gla_f, gla_b = gla_scan(gla_qk, gla_v, small, p["gla_w_lr2"][l], p["gla_b_lr"][l], B, T)
        o_swa = swa_attention(swa_q, swa_kv, p["swa_sink"][l], bias, B, T)
        dn_qkvn, dn_gates = dn_prep(dn_qkv, small, p["dn_conv_w"][l], p["dn_a_log"][l], p["dn_dt_bias"][l], B, T)
        dn_f, dn_b = dn_scan(dn_qkvn, dn_gates, B, T)
        x2 = out_mlp(gla_f, gla_b, gla_g, o_swa, dn_f, dn_b, dn_z, x2, p["w_out_bf16"][l], p["gla_norm_g"][l],
                     p["dn_norm_g"][l], p["mix_post_g"][l], p["mlp_pre_g"][l],
                     p["mlp_w1_bf16"][l], p["mlp_w2_bf16"][l], p["mlp_post_g"][l])
    return x2.reshape(B, T, D)


def _prepare_params(rel_bias_table, mix_pre_g, w_in, gla_w_lr2, gla_b_lr, gla_norm_g, swa_sink, dn_conv_w, dn_a_log,
                    dn_dt_bias, dn_norm_g, w_out, mix_post_g, mlp_pre_g, mlp_w1, mlp_w2, mlp_post_g):
    return dict(mix_pre_g=mix_pre_g, gla_w_lr2=gla_w_lr2, gla_b_lr=gla_b_lr,
                gla_norm_g=gla_norm_g, swa_sink=swa_sink, dn_conv_w=dn_conv_w, dn_a_log=dn_a_log,
                dn_dt_bias=dn_dt_bias, dn_norm_g=dn_norm_g, mix_post_g=mix_post_g, mlp_pre_g=mlp_pre_g,
                mlp_post_g=mlp_post_g,
                band_bias=band_bias(rel_bias_table),
                w_in_packed=jax.vmap(pack_w_in)(w_in), w_out_bf16=w_out.astype(jnp.bfloat16),
                mlp_w1_bf16=mlp_w1.astype(jnp.bfloat16), mlp_w2_bf16=mlp_w2.astype(jnp.bfloat16))


def kernel(x_prompt, x_sample, rel_bias_table, mix_pre_g, w_in, gla_w_lr2, gla_b_lr, gla_norm_g, swa_sink, dn_conv_w, dn_a_log, dn_dt_bias, dn_norm_g, w_out, mix_post_g, mlp_pre_g, mlp_w1, mlp_w2, mlp_post_g):
    p = _prepare_params(rel_bias_table, mix_pre_g, w_in, gla_w_lr2, gla_b_lr, gla_norm_g, swa_sink, dn_conv_w,
                        dn_a_log, dn_dt_bias, dn_norm_g, w_out, mix_post_g, mlp_pre_g, mlp_w1, mlp_w2, mlp_post_g)
    return (_trunk(x_prompt, p), _trunk(x_sample, p))
```

```python
import functools
import math

import numpy as np

import jax
import jax.numpy as jnp
from jax import lax
from jax.experimental import pallas as pl
from jax.experimental.pallas import tpu as pltpu

D_MODEL = 1024
DEPTH = 4
HEAD_DIM = 64
EPS = 1e-6
GLA_HEADS = 4
GLA_DK = 32
GLA_DV = 64
GLA_LOWRANK = 16
GLA_GATE_NORMALIZER = 16.0
GLA_CHUNK = 32
SWA_HEADS = 8
SWA_KV_HEADS = 2
SWA_WINDOW = 128
SWA_BLOCK = 128
NUM_BUCKETS = 32
MAX_DISTANCE = 128
DN_HEADS = 4
DN_DK = 64
DN_DV = 64
DN_CONV = 5
DN_CHUNK = 64
GLA_WIDTH = GLA_HEADS * GLA_DV
SWA_WIDTH = SWA_HEADS * HEAD_DIM
DN_WIDTH = DN_HEADS * DN_DV
D_MIX = GLA_WIDTH + SWA_WIDTH + DN_WIDTH
D_FF = 4 * D_MODEL
DN_QKV = DN_HEADS * (2 * DN_DK + DN_DV)
IN_SIZES = (GLA_HEADS * GLA_DK, GLA_HEADS * GLA_DK, GLA_WIDTH, GLA_WIDTH, 2 * GLA_LOWRANK,
            SWA_WIDTH, SWA_KV_HEADS * HEAD_DIM, SWA_KV_HEADS * HEAD_DIM,
            DN_QKV, DN_WIDTH, 2 * DN_HEADS, 2 * DN_HEADS)
D_IN = sum(IN_SIZES)

VMEM_LIMIT_BYTES = 56 * 1024 * 1024
ROW_TILE = 512
FF_CHUNK = 1024


def _rms(xf, g):
    return xf * lax.rsqrt(jnp.mean(xf * xf, axis=-1, keepdims=True) + EPS) * g


def _const_spec(shape):
    return pl.BlockSpec(shape, lambda i: (0,) * len(shape), pipeline_mode=pl.Buffered(1))


_O = {}
_acc = 0
for _name, _width in (("dn_qkv", DN_QKV), ("small", 128), ("swa_q", SWA_WIDTH), ("swa_kv", 4 * 2 * HEAD_DIM),
                      ("gla_qk", 2 * GLA_HEADS * GLA_DK), ("gla_v", GLA_WIDTH), ("gla_g", GLA_WIDTH),
                      ("dn_z", DN_WIDTH)):
    _O[_name] = (_acc, _width)
    _acc += _width
D_IN_PACKED = _acc
D_IN_LEAD = _O["small"][0] + _O["small"][1]
_OUT_DTYPES = {"swa_q": jnp.bfloat16, "swa_kv": jnp.bfloat16}


def pack_w_in(w_in):
    offs, acc = [], 0
    for s in IN_SIZES:
        offs.append(acc)
        acc += s
    gq, gk, gv, gg, glr, sq, sk, sv, dqkv, dz, dbeta, da = [w_in[:, o:o + s] for o, s in zip(offs, IN_SIZES)]
    hd = HEAD_DIM
    kv = [sk[:, 0:hd], sk[:, 0:hd], sk[:, hd:], sk[:, hd:], sv[:, 0:hd], sv[:, 0:hd], sv[:, hd:], sv[:, hd:]]
    small = jnp.concatenate([glr, dbeta, da], axis=1)
    small = jnp.pad(small, ((0, 0), (0, 128 - small.shape[1])))
    pieces = dict(dn_qkv=dqkv, small=small, swa_q=sq, swa_kv=jnp.concatenate(kv, axis=1),
                  gla_qk=jnp.concatenate([gq, gk], axis=1), gla_v=gv, gla_g=gg, dn_z=dz)
    return jnp.concatenate([pieces[name] for name in _O], axis=1).astype(jnp.bfloat16)


def _in_proj_kernel(tiles_per_seq, xp_ref, xc_ref, xn_ref, g_ref, w_ref, cw_ref, na_ref, dt_ref, *refs):
    o_refs, gate_ref, ext_ref = refs[:len(_O)], refs[len(_O)], refs[len(_O) + 1]
    tm, H = xc_ref.shape[0], DN_HALO
    i = pl.program_id(0)
    first = (i % tiles_per_seq) == 0
    last = (i % tiles_per_seq) == tiles_per_seq - 1
    x = jnp.concatenate([xp_ref[...], xc_ref[...], xn_ref[...]], axis=0)
    h = _rms(x, g_ref[...])
    y = jnp.dot(h.astype(jnp.bfloat16), w_ref[:, :D_IN_LEAD], preferred_element_type=jnp.float32)
    y_rest = jnp.dot(h[H:H + tm].astype(jnp.bfloat16), w_ref[:, D_IN_LEAD:], preferred_element_type=jnp.float32)
    for (name, (off, width)), o_ref in zip(_O.items(), o_refs):
        if name == "small":
            o_ref[...] = y[H:H + tm, off:off + width]
        elif name != "dn_qkv":
            o_ref[...] = y_rest[:, off - D_IN_LEAD:off - D_IN_LEAD + width].astype(o_ref.dtype)
    off, width = _O["dn_qkv"]
    ext_ref[0:H, :] = jnp.where(first, 0.0, y[0:H, off:off + width])
    ext_ref[H:H + tm, :] = y[H:H + tm, off:off + width]
    ext_ref[H + tm:, :] = jnp.where(last, 0.0, y[H + tm:, off:off + width])
    half = DN_CONV // 2
    acc = ext_ref[H - half:H - half + tm, :] * cw_ref[0:1, :]
    for j in range(1, DN_CONV):
        acc = acc + ext_ref[H - half + j:H - half + j + tm, :] * cw_ref[j:j + 1, :]
    c = acc * (1.0 / (1.0 + jnp.exp(-acc)))
    w = DN_HEADS * DN_DK
    ind = _group_indicator(w, DN_DK)
    sumsq = lambda a: jnp.dot((a * a).astype(jnp.bfloat16), ind, preferred_element_type=jnp.float32)
    q, k = c[:, :w], c[:, w:2 * w]
    qkv_ref = o_refs[list(_O).index("dn_qkv")]
    qkv_ref[:, :w] = q * lax.rsqrt(sumsq(q) + EPS) * (DN_DK ** -0.5)
    qkv_ref[:, w:2 * w] = k * lax.rsqrt(sumsq(k) + EPS)
    qkv_ref[:, 2 * w:] = c[:, 2 * w:]
    off, width = _O["small"]
    s = y[H:H + tm, off:off + width]
    lane = lax.broadcasted_iota(jnp.int32, s.shape, 1)
    beta = 1.0 / (1.0 + jnp.exp(-s))
    a = s + dt_ref[...]
    gdec = na_ref[...] * (jnp.maximum(a, 0.0) + jnp.log1p(jnp.exp(-jnp.abs(a))))
    gate_ref[...] = jnp.where((lane >= DN_BETA_LANE) & (lane < DN_A_LANE), beta,
                              jnp.where((lane >= DN_A_LANE) & (lane < DN_A_LANE + 2 * DN_HEADS), gdec, 0.0))


def in_proj(x2d, g, w_packed, conv_w, a_log, dt_bias, seq):
    n, d = x2d.shape
    tm = min(ROW_TILE, seq)
    hb = tm // DN_HALO
    nh = n // DN_HALO
    lanes = lambda v: jnp.zeros((1, 128), jnp.float32).at[0, DN_A_LANE:DN_A_LANE + 2 * DN_HEADS].set(v.reshape(-1))
    row = lambda w: pl.BlockSpec((tm, w), lambda i: (i, 0))
    return pl.pallas_call(
        functools.partial(_in_proj_kernel, seq // tm),
        grid=(n // tm,),
        in_specs=[pl.BlockSpec((DN_HALO, d), lambda i: (jnp.maximum(i * hb - 1, 0), 0)),
                  row(d),
                  pl.BlockSpec((DN_HALO, d), lambda i: (jnp.minimum((i + 1) * hb, nh - 1), 0)),
                  _const_spec((1, d)),
                  _const_spec((d, D_IN_PACKED)),
                  _const_spec((DN_CONV, DN_QKV)), _const_spec((1, 128)), _const_spec((1, 128))],
        out_specs=[row(w) for _, w in _O.values()] + [row(128)],
        out_shape=[jax.ShapeDtypeStruct((n, w), _OUT_DTYPES.get(name, jnp.float32))
                   for name, (_, w) in _O.items()] + [jax.ShapeDtypeStruct((n, 128), jnp.float32)],
        scratch_shapes=[pltpu.VMEM((tm + 2 * DN_HALO, DN_QKV), jnp.float32)],
        compiler_params=pltpu.CompilerParams(dimension_semantics=("arbitrary",),
                                             vmem_limit_bytes=VMEM_LIMIT_BYTES),
        name="in_proj",
    )(x2d, x2d, x2d, g.reshape(1, d), w_packed, conv_w.astype(jnp.float32),
      lanes(-jnp.exp(a_log.astype(jnp.float32))), lanes(dt_bias.astype(jnp.float32)))


SWA_Q_TILE = 512
SWA_GROUP = SWA_HEADS // SWA_KV_HEADS


def _t5_bucket(rel):
    nb = NUM_BUCKETS // 2
    max_exact = nb // 2
    base = jnp.where(rel > 0, nb, 0)
    n = jnp.abs(rel)
    nf = jnp.maximum(n, 1).astype(jnp.float32)
    large = max_exact + (jnp.log(nf / max_exact) / math.log(MAX_DISTANCE / max_exact)
                         * (nb - max_exact)).astype(jnp.int32)
    large = jnp.minimum(large, nb - 1)
    return base + jnp.where(n < max_exact, n, large)


def _band_bias_kernel(tab_ref, idx_ref, o_ref):
    W = SWA_BLOCK
    idx = idx_ref[...]
    i = lax.broadcasted_iota(jnp.int32, idx.shape, 0)
    j = lax.broadcasted_iota(jnp.int32, idx.shape, 1)
    in_band = jnp.abs(j - W - i) <= SWA_WINDOW
    for h in range(SWA_HEADS):
        acc = jnp.zeros(idx.shape, jnp.float32)
        for b in range(NUM_BUCKETS):
            acc = jnp.where(idx == b, tab_ref[b, h], acc)
        o_ref[h] = jnp.where(in_band, acc, -jnp.inf)


def band_bias(table):
    W = SWA_BLOCK
    rel = jnp.arange(3 * W)[None, :] - W - jnp.arange(W)[:, None]
    idx = _t5_bucket(rel).astype(jnp.int32)
    return pl.pallas_call(
        _band_bias_kernel,
        in_specs=[pl.BlockSpec(memory_space=pltpu.SMEM), pl.BlockSpec(memory_space=pltpu.VMEM)],
        out_specs=pl.BlockSpec(memory_space=pltpu.VMEM),
        out_shape=jax.ShapeDtypeStruct((SWA_HEADS, W, 3 * W), jnp.float32),
        name="band_bias",
    )(table.astype(jnp.float32), idx)


def _swa_kernel(sink_ref, q_ref, kvp_ref, kvc_ref, kvn_ref, bias_ref, o_ref, kv_buf):
    W = SWA_BLOCK
    n_sub = q_ref.shape[0] // W
    n = pl.program_id(1)
    last = pl.num_programs(1) - 1
    kv_buf[0:W, :] = kvp_ref[...]
    kv_buf[W:W + n_sub * W, :] = kvc_ref[...]
    kv_buf[W + n_sub * W:, :] = kvn_ref[...]
    lane = lax.broadcasted_iota(jnp.int32, (W, 2 * HEAD_DIM), 1)
    lo = lane < HEAD_DIM
    key_blk = lax.broadcasted_iota(jnp.int32, (1, 3 * W), 1) // W
    neg = jnp.float32(-jnp.inf)
    for s in range(n_sub):
        edge = jnp.zeros((1, 3 * W), jnp.float32)
        if s == 0:
            edge = jnp.where((key_blk == 0) & (n == 0), neg, edge)
        if s == n_sub - 1:
            edge = jnp.where((key_blk == 2) & (n == last), neg, edge)
        rows = slice(s * W, (s + 1) * W)
        for g in range(SWA_KV_HEADS):
            kd = kv_buf[s * W:(s + 3) * W, g * 128:(g + 1) * 128]
            vd = kv_buf[s * W:(s + 3) * W, 256 + g * 128:256 + (g + 1) * 128]
            qs = []
            for pair in range(SWA_GROUP // 2):
                col = (g * SWA_GROUP // 2 + pair) * 128
                qp = q_ref[rows, col:col + 128]
                qs += [jnp.where(lo, qp, 0), jnp.where(lo, 0, qp)]
            q_stack = jnp.concatenate(qs, axis=0)
            s_all = lax.dot_general(q_stack, kd, (((1,), (1,)), ((), ())),
                                    preferred_element_type=jnp.float32)
            ps, rinv = [], []
            for hh in range(SWA_GROUP):
                h = g * SWA_GROUP + hh
                sc = s_all[hh * W:(hh + 1) * W] * (HEAD_DIM ** -0.5) + bias_ref[h]
                if s == 0 or s == n_sub - 1:
                    sc = sc + edge
                sink = sink_ref[h]
                m = jnp.maximum(jnp.max(sc, axis=-1, keepdims=True), sink)
                p = jnp.exp(sc - m)
                denom = jnp.sum(p, axis=-1, keepdims=True) + jnp.exp(sink - m)
                ps.append(p.astype(jnp.bfloat16))
                rinv.append(1.0 / denom)
            o_all = jnp.dot(jnp.concatenate(ps, axis=0), vd, preferred_element_type=jnp.float32)
            for pair in range(SWA_GROUP // 2):
                e, o = 2 * pair, 2 * pair + 1
                col = (g * SWA_GROUP // 2 + pair) * 128
                o_ref[rows, col:col + 128] = jnp.where(lo, o_all[e * W:(e + 1) * W] * rinv[e],
                                                       o_all[o * W:(o + 1) * W] * rinv[o]).astype(o_ref.dtype)


def swa_attention(q, kv, sink, bias, batch, seq):
    W = SWA_BLOCK
    tq = min(SWA_Q_TILE, seq)
    n_sub = tq // W
    nq = seq // tq
    nb = seq // W
    return pl.pallas_call(
        _swa_kernel,
        grid=(batch, nq),
        in_specs=[pl.BlockSpec(memory_space=pltpu.SMEM),
                  pl.BlockSpec((tq, SWA_WIDTH), lambda b, i: (b * nq + i, 0)),
                  pl.BlockSpec((W, 512), lambda b, i: (b * nb + jnp.maximum(i * n_sub - 1, 0), 0)),
                  pl.BlockSpec((tq, 512), lambda b, i: (b * nq + i, 0)),
                  pl.BlockSpec((W, 512), lambda b, i: (b * nb + jnp.minimum((i + 1) * n_sub, nb - 1), 0)),
                  pl.BlockSpec((SWA_HEADS, W, 3 * W), lambda b, i: (0, 0, 0), pipeline_mode=pl.Buffered(1))],
        out_specs=pl.BlockSpec((tq, SWA_WIDTH), lambda b, i: (b * nq + i, 0)),
        out_shape=jax.ShapeDtypeStruct((batch * seq, SWA_WIDTH), jnp.bfloat16),
        scratch_shapes=[pltpu.VMEM((tq + 2 * W, 512), jnp.bfloat16)],
        compiler_params=pltpu.CompilerParams(dimension_semantics=("arbitrary", "arbitrary"),
                                             vmem_limit_bytes=VMEM_LIMIT_BYTES),
        name="swa",
    )(sink.astype(jnp.float32), q, kv, kv, kv, bias)


GLA_TILE = 256
GLA_SUB = 16
_NT = (((1,), (1,)), ((), ()))
_TN = (((0,), (0,)), ((), ()))


def _seg_cumsum(x, seg, reverse):
    n = x.shape[0]
    rowmod = lax.broadcasted_iota(jnp.int32, x.shape, 0) % seg
    sh = 1
    while sh < seg:
        if reverse:
            x = x + jnp.where(rowmod < seg - sh, pltpu.roll(x, n - sh, 0), 0.0)
        else:
            x = x + jnp.where(rowmod >= sh, pltpu.roll(x, sh, 0), 0.0)
        sh *= 2
    return x


def _log_sigmoid(x):
    return jnp.minimum(x, 0.0) - jnp.log1p(jnp.exp(-jnp.abs(x)))


def _gla_constants(tb):
    S, hk = GLA_SUB, GLA_HEADS * GLA_DK
    r = np.arange(tb)
    same = (r[:, None] // S) == (r[None, :] // S)
    tri = np.stack([same & (r[None, :] <= r[:, None]), same & (r[None, :] >= r[:, None])])
    src = np.arange(S * hk)
    d, h = src // hk, (src % hk) // GLA_DK
    lane = np.arange(128)
    place = np.stack([lane[None, :] == (32 * h + S - d)[:, None], lane[None, :] == (32 * h + d)[:, None]])
    return jnp.asarray(tri, jnp.bfloat16), jnp.asarray(place, jnp.bfloat16)


def _gla_direction(qk_ref, v_ref, sm_ref, wg_ref, bg_ref, tri_ref, place_ref, st_ref, o_ref, reverse):
    tb, S = qk_ref.shape[0], GLA_SUB
    hk = GLA_HEADS * GLA_DK
    logits = jnp.dot(sm_ref[...].astype(jnp.bfloat16), wg_ref[...], preferred_element_type=jnp.float32)
    g = _log_sigmoid(logits + bg_ref[...]) * (1.0 / GLA_GATE_NORMALIZER)
    g_hi = g.astype(jnp.bfloat16)
    g_lo = (g - g_hi.astype(jnp.float32)).astype(jnp.bfloat16)
    cs = (jnp.dot(tri_ref[...], g_hi, preferred_element_type=jnp.float32) +
          jnp.dot(tri_ref[...], g_lo, preferred_element_type=jnp.float32))
    yield
    a = jnp.exp(g)
    q = qk_ref[:, :hk] * (GLA_DK ** -0.5)
    k = qk_ref[:, hk:]
    v = v_ref[...]
    rowmod = lax.broadcasted_iota(jnp.int32, (tb, hk), 0) % S
    has_prev = (rowmod < S - 1) if reverse else (rowmod >= 1)
    kd = k
    ps = [(q * kd).astype(jnp.bfloat16)]
    for d in range(1, S):
        kd = a * jnp.where(has_prev, pltpu.roll(kd, tb - 1 if reverse else 1, 0), 0.0)
        ps.append((q * kd).astype(jnp.bfloat16))
    scores = jnp.dot(jnp.concatenate(ps, axis=1), place_ref[...], preferred_element_type=jnp.float32)
    yield
    head_of_lane = lax.broadcasted_iota(jnp.int32, (S, GLA_WIDTH), 1) // GLA_DV
    zeros = jnp.zeros((S, GLA_WIDTH), jnp.float32)
    acc = {}
    for s in range(tb // S):
        rows = slice(s * S, (s + 1) * S)
        band = pltpu.roll(scores[rows], 0, 1, stride=1, stride_axis=0).astype(jnp.bfloat16)
        v_blk = v[rows]
        pieces = []
        for h in range(GLA_HEADS):
            vh = jnp.where(head_of_lane == h, v_blk, 0.0)
            pieces += [vh, zeros] if reverse else [zeros, vh]
        acc[s] = jnp.dot(band, jnp.concatenate(pieces, axis=0).astype(jnp.bfloat16),
                         preferred_element_type=jnp.float32)
        if s % 2 == 1:
            yield
    qd = (q * jnp.exp(cs)).astype(jnp.bfloat16)
    blockdiag = (lax.broadcasted_iota(jnp.int32, (GLA_WIDTH, hk), 0) // GLA_DV ==
                 lax.broadcasted_iota(jnp.int32, (GLA_WIDTH, hk), 1) // GLA_DK)
    n_blk = tb // S
    order = range(n_blk - 1, -1, -1) if reverse else range(n_blk)
    dec, upd = {}, {}
    for s in order:
        rows = slice(s * S, (s + 1) * S)
        cs_s = cs[rows]
        edge = cs_s[0:1] if reverse else cs_s[S - 1:S]
        kt = (k[rows] * jnp.exp(edge - cs_s)).astype(jnp.bfloat16)
        dec[s] = jnp.exp(edge)
        upd[s] = jnp.where(blockdiag, lax.dot_general(v[rows].astype(jnp.bfloat16), kt, _TN,
                                                      preferred_element_type=jnp.float32), 0.0)
        yield
    st = st_ref[...]
    for s in order:
        rows = slice(s * S, (s + 1) * S)
        o_ref[rows, :] = acc[s] + lax.dot_general(qd[rows], st.astype(jnp.bfloat16), _NT,
                                                  preferred_element_type=jnp.float32)
        st = st * dec[s] + upd[s]
        yield
    st_ref[...] = st


def _gla_kernel(qkf_ref, vf_ref, smf_ref, qkb_ref, vb_ref, smb_ref, wg_ref, bg_ref, tri_ref, place_ref,
                of_ref, ob_ref, stf_ref, stb_ref):
    @pl.when(pl.program_id(1) == 0)
    def _():
        stf_ref[...] = jnp.zeros_like(stf_ref)
        stb_ref[...] = jnp.zeros_like(stb_ref)

    _lockstep([_gla_direction(qkf_ref, vf_ref, smf_ref, wg_ref.at[0], bg_ref.at[0], tri_ref.at[0], place_ref.at[0],
                              stf_ref, of_ref, False),
               _gla_direction(qkb_ref, vb_ref, smb_ref, wg_ref.at[1], bg_ref.at[1], tri_ref.at[1], place_ref.at[1],
                              stb_ref, ob_ref, True)])


def gla_scan(gla_qk, gla_v, small, w_lr2, b_lr, batch, seq):
    hk = GLA_HEADS * GLA_DK
    tb = min(GLA_TILE, seq)
    nt = seq // tb
    wg = jnp.zeros((2, 128, hk), jnp.float32)
    for z in range(2):
        wg = wg.at[z, z * GLA_LOWRANK:(z + 1) * GLA_LOWRANK].set(w_lr2[z])
    tri, place = _gla_constants(tb)
    fwd = lambda w: pl.BlockSpec((tb, w), lambda b, i: (b * nt + i, 0))
    bwd = lambda w: pl.BlockSpec((tb, w), lambda b, i: (b * nt + nt - 1 - i, 0))
    const = lambda a: pl.BlockSpec(a.shape, lambda b, i: (0,) * a.ndim)
    out = jax.ShapeDtypeStruct((batch * seq, GLA_WIDTH), jnp.float32)
    return pl.pallas_call(
        _gla_kernel,
        grid=(batch, nt),
        in_specs=[fwd(2 * hk), fwd(GLA_WIDTH), fwd(128), bwd(2 * hk), bwd(GLA_WIDTH), bwd(128),
                  pl.BlockSpec((2, 128, hk), lambda b, i: (0, 0, 0)),
                  pl.BlockSpec((2, 1, hk), lambda b, i: (0, 0, 0)), const(tri), const(place)],
        out_specs=[fwd(GLA_WIDTH), bwd(GLA_WIDTH)],
        out_shape=[out, out],
        scratch_shapes=[pltpu.VMEM((GLA_WIDTH, hk), jnp.float32), pltpu.VMEM((GLA_WIDTH, hk), jnp.float32)],
        compiler_params=pltpu.CompilerParams(dimension_semantics=("arbitrary", "arbitrary"),
                                             vmem_limit_bytes=VMEM_LIMIT_BYTES),
        name="gla",
    )(gla_qk, gla_v, small, gla_qk, gla_v, small, wg.astype(jnp.bfloat16), b_lr.reshape(2, 1, hk).astype(jnp.float32),
      tri, place)


DN_PREP_TILE = 512
DN_HALO = 8
DN_TILE = 256
DN_BETA_LANE = 2 * GLA_LOWRANK
DN_A_LANE = DN_BETA_LANE + 2 * DN_HEADS


def _split_dot(x, w_bf16):
    hi = x.astype(jnp.bfloat16)
    lo = (x - hi.astype(jnp.float32)).astype(jnp.bfloat16)
    return (jnp.dot(hi, w_bf16, preferred_element_type=jnp.float32) +
            jnp.dot(lo, w_bf16, preferred_element_type=jnp.float32))


def _group_indicator(n, group):
    r = lax.broadcasted_iota(jnp.int32, (n, n), 0) // group
    c = lax.broadcasted_iota(jnp.int32, (n, n), 1) // group
    return (r == c).astype(jnp.bfloat16)


def _dn_prep_kernel(xp_ref, xc_ref, xn_ref, sm_ref, cw_ref, na_ref, dt_ref, qkv_ref, gate_ref, ext_ref):
    tm = xc_ref.shape[0]
    i = pl.program_id(1)
    last = pl.num_programs(1) - 1
    H = DN_HALO
    ext_ref[0:H, :] = jnp.where(i == 0, 0.0, xp_ref[...])
    ext_ref[H:H + tm, :] = xc_ref[...]
    ext_ref[H + tm:, :] = jnp.where(i == last, 0.0, xn_ref[...])
    half = DN_CONV // 2
    acc = ext_ref[H - half:H - half + tm, :] * cw_ref[0:1, :]
    for j in range(1, DN_CONV):
        acc = acc + ext_ref[H - half + j:H - half + j + tm, :] * cw_ref[j:j + 1, :]
    y = acc * (1.0 / (1.0 + jnp.exp(-acc)))
    ind = _group_indicator(DN_HEADS * DN_DK, DN_DK)
    w = DN_HEADS * DN_DK
    q, k = y[:, :w], y[:, w:2 * w]
    qkv_ref[:, :w] = q * lax.rsqrt(_split_dot(q * q, ind) + EPS) * (DN_DK ** -0.5)
    qkv_ref[:, w:2 * w] = k * lax.rsqrt(_split_dot(k * k, ind) + EPS)
    qkv_ref[:, 2 * w:] = y[:, 2 * w:]
    s = sm_ref[...]
    lane = lax.broadcasted_iota(jnp.int32, s.shape, 1)
    beta = 1.0 / (1.0 + jnp.exp(-s))
    a = s + dt_ref[...]
    g = na_ref[...] * (jnp.maximum(a, 0.0) + jnp.log1p(jnp.exp(-jnp.abs(a))))
    gate_ref[...] = jnp.where((lane >= DN_BETA_LANE) & (lane < DN_A_LANE), beta,
                              jnp.where((lane >= DN_A_LANE) & (lane < DN_A_LANE + 2 * DN_HEADS), g, 0.0))


def dn_prep(dn_qkv, small, conv_w, a_log, dt_bias, batch, seq):
    tm = min(DN_PREP_TILE, seq)
    nt = seq // tm
    hb = tm // DN_HALO
    nh = seq // DN_HALO
    pad = lambda x: jnp.zeros((1, 128), jnp.float32).at[0, DN_A_LANE:DN_A_LANE + 2 * DN_HEADS].set(x.reshape(-1))
    row = lambda w: pl.BlockSpec((tm, w), lambda b, i: (b * nt + i, 0))
    return pl.pallas_call(
        _dn_prep_kernel,
        grid=(batch, nt),
        in_specs=[pl.BlockSpec((DN_HALO, DN_QKV), lambda b, i: (b * nh + jnp.maximum(i * hb - 1, 0), 0)),
                  row(DN_QKV),
                  pl.BlockSpec((DN_HALO, DN_QKV), lambda b, i: (b * nh + jnp.minimum((i + 1) * hb, nh - 1), 0)),
                  row(128),
                  pl.BlockSpec((DN_CONV, DN_QKV), lambda b, i: (0, 0)),
                  pl.BlockSpec((1, 128), lambda b, i: (0, 0)),
                  pl.BlockSpec((1, 128), lambda b, i: (0, 0))],
        out_specs=[row(DN_QKV), row(128)],
        out_shape=[jax.ShapeDtypeStruct((batch * seq, DN_QKV), jnp.float32),
                   jax.ShapeDtypeStruct((batch * seq, 128), jnp.float32)],
        scratch_shapes=[pltpu.VMEM((tm + 2 * DN_HALO, DN_QKV), jnp.float32)],
        compiler_params=pltpu.CompilerParams(dimension_semantics=("arbitrary", "arbitrary"),
                                             vmem_limit_bytes=VMEM_LIMIT_BYTES),
        name="dn_prep",
    )(dn_qkv, dn_qkv, dn_qkv, small, conv_w.astype(jnp.float32), pad(-jnp.exp(a_log.astype(jnp.float32))),
      pad(dt_bias.astype(jnp.float32)))


def _mm(a, b, dims=None):
    a, b = a.astype(jnp.bfloat16), b.astype(jnp.bfloat16)
    if dims is None:
        return jnp.dot(a, b, preferred_element_type=jnp.float32)
    return lax.dot_general(a, b, dims, preferred_element_type=jnp.float32)


def _head_blockdiag(x):
    c, hw = x.shape
    w = hw // DN_HEADS
    stacked = jnp.concatenate([x] * DN_HEADS, axis=0)
    r = lax.broadcasted_iota(jnp.int32, stacked.shape, 0) // c
    l = lax.broadcasted_iota(jnp.int32, stacked.shape, 1) // w
    return jnp.where(r == l, stacked, 0.0)


def _dn_chunk_prepare(qkv, gate, direction):
    C, W = DN_CHUNK, DN_HEADS * DN_DK
    reverse = direction == 1
    q, k, v = qkv[:, :W], qkv[:, W:2 * W], qkv[:, 2 * W:]
    lane = lax.broadcasted_iota(jnp.int32, (C, 128), 1)
    g_lo = DN_A_LANE + direction * DN_HEADS
    b_lo = DN_BETA_LANE + direction * DN_HEADS
    cs = _seg_cumsum(jnp.where((lane >= g_lo) & (lane < g_lo + DN_HEADS), gate, 0.0), C, reverse)
    beta = jnp.where((lane >= b_lo) & (lane < b_lo + DN_HEADS), gate, 0.0)
    src = lax.broadcasted_iota(jnp.int32, (128, W), 0)
    dst = lax.broadcasted_iota(jnp.int32, (128, W), 1) // DN_DK
    expand = ((src - g_lo == dst) | (src - b_lo == dst)).astype(jnp.bfloat16)
    both = _split_dot(jnp.concatenate([cs, beta], axis=0), expand)
    bcol, bexp = both[:C], both[C:]
    yield
    row =lax.broadcasted_iota(jnp.int32, (C, W), 0)
    col = lax.broadcasted_iota(jnp.int32, (C, W), 1) % C
    diag = jnp.where(row == col, bcol, 0.0)
    diag_hi = diag.astype(jnp.bfloat16)
    diag_lo = (diag - diag_hi.astype(jnp.float32)).astype(jnp.bfloat16)
    ones = jnp.ones((C, C), jnp.bfloat16)
    brow = (jnp.dot(ones, diag_hi, preferred_element_type=jnp.float32) +
            jnp.dot(ones, diag_lo, preferred_element_type=jnp.float32))
    incl = (col >= row) if reverse else (col <= row)
    strict = (col > row) if reverse else (col < row)
    lmat = jnp.exp(jnp.where(incl, bcol - brow, -jnp.inf))
    yield
    kb = k * bexp
    kk = _mm(jnp.concatenate([kb, q], axis=0), _head_blockdiag(k), _NT)
    m = jnp.where(strict, kk[:C] * lmat, 0.0)
    aqk = jnp.where(incl, kk[C:] * lmat, 0.0)
    yield
    t = jnp.where(row == col, 1.0, 0.0)
    s = 1
    while s < C:
        same = (row // (2 * s)) == (col // (2 * s))
        r_hi, c_hi = (row % (2 * s)) >= s, (col % (2 * s)) >= s
        off = same & ((c_hi & ~r_hi) if reverse else (r_hi & ~c_hi))
        m_off = jnp.where(off, m, 0.0)
        if s == 1:
            t = t - m_off
        else:
            x = _mm(m_off, _head_blockdiag(t))
            yield
            t = t - _mm(t, _head_blockdiag(x))
            yield
        s *= 2
    ecs = jnp.exp(bcol)
    u = _mm(t, _head_blockdiag(v * bexp))
    w = _mm(t, _head_blockdiag(kb * ecs))
    yield
    edge = bcol[0:1] if reverse else bcol[C - 1:C]
    k_dec = k * jnp.exp(edge - bcol)
    bd = (lax.broadcasted_iota(jnp.int32, (W, W), 0) // DN_DK ==
          lax.broadcasted_iota(jnp.int32, (W, W), 1) // DN_DV)
    kwu = _mm(k_dec, jnp.concatenate([w, u], axis=1), _TN)
    kw = jnp.where(bd, kwu[:, :W], 0.0)
    ku = jnp.where(bd, kwu[:, W:], 0.0)
    yield
    q_eff = q * ecs - _mm(aqk, _head_blockdiag(w))
    o_own = _mm(aqk, _head_blockdiag(u))
    yield
    return dict(kw=kw, ku=ku, q_eff=q_eff, o_own=o_own, dl=jnp.exp(edge))


def _lockstep_stages(generators):
    results = [None] * len(generators)
    live = list(range(len(generators)))
    while live:
        still = []
        for i in live:
            try:
                next(generators[i])
                still.append(i)
            except StopIteration as stop:
                results[i] = stop.value
        live = still
        yield
    return results


def _lockstep(generators):
    stages = _lockstep_stages(generators)
    while True:
        try:
            next(stages)
        except StopIteration as stop:
            return stop.value


def _dn_state_chain(prepared, s_ref, o_ref, reverse):
    C = DN_CHUNK
    n_chunks = len(prepared)
    s = s_ref[...]
    for c in (range(n_chunks - 1, -1, -1) if reverse else range(n_chunks)):
        pc = prepared[c]
        both = _mm(jnp.concatenate([pc["q_eff"], pc["kw"]], axis=0), s)
        o_ref[c * C:(c + 1) * C, :] = both[:C] + pc["o_own"]
        s = s * pc["dl"] - both[C:] + pc["ku"]
        yield
    s_ref[...] = s


def _dn_tile(qkvf_ref, gf_ref, qkvb_ref, gb_ref, of_ref, ob_ref, sf_ref, sb_ref):
    C = DN_CHUNK
    n_chunks = qkvf_ref.shape[0] // C
    rows = lambda c: slice(c * C, (c + 1) * C)
    prepared = yield from _lockstep_stages(
        [_dn_chunk_prepare(qkvf_ref[rows(c), :], gf_ref[rows(c), :], 0) for c in range(n_chunks)] +
        [_dn_chunk_prepare(qkvb_ref[rows(c), :], gb_ref[rows(c), :], 1) for c in range(n_chunks)])
    yield from _lockstep_stages([_dn_state_chain(prepared[:n_chunks], sf_ref, of_ref, False),
                                 _dn_state_chain(prepared[n_chunks:], sb_ref, ob_ref, True)])


def _seq_mixers_kernel(qkf_ref, vf_ref, smf_ref, qkb_ref, vb_ref, smb_ref, wg_ref, bg_ref, tri_ref, place_ref,
                       dqkvf_ref, dgf_ref, dqkvb_ref, dgb_ref,
                       gof_ref, gob_ref, dof_ref, dob_ref, stf_ref, stb_ref, sf_ref, sb_ref):
    @pl.when(pl.program_id(1) == 0)
    def _():
        for ref in (stf_ref, stb_ref, sf_ref, sb_ref):
            ref[...] = jnp.zeros_like(ref)

    _lockstep([_dn_tile(dqkvf_ref, dgf_ref, dqkvb_ref, dgb_ref, dof_ref, dob_ref, sf_ref, sb_ref),
               _gla_direction(qkf_ref, vf_ref, smf_ref, wg_ref.at[0], bg_ref.at[0], tri_ref.at[0], place_ref.at[0],
                              stf_ref, gof_ref, False),
               _gla_direction(qkb_ref, vb_ref, smb_ref, wg_ref.at[1], bg_ref.at[1], tri_ref.at[1], place_ref.at[1],
                              stb_ref, gob_ref, True)])


def seq_mixers(gla_qk, gla_v, small, w_lr2, b_lr, dn_qkvn, dn_gates, batch, seq):
    hk = GLA_HEADS * GLA_DK
    W = DN_HEADS * DN_DK
    tb = min(GLA_TILE, seq)
    nt = seq // tb
    wg = jnp.zeros((2, 128, hk), jnp.float32)
    for z in range(2):
        wg = wg.at[z, z * GLA_LOWRANK:(z + 1) * GLA_LOWRANK].set(w_lr2[z])
    tri, place = _gla_constants(tb)
    fwd = lambda w: pl.BlockSpec((tb, w), lambda b, i: (b * nt + i, 0))
    bwd = lambda w: pl.BlockSpec((tb, w), lambda b, i: (b * nt + nt - 1 - i, 0))
    const = lambda a: pl.BlockSpec(a.shape, lambda b, i: (0,) * a.ndim)
    out = jax.ShapeDtypeStruct((batch * seq, GLA_WIDTH), jnp.float32)
    bg = b_lr.reshape(2, 1, hk).astype(jnp.float32)
    wg = wg.astype(jnp.bfloat16)
    return pl.pallas_call(
        _seq_mixers_kernel,
        grid=(batch, nt),
        in_specs=[fwd(2 * hk), fwd(GLA_WIDTH), fwd(128), bwd(2 * hk), bwd(GLA_WIDTH), bwd(128),
                  const(wg), const(bg), const(tri), const(place),
                  fwd(DN_QKV), fwd(128), bwd(DN_QKV), bwd(128)],
        out_specs=[fwd(GLA_WIDTH), bwd(GLA_WIDTH), fwd(DN_WIDTH), bwd(DN_WIDTH)],
        out_shape=[out, out, out, out],
        scratch_shapes=[pltpu.VMEM((GLA_WIDTH, hk), jnp.float32), pltpu.VMEM((GLA_WIDTH, hk), jnp.float32),
                        pltpu.VMEM((W, DN_WIDTH), jnp.float32), pltpu.VMEM((W, DN_WIDTH), jnp.float32)],
        compiler_params=pltpu.CompilerParams(dimension_semantics=("arbitrary", "arbitrary"),
                                             vmem_limit_bytes=VMEM_LIMIT_BYTES),
        name="seq_mixers",
    )(gla_qk, gla_v, small, gla_qk, gla_v, small, wg, bg, tri, place, dn_qkvn, dn_gates, dn_qkvn, dn_gates)


def _dn_kernel(qkvf_ref, gf_ref, qkvb_ref, gb_ref, of_ref, ob_ref, sf_ref, sb_ref):
    @pl.when(pl.program_id(1) == 0)
    def _():
        sf_ref[...] = jnp.zeros_like(sf_ref)
        sb_ref[...] = jnp.zeros_like(sb_ref)

    _lockstep([_dn_tile(qkvf_ref, gf_ref, qkvb_ref, gb_ref, of_ref, ob_ref, sf_ref, sb_ref)])


def dn_scan(qkv, gates, batch, seq):
    tb = min(DN_TILE, seq)
    nt = seq // tb
    W = DN_HEADS * DN_DK
    fwd = lambda w: pl.BlockSpec((tb, w), lambda b, i: (b * nt + i, 0))
    bwd = lambda w: pl.BlockSpec((tb, w), lambda b, i: (b * nt + nt - 1 - i, 0))
    out = jax.ShapeDtypeStruct((batch * seq, DN_WIDTH), jnp.float32)
    return pl.pallas_call(
        _dn_kernel,
        grid=(batch, nt),
        in_specs=[fwd(DN_QKV), fwd(128), bwd(DN_QKV), bwd(128)],
        out_specs=[fwd(DN_WIDTH), bwd(DN_WIDTH)],
        out_shape=[out, out],
        scratch_shapes=[pltpu.VMEM((W, DN_WIDTH), jnp.float32), pltpu.VMEM((W, DN_WIDTH), jnp.float32)],
        compiler_params=pltpu.CompilerParams(dimension_semantics=("arbitrary", "arbitrary"),
                                             vmem_limit_bytes=VMEM_LIMIT_BYTES),
        name="dn_scan",
    )(qkv, gates, qkv, gates)


def _gated_group_norm(of_ref, ob_ref, gate_ref, gn_ref, ind):
    o = of_ref[...] + ob_ref[...]
    ms = _split_dot(o * o, ind) * (1.0 / HEAD_DIM)
    gate = gate_ref[...]
    return (o * lax.rsqrt(ms + EPS) * gn_ref[...] * (gate * (1.0 / (1.0 + jnp.exp(-gate))))).astype(jnp.bfloat16)


def _out_mlp_kernel(glaf_ref, glab_ref, glag_ref, swa_ref, dnf_ref, dnb_ref, dnz_ref, x_ref, wo_ref,
                    gng_ref, gnd_ref, g1_ref, g2_ref, w1_ref, w2_ref, g3_ref, o_ref):
    ind = _group_indicator(GLA_WIDTH, HEAD_DIM)
    y_gla = _gated_group_norm(glaf_ref, glab_ref, glag_ref, gng_ref, ind)
    y_dn = _gated_group_norm(dnf_ref, dnb_ref, dnz_ref, gnd_ref, ind)
    a, b = GLA_WIDTH, GLA_WIDTH + SWA_WIDTH
    m = (jnp.dot(y_gla, wo_ref[0:a, :], preferred_element_type=jnp.float32) +
         jnp.dot(swa_ref[...], wo_ref[a:b, :], preferred_element_type=jnp.float32) +
         jnp.dot(y_dn, wo_ref[b:, :], preferred_element_type=jnp.float32))
    x1 = x_ref[...] + _rms(m, g1_ref[...])
    h = _rms(x1, g2_ref[...]).astype(jnp.bfloat16)
    acc = jnp.zeros_like(x1)
    for c in range(D_FF // FF_CHUNK):
        f = jnp.dot(h, w1_ref[:, c * FF_CHUNK:(c + 1) * FF_CHUNK], preferred_element_type=jnp.float32)
        f = jnp.square(jnp.maximum(f, 0.0)).astype(jnp.bfloat16)
        acc = acc + jnp.dot(f, w2_ref[c * FF_CHUNK:(c + 1) * FF_CHUNK, :], preferred_element_type=jnp.float32)
    o_ref[...] = x1 + _rms(acc, g3_ref[...])


def out_mlp(gla_f, gla_b, gla_g, swa_o, dn_f, dn_b, dn_z, x2d, wo, gla_norm_g, dn_norm_g, g1, g2, w1, w2, g3):
    n, d = x2d.shape
    tm = min(ROW_TILE, n)
    row = lambda w: pl.BlockSpec((tm, w), lambda i: (i, 0))
    vec = _const_spec((1, d))
    hvec = _const_spec((1, GLA_WIDTH))
    tile_gain = lambda g, h: jnp.tile(g.astype(jnp.float32), h).reshape(1, -1)
    return pl.pallas_call(
        _out_mlp_kernel,
        grid=(n // tm,),
        in_specs=[row(GLA_WIDTH), row(GLA_WIDTH), row(GLA_WIDTH), row(SWA_WIDTH), row(DN_WIDTH), row(DN_WIDTH),
                  row(DN_WIDTH), row(d), _const_spec((d, d)), hvec, hvec, vec, vec,
                  _const_spec((d, D_FF)), _const_spec((D_FF, d)), vec],
        out_specs=row(d),
        out_shape=jax.ShapeDtypeStruct((n, d), jnp.float32),
        compiler_params=pltpu.CompilerParams(dimension_semantics=("arbitrary",),
                                             vmem_limit_bytes=VMEM_LIMIT_BYTES),
        name="out_mlp",
    )(gla_f, gla_b, gla_g, swa_o, dn_f, dn_b, dn_z, x2d, wo, tile_gain(gla_norm_g, GLA_HEADS),
      tile_gain(dn_norm_g, DN_HEADS), g1.reshape(1, d), g2.reshape(1, d), w1, w2, g3.reshape(1, d))


def _rmsnorm(x, g):
    xf = x.astype(jnp.float32)
    y = xf * lax.rsqrt(jnp.mean(xf * xf, axis=-1, keepdims=True) + EPS)
    return (y * g.astype(jnp.float32)).astype(x.dtype)


def _l2norm(x):
    return x * lax.rsqrt(jnp.sum(x * x, axis=-1, keepdims=True) + EPS)


def _split_heads(x, h):
    B, T, _ = x.shape
    return x.reshape(B, T, h, -1).transpose(0, 2, 1, 3)


def _merge_heads(x):
    B, H, T, d = x.shape
    return x.transpose(0, 2, 1, 3).reshape(B, T, H * d)


def _split_columns(proj):
    idx, acc = [], 0
    for s in IN_SIZES[:-1]:
        acc += s
        idx.append(acc)
    return jnp.split(proj, idx, axis=-1)


def _gla_chunk_scan(q, k, v, gk):
    B, H, T, dk = q.shape
    dv = v.shape[-1]
    C = GLA_CHUNK
    N = T // C
    q, k, v, gk = [a.reshape(B, H, N, C, a.shape[-1]) for a in (q, k, v, gk)]
    b = jnp.cumsum(gk, axis=3)
    causal = jnp.tril(jnp.ones((C, C), dtype=bool))
    diff = jnp.where(causal[:, :, None], b[..., :, None, :] - b[..., None, :, :], -jnp.inf)
    A = jnp.einsum('bhnid,bhnjd,bhnijd->bhnij', q, k, jnp.exp(diff))
    o_intra = jnp.einsum('bhnij,bhnjv->bhniv', A, v)
    b_last = b[..., -1, :]
    k_dec = k * jnp.exp(b_last[..., None, :] - b)
    upd = jnp.einsum('bhncd,bhncv->nbhdv', k_dec, v)
    dec = jnp.moveaxis(jnp.exp(b_last), 2, 0)

    def step(S, inp):
        d, u = inp
        return d[..., None] * S + u, S

    _, S_prev = lax.scan(step, jnp.zeros((B, H, dk, dv), q.dtype), (dec, upd))
    o_inter = jnp.einsum('bhncd,nbhdv->bhncv', q * jnp.exp(b), S_prev)
    return (o_intra + o_inter).reshape(B, H, T, dv)


def _gla_mixer(q, k, v, g, lr, w_lr2, b_lr, norm_g):
    B, T, _ = q.shape
    lr = lr.reshape(B, T, 2, GLA_LOWRANK)
    logits = jnp.einsum('btzr,zrk->zbtk', lr, w_lr2) + b_lr[:, None, None, :]
    gk = jax.nn.log_sigmoid(logits) / GLA_GATE_NORMALIZER
    gk = gk.reshape(2, B, T, GLA_HEADS, GLA_DK).transpose(0, 1, 3, 2, 4)
    qh = _split_heads(q, GLA_HEADS) * (GLA_DK ** -0.5)
    kh = _split_heads(k, GLA_HEADS)
    vh = _split_heads(v, GLA_HEADS)
    o_f = _gla_chunk_scan(qh, kh, vh, gk[0])
    fl = lambda a: jnp.flip(a, axis=2)
    o_b = fl(_gla_chunk_scan(fl(qh), fl(kh), fl(vh), fl(gk[1])))
    o = _rmsnorm(o_f + o_b, norm_g)
    return _merge_heads(o) * jax.nn.silu(g)


def _short_conv(x, w):
    K = w.shape[0]
    p = K // 2
    T = x.shape[1]
    xp = jnp.pad(x, ((0, 0), (p, p), (0, 0)))
    acc = xp[:, 0:T] * w[0]
    for j in range(1, K):
        acc = acc + xp[:, j:j + T] * w[j]
    return acc


def _gated_delta_chunk(q, k, v, g, beta):
    B, H, T, dk = q.shape
    dv = v.shape[-1]
    C = DN_CHUNK
    N = T // C
    q, k, v = [a.reshape(B, H, N, C, a.shape[-1]) for a in (q, k, v)]
    g = g.reshape(B, H, N, C)
    beta = beta.reshape(B, H, N, C)
    b = jnp.cumsum(g, axis=-1)
    tri = jnp.tril(jnp.ones((C, C), dtype=bool))
    strict = jnp.tril(jnp.ones((C, C), dtype=bool), -1)
    L = jnp.exp(jnp.where(tri, b[..., :, None] - b[..., None, :], -jnp.inf))
    kb = k * beta[..., None]
    M = jnp.where(strict, jnp.einsum('bhnid,bhnjd->bhnij', kb, k) * L, 0.0)
    Amat = M + jnp.eye(C, dtype=M.dtype)
    rhs = jnp.concatenate([v * beta[..., None], kb * jnp.exp(b)[..., None]], axis=-1)
    sol = lax.linalg.triangular_solve(Amat, rhs, left_side=True, lower=True, unit_diagonal=True)
    u, w = sol[..., :dv], sol[..., dv:]
    Aqk = jnp.where(tri, jnp.einsum('bhnid,bhnjd->bhnij', q, k) * L, 0.0)
    q_dec = q * jnp.exp(b)[..., None]
    k_dec = k * jnp.exp(b[..., -1:] - b)[..., None]
    dec_last = jnp.exp(b[..., -1])
    xs = tuple(jnp.moveaxis(a, 2, 0) for a in (u, w, Aqk, q_dec, k_dec, dec_last))

    def step(S, inp):
        uc, wc, Ac, qdc, kdc, dl = inp
        v_new = uc - jnp.einsum('bhcd,bhdv->bhcv', wc, S)
        o = jnp.einsum('bhcd,bhdv->bhcv', qdc, S) + jnp.einsum('bhij,bhjv->bhiv', Ac, v_new)
        S = dl[..., None, None] * S + jnp.einsum('bhcd,bhcv->bhdv', kdc, v_new)
        return S, o

    _, o = lax.scan(step, jnp.zeros((B, H, dk, dv), q.dtype), xs)
    return jnp.moveaxis(o, 0, 2).reshape(B, H, T, dv)


def _deltanet_mixer(qkv, z, beta_raw, a_raw, conv_w, a_log, dt_bias, norm_g):
    B, T, _ = qkv.shape
    qkv = jax.nn.silu(_short_conv(qkv, conv_w))
    q, k, v = jnp.split(qkv, [DN_HEADS * DN_DK, 2 * DN_HEADS * DN_DK], axis=-1)
    qh = _l2norm(_split_heads(q, DN_HEADS)) * (DN_DK ** -0.5)
    kh = _l2norm(_split_heads(k, DN_HEADS))
    vh = _split_heads(v, DN_HEADS)
    beta = jax.nn.sigmoid(beta_raw.reshape(B, T, 2, DN_HEADS)).transpose(2, 0, 3, 1)
    a = a_raw.reshape(B, T, 2, DN_HEADS).transpose(2, 0, 1, 3)
    g = -jnp.exp(a_log)[:, None, None, :] * jax.nn.softplus(a + dt_bias[:, None, None, :])
    g = g.transpose(0, 1, 3, 2)
    o_f = _gated_delta_chunk(qh, kh, vh, g[0], beta[0])
    fl = lambda x: jnp.flip(x, axis=2)
    o_b = fl(_gated_delta_chunk(fl(qh), fl(kh), fl(vh), fl(g[1]), fl(beta[1])))
    o = _rmsnorm(o_f + o_b, norm_g)
    return _merge_heads(o) * jax.nn.silu(z)


def _trunk(x, p):
    B, T, D = x.shape
    bias = p["band_bias"]
    x2 = x.reshape(B * T, D)
    for l in range(DEPTH):
        dn_qkvn, small, swa_q, swa_kv, gla_qk, gla_v, gla_g, dn_z, dn_gates = in_proj(
            x2, p["mix_pre_g"][l], p["w_in_packed"][l], p["dn_conv_w"][l], p["dn_a_log"][l], p["dn_dt_bias"][l], T)
        o_swa = swa_attention(swa_q, swa_kv, p["swa_sink"][l], bias, B, T)
        gla_f, gla_b, dn_f, dn_b = seq_mixers(gla_qk, gla_v, small, p["gla_w_lr2"][l], p["gla_b_lr"][l],
                                              dn_qkvn, dn_gates, B, T)
        x2 = out_mlp(gla_f, gla_b, gla_g, o_swa, dn_f, dn_b, dn_z, x2, p["w_out_bf16"][l], p["gla_norm_g"][l],
                     p["dn_norm_g"][l], p["mix_post_g"][l], p["mlp_pre_g"][l],
                     p["mlp_w1_bf16"][l], p["mlp_w2_bf16"][l], p["mlp_post_g"][l])
    return x2.reshape(B, T, D)


def _prepare_params(rel_bias_table, mix_pre_g, w_in, gla_w_lr2, gla_b_lr, gla_norm_g, swa_sink, dn_conv_w, dn_a_log,
                    dn_dt_bias, dn_norm_g, w_out, mix_post_g, mlp_pre_g, mlp_w1, mlp_w2, mlp_post_g):
    return dict(mix_pre_g=mix_pre_g, gla_w_lr2=gla_w_lr2, gla_b_lr=gla_b_lr,
                gla_norm_g=gla_norm_g, swa_sink=swa_sink, dn_conv_w=dn_conv_w, dn_a_log=dn_a_log,
                dn_dt_bias=dn_dt_bias, dn_norm_g=dn_norm_g, mix_post_g=mix_post_g, mlp_pre_g=mlp_pre_g,
                mlp_post_g=mlp_post_g,
                band_bias=band_bias(rel_bias_table),
                w_in_packed=jax.vmap(pack_w_in)(w_in), w_out_bf16=w_out.astype(jnp.bfloat16),
                mlp_w1_bf16=mlp_w1.astype(jnp.bfloat16), mlp_w2_bf16=mlp_w2.astype(jnp.bfloat16))


def kernel(x_prompt, x_sample, rel_bias_table, mix_pre_g, w_in, gla_w_lr2, gla_b_lr, gla_norm_g, swa_sink, dn_conv_w, dn_a_log, dn_dt_bias, dn_norm_g, w_out, mix_post_g, mlp_pre_g, mlp_w1, mlp_w2, mlp_post_g):
    p = _prepare_params(rel_bias_table, mix_pre_g, w_in, gla_w_lr2, gla_b_lr, gla_norm_g, swa_sink, dn_conv_w,
                        dn_a_log, dn_dt_bias, dn_norm_g, w_out, mix_post_g, mlp_pre_g, mlp_w1, mlp_w2, mlp_post_g)
    return (_trunk(x_prompt, p), _trunk(x_sample, p))
```

```python
import functools
import math

import numpy as np

import jax
import jax.numpy as jnp
from jax import lax
from jax.experimental import pallas as pl
from jax.experimental.pallas import tpu as pltpu

D_MODEL = 1024
DEPTH = 4
HEAD_DIM = 64
EPS = 1e-6
GLA_HEADS = 4
GLA_DK = 32
GLA_DV = 64
GLA_LOWRANK = 16
GLA_GATE_NORMALIZER = 16.0
GLA_CHUNK = 32
SWA_HEADS = 8
SWA_KV_HEADS = 2
SWA_WINDOW = 128
SWA_BLOCK = 128
NUM_BUCKETS = 32
MAX_DISTANCE = 128
DN_HEADS = 4
DN_DK = 64
DN_DV = 64
DN_CONV = 5
DN_CHUNK = 64
GLA_WIDTH = GLA_HEADS * GLA_DV
SWA_WIDTH = SWA_HEADS * HEAD_DIM
DN_WIDTH = DN_HEADS * DN_DV
D_MIX = GLA_WIDTH + SWA_WIDTH + DN_WIDTH
D_FF = 4 * D_MODEL
DN_QKV = DN_HEADS * (2 * DN_DK + DN_DV)
IN_SIZES = (GLA_HEADS * GLA_DK, GLA_HEADS * GLA_DK, GLA_WIDTH, GLA_WIDTH, 2 * GLA_LOWRANK,
            SWA_WIDTH, SWA_KV_HEADS * HEAD_DIM, SWA_KV_HEADS * HEAD_DIM,
            DN_QKV, DN_WIDTH, 2 * DN_HEADS, 2 * DN_HEADS)
D_IN = sum(IN_SIZES)

VMEM_LIMIT_BYTES = 56 * 1024 * 1024
ROW_TILE = 512
FF_CHUNK = 1024


def _rms(xf, g):
    return xf * lax.rsqrt(jnp.mean(xf * xf, axis=-1, keepdims=True) + EPS) * g


def _const_spec(shape):
    return pl.BlockSpec(shape, lambda i: (0,) * len(shape), pipeline_mode=pl.Buffered(1))


_O = {}
_acc = 0
for _name, _width in (("dn_qkv", DN_QKV), ("small", 128), ("swa_q", SWA_WIDTH), ("swa_kv", 4 * 2 * HEAD_DIM),
                      ("gla_qk", 2 * GLA_HEADS * GLA_DK), ("gla_v", GLA_WIDTH), ("gla_g", GLA_WIDTH),
                      ("dn_z", DN_WIDTH)):
    _O[_name] = (_acc, _width)
    _acc += _width
D_IN_PACKED = _acc
D_IN_LEAD = _O["small"][0] + _O["small"][1]
_OUT_DTYPES = {"swa_q": jnp.bfloat16, "swa_kv": jnp.bfloat16}


def pack_w_in(w_in):
    offs, acc = [], 0
    for s in IN_SIZES:
        offs.append(acc)
        acc += s
    gq, gk, gv, gg, glr, sq, sk, sv, dqkv, dz, dbeta, da = [w_in[:, o:o + s] for o, s in zip(offs, IN_SIZES)]
    hd = HEAD_DIM
    kv = [sk[:, 0:hd], sk[:, 0:hd], sk[:, hd:], sk[:, hd:], sv[:, 0:hd], sv[:, 0:hd], sv[:, hd:], sv[:, hd:]]
    small = jnp.concatenate([glr, dbeta, da], axis=1)
    small = jnp.pad(small, ((0, 0), (0, 128 - small.shape[1])))
    pieces = dict(dn_qkv=dqkv, small=small, swa_q=sq, swa_kv=jnp.concatenate(kv, axis=1),
                  gla_qk=jnp.concatenate([gq, gk], axis=1), gla_v=gv, gla_g=gg, dn_z=dz)
    return jnp.concatenate([pieces[name] for name in _O], axis=1).astype(jnp.bfloat16)


def _in_proj_kernel(tiles_per_seq, xp_ref, xc_ref, xn_ref, g_ref, w_ref, cw_ref, na_ref, dt_ref, *refs):
    o_refs, gate_ref, ext_ref = refs[:len(_O)], refs[len(_O)], refs[len(_O) + 1]
    tm, H = xc_ref.shape[0], DN_HALO
    i = pl.program_id(0)
    first = (i % tiles_per_seq) == 0
    last = (i % tiles_per_seq) == tiles_per_seq - 1
    x = jnp.concatenate([xp_ref[...], xc_ref[...], xn_ref[...]], axis=0)
    h = _rms(x, g_ref[...])
    y = jnp.dot(h.astype(jnp.bfloat16), w_ref[:, :D_IN_LEAD], preferred_element_type=jnp.float32)
    y_rest = jnp.dot(h[H:H + tm].astype(jnp.bfloat16), w_ref[:, D_IN_LEAD:], preferred_element_type=jnp.float32)
    for (name, (off, width)), o_ref in zip(_O.items(), o_refs):
        if name == "small":
            o_ref[...] = y[H:H + tm, off:off + width]
        elif name != "dn_qkv":
            o_ref[...] = y_rest[:, off - D_IN_LEAD:off - D_IN_LEAD + width].astype(o_ref.dtype)
    off, width = _O["dn_qkv"]
    ext_ref[0:H, :] = jnp.where(first, 0.0, y[0:H, off:off + width])
    ext_ref[H:H + tm, :] = y[H:H + tm, off:off + width]
    ext_ref[H + tm:, :] = jnp.where(last, 0.0, y[H + tm:, off:off + width])
    half = DN_CONV // 2
    acc = ext_ref[H - half:H - half + tm, :] * cw_ref[0:1, :]
    for j in range(1, DN_CONV):
        acc = acc + ext_ref[H - half + j:H - half + j + tm, :] * cw_ref[j:j + 1, :]
    c = acc * (1.0 / (1.0 + jnp.exp(-acc)))
    w = DN_HEADS * DN_DK
    ind = _group_indicator(w, DN_DK)
    sumsq = lambda a: jnp.dot((a * a).astype(jnp.bfloat16), ind, preferred_element_type=jnp.float32)
    q, k = c[:, :w], c[:, w:2 * w]
    qkv_ref = o_refs[list(_O).index("dn_qkv")]
    qkv_ref[:, :w] = q * lax.rsqrt(sumsq(q) + EPS) * (DN_DK ** -0.5)
    qkv_ref[:, w:2 * w] = k * lax.rsqrt(sumsq(k) + EPS)
    qkv_ref[:, 2 * w:] = c[:, 2 * w:]
    off, width = _O["small"]
    s = y[H:H + tm, off:off + width]
    lane = lax.broadcasted_iota(jnp.int32, s.shape, 1)
    beta = 1.0 / (1.0 + jnp.exp(-s))
    a = s + dt_ref[...]
    gdec = na_ref[...] * (jnp.maximum(a, 0.0) + jnp.log1p(jnp.exp(-jnp.abs(a))))
    gate_ref[...] = jnp.where((lane >= DN_BETA_LANE) & (lane < DN_A_LANE), beta,
                              jnp.where((lane >= DN_A_LANE) & (lane < DN_A_LANE + 2 * DN_HEADS), gdec, 0.0))


def in_proj(x2d, g, w_packed, conv_w, a_log, dt_bias, seq):
    n, d = x2d.shape
    tm = min(ROW_TILE, seq)
    hb = tm // DN_HALO
    nh = n // DN_HALO
    lanes = lambda v: jnp.zeros((1, 128), jnp.float32).at[0, DN_A_LANE:DN_A_LANE + 2 * DN_HEADS].set(v.reshape(-1))
    row = lambda w: pl.BlockSpec((tm, w), lambda i: (i, 0))
    return pl.pallas_call(
        functools.partial(_in_proj_kernel, seq // tm),
        grid=(n // tm,),
        in_specs=[pl.BlockSpec((DN_HALO, d), lambda i: (jnp.maximum(i * hb - 1, 0), 0)),
                  row(d),
                  pl.BlockSpec((DN_HALO, d), lambda i: (jnp.minimum((i + 1) * hb, nh - 1), 0)),
                  _const_spec((1, d)),
                  _const_spec((d, D_IN_PACKED)),
                  _const_spec((DN_CONV, DN_QKV)), _const_spec((1, 128)), _const_spec((1, 128))],
        out_specs=[row(w) for _, w in _O.values()] + [row(128)],
        out_shape=[jax.ShapeDtypeStruct((n, w), _OUT_DTYPES.get(name, jnp.float32))
                   for name, (_, w) in _O.items()] + [jax.ShapeDtypeStruct((n, 128), jnp.float32)],
        scratch_shapes=[pltpu.VMEM((tm + 2 * DN_HALO, DN_QKV), jnp.float32)],
        compiler_params=pltpu.CompilerParams(dimension_semantics=("arbitrary",),
                                             vmem_limit_bytes=VMEM_LIMIT_BYTES),
        name="in_proj",
    )(x2d, x2d, x2d, g.reshape(1, d), w_packed, conv_w.astype(jnp.float32),
      lanes(-jnp.exp(a_log.astype(jnp.float32))), lanes(dt_bias.astype(jnp.float32)))


SWA_Q_TILE = 512
SWA_GROUP = SWA_HEADS // SWA_KV_HEADS


def _t5_bucket(rel):
    nb = NUM_BUCKETS // 2
    max_exact = nb // 2
    base = jnp.where(rel > 0, nb, 0)
    n = jnp.abs(rel)
    nf = jnp.maximum(n, 1).astype(jnp.float32)
    large = max_exact + (jnp.log(nf / max_exact) / math.log(MAX_DISTANCE / max_exact)
                         * (nb - max_exact)).astype(jnp.int32)
    large = jnp.minimum(large, nb - 1)
    return base + jnp.where(n < max_exact, n, large)


def _band_bias_kernel(tab_ref, idx_ref, o_ref):
    W = SWA_BLOCK
    idx = idx_ref[...]
    i = lax.broadcasted_iota(jnp.int32, idx.shape, 0)
    j = lax.broadcasted_iota(jnp.int32, idx.shape, 1)
    in_band = jnp.abs(j - W - i) <= SWA_WINDOW
    for h in range(SWA_HEADS):
        acc = jnp.zeros(idx.shape, jnp.float32)
        for b in range(NUM_BUCKETS):
            acc = jnp.where(idx == b, tab_ref[b, h], acc)
        o_ref[h] = jnp.where(in_band, acc, -jnp.inf)


def band_bias(table):
    W = SWA_BLOCK
    rel = jnp.arange(3 * W)[None, :] - W - jnp.arange(W)[:, None]
    idx = _t5_bucket(rel).astype(jnp.int32)
    return pl.pallas_call(
        _band_bias_kernel,
        in_specs=[pl.BlockSpec(memory_space=pltpu.SMEM), pl.BlockSpec(memory_space=pltpu.VMEM)],
        out_specs=pl.BlockSpec(memory_space=pltpu.VMEM),
        out_shape=jax.ShapeDtypeStruct((SWA_HEADS, W, 3 * W), jnp.float32),
        name="band_bias",
    )(table.astype(jnp.float32), idx)


def _swa_kernel(sink_ref, q_ref, kvp_ref, kvc_ref, kvn_ref, bias_ref, o_ref, kv_buf):
    W = SWA_BLOCK
    n_sub = q_ref.shape[0] // W
    n = pl.program_id(1)
    last = pl.num_programs(1) - 1
    kv_buf[0:W, :] = kvp_ref[...]
    kv_buf[W:W + n_sub * W, :] = kvc_ref[...]
    kv_buf[W + n_sub * W:, :] = kvn_ref[...]
    lane = lax.broadcasted_iota(jnp.int32, (W, 2 * HEAD_DIM), 1)
    lo = lane < HEAD_DIM
    key_blk = lax.broadcasted_iota(jnp.int32, (1, 3 * W), 1) // W
    neg = jnp.float32(-jnp.inf)
    for s in range(n_sub):
        edge = jnp.zeros((1, 3 * W), jnp.float32)
        if s == 0:
            edge = jnp.where((key_blk == 0) & (n == 0), neg, edge)
        if s == n_sub - 1:
            edge = jnp.where((key_blk == 2) & (n == last), neg, edge)
        rows = slice(s * W, (s + 1) * W)
        for g in range(SWA_KV_HEADS):
            kd = kv_buf[s * W:(s + 3) * W, g * 128:(g + 1) * 128]
            vd = kv_buf[s * W:(s + 3) * W, 256 + g * 128:256 + (g + 1) * 128]
            vd1 = jnp.concatenate([vd, jnp.ones_like(vd)], axis=1)
            qs = []
            for pair in range(SWA_GROUP // 2):
                col = (g * SWA_GROUP // 2 + pair) * 128
                qp = q_ref[rows, col:col + 128] * (HEAD_DIM ** -0.5)
                qs += [jnp.where(lo, qp, 0), jnp.where(lo, 0, qp)]
            q_stack = jnp.concatenate(qs, axis=0)
            s_all = lax.dot_general(q_stack, kd, (((1,), (1,)), ((), ())),
                                    preferred_element_type=jnp.float32)
            ps, sink_terms = [], []
            for hh in range(SWA_GROUP):
                h = g * SWA_GROUP + hh
                sc = s_all[hh * W:(hh + 1) * W] + bias_ref[h]
                if s == 0 or s == n_sub - 1:
                    sc = sc + edge
                sink = sink_ref[h]
                m = jnp.maximum(jnp.max(sc, axis=-1, keepdims=True), sink)
                ps.append(jnp.exp((sc - m).astype(jnp.bfloat16)))
                sink_terms.append(jnp.exp(sink - m))
            o_all = jnp.dot(jnp.concatenate(ps, axis=0), vd1, preferred_element_type=jnp.float32)
            outs = []
            for hh in range(SWA_GROUP):
                blk = o_all[hh * W:(hh + 1) * W]
                outs.append(blk[:, :128] * (1.0 / (blk[:, 128:] + sink_terms[hh])))
            for pair in range(SWA_GROUP // 2):
                col = (g * SWA_GROUP // 2 + pair) * 128
                o_ref[rows, col:col + 128] = jnp.where(lo, outs[2 * pair], outs[2 * pair + 1]).astype(o_ref.dtype)


def swa_attention(q, kv, sink, bias, batch, seq):
    W = SWA_BLOCK
    tq = min(SWA_Q_TILE, seq)
    n_sub = tq // W
    nq = seq // tq
    nb = seq // W
    return pl.pallas_call(
        _swa_kernel,
        grid=(batch, nq),
        in_specs=[pl.BlockSpec(memory_space=pltpu.SMEM),
                  pl.BlockSpec((tq, SWA_WIDTH), lambda b, i: (b * nq + i, 0)),
                  pl.BlockSpec((W, 512), lambda b, i: (b * nb + jnp.maximum(i * n_sub - 1, 0), 0)),
                  pl.BlockSpec((tq, 512), lambda b, i: (b * nq + i, 0)),
                  pl.BlockSpec((W, 512), lambda b, i: (b * nb + jnp.minimum((i + 1) * n_sub, nb - 1), 0)),
                  pl.BlockSpec((SWA_HEADS, W, 3 * W), lambda b, i: (0, 0, 0), pipeline_mode=pl.Buffered(1))],
        out_specs=pl.BlockSpec((tq, SWA_WIDTH), lambda b, i: (b * nq + i, 0)),
        out_shape=jax.ShapeDtypeStruct((batch * seq, SWA_WIDTH), jnp.bfloat16),
        scratch_shapes=[pltpu.VMEM((tq + 2 * W, 512), jnp.bfloat16)],
        compiler_params=pltpu.CompilerParams(dimension_semantics=("arbitrary", "arbitrary"),
                                             vmem_limit_bytes=VMEM_LIMIT_BYTES),
        name="swa",
    )(sink.astype(jnp.float32), q, kv, kv, kv, bias)


GLA_TILE = 256
GLA_SUB = 16
_NT = (((1,), (1,)), ((), ()))
_TN = (((0,), (0,)), ((), ()))


def _seg_cumsum(x, seg, reverse):
    n = x.shape[0]
    rowmod = lax.broadcasted_iota(jnp.int32, x.shape, 0) % seg
    sh = 1
    while sh < seg:
        if reverse:
            x = x + jnp.where(rowmod < seg - sh, pltpu.roll(x, n - sh, 0), 0.0)
        else:
            x = x + jnp.where(rowmod >= sh, pltpu.roll(x, sh, 0), 0.0)
        sh *= 2
    return x


def _log_sigmoid(x):
    return jnp.minimum(x, 0.0) - jnp.log1p(jnp.exp(-jnp.abs(x)))


def _gla_constants(tb):
    S, hk = GLA_SUB, GLA_HEADS * GLA_DK
    r = np.arange(tb)
    same = (r[:, None] // S) == (r[None, :] // S)
    tri = np.stack([same & (r[None, :] <= r[:, None]), same & (r[None, :] >= r[:, None])])
    src = np.arange(S * hk)
    d, h = src // hk, (src % hk) // GLA_DK
    lane = np.arange(128)
    place = np.stack([lane[None, :] == (32 * h + S - d)[:, None], lane[None, :] == (32 * h + d)[:, None]])
    return jnp.asarray(tri, jnp.bfloat16), jnp.asarray(place, jnp.bfloat16)


def _gla_direction(qk_ref, v_ref, sm_ref, wg_ref, bg_ref, tri_ref, place_ref, st_ref, o_ref, reverse):
    tb, S = qk_ref.shape[0], GLA_SUB
    hk = GLA_HEADS * GLA_DK
    logits = jnp.dot(sm_ref[...].astype(jnp.bfloat16), wg_ref[...], preferred_element_type=jnp.float32)
    g = _log_sigmoid(logits + bg_ref[...]) * (1.0 / GLA_GATE_NORMALIZER)
    g_hi = g.astype(jnp.bfloat16)
    g_lo = (g - g_hi.astype(jnp.float32)).astype(jnp.bfloat16)
    cs = (jnp.dot(tri_ref[...], g_hi, preferred_element_type=jnp.float32) +
          jnp.dot(tri_ref[...], g_lo, preferred_element_type=jnp.float32))
    yield
    a = jnp.exp(g)
    q = qk_ref[:, :hk] * (GLA_DK ** -0.5)
    k = qk_ref[:, hk:]
    v = v_ref[...]
    rowmod = lax.broadcasted_iota(jnp.int32, (tb, hk), 0) % S
    has_prev = (rowmod < S - 1) if reverse else (rowmod >= 1)
    kd = k
    ps = [(q * kd).astype(jnp.bfloat16)]
    for d in range(1, S):
        kd = a * jnp.where(has_prev, pltpu.roll(kd, tb - 1 if reverse else 1, 0), 0.0)
        ps.append((q * kd).astype(jnp.bfloat16))
    scores = jnp.dot(jnp.concatenate(ps, axis=1), place_ref[...], preferred_element_type=jnp.float32)
    yield
    head_of_lane = lax.broadcasted_iota(jnp.int32, (S, GLA_WIDTH), 1) // GLA_DV
    zeros = jnp.zeros((S, GLA_WIDTH), jnp.float32)
    acc = {}
    for s in range(tb // S):
        rows = slice(s * S, (s + 1) * S)
        band = pltpu.roll(scores[rows], 0, 1, stride=1, stride_axis=0).astype(jnp.bfloat16)
        v_blk = v[rows]
        pieces = []
        for h in range(GLA_HEADS):
            vh = jnp.where(head_of_lane == h, v_blk, 0.0)
            pieces += [vh, zeros] if reverse else [zeros, vh]
        acc[s] = jnp.dot(band, jnp.concatenate(pieces, axis=0).astype(jnp.bfloat16),
                         preferred_element_type=jnp.float32)
        if s % 2 == 1:
            yield
    qd = (q * jnp.exp(cs)).astype(jnp.bfloat16)
    blockdiag = (lax.broadcasted_iota(jnp.int32, (GLA_WIDTH, hk), 0) // GLA_DV ==
                 lax.broadcasted_iota(jnp.int32, (GLA_WIDTH, hk), 1) // GLA_DK)
    n_blk = tb // S
    order = range(n_blk - 1, -1, -1) if reverse else range(n_blk)
    dec, upd = {}, {}
    for s in order:
        rows = slice(s * S, (s + 1) * S)
        cs_s = cs[rows]
        edge = cs_s[0:1] if reverse else cs_s[S - 1:S]
        kt = (k[rows] * jnp.exp(edge - cs_s)).astype(jnp.bfloat16)
        dec[s] = jnp.exp(edge)
        upd[s] = jnp.where(blockdiag, lax.dot_general(v[rows].astype(jnp.bfloat16), kt, _TN,
                                                      preferred_element_type=jnp.float32), 0.0)
        yield
    st = st_ref[...]
    for s in order:
        rows = slice(s * S, (s + 1) * S)
        o_ref[rows, :] = acc[s] + lax.dot_general(qd[rows], st.astype(jnp.bfloat16), _NT,
                                                  preferred_element_type=jnp.float32)
        st = st * dec[s] + upd[s]
        yield
    st_ref[...] = st


def _gla_kernel(qkf_ref, vf_ref, smf_ref, qkb_ref, vb_ref, smb_ref, wg_ref, bg_ref, tri_ref, place_ref,
                of_ref, ob_ref, stf_ref, stb_ref):
    @pl.when(pl.program_id(1) == 0)
    def _():
        stf_ref[...] = jnp.zeros_like(stf_ref)
        stb_ref[...] = jnp.zeros_like(stb_ref)

    _lockstep([_gla_direction(qkf_ref, vf_ref, smf_ref, wg_ref.at[0], bg_ref.at[0], tri_ref.at[0], place_ref.at[0],
                              stf_ref, of_ref, False),
               _gla_direction(qkb_ref, vb_ref, smb_ref, wg_ref.at[1], bg_ref.at[1], tri_ref.at[1], place_ref.at[1],
                              stb_ref, ob_ref, True)])


def gla_scan(gla_qk, gla_v, small, w_lr2, b_lr, batch, seq):
    hk = GLA_HEADS * GLA_DK
    tb = min(GLA_TILE, seq)
    nt = seq // tb
    wg = jnp.zeros((2, 128, hk), jnp.float32)
    for z in range(2):
        wg = wg.at[z, z * GLA_LOWRANK:(z + 1) * GLA_LOWRANK].set(w_lr2[z])
    tri, place = _gla_constants(tb)
    fwd = lambda w: pl.BlockSpec((tb, w), lambda b, i: (b * nt + i, 0))
    bwd = lambda w: pl.BlockSpec((tb, w), lambda b, i: (b * nt + nt - 1 - i, 0))
    const = lambda a: pl.BlockSpec(a.shape, lambda b, i: (0,) * a.ndim)
    out = jax.ShapeDtypeStruct((batch * seq, GLA_WIDTH), jnp.float32)
    return pl.pallas_call(
        _gla_kernel,
        grid=(batch, nt),
        in_specs=[fwd(2 * hk), fwd(GLA_WIDTH), fwd(128), bwd(2 * hk), bwd(GLA_WIDTH), bwd(128),
                  pl.BlockSpec((2, 128, hk), lambda b, i: (0, 0, 0)),
                  pl.BlockSpec((2, 1, hk), lambda b, i: (0, 0, 0)), const(tri), const(place)],
        out_specs=[fwd(GLA_WIDTH), bwd(GLA_WIDTH)],
        out_shape=[out, out],
        scratch_shapes=[pltpu.VMEM((GLA_WIDTH, hk), jnp.float32), pltpu.VMEM((GLA_WIDTH, hk), jnp.float32)],
        compiler_params=pltpu.CompilerParams(dimension_semantics=("arbitrary", "arbitrary"),
                                             vmem_limit_bytes=VMEM_LIMIT_BYTES),
        name="gla",
    )(gla_qk, gla_v, small, gla_qk, gla_v, small, wg.astype(jnp.bfloat16), b_lr.reshape(2, 1, hk).astype(jnp.float32),
      tri, place)


DN_PREP_TILE = 512
DN_HALO = 8
DN_TILE = 256
DN_BETA_LANE = 2 * GLA_LOWRANK
DN_A_LANE = DN_BETA_LANE + 2 * DN_HEADS


def _split_dot(x, w_bf16):
    hi = x.astype(jnp.bfloat16)
    lo = (x - hi.astype(jnp.float32)).astype(jnp.bfloat16)
    return (jnp.dot(hi, w_bf16, preferred_element_type=jnp.float32) +
            jnp.dot(lo, w_bf16, preferred_element_type=jnp.float32))


def _group_indicator(n, group):
    r = lax.broadcasted_iota(jnp.int32, (n, n), 0) // group
    c = lax.broadcasted_iota(jnp.int32, (n, n), 1) // group
    return (r == c).astype(jnp.bfloat16)


def _dn_prep_kernel(xp_ref, xc_ref, xn_ref, sm_ref, cw_ref, na_ref, dt_ref, qkv_ref, gate_ref, ext_ref):
    tm = xc_ref.shape[0]
    i = pl.program_id(1)
    last = pl.num_programs(1) - 1
    H = DN_HALO
    ext_ref[0:H, :] = jnp.where(i == 0, 0.0, xp_ref[...])
    ext_ref[H:H + tm, :] = xc_ref[...]
    ext_ref[H + tm:, :] = jnp.where(i == last, 0.0, xn_ref[...])
    half = DN_CONV // 2
    acc = ext_ref[H - half:H - half + tm, :] * cw_ref[0:1, :]
    for j in range(1, DN_CONV):
        acc = acc + ext_ref[H - half + j:H - half + j + tm, :] * cw_ref[j:j + 1, :]
    y = acc * (1.0 / (1.0 + jnp.exp(-acc)))
    ind = _group_indicator(DN_HEADS * DN_DK, DN_DK)
    w = DN_HEADS * DN_DK
    q, k = y[:, :w], y[:, w:2 * w]
    qkv_ref[:, :w] = q * lax.rsqrt(_split_dot(q * q, ind) + EPS) * (DN_DK ** -0.5)
    qkv_ref[:, w:2 * w] = k * lax.rsqrt(_split_dot(k * k, ind) + EPS)
    qkv_ref[:, 2 * w:] = y[:, 2 * w:]
    s = sm_ref[...]
    lane = lax.broadcasted_iota(jnp.int32, s.shape, 1)
    beta = 1.0 / (1.0 + jnp.exp(-s))
    a = s + dt_ref[...]
    g = na_ref[...] * (jnp.maximum(a, 0.0) + jnp.log1p(jnp.exp(-jnp.abs(a))))
    gate_ref[...] = jnp.where((lane >= DN_BETA_LANE) & (lane < DN_A_LANE), beta,
                              jnp.where((lane >= DN_A_LANE) & (lane < DN_A_LANE + 2 * DN_HEADS), g, 0.0))


def dn_prep(dn_qkv, small, conv_w, a_log, dt_bias, batch, seq):
    tm = min(DN_PREP_TILE, seq)
    nt = seq // tm
    hb = tm // DN_HALO
    nh = seq // DN_HALO
    pad = lambda x: jnp.zeros((1, 128), jnp.float32).at[0, DN_A_LANE:DN_A_LANE + 2 * DN_HEADS].set(x.reshape(-1))
    row = lambda w: pl.BlockSpec((tm, w), lambda b, i: (b * nt + i, 0))
    return pl.pallas_call(
        _dn_prep_kernel,
        grid=(batch, nt),
        in_specs=[pl.BlockSpec((DN_HALO, DN_QKV), lambda b, i: (b * nh + jnp.maximum(i * hb - 1, 0), 0)),
                  row(DN_QKV),
                  pl.BlockSpec((DN_HALO, DN_QKV), lambda b, i: (b * nh + jnp.minimum((i + 1) * hb, nh - 1), 0)),
                  row(128),
                  pl.BlockSpec((DN_CONV, DN_QKV), lambda b, i: (0, 0)),
                  pl.BlockSpec((1, 128), lambda b, i: (0, 0)),
                  pl.BlockSpec((1, 128), lambda b, i: (0, 0))],
        out_specs=[row(DN_QKV), row(128)],
        out_shape=[jax.ShapeDtypeStruct((batch * seq, DN_QKV), jnp.float32),
                   jax.ShapeDtypeStruct((batch * seq, 128), jnp.float32)],
        scratch_shapes=[pltpu.VMEM((tm + 2 * DN_HALO, DN_QKV), jnp.float32)],
        compiler_params=pltpu.CompilerParams(dimension_semantics=("arbitrary", "arbitrary"),
                                             vmem_limit_bytes=VMEM_LIMIT_BYTES),
        name="dn_prep",
    )(dn_qkv, dn_qkv, dn_qkv, small, conv_w.astype(jnp.float32), pad(-jnp.exp(a_log.astype(jnp.float32))),
      pad(dt_bias.astype(jnp.float32)))


def _mm(a, b, dims=None):
    a, b = a.astype(jnp.bfloat16), b.astype(jnp.bfloat16)
    if dims is None:
        return jnp.dot(a, b, preferred_element_type=jnp.float32)
    return lax.dot_general(a, b, dims, preferred_element_type=jnp.float32)


def _head_blockdiag(x):
    c, hw = x.shape
    w = hw // DN_HEADS
    stacked = jnp.concatenate([x] * DN_HEADS, axis=0)
    r = lax.broadcasted_iota(jnp.int32, stacked.shape, 0) // c
    l = lax.broadcasted_iota(jnp.int32, stacked.shape, 1) // w
    return jnp.where(r == l, stacked, 0.0)


def _dn_chunk_prepare(qkv, gate, direction):
    C, W = DN_CHUNK, DN_HEADS * DN_DK
    reverse = direction == 1
    q, k, v = qkv[:, :W], qkv[:, W:2 * W], qkv[:, 2 * W:]
    lane = lax.broadcasted_iota(jnp.int32, (C, 128), 1)
    g_lo = DN_A_LANE + direction * DN_HEADS
    b_lo = DN_BETA_LANE + direction * DN_HEADS
    cs = _seg_cumsum(jnp.where((lane >= g_lo) & (lane < g_lo + DN_HEADS), gate, 0.0), C, reverse)
    beta = jnp.where((lane >= b_lo) & (lane < b_lo + DN_HEADS), gate, 0.0)
    src = lax.broadcasted_iota(jnp.int32, (128, W), 0)
    dst = lax.broadcasted_iota(jnp.int32, (128, W), 1) // DN_DK
    expand = ((src - g_lo == dst) | (src - b_lo == dst)).astype(jnp.bfloat16)
    both = _split_dot(jnp.concatenate([cs, beta], axis=0), expand)
    bcol, bexp = both[:C], both[C:]
    yield
    row =lax.broadcasted_iota(jnp.int32, (C, W), 0)
    col = lax.broadcasted_iota(jnp.int32, (C, W), 1) % C
    diag = jnp.where(row == col, bcol, 0.0)
    diag_hi = diag.astype(jnp.bfloat16)
    diag_lo = (diag - diag_hi.astype(jnp.float32)).astype(jnp.bfloat16)
    ones = jnp.ones((C, C), jnp.bfloat16)
    brow = (jnp.dot(ones, diag_hi, preferred_element_type=jnp.float32) +
            jnp.dot(ones, diag_lo, preferred_element_type=jnp.float32))
    incl = (col >= row) if reverse else (col <= row)
    strict = (col > row) if reverse else (col < row)
    lmat = jnp.exp(jnp.where(incl, bcol - brow, -jnp.inf))
    yield
    kb = k * bexp
    kk = _mm(jnp.concatenate([kb, q], axis=0), _head_blockdiag(k), _NT)
    m = jnp.where(strict, kk[:C] * lmat, 0.0)
    aqk = jnp.where(incl, kk[C:] * lmat, 0.0)
    yield
    t = jnp.where(row == col, 1.0, 0.0)
    s = 1
    while s < C:
        same = (row // (2 * s)) == (col // (2 * s))
        r_hi, c_hi = (row % (2 * s)) >= s, (col % (2 * s)) >= s
        off = same & ((c_hi & ~r_hi) if reverse else (r_hi & ~c_hi))
        m_off = jnp.where(off, m, 0.0)
        if s == 1:
            t = t - m_off
        else:
            x = _mm(m_off, _head_blockdiag(t))
            yield
            t = t - _mm(t, _head_blockdiag(x))
            yield
        s *= 2
    ecs = jnp.exp(bcol)
    u = _mm(t, _head_blockdiag(v * bexp))
    w = _mm(t, _head_blockdiag(kb * ecs))
    yield
    edge = bcol[0:1] if reverse else bcol[C - 1:C]
    k_dec = k * jnp.exp(edge - bcol)
    bd = (lax.broadcasted_iota(jnp.int32, (W, W), 0) // DN_DK ==
          lax.broadcasted_iota(jnp.int32, (W, W), 1) // DN_DV)
    kwu = _mm(k_dec, jnp.concatenate([w, u], axis=1), _TN)
    kw = jnp.where(bd, kwu[:, :W], 0.0)
    ku = jnp.where(bd, kwu[:, W:], 0.0)
    yield
    q_eff = q * ecs - _mm(aqk, _head_blockdiag(w))
    o_own = _mm(aqk, _head_blockdiag(u))
    yield
    return dict(kw=kw, ku=ku, q_eff=q_eff, o_own=o_own, dl=jnp.exp(edge))


def _lockstep_stages(generators):
    results = [None] * len(generators)
    live = list(range(len(generators)))
    while live:
        still = []
        for i in live:
            try:
                next(generators[i])
                still.append(i)
            except StopIteration as stop:
                results[i] = stop.value
        live = still
        yield
    return results


def _lockstep(generators):
    stages = _lockstep_stages(generators)
    while True:
        try:
            next(stages)
        except StopIteration as stop:
            return stop.value


def _dn_state_chain(prepared, s_ref, o_ref, reverse):
    C = DN_CHUNK
    n_chunks = len(prepared)
    s = s_ref[...]
    for c in (range(n_chunks - 1, -1, -1) if reverse else range(n_chunks)):
        pc = prepared[c]
        both = _mm(jnp.concatenate([pc["q_eff"], pc["kw"]], axis=0), s)
        o_ref[c * C:(c + 1) * C, :] = both[:C] + pc["o_own"]
        s = s * pc["dl"] - both[C:] + pc["ku"]
        yield
    s_ref[...] = s


def _dn_tile(qkvf_ref, gf_ref, qkvb_ref, gb_ref, of_ref, ob_ref, sf_ref, sb_ref):
    C = DN_CHUNK
    n_chunks = qkvf_ref.shape[0] // C
    rows = lambda c: slice(c * C, (c + 1) * C)
    prepared = yield from _lockstep_stages(
        [_dn_chunk_prepare(qkvf_ref[rows(c), :], gf_ref[rows(c), :], 0) for c in range(n_chunks)] +
        [_dn_chunk_prepare(qkvb_ref[rows(c), :], gb_ref[rows(c), :], 1) for c in range(n_chunks)])
    yield from _lockstep_stages([_dn_state_chain(prepared[:n_chunks], sf_ref, of_ref, False),
                                 _dn_state_chain(prepared[n_chunks:], sb_ref, ob_ref, True)])


def _seq_mixers_kernel(qkf_ref, vf_ref, smf_ref, qkb_ref, vb_ref, smb_ref, wg_ref, bg_ref, tri_ref, place_ref,
                       dqkvf_ref, dgf_ref, dqkvb_ref, dgb_ref,
                       gof_ref, gob_ref, dof_ref, dob_ref, stf_ref, stb_ref, sf_ref, sb_ref):
    @pl.when(pl.program_id(1) == 0)
    def _():
        for ref in (stf_ref, stb_ref, sf_ref, sb_ref):
            ref[...] = jnp.zeros_like(ref)

    _lockstep([_dn_tile(dqkvf_ref, dgf_ref, dqkvb_ref, dgb_ref, dof_ref, dob_ref, sf_ref, sb_ref),
               _gla_direction(qkf_ref, vf_ref, smf_ref, wg_ref.at[0], bg_ref.at[0], tri_ref.at[0], place_ref.at[0],
                              stf_ref, gof_ref, False),
               _gla_direction(qkb_ref, vb_ref, smb_ref, wg_ref.at[1], bg_ref.at[1], tri_ref.at[1], place_ref.at[1],
                              stb_ref, gob_ref, True)])


def seq_mixers(gla_qk, gla_v, small, w_lr2, b_lr, dn_qkvn, dn_gates, batch, seq):
    hk = GLA_HEADS * GLA_DK
    W = DN_HEADS * DN_DK
    tb = min(GLA_TILE, seq)
    nt = seq // tb
    wg = jnp.zeros((2, 128, hk), jnp.float32)
    for z in range(2):
        wg = wg.at[z, z * GLA_LOWRANK:(z + 1) * GLA_LOWRANK].set(w_lr2[z])
    tri, place = _gla_constants(tb)
    fwd = lambda w: pl.BlockSpec((tb, w), lambda b, i: (b * nt + i, 0))
    bwd = lambda w: pl.BlockSpec((tb, w), lambda b, i: (b * nt + nt - 1 - i, 0))
    const = lambda a: pl.BlockSpec(a.shape, lambda b, i: (0,) * a.ndim)
    out = jax.ShapeDtypeStruct((batch * seq, GLA_WIDTH), jnp.float32)
    bg = b_lr.reshape(2, 1, hk).astype(jnp.float32)
    wg = wg.astype(jnp.bfloat16)
    return pl.pallas_call(
        _seq_mixers_kernel,
        grid=(batch, nt),
        in_specs=[fwd(2 * hk), fwd(GLA_WIDTH), fwd(128), bwd(2 * hk), bwd(GLA_WIDTH), bwd(128),
                  const(wg), const(bg), const(tri), const(place),
                  fwd(DN_QKV), fwd(128), bwd(DN_QKV), bwd(128)],
        out_specs=[fwd(GLA_WIDTH), bwd(GLA_WIDTH), fwd(DN_WIDTH), bwd(DN_WIDTH)],
        out_shape=[out, out, out, out],
        scratch_shapes=[pltpu.VMEM((GLA_WIDTH, hk), jnp.float32), pltpu.VMEM((GLA_WIDTH, hk), jnp.float32),
                        pltpu.VMEM((W, DN_WIDTH), jnp.float32), pltpu.VMEM((W, DN_WIDTH), jnp.float32)],
        compiler_params=pltpu.CompilerParams(dimension_semantics=("arbitrary", "arbitrary"),
                                             vmem_limit_bytes=VMEM_LIMIT_BYTES),
        name="seq_mixers",
    )(gla_qk, gla_v, small, gla_qk, gla_v, small, wg, bg, tri, place, dn_qkvn, dn_gates, dn_qkvn, dn_gates)


def _dn_kernel(qkvf_ref, gf_ref, qkvb_ref, gb_ref, of_ref, ob_ref, sf_ref, sb_ref):
    @pl.when(pl.program_id(1) == 0)
    def _():
        sf_ref[...] = jnp.zeros_like(sf_ref)
        sb_ref[...] = jnp.zeros_like(sb_ref)

    _lockstep([_dn_tile(qkvf_ref, gf_ref, qkvb_ref, gb_ref, of_ref, ob_ref, sf_ref, sb_ref)])


def dn_scan(qkv, gates, batch, seq):
    tb = min(DN_TILE, seq)
    nt = seq // tb
    W = DN_HEADS * DN_DK
    fwd = lambda w: pl.BlockSpec((tb, w), lambda b, i: (b * nt + i, 0))
    bwd = lambda w: pl.BlockSpec((tb, w), lambda b, i: (b * nt + nt - 1 - i, 0))
    out = jax.ShapeDtypeStruct((batch * seq, DN_WIDTH), jnp.float32)
    return pl.pallas_call(
        _dn_kernel,
        grid=(batch, nt),
        in_specs=[fwd(DN_QKV), fwd(128), bwd(DN_QKV), bwd(128)],
        out_specs=[fwd(DN_WIDTH), bwd(DN_WIDTH)],
        out_shape=[out, out],
        scratch_shapes=[pltpu.VMEM((W, DN_WIDTH), jnp.float32), pltpu.VMEM((W, DN_WIDTH), jnp.float32)],
        compiler_params=pltpu.CompilerParams(dimension_semantics=("arbitrary", "arbitrary"),
                                             vmem_limit_bytes=VMEM_LIMIT_BYTES),
        name="dn_scan",
    )(qkv, gates, qkv, gates)


def _gated_group_norm(of_ref, ob_ref, gate_ref, gn_ref, ind):
    o = of_ref[...] + ob_ref[...]
    ms = _split_dot(o * o, ind) * (1.0 / HEAD_DIM)
    gate = gate_ref[...]
    return (o * lax.rsqrt(ms + EPS) * gn_ref[...] * (gate * (1.0 / (1.0 + jnp.exp(-gate))))).astype(jnp.bfloat16)


def _out_mlp_kernel(glaf_ref, glab_ref, glag_ref, swa_ref, dnf_ref, dnb_ref, dnz_ref, x_ref, wo_ref,
                    gng_ref, gnd_ref, g1_ref, g2_ref, w1_ref, w2_ref, g3_ref, o_ref):
    ind = _group_indicator(GLA_WIDTH, HEAD_DIM)
    y_gla = _gated_group_norm(glaf_ref, glab_ref, glag_ref, gng_ref, ind)
    y_dn = _gated_group_norm(dnf_ref, dnb_ref, dnz_ref, gnd_ref, ind)
    a, b = GLA_WIDTH, GLA_WIDTH + SWA_WIDTH
    m = (jnp.dot(y_gla, wo_ref[0:a, :], preferred_element_type=jnp.float32) +
         jnp.dot(swa_ref[...], wo_ref[a:b, :], preferred_element_type=jnp.float32) +
         jnp.dot(y_dn, wo_ref[b:, :], preferred_element_type=jnp.float32))
    x1 = x_ref[...] + _rms(m, g1_ref[...])
    h = _rms(x1, g2_ref[...]).astype(jnp.bfloat16)
    acc = jnp.zeros_like(x1)
    for c in range(D_FF // FF_CHUNK):
        f = jnp.dot(h, w1_ref[:, c * FF_CHUNK:(c + 1) * FF_CHUNK], preferred_element_type=jnp.float32)
        f = jnp.square(jnp.maximum(f, 0.0)).astype(jnp.bfloat16)
        acc = acc + jnp.dot(f, w2_ref[c * FF_CHUNK:(c + 1) * FF_CHUNK, :], preferred_element_type=jnp.float32)
    o_ref[...] = x1 + _rms(acc, g3_ref[...])


def out_mlp(gla_f, gla_b, gla_g, swa_o, dn_f, dn_b, dn_z, x2d, wo, gla_norm_g, dn_norm_g, g1, g2, w1, w2, g3):
    n, d = x2d.shape
    tm = min(ROW_TILE, n)
    row = lambda w: pl.BlockSpec((tm, w), lambda i: (i, 0))
    vec = _const_spec((1, d))
    hvec = _const_spec((1, GLA_WIDTH))
    tile_gain = lambda g, h: jnp.tile(g.astype(jnp.float32), h).reshape(1, -1)
    return pl.pallas_call(
        _out_mlp_kernel,
        grid=(n // tm,),
        in_specs=[row(GLA_WIDTH), row(GLA_WIDTH), row(GLA_WIDTH), row(SWA_WIDTH), row(DN_WIDTH), row(DN_WIDTH),
                  row(DN_WIDTH), row(d), _const_spec((d, d)), hvec, hvec, vec, vec,
                  _const_spec((d, D_FF)), _const_spec((D_FF, d)), vec],
        out_specs=row(d),
        out_shape=jax.ShapeDtypeStruct((n, d), jnp.float32),
        compiler_params=pltpu.CompilerParams(dimension_semantics=("arbitrary",),
                                             vmem_limit_bytes=VMEM_LIMIT_BYTES),
        name="out_mlp",
    )(gla_f, gla_b, gla_g, swa_o, dn_f, dn_b, dn_z, x2d, wo, tile_gain(gla_norm_g, GLA_HEADS),
      tile_gain(dn_norm_g, DN_HEADS), g1.reshape(1, d), g2.reshape(1, d), w1, w2, g3.reshape(1, d))


def _rmsnorm(x, g):
    xf = x.astype(jnp.float32)
    y = xf * lax.rsqrt(jnp.mean(xf * xf, axis=-1, keepdims=True) + EPS)
    return (y * g.astype(jnp.float32)).astype(x.dtype)


def _l2norm(x):
    return x * lax.rsqrt(jnp.sum(x * x, axis=-1, keepdims=True) + EPS)


def _split_heads(x, h):
    B, T, _ = x.shape
    return x.reshape(B, T, h, -1).transpose(0, 2, 1, 3)


def _merge_heads(x):
    B, H, T, d = x.shape
    return x.transpose(0, 2, 1, 3).reshape(B, T, H * d)


def _split_columns(proj):
    idx, acc = [], 0
    for s in IN_SIZES[:-1]:
        acc += s
        idx.append(acc)
    return jnp.split(proj, idx, axis=-1)


def _gla_chunk_scan(q, k, v, gk):
    B, H, T, dk = q.shape
    dv = v.shape[-1]
    C = GLA_CHUNK
    N = T // C
    q, k, v, gk = [a.reshape(B, H, N, C, a.shape[-1]) for a in (q, k, v, gk)]
    b = jnp.cumsum(gk, axis=3)
    causal = jnp.tril(jnp.ones((C, C), dtype=bool))
    diff = jnp.where(causal[:, :, None], b[..., :, None, :] - b[..., None, :, :], -jnp.inf)
    A = jnp.einsum('bhnid,bhnjd,bhnijd->bhnij', q, k, jnp.exp(diff))
    o_intra = jnp.einsum('bhnij,bhnjv->bhniv', A, v)
    b_last = b[..., -1, :]
    k_dec = k * jnp.exp(b_last[..., None, :] - b)
    upd = jnp.einsum('bhncd,bhncv->nbhdv', k_dec, v)
    dec = jnp.moveaxis(jnp.exp(b_last), 2, 0)

    def step(S, inp):
        d, u = inp
        return d[..., None] * S + u, S

    _, S_prev = lax.scan(step, jnp.zeros((B, H, dk, dv), q.dtype), (dec, upd))
    o_inter = jnp.einsum('bhncd,nbhdv->bhncv', q * jnp.exp(b), S_prev)
    return (o_intra + o_inter).reshape(B, H, T, dv)


def _gla_mixer(q, k, v, g, lr, w_lr2, b_lr, norm_g):
    B, T, _ = q.shape
    lr = lr.reshape(B, T, 2, GLA_LOWRANK)
    logits = jnp.einsum('btzr,zrk->zbtk', lr, w_lr2) + b_lr[:, None, None, :]
    gk = jax.nn.log_sigmoid(logits) / GLA_GATE_NORMALIZER
    gk = gk.reshape(2, B, T, GLA_HEADS, GLA_DK).transpose(0, 1, 3, 2, 4)
    qh = _split_heads(q, GLA_HEADS) * (GLA_DK ** -0.5)
    kh = _split_heads(k, GLA_HEADS)
    vh = _split_heads(v, GLA_HEADS)
    o_f = _gla_chunk_scan(qh, kh, vh, gk[0])
    fl = lambda a: jnp.flip(a, axis=2)
    o_b = fl(_gla_chunk_scan(fl(qh), fl(kh), fl(vh), fl(gk[1])))
    o = _rmsnorm(o_f + o_b, norm_g)
    return _merge_heads(o) * jax.nn.silu(g)


def _short_conv(x, w):
    K = w.shape[0]
    p = K // 2
    T = x.shape[1]
    xp = jnp.pad(x, ((0, 0), (p, p), (0, 0)))
    acc = xp[:, 0:T] * w[0]
    for j in range(1, K):
        acc = acc + xp[:, j:j + T] * w[j]
    return acc


def _gated_delta_chunk(q, k, v, g, beta):
    B, H, T, dk = q.shape
    dv = v.shape[-1]
    C = DN_CHUNK
    N = T // C
    q, k, v = [a.reshape(B, H, N, C, a.shape[-1]) for a in (q, k, v)]
    g = g.reshape(B, H, N, C)
    beta = beta.reshape(B, H, N, C)
    b = jnp.cumsum(g, axis=-1)
    tri = jnp.tril(jnp.ones((C, C), dtype=bool))
    strict = jnp.tril(jnp.ones((C, C), dtype=bool), -1)
    L = jnp.exp(jnp.where(tri, b[..., :, None] - b[..., None, :], -jnp.inf))
    kb = k * beta[..., None]
    M = jnp.where(strict, jnp.einsum('bhnid,bhnjd->bhnij', kb, k) * L, 0.0)
    Amat = M + jnp.eye(C, dtype=M.dtype)
    rhs = jnp.concatenate([v * beta[..., None], kb * jnp.exp(b)[..., None]], axis=-1)
    sol = lax.linalg.triangular_solve(Amat, rhs, left_side=True, lower=True, unit_diagonal=True)
    u, w = sol[..., :dv], sol[..., dv:]
    Aqk = jnp.where(tri, jnp.einsum('bhnid,bhnjd->bhnij', q, k) * L, 0.0)
    q_dec = q * jnp.exp(b)[..., None]
    k_dec = k * jnp.exp(b[..., -1:] - b)[..., None]
    dec_last = jnp.exp(b[..., -1])
    xs = tuple(jnp.moveaxis(a, 2, 0) for a in (u, w, Aqk, q_dec, k_dec, dec_last))

    def step(S, inp):
        uc, wc, Ac, qdc, kdc, dl = inp
        v_new = uc - jnp.einsum('bhcd,bhdv->bhcv', wc, S)
        o = jnp.einsum('bhcd,bhdv->bhcv', qdc, S) + jnp.einsum('bhij,bhjv->bhiv', Ac, v_new)
        S = dl[..., None, None] * S + jnp.einsum('bhcd,bhcv->bhdv', kdc, v_new)
        return S, o

    _, o = lax.scan(step, jnp.zeros((B, H, dk, dv), q.dtype), xs)
    return jnp.moveaxis(o, 0, 2).reshape(B, H, T, dv)


def _deltanet_mixer(qkv, z, beta_raw, a_raw, conv_w, a_log, dt_bias, norm_g):
    B, T, _ = qkv.shape
    qkv = jax.nn.silu(_short_conv(qkv, conv_w))
    q, k, v = jnp.split(qkv, [DN_HEADS * DN_DK, 2 * DN_HEADS * DN_DK], axis=-1)
    qh = _l2norm(_split_heads(q, DN_HEADS)) * (DN_DK ** -0.5)
    kh = _l2norm(_split_heads(k, DN_HEADS))
    vh = _split_heads(v, DN_HEADS)
    beta = jax.nn.sigmoid(beta_raw.reshape(B, T, 2, DN_HEADS)).transpose(2, 0, 3, 1)
    a = a_raw.reshape(B, T, 2, DN_HEADS).transpose(2, 0, 1, 3)
    g = -jnp.exp(a_log)[:, None, None, :] * jax.nn.softplus(a + dt_bias[:, None, None, :])
    g = g.transpose(0, 1, 3, 2)
    o_f = _gated_delta_chunk(qh, kh, vh, g[0], beta[0])
    fl = lambda x: jnp.flip(x, axis=2)
    o_b = fl(_gated_delta_chunk(fl(qh), fl(kh), fl(vh), fl(g[1]), fl(beta[1])))
    o = _rmsnorm(o_f + o_b, norm_g)
    return _merge_heads(o) * jax.nn.silu(z)


def _trunk(x, p):
    B, T, D = x.shape
    bias = p["band_bias"]
    x2 = x.reshape(B * T, D)
    for l in range(DEPTH):
        dn_qkvn, small, swa_q, swa_kv, gla_qk, gla_v, gla_g, dn_z, dn_gates = in_proj(
            x2, p["mix_pre_g"][l], p["w_in_packed"][l], p["dn_conv_w"][l], p["dn_a_log"][l], p["dn_dt_bias"][l], T)
        o_swa = swa_attention(swa_q, swa_kv, p["swa_sink"][l], bias, B, T)
        gla_f, gla_b, dn_f, dn_b = seq_mixers(gla_qk, gla_v, small, p["gla_w_lr2"][l], p["gla_b_lr"][l],
                                              dn_qkvn, dn_gates, B, T)
        x2 = out_mlp(gla_f, gla_b, gla_g, o_swa, dn_f, dn_b, dn_z, x2, p["w_out_bf16"][l], p["gla_norm_g"][l],
                     p["dn_norm_g"][l], p["mix_post_g"][l], p["mlp_pre_g"][l],
                     p["mlp_w1_bf16"][l], p["mlp_w2_bf16"][l], p["mlp_post_g"][l])
    return x2.reshape(B, T, D)


def _prepare_params(rel_bias_table, mix_pre_g, w_in, gla_w_lr2, gla_b_lr, gla_norm_g, swa_sink, dn_conv_w, dn_a_log,
                    dn_dt_bias, dn_norm_g, w_out, mix_post_g, mlp_pre_g, mlp_w1, mlp_w2, mlp_post_g):
    return dict(mix_pre_g=mix_pre_g, gla_w_lr2=gla_w_lr2, gla_b_lr=gla_b_lr,
                gla_norm_g=gla_norm_g, swa_sink=swa_sink, dn_conv_w=dn_conv_w, dn_a_log=dn_a_log,
                dn_dt_bias=dn_dt_bias, dn_norm_g=dn_norm_g, mix_post_g=mix_post_g, mlp_pre_g=mlp_pre_g,
                mlp_post_g=mlp_post_g,
                band_bias=band_bias(rel_bias_table),
                w_in_packed=jax.vmap(pack_w_in)(w_in), w_out_bf16=w_out.astype(jnp.bfloat16),
                mlp_w1_bf16=mlp_w1.astype(jnp.bfloat16), mlp_w2_bf16=mlp_w2.astype(jnp.bfloat16))


def kernel(x_prompt, x_sample, rel_bias_table, mix_pre_g, w_in, gla_w_lr2, gla_b_lr, gla_norm_g, swa_sink, dn_conv_w, dn_a_log, dn_dt_bias, dn_norm_g, w_out, mix_post_g, mlp_pre_g, mlp_w1, mlp_w2, mlp_post_g):
    p = _prepare_params(rel_bias_table, mix_pre_g, w_in, gla_w_lr2, gla_b_lr, gla_norm_g, swa_sink, dn_conv_w,
                        dn_a_log, dn_dt_bias, dn_norm_g, w_out, mix_post_g, mlp_pre_g, mlp_w1, mlp_w2, mlp_post_g)
    return (_trunk(x_prompt, p), _trunk(x_sample, p))
```

```python
import functools
import math

import numpy as np

import jax
import jax.numpy as jnp
from jax import lax
from jax.experimental import pallas as pl
from jax.experimental.pallas import tpu as pltpu

D_MODEL = 1024
DEPTH = 4
HEAD_DIM = 64
EPS = 1e-6
GLA_HEADS = 4
GLA_DK = 32
GLA_DV = 64
GLA_LOWRANK = 16
GLA_GATE_NORMALIZER = 16.0
GLA_CHUNK = 32
SWA_HEADS = 8
SWA_KV_HEADS = 2
SWA_WINDOW = 128
SWA_BLOCK = 128
NUM_BUCKETS = 32
MAX_DISTANCE = 128
DN_HEADS = 4
DN_DK = 64
DN_DV = 64
DN_CONV = 5
DN_CHUNK = 64
GLA_WIDTH = GLA_HEADS * GLA_DV
SWA_WIDTH = SWA_HEADS * HEAD_DIM
DN_WIDTH = DN_HEADS * DN_DV
D_MIX = GLA_WIDTH + SWA_WIDTH + DN_WIDTH
D_FF = 4 * D_MODEL
DN_QKV = DN_HEADS * (2 * DN_DK + DN_DV)
IN_SIZES = (GLA_HEADS * GLA_DK, GLA_HEADS * GLA_DK, GLA_WIDTH, GLA_WIDTH, 2 * GLA_LOWRANK,
            SWA_WIDTH, SWA_KV_HEADS * HEAD_DIM, SWA_KV_HEADS * HEAD_DIM,
            DN_QKV, DN_WIDTH, 2 * DN_HEADS, 2 * DN_HEADS)
D_IN = sum(IN_SIZES)

VMEM_LIMIT_BYTES = 56 * 1024 * 1024
ROW_TILE = 512
FF_CHUNK = 1024


def _rms(xf, g):
    return xf * lax.rsqrt(jnp.mean(xf * xf, axis=-1, keepdims=True) + EPS) * g


def _const_spec(shape):
    return pl.BlockSpec(shape, lambda i: (0,) * len(shape), pipeline_mode=pl.Buffered(1))


_O = {}
_acc = 0
for _name, _width in (("dn_qkv", DN_QKV), ("small", 128), ("swa_q", SWA_WIDTH), ("swa_kv", 4 * 2 * HEAD_DIM),
                      ("gla_qk", 2 * GLA_HEADS * GLA_DK), ("gla_v", GLA_WIDTH), ("gla_g", GLA_WIDTH),
                      ("dn_z", DN_WIDTH)):
    _O[_name] = (_acc, _width)
    _acc += _width
D_IN_PACKED = _acc
D_IN_LEAD = _O["small"][0] + _O["small"][1]
_OUT_DTYPES = {"swa_q": jnp.bfloat16, "swa_kv": jnp.bfloat16}


def pack_w_in(w_in):
    offs, acc = [], 0
    for s in IN_SIZES:
        offs.append(acc)
        acc += s
    gq, gk, gv, gg, glr, sq, sk, sv, dqkv, dz, dbeta, da = [w_in[:, o:o + s] for o, s in zip(offs, IN_SIZES)]
    hd = HEAD_DIM
    kv = [sk[:, 0:hd], sk[:, 0:hd], sk[:, hd:], sk[:, hd:], sv[:, 0:hd], sv[:, 0:hd], sv[:, hd:], sv[:, hd:]]
    small = jnp.concatenate([glr, dbeta, da], axis=1)
    small = jnp.pad(small, ((0, 0), (0, 128 - small.shape[1])))
    pieces = dict(dn_qkv=dqkv, small=small, swa_q=sq, swa_kv=jnp.concatenate(kv, axis=1),
                  gla_qk=jnp.concatenate([gq, gk], axis=1), gla_v=gv, gla_g=gg, dn_z=dz)
    return jnp.concatenate([pieces[name] for name in _O], axis=1).astype(jnp.bfloat16)


def _in_proj_kernel(tiles_per_seq, xp_ref, xc_ref, xn_ref, g_ref, w_ref, cw_ref, na_ref, dt_ref, *refs):
    o_refs, gate_ref, ext_ref = refs[:len(_O)], refs[len(_O)], refs[len(_O) + 1]
    tm, H = xc_ref.shape[0], DN_HALO
    i = pl.program_id(0)
    first = (i % tiles_per_seq) == 0
    last = (i % tiles_per_seq) == tiles_per_seq - 1
    x = jnp.concatenate([xp_ref[...], xc_ref[...], xn_ref[...]], axis=0)
    h = _rms(x, g_ref[...])
    y = jnp.dot(h.astype(jnp.bfloat16), w_ref[:, :D_IN_LEAD], preferred_element_type=jnp.float32)
    y_rest = jnp.dot(h[H:H + tm].astype(jnp.bfloat16), w_ref[:, D_IN_LEAD:], preferred_element_type=jnp.float32)
    for (name, (off, width)), o_ref in zip(_O.items(), o_refs):
        if name == "small":
            o_ref[...] = y[H:H + tm, off:off + width]
        elif name != "dn_qkv":
            o_ref[...] = y_rest[:, off - D_IN_LEAD:off - D_IN_LEAD + width].astype(o_ref.dtype)
    off, width = _O["dn_qkv"]
    ext_ref[0:H, :] = jnp.where(first, 0.0, y[0:H, off:off + width])
    ext_ref[H:H + tm, :] = y[H:H + tm, off:off + width]
    ext_ref[H + tm:, :] = jnp.where(last, 0.0, y[H + tm:, off:off + width])
    half = DN_CONV // 2
    acc = ext_ref[H - half:H - half + tm, :] * cw_ref[0:1, :]
    for j in range(1, DN_CONV):
        acc = acc + ext_ref[H - half + j:H - half + j + tm, :] * cw_ref[j:j + 1, :]
    c = acc * (1.0 / (1.0 + jnp.exp(-acc)))
    w = DN_HEADS * DN_DK
    ind = _group_indicator(w, DN_DK)
    sumsq = lambda a: jnp.dot((a * a).astype(jnp.bfloat16), ind, preferred_element_type=jnp.float32)
    q, k = c[:, :w], c[:, w:2 * w]
    qkv_ref = o_refs[list(_O).index("dn_qkv")]
    qkv_ref[:, :w] = q * lax.rsqrt(sumsq(q) + EPS) * (DN_DK ** -0.5)
    qkv_ref[:, w:2 * w] = k * lax.rsqrt(sumsq(k) + EPS)
    qkv_ref[:, 2 * w:] = c[:, 2 * w:]
    off, width = _O["small"]
    s = y[H:H + tm, off:off + width]
    lane = lax.broadcasted_iota(jnp.int32, s.shape, 1)
    beta = 1.0 / (1.0 + jnp.exp(-s))
    a = s + dt_ref[...]
    gdec = na_ref[...] * (jnp.maximum(a, 0.0) + jnp.log1p(jnp.exp(-jnp.abs(a))))
    gate_ref[...] = jnp.where((lane >= DN_BETA_LANE) & (lane < DN_A_LANE), beta,
                              jnp.where((lane >= DN_A_LANE) & (lane < DN_A_LANE + 2 * DN_HEADS), gdec, 0.0))


def in_proj(x2d, g, w_packed, conv_w, a_log, dt_bias, seq):
    n, d = x2d.shape
    tm = min(ROW_TILE, seq)
    hb = tm // DN_HALO
    nh = n // DN_HALO
    lanes = lambda v: jnp.zeros((1, 128), jnp.float32).at[0, DN_A_LANE:DN_A_LANE + 2 * DN_HEADS].set(v.reshape(-1))
    row = lambda w: pl.BlockSpec((tm, w), lambda i: (i, 0))
    return pl.pallas_call(
        functools.partial(_in_proj_kernel, seq // tm),
        grid=(n // tm,),
        in_specs=[pl.BlockSpec((DN_HALO, d), lambda i: (jnp.maximum(i * hb - 1, 0), 0)),
                  row(d),
                  pl.BlockSpec((DN_HALO, d), lambda i: (jnp.minimum((i + 1) * hb, nh - 1), 0)),
                  _const_spec((1, d)),
                  _const_spec((d, D_IN_PACKED)),
                  _const_spec((DN_CONV, DN_QKV)), _const_spec((1, 128)), _const_spec((1, 128))],
        out_specs=[row(w) for _, w in _O.values()] + [row(128)],
        out_shape=[jax.ShapeDtypeStruct((n, w), _OUT_DTYPES.get(name, jnp.float32))
                   for name, (_, w) in _O.items()] + [jax.ShapeDtypeStruct((n, 128), jnp.float32)],
        scratch_shapes=[pltpu.VMEM((tm + 2 * DN_HALO, DN_QKV), jnp.float32)],
        compiler_params=pltpu.CompilerParams(dimension_semantics=("arbitrary",),
                                             vmem_limit_bytes=VMEM_LIMIT_BYTES),
        name="in_proj",
    )(x2d, x2d, x2d, g.reshape(1, d), w_packed, conv_w.astype(jnp.float32),
      lanes(-jnp.exp(a_log.astype(jnp.float32))), lanes(dt_bias.astype(jnp.float32)))


SWA_Q_TILE = 512
SWA_GROUP = SWA_HEADS // SWA_KV_HEADS


def _t5_bucket(rel):
    nb = NUM_BUCKETS // 2
    max_exact = nb // 2
    base = jnp.where(rel > 0, nb, 0)
    n = jnp.abs(rel)
    nf = jnp.maximum(n, 1).astype(jnp.float32)
    large = max_exact + (jnp.log(nf / max_exact) / math.log(MAX_DISTANCE / max_exact)
                         * (nb - max_exact)).astype(jnp.int32)
    large = jnp.minimum(large, nb - 1)
    return base + jnp.where(n < max_exact, n, large)


def _band_bias_kernel(tab_ref, idx_ref, o_ref):
    W = SWA_BLOCK
    idx = idx_ref[...]
    i = lax.broadcasted_iota(jnp.int32, idx.shape, 0)
    j = lax.broadcasted_iota(jnp.int32, idx.shape, 1)
    in_band = jnp.abs(j - W - i) <= SWA_WINDOW
    for h in range(SWA_HEADS):
        acc = jnp.zeros(idx.shape, jnp.float32)
        for b in range(NUM_BUCKETS):
            acc = jnp.where(idx == b, tab_ref[b, h], acc)
        o_ref[h] = jnp.where(in_band, acc, -jnp.inf)


def band_bias(table):
    W = SWA_BLOCK
    rel = jnp.arange(3 * W)[None, :] - W - jnp.arange(W)[:, None]
    idx = _t5_bucket(rel).astype(jnp.int32)
    return pl.pallas_call(
        _band_bias_kernel,
        in_specs=[pl.BlockSpec(memory_space=pltpu.SMEM), pl.BlockSpec(memory_space=pltpu.VMEM)],
        out_specs=pl.BlockSpec(memory_space=pltpu.VMEM),
        out_shape=jax.ShapeDtypeStruct((SWA_HEADS, W, 3 * W), jnp.float32),
        name="band_bias",
    )(table.astype(jnp.float32), idx)


def _swa_kernel(sink_ref, q_ref, kvp_ref, kvc_ref, kvn_ref, bias_ref, o_ref, kv_buf):
    W = SWA_BLOCK
    n_sub = q_ref.shape[0] // W
    n = pl.program_id(1)
    last = pl.num_programs(1) - 1
    kv_buf[0:W, :] = kvp_ref[...]
    kv_buf[W:W + n_sub * W, :] = kvc_ref[...]
    kv_buf[W + n_sub * W:, :] = kvn_ref[...]
    lane = lax.broadcasted_iota(jnp.int32, (W, 2 * HEAD_DIM), 1)
    lo = lane < HEAD_DIM
    key_blk = lax.broadcasted_iota(jnp.int32, (1, 3 * W), 1) // W
    neg = jnp.float32(-jnp.inf)
    for s in range(n_sub):
        edge = jnp.zeros((1, 3 * W), jnp.float32)
        if s == 0:
            edge = jnp.where((key_blk == 0) & (n == 0), neg, edge)
        if s == n_sub - 1:
            edge = jnp.where((key_blk == 2) & (n == last), neg, edge)
        rows = slice(s * W, (s + 1) * W)
        for g in range(SWA_KV_HEADS):
            kd = kv_buf[s * W:(s + 3) * W, g * 128:(g + 1) * 128]
            vd = kv_buf[s * W:(s + 3) * W, 256 + g * 128:256 + (g + 1) * 128]
            vd1 = jnp.concatenate([vd, jnp.ones_like(vd)], axis=1)
            qs = []
            for pair in range(SWA_GROUP // 2):
                col = (g * SWA_GROUP // 2 + pair) * 128
                qp = q_ref[rows, col:col + 128] * (HEAD_DIM ** -0.5)
                qs += [jnp.where(lo, qp, 0), jnp.where(lo, 0, qp)]
            q_stack = jnp.concatenate(qs, axis=0)
            s_all = lax.dot_general(q_stack, kd, (((1,), (1,)), ((), ())),
                                    preferred_element_type=jnp.float32)
            ps, sink_terms = [], []
            for hh in range(SWA_GROUP):
                h = g * SWA_GROUP + hh
                sc = s_all[hh * W:(hh + 1) * W] + bias_ref[h]
                if s == 0 or s == n_sub - 1:
                    sc = sc + edge
                sink = sink_ref[h]
                m = jnp.maximum(jnp.max(sc, axis=-1, keepdims=True), sink)
                ps.append(jnp.exp((sc - m).astype(jnp.bfloat16)))
                sink_terms.append(jnp.exp(sink - m))
            o_all = jnp.dot(jnp.concatenate(ps, axis=0), vd1, preferred_element_type=jnp.float32)
            outs = []
            for hh in range(SWA_GROUP):
                blk = o_all[hh * W:(hh + 1) * W]
                outs.append(blk[:, :128] * (1.0 / (blk[:, 128:] + sink_terms[hh])))
            for pair in range(SWA_GROUP // 2):
                col = (g * SWA_GROUP // 2 + pair) * 128
                o_ref[rows, col:col + 128] = jnp.where(lo, outs[2 * pair], outs[2 * pair + 1]).astype(o_ref.dtype)


def swa_attention(q, kv, sink, bias, batch, seq):
    W = SWA_BLOCK
    tq = min(SWA_Q_TILE, seq)
    n_sub = tq // W
    nq = seq // tq
    nb = seq // W
    return pl.pallas_call(
        _swa_kernel,
        grid=(batch, nq),
        in_specs=[pl.BlockSpec(memory_space=pltpu.SMEM),
                  pl.BlockSpec((tq, SWA_WIDTH), lambda b, i: (b * nq + i, 0)),
                  pl.BlockSpec((W, 512), lambda b, i: (b * nb + jnp.maximum(i * n_sub - 1, 0), 0)),
                  pl.BlockSpec((tq, 512), lambda b, i: (b * nq + i, 0)),
                  pl.BlockSpec((W, 512), lambda b, i: (b * nb + jnp.minimum((i + 1) * n_sub, nb - 1), 0)),
                  pl.BlockSpec((SWA_HEADS, W, 3 * W), lambda b, i: (0, 0, 0), pipeline_mode=pl.Buffered(1))],
        out_specs=pl.BlockSpec((tq, SWA_WIDTH), lambda b, i: (b * nq + i, 0)),
        out_shape=jax.ShapeDtypeStruct((batch * seq, SWA_WIDTH), jnp.bfloat16),
        scratch_shapes=[pltpu.VMEM((tq + 2 * W, 512), jnp.bfloat16)],
        compiler_params=pltpu.CompilerParams(dimension_semantics=("arbitrary", "arbitrary"),
                                             vmem_limit_bytes=VMEM_LIMIT_BYTES),
        name="swa",
    )(sink.astype(jnp.float32), q, kv, kv, kv, bias)


GLA_TILE = 256
GLA_SUB = 16
_NT = (((1,), (1,)), ((), ()))
_TN = (((0,), (0,)), ((), ()))


def _seg_cumsum(x, seg, reverse):
    n = x.shape[0]
    rowmod = lax.broadcasted_iota(jnp.int32, x.shape, 0) % seg
    sh = 1
    while sh < seg:
        if reverse:
            x = x + jnp.where(rowmod < seg - sh, pltpu.roll(x, n - sh, 0), 0.0)
        else:
            x = x + jnp.where(rowmod >= sh, pltpu.roll(x, sh, 0), 0.0)
        sh *= 2
    return x


def _log_sigmoid(x):
    return jnp.minimum(x, 0.0) - jnp.log1p(jnp.exp(-jnp.abs(x)))


def _gla_constants(tb):
    S, hk = GLA_SUB, GLA_HEADS * GLA_DK
    r = np.arange(tb)
    same = (r[:, None] // S) == (r[None, :] // S)
    tri = np.stack([same & (r[None, :] <= r[:, None]), same & (r[None, :] >= r[:, None])])
    src = np.arange(S * hk)
    d, h = src // hk, (src % hk) // GLA_DK
    lane = np.arange(128)
    place = np.stack([lane[None, :] == (32 * h + S - d)[:, None], lane[None, :] == (32 * h + d)[:, None]])
    return jnp.asarray(tri, jnp.bfloat16), jnp.asarray(place, jnp.bfloat16)


def _gla_direction(qk_ref, v_ref, sm_ref, wg_ref, bg_ref, tri_ref, place_ref, st_ref, o_ref, reverse):
    tb, S = qk_ref.shape[0], GLA_SUB
    hk = GLA_HEADS * GLA_DK
    logits = jnp.dot(sm_ref[...].astype(jnp.bfloat16), wg_ref[...], preferred_element_type=jnp.float32)
    g = _log_sigmoid(logits + bg_ref[...]) * (1.0 / GLA_GATE_NORMALIZER)
    g_hi = g.astype(jnp.bfloat16)
    g_lo = (g - g_hi.astype(jnp.float32)).astype(jnp.bfloat16)
    cs = (jnp.dot(tri_ref[...], g_hi, preferred_element_type=jnp.float32) +
          jnp.dot(tri_ref[...], g_lo, preferred_element_type=jnp.float32))
    yield
    a = jnp.exp(g)
    q = qk_ref[:, :hk] * (GLA_DK ** -0.5)
    k = qk_ref[:, hk:]
    v = v_ref[...]
    rowmod = lax.broadcasted_iota(jnp.int32, (tb, hk), 0) % S
    has_prev = (rowmod < S - 1) if reverse else (rowmod >= 1)
    kd = k
    ps = [(q * kd).astype(jnp.bfloat16)]
    for d in range(1, S):
        kd = a * jnp.where(has_prev, pltpu.roll(kd, tb - 1 if reverse else 1, 0), 0.0)
        ps.append((q * kd).astype(jnp.bfloat16))
    scores = jnp.dot(jnp.concatenate(ps, axis=1), place_ref[...], preferred_element_type=jnp.float32)
    yield
    head_of_lane = lax.broadcasted_iota(jnp.int32, (S, GLA_WIDTH), 1) // GLA_DV
    zeros = jnp.zeros((S, GLA_WIDTH), jnp.float32)
    acc = {}
    for s in range(tb // S):
        rows = slice(s * S, (s + 1) * S)
        band = pltpu.roll(scores[rows], 0, 1, stride=1, stride_axis=0).astype(jnp.bfloat16)
        v_blk = v[rows]
        pieces = []
        for h in range(GLA_HEADS):
            vh = jnp.where(head_of_lane == h, v_blk, 0.0)
            pieces += [vh, zeros] if reverse else [zeros, vh]
        acc[s] = jnp.dot(band, jnp.concatenate(pieces, axis=0).astype(jnp.bfloat16),
                         preferred_element_type=jnp.float32)
        if s % 2 == 1:
            yield
    qd = (q * jnp.exp(cs)).astype(jnp.bfloat16)
    blockdiag = (lax.broadcasted_iota(jnp.int32, (hk, GLA_WIDTH), 0) // GLA_DK ==
                 lax.broadcasted_iota(jnp.int32, (hk, GLA_WIDTH), 1) // GLA_DV)
    n_blk = tb // S
    order = range(n_blk - 1, -1, -1) if reverse else range(n_blk)
    edge_row = 0 if reverse else S - 1
    edges = jnp.concatenate([cs[s * S + edge_row:s * S + edge_row + 1] for s in range(n_blk)], axis=0)
    dec_cols = jnp.exp(edges).T
    upd = {}
    for s in order:
        rows = slice(s * S, (s + 1) * S)
        kt = (k[rows] * jnp.exp(edges[s:s + 1] - cs[rows])).astype(jnp.bfloat16)
        upd[s] = jnp.where(blockdiag, lax.dot_general(kt, v[rows].astype(jnp.bfloat16), _TN,
                                                      preferred_element_type=jnp.float32), 0.0)
        yield
    st = st_ref[...]
    for s in order:
        rows = slice(s * S, (s + 1) * S)
        o_ref[rows, :] = acc[s] + jnp.dot(qd[rows], st.astype(jnp.bfloat16), preferred_element_type=jnp.float32)
        st = st * dec_cols[:, s:s + 1] + upd[s]
        yield
    st_ref[...] = st


def _gla_kernel(qkf_ref, vf_ref, smf_ref, qkb_ref, vb_ref, smb_ref, wg_ref, bg_ref, tri_ref, place_ref,
                of_ref, ob_ref, stf_ref, stb_ref):
    @pl.when(pl.program_id(1) == 0)
    def _():
        stf_ref[...] = jnp.zeros_like(stf_ref)
        stb_ref[...] = jnp.zeros_like(stb_ref)

    _lockstep([_gla_direction(qkf_ref, vf_ref, smf_ref, wg_ref.at[0], bg_ref.at[0], tri_ref.at[0], place_ref.at[0],
                              stf_ref, of_ref, False),
               _gla_direction(qkb_ref, vb_ref, smb_ref, wg_ref.at[1], bg_ref.at[1], tri_ref.at[1], place_ref.at[1],
                              stb_ref, ob_ref, True)])


def gla_scan(gla_qk, gla_v, small, w_lr2, b_lr, batch, seq):
    hk = GLA_HEADS * GLA_DK
    tb = min(GLA_TILE, seq)
    nt = seq // tb
    wg = jnp.zeros((2, 128, hk), jnp.float32)
    for z in range(2):
        wg = wg.at[z, z * GLA_LOWRANK:(z + 1) * GLA_LOWRANK].set(w_lr2[z])
    tri, place = _gla_constants(tb)
    fwd = lambda w: pl.BlockSpec((tb, w), lambda b, i: (b * nt + i, 0))
    bwd = lambda w: pl.BlockSpec((tb, w), lambda b, i: (b * nt + nt - 1 - i, 0))
    const = lambda a: pl.BlockSpec(a.shape, lambda b, i: (0,) * a.ndim)
    out = jax.ShapeDtypeStruct((batch * seq, GLA_WIDTH), jnp.float32)
    return pl.pallas_call(
        _gla_kernel,
        grid=(batch, nt),
        in_specs=[fwd(2 * hk), fwd(GLA_WIDTH), fwd(128), bwd(2 * hk), bwd(GLA_WIDTH), bwd(128),
                  pl.BlockSpec((2, 128, hk), lambda b, i: (0, 0, 0)),
                  pl.BlockSpec((2, 1, hk), lambda b, i: (0, 0, 0)), const(tri), const(place)],
        out_specs=[fwd(GLA_WIDTH), bwd(GLA_WIDTH)],
        out_shape=[out, out],
        scratch_shapes=[pltpu.VMEM((hk, GLA_WIDTH), jnp.float32), pltpu.VMEM((hk, GLA_WIDTH), jnp.float32)],
        compiler_params=pltpu.CompilerParams(dimension_semantics=("arbitrary", "arbitrary"),
                                             vmem_limit_bytes=VMEM_LIMIT_BYTES),
        name="gla",
    )(gla_qk, gla_v, small, gla_qk, gla_v, small, wg.astype(jnp.bfloat16), b_lr.reshape(2, 1, hk).astype(jnp.float32),
      tri, place)


DN_PREP_TILE = 512
DN_HALO = 8
DN_TILE = 256
DN_BETA_LANE = 2 * GLA_LOWRANK
DN_A_LANE = DN_BETA_LANE + 2 * DN_HEADS


def _split_dot(x, w_bf16):
    hi = x.astype(jnp.bfloat16)
    lo = (x - hi.astype(jnp.float32)).astype(jnp.bfloat16)
    return (jnp.dot(hi, w_bf16, preferred_element_type=jnp.float32) +
            jnp.dot(lo, w_bf16, preferred_element_type=jnp.float32))


def _group_indicator(n, group):
    r = lax.broadcasted_iota(jnp.int32, (n, n), 0) // group
    c = lax.broadcasted_iota(jnp.int32, (n, n), 1) // group
    return (r == c).astype(jnp.bfloat16)


def _dn_prep_kernel(xp_ref, xc_ref, xn_ref, sm_ref, cw_ref, na_ref, dt_ref, qkv_ref, gate_ref, ext_ref):
    tm = xc_ref.shape[0]
    i = pl.program_id(1)
    last = pl.num_programs(1) - 1
    H = DN_HALO
    ext_ref[0:H, :] = jnp.where(i == 0, 0.0, xp_ref[...])
    ext_ref[H:H + tm, :] = xc_ref[...]
    ext_ref[H + tm:, :] = jnp.where(i == last, 0.0, xn_ref[...])
    half = DN_CONV // 2
    acc = ext_ref[H - half:H - half + tm, :] * cw_ref[0:1, :]
    for j in range(1, DN_CONV):
        acc = acc + ext_ref[H - half + j:H - half + j + tm, :] * cw_ref[j:j + 1, :]
    y = acc * (1.0 / (1.0 + jnp.exp(-acc)))
    ind = _group_indicator(DN_HEADS * DN_DK, DN_DK)
    w = DN_HEADS * DN_DK
    q, k = y[:, :w], y[:, w:2 * w]
    qkv_ref[:, :w] = q * lax.rsqrt(_split_dot(q * q, ind) + EPS) * (DN_DK ** -0.5)
    qkv_ref[:, w:2 * w] = k * lax.rsqrt(_split_dot(k * k, ind) + EPS)
    qkv_ref[:, 2 * w:] = y[:, 2 * w:]
    s = sm_ref[...]
    lane = lax.broadcasted_iota(jnp.int32, s.shape, 1)
    beta = 1.0 / (1.0 + jnp.exp(-s))
    a = s + dt_ref[...]
    g = na_ref[...] * (jnp.maximum(a, 0.0) + jnp.log1p(jnp.exp(-jnp.abs(a))))
    gate_ref[...] = jnp.where((lane >= DN_BETA_LANE) & (lane < DN_A_LANE), beta,
                              jnp.where((lane >= DN_A_LANE) & (lane < DN_A_LANE + 2 * DN_HEADS), g, 0.0))


def dn_prep(dn_qkv, small, conv_w, a_log, dt_bias, batch, seq):
    tm = min(DN_PREP_TILE, seq)
    nt = seq // tm
    hb = tm // DN_HALO
    nh = seq // DN_HALO
    pad = lambda x: jnp.zeros((1, 128), jnp.float32).at[0, DN_A_LANE:DN_A_LANE + 2 * DN_HEADS].set(x.reshape(-1))
    row = lambda w: pl.BlockSpec((tm, w), lambda b, i: (b * nt + i, 0))
    return pl.pallas_call(
        _dn_prep_kernel,
        grid=(batch, nt),
        in_specs=[pl.BlockSpec((DN_HALO, DN_QKV), lambda b, i: (b * nh + jnp.maximum(i * hb - 1, 0), 0)),
                  row(DN_QKV),
                  pl.BlockSpec((DN_HALO, DN_QKV), lambda b, i: (b * nh + jnp.minimum((i + 1) * hb, nh - 1), 0)),
                  row(128),
                  pl.BlockSpec((DN_CONV, DN_QKV), lambda b, i: (0, 0)),
                  pl.BlockSpec((1, 128), lambda b, i: (0, 0)),
                  pl.BlockSpec((1, 128), lambda b, i: (0, 0))],
        out_specs=[row(DN_QKV), row(128)],
        out_shape=[jax.ShapeDtypeStruct((batch * seq, DN_QKV), jnp.float32),
                   jax.ShapeDtypeStruct((batch * seq, 128), jnp.float32)],
        scratch_shapes=[pltpu.VMEM((tm + 2 * DN_HALO, DN_QKV), jnp.float32)],
        compiler_params=pltpu.CompilerParams(dimension_semantics=("arbitrary", "arbitrary"),
                                             vmem_limit_bytes=VMEM_LIMIT_BYTES),
        name="dn_prep",
    )(dn_qkv, dn_qkv, dn_qkv, small, conv_w.astype(jnp.float32), pad(-jnp.exp(a_log.astype(jnp.float32))),
      pad(dt_bias.astype(jnp.float32)))


def _mm(a, b, dims=None):
    a, b = a.astype(jnp.bfloat16), b.astype(jnp.bfloat16)
    if dims is None:
        return jnp.dot(a, b, preferred_element_type=jnp.float32)
    return lax.dot_general(a, b, dims, preferred_element_type=jnp.float32)


def _head_blockdiag(x):
    c, hw = x.shape
    w = hw // DN_HEADS
    stacked = jnp.concatenate([x] * DN_HEADS, axis=0)
    r = lax.broadcasted_iota(jnp.int32, stacked.shape, 0) // c
    l = lax.broadcasted_iota(jnp.int32, stacked.shape, 1) // w
    return jnp.where(r == l, stacked, 0.0)


def _dn_chunk_prepare(qkv, gate, direction):
    C, W = DN_CHUNK, DN_HEADS * DN_DK
    reverse = direction == 1
    q, k, v = qkv[:, :W], qkv[:, W:2 * W], qkv[:, 2 * W:]
    lane = lax.broadcasted_iota(jnp.int32, (C, 128), 1)
    g_lo = DN_A_LANE + direction * DN_HEADS
    b_lo = DN_BETA_LANE + direction * DN_HEADS
    cs = _seg_cumsum(jnp.where((lane >= g_lo) & (lane < g_lo + DN_HEADS), gate, 0.0), C, reverse)
    beta = jnp.where((lane >= b_lo) & (lane < b_lo + DN_HEADS), gate, 0.0)
    src = lax.broadcasted_iota(jnp.int32, (128, W), 0)
    dst = lax.broadcasted_iota(jnp.int32, (128, W), 1) // DN_DK
    expand = ((src - g_lo == dst) | (src - b_lo == dst)).astype(jnp.bfloat16)
    both = _split_dot(jnp.concatenate([cs, beta], axis=0), expand)
    bcol, bexp = both[:C], both[C:]
    yield
    row =lax.broadcasted_iota(jnp.int32, (C, W), 0)
    col = lax.broadcasted_iota(jnp.int32, (C, W), 1) % C
    diag = jnp.where(row == col, bcol, 0.0)
    diag_hi = diag.astype(jnp.bfloat16)
    diag_lo = (diag - diag_hi.astype(jnp.float32)).astype(jnp.bfloat16)
    ones = jnp.ones((C, C), jnp.bfloat16)
    brow = (jnp.dot(ones, diag_hi, preferred_element_type=jnp.float32) +
            jnp.dot(ones, diag_lo, preferred_element_type=jnp.float32))
    incl = (col >= row) if reverse else (col <= row)
    strict = (col > row) if reverse else (col < row)
    lmat = jnp.exp(jnp.where(incl, bcol - brow, -jnp.inf))
    yield
    kb = k * bexp
    kk = _mm(jnp.concatenate([kb, q], axis=0), _head_blockdiag(k), _NT)
    m = jnp.where(strict, kk[:C] * lmat, 0.0)
    aqk = jnp.where(incl, kk[C:] * lmat, 0.0)
    yield
    t = jnp.where(row == col, 1.0, 0.0)
    s = 1
    while s < C:
        same = (row // (2 * s)) == (col // (2 * s))
        r_hi, c_hi = (row % (2 * s)) >= s, (col % (2 * s)) >= s
        off = same & ((c_hi & ~r_hi) if reverse else (r_hi & ~c_hi))
        m_off = jnp.where(off, m, 0.0)
        if s == 1:
            t = t - m_off
        else:
            x = _mm(m_off, _head_blockdiag(t))
            yield
            t = t - _mm(t, _head_blockdiag(x))
            yield
        s *= 2
    ecs = jnp.exp(bcol)
    u = _mm(t, _head_blockdiag(v * bexp))
    w = _mm(t, _head_blockdiag(kb * ecs))
    yield
    edge = bcol[0:1] if reverse else bcol[C - 1:C]
    return dict(u=u, wq=jnp.concatenate([w, q * ecs], axis=0), aqk=aqk, k_dec=k * jnp.exp(edge - bcol),
                dl=jnp.exp(edge))


def _lockstep_stages(generators):
    results = [None] * len(generators)
    live = list(range(len(generators)))
    while live:
        still = []
        for i in live:
            try:
                next(generators[i])
                still.append(i)
            except StopIteration as stop:
                results[i] = stop.value
        live = still
        yield
    return results


def _lockstep(generators):
    stages = _lockstep_stages(generators)
    while True:
        try:
            next(stages)
        except StopIteration as stop:
            return stop.value


def _dn_state_chain(prepared, s_ref, o_ref, reverse):
    C, W = DN_CHUNK, DN_HEADS * DN_DK
    n_chunks = len(prepared)
    bd = (lax.broadcasted_iota(jnp.int32, (W, W), 0) // DN_DK ==
          lax.broadcasted_iota(jnp.int32, (W, W), 1) // DN_DV)
    s = s_ref[...]
    for c in (range(n_chunks - 1, -1, -1) if reverse else range(n_chunks)):
        pc = prepared[c]
        both = _mm(pc["wq"], s)
        v_new = pc["u"] - both[:C]
        yield
        o_ref[c * C:(c + 1) * C, :] = both[C:] + _mm(pc["aqk"], _head_blockdiag(v_new))
        s = s * pc["dl"] + jnp.where(bd, _mm(pc["k_dec"], v_new, _TN), 0.0)
        yield
    s_ref[...] = s


def _dn_tile(qkvf_ref, gf_ref, qkvb_ref, gb_ref, of_ref, ob_ref, sf_ref, sb_ref):
    C = DN_CHUNK
    n_chunks = qkvf_ref.shape[0] // C
    rows = lambda c: slice(c * C, (c + 1) * C)
    prepared = yield from _lockstep_stages(
        [_dn_chunk_prepare(qkvf_ref[rows(c), :], gf_ref[rows(c), :], 0) for c in range(n_chunks)] +
        [_dn_chunk_prepare(qkvb_ref[rows(c), :], gb_ref[rows(c), :], 1) for c in range(n_chunks)])
    yield from _lockstep_stages([_dn_state_chain(prepared[:n_chunks], sf_ref, of_ref, False),
                                 _dn_state_chain(prepared[n_chunks:], sb_ref, ob_ref, True)])


def _seq_mixers_kernel(qkf_ref, vf_ref, smf_ref, qkb_ref, vb_ref, smb_ref, wg_ref, bg_ref, tri_ref, place_ref,
                       dqkvf_ref, dgf_ref, dqkvb_ref, dgb_ref,
                       gof_ref, gob_ref, dof_ref, dob_ref, stf_ref, stb_ref, sf_ref, sb_ref):
    @pl.when(pl.program_id(1) == 0)
    def _():
        for ref in (stf_ref, stb_ref, sf_ref, sb_ref):
            ref[...] = jnp.zeros_like(ref)

    _lockstep([_dn_tile(dqkvf_ref, dgf_ref, dqkvb_ref, dgb_ref, dof_ref, dob_ref, sf_ref, sb_ref),
               _gla_direction(qkf_ref, vf_ref, smf_ref, wg_ref.at[0], bg_ref.at[0], tri_ref.at[0], place_ref.at[0],
                              stf_ref, gof_ref, False),
               _gla_direction(qkb_ref, vb_ref, smb_ref, wg_ref.at[1], bg_ref.at[1], tri_ref.at[1], place_ref.at[1],
                              stb_ref, gob_ref, True)])


def seq_mixers(gla_qk, gla_v, small, w_lr2, b_lr, dn_qkvn, dn_gates, batch, seq):
    hk = GLA_HEADS * GLA_DK
    W = DN_HEADS * DN_DK
    tb = min(GLA_TILE, seq)
    nt = seq // tb
    wg = jnp.zeros((2, 128, hk), jnp.float32)
    for z in range(2):
        wg = wg.at[z, z * GLA_LOWRANK:(z + 1) * GLA_LOWRANK].set(w_lr2[z])
    tri, place = _gla_constants(tb)
    fwd = lambda w: pl.BlockSpec((tb, w), lambda b, i: (b * nt + i, 0))
    bwd = lambda w: pl.BlockSpec((tb, w), lambda b, i: (b * nt + nt - 1 - i, 0))
    const = lambda a: pl.BlockSpec(a.shape, lambda b, i: (0,) * a.ndim)
    out = jax.ShapeDtypeStruct((batch * seq, GLA_WIDTH), jnp.float32)
    bg = b_lr.reshape(2, 1, hk).astype(jnp.float32)
    wg = wg.astype(jnp.bfloat16)
    return pl.pallas_call(
        _seq_mixers_kernel,
        grid=(batch, nt),
        in_specs=[fwd(2 * hk), fwd(GLA_WIDTH), fwd(128), bwd(2 * hk), bwd(GLA_WIDTH), bwd(128),
                  const(wg), const(bg), const(tri), const(place),
                  fwd(DN_QKV), fwd(128), bwd(DN_QKV), bwd(128)],
        out_specs=[fwd(GLA_WIDTH), bwd(GLA_WIDTH), fwd(DN_WIDTH), bwd(DN_WIDTH)],
        out_shape=[out, out, out, out],
        scratch_shapes=[pltpu.VMEM((hk, GLA_WIDTH), jnp.float32), pltpu.VMEM((hk, GLA_WIDTH), jnp.float32),
                        pltpu.VMEM((W, DN_WIDTH), jnp.float32), pltpu.VMEM((W, DN_WIDTH), jnp.float32)],
        compiler_params=pltpu.CompilerParams(dimension_semantics=("arbitrary", "arbitrary"),
                                             vmem_limit_bytes=VMEM_LIMIT_BYTES),
        name="seq_mixers",
    )(gla_qk, gla_v, small, gla_qk, gla_v, small, wg, bg, tri, place, dn_qkvn, dn_gates, dn_qkvn, dn_gates)


def _dn_kernel(qkvf_ref, gf_ref, qkvb_ref, gb_ref, of_ref, ob_ref, sf_ref, sb_ref):
    @pl.when(pl.program_id(1) == 0)
    def _():
        sf_ref[...] = jnp.zeros_like(sf_ref)
        sb_ref[...] = jnp.zeros_like(sb_ref)

    _lockstep([_dn_tile(qkvf_ref, gf_ref, qkvb_ref, gb_ref, of_ref, ob_ref, sf_ref, sb_ref)])


def dn_scan(qkv, gates, batch, seq):
    tb = min(DN_TILE, seq)
    nt = seq // tb
    W = DN_HEADS * DN_DK
    fwd = lambda w: pl.BlockSpec((tb, w), lambda b, i: (b * nt + i, 0))
    bwd = lambda w: pl.BlockSpec((tb, w), lambda b, i: (b * nt + nt - 1 - i, 0))
    out = jax.ShapeDtypeStruct((batch * seq, DN_WIDTH), jnp.float32)
    return pl.pallas_call(
        _dn_kernel,
        grid=(batch, nt),
        in_specs=[fwd(DN_QKV), fwd(128), bwd(DN_QKV), bwd(128)],
        out_specs=[fwd(DN_WIDTH), bwd(DN_WIDTH)],
        out_shape=[out, out],
        scratch_shapes=[pltpu.VMEM((W, DN_WIDTH), jnp.float32), pltpu.VMEM((W, DN_WIDTH), jnp.float32)],
        compiler_params=pltpu.CompilerParams(dimension_semantics=("arbitrary", "arbitrary"),
                                             vmem_limit_bytes=VMEM_LIMIT_BYTES),
        name="dn_scan",
    )(qkv, gates, qkv, gates)


def _gated_group_norm(of_ref, ob_ref, gate_ref, gn_ref, ind):
    o = of_ref[...] + ob_ref[...]
    ms = _split_dot(o * o, ind) * (1.0 / HEAD_DIM)
    gate = gate_ref[...]
    return (o * lax.rsqrt(ms + EPS) * gn_ref[...] * (gate * (1.0 / (1.0 + jnp.exp(-gate))))).astype(jnp.bfloat16)


def _out_mlp_kernel(glaf_ref, glab_ref, glag_ref, swa_ref, dnf_ref, dnb_ref, dnz_ref, x_ref, wo_ref,
                    gng_ref, gnd_ref, g1_ref, g2_ref, w1_ref, w2_ref, g3_ref, o_ref):
    ind = _group_indicator(GLA_WIDTH, HEAD_DIM)
    y_gla = _gated_group_norm(glaf_ref, glab_ref, glag_ref, gng_ref, ind)
    y_dn = _gated_group_norm(dnf_ref, dnb_ref, dnz_ref, gnd_ref, ind)
    a, b = GLA_WIDTH, GLA_WIDTH + SWA_WIDTH
    m = (jnp.dot(y_gla, wo_ref[0:a, :], preferred_element_type=jnp.float32) +
         jnp.dot(swa_ref[...], wo_ref[a:b, :], preferred_element_type=jnp.float32) +
         jnp.dot(y_dn, wo_ref[b:, :], preferred_element_type=jnp.float32))
    x1 = x_ref[...] + _rms(m, g1_ref[...])
    h = _rms(x1, g2_ref[...]).astype(jnp.bfloat16)
    acc = jnp.zeros_like(x1)
    for c in range(D_FF // FF_CHUNK):
        f = jnp.dot(h, w1_ref[:, c * FF_CHUNK:(c + 1) * FF_CHUNK], preferred_element_type=jnp.float32)
        f = jnp.square(jnp.maximum(f, 0.0)).astype(jnp.bfloat16)
        acc = acc + jnp.dot(f, w2_ref[c * FF_CHUNK:(c + 1) * FF_CHUNK, :], preferred_element_type=jnp.float32)
    o_ref[...] = x1 + _rms(acc, g3_ref[...])


def out_mlp(gla_f, gla_b, gla_g, swa_o, dn_f, dn_b, dn_z, x2d, wo, gla_norm_g, dn_norm_g, g1, g2, w1, w2, g3):
    n, d = x2d.shape
    tm = min(ROW_TILE, n)
    row = lambda w: pl.BlockSpec((tm, w), lambda i: (i, 0))
    vec = _const_spec((1, d))
    hvec = _const_spec((1, GLA_WIDTH))
    tile_gain = lambda g, h: jnp.tile(g.astype(jnp.float32), h).reshape(1, -1)
    return pl.pallas_call(
        _out_mlp_kernel,
        grid=(n // tm,),
        in_specs=[row(GLA_WIDTH), row(GLA_WIDTH), row(GLA_WIDTH), row(SWA_WIDTH), row(DN_WIDTH), row(DN_WIDTH),
                  row(DN_WIDTH), row(d), _const_spec((d, d)), hvec, hvec, vec, vec,
                  _const_spec((d, D_FF)), _const_spec((D_FF, d)), vec],
        out_specs=row(d),
        out_shape=jax.ShapeDtypeStruct((n, d), jnp.float32),
        compiler_params=pltpu.CompilerParams(dimension_semantics=("arbitrary",),
                                             vmem_limit_bytes=VMEM_LIMIT_BYTES),
        name="out_mlp",
    )(gla_f, gla_b, gla_g, swa_o, dn_f, dn_b, dn_z, x2d, wo, tile_gain(gla_norm_g, GLA_HEADS),
      tile_gain(dn_norm_g, DN_HEADS), g1.reshape(1, d), g2.reshape(1, d), w1, w2, g3.reshape(1, d))


def _rmsnorm(x, g):
    xf = x.astype(jnp.float32)
    y = xf * lax.rsqrt(jnp.mean(xf * xf, axis=-1, keepdims=True) + EPS)
    return (y * g.astype(jnp.float32)).astype(x.dtype)


def _l2norm(x):
    return x * lax.rsqrt(jnp.sum(x * x, axis=-1, keepdims=True) + EPS)


def _split_heads(x, h):
    B, T, _ = x.shape
    return x.reshape(B, T, h, -1).transpose(0, 2, 1, 3)


def _merge_heads(x):
    B, H, T, d = x.shape
    return x.transpose(0, 2, 1, 3).reshape(B, T, H * d)


def _split_columns(proj):
    idx, acc = [], 0
    for s in IN_SIZES[:-1]:
        acc += s
        idx.append(acc)
    return jnp.split(proj, idx, axis=-1)


def _gla_chunk_scan(q, k, v, gk):
    B, H, T, dk = q.shape
    dv = v.shape[-1]
    C = GLA_CHUNK
    N = T // C
    q, k, v, gk = [a.reshape(B, H, N, C, a.shape[-1]) for a in (q, k, v, gk)]
    b = jnp.cumsum(gk, axis=3)
    causal = jnp.tril(jnp.ones((C, C), dtype=bool))
    diff = jnp.where(causal[:, :, None], b[..., :, None, :] - b[..., None, :, :], -jnp.inf)
    A = jnp.einsum('bhnid,bhnjd,bhnijd->bhnij', q, k, jnp.exp(diff))
    o_intra = jnp.einsum('bhnij,bhnjv->bhniv', A, v)
    b_last = b[..., -1, :]
    k_dec = k * jnp.exp(b_last[..., None, :] - b)
    upd = jnp.einsum('bhncd,bhncv->nbhdv', k_dec, v)
    dec = jnp.moveaxis(jnp.exp(b_last), 2, 0)

    def step(S, inp):
        d, u = inp
        return d[..., None] * S + u, S

    _, S_prev = lax.scan(step, jnp.zeros((B, H, dk, dv), q.dtype), (dec, upd))
    o_inter = jnp.einsum('bhncd,nbhdv->bhncv', q * jnp.exp(b), S_prev)
    return (o_intra + o_inter).reshape(B, H, T, dv)


def _gla_mixer(q, k, v, g, lr, w_lr2, b_lr, norm_g):
    B, T, _ = q.shape
    lr = lr.reshape(B, T, 2, GLA_LOWRANK)
    logits = jnp.einsum('btzr,zrk->zbtk', lr, w_lr2) + b_lr[:, None, None, :]
    gk = jax.nn.log_sigmoid(logits) / GLA_GATE_NORMALIZER
    gk = gk.reshape(2, B, T, GLA_HEADS, GLA_DK).transpose(0, 1, 3, 2, 4)
    qh = _split_heads(q, GLA_HEADS) * (GLA_DK ** -0.5)
    kh = _split_heads(k, GLA_HEADS)
    vh = _split_heads(v, GLA_HEADS)
    o_f = _gla_chunk_scan(qh, kh, vh, gk[0])
    fl = lambda a: jnp.flip(a, axis=2)
    o_b = fl(_gla_chunk_scan(fl(qh), fl(kh), fl(vh), fl(gk[1])))
    o = _rmsnorm(o_f + o_b, norm_g)
    return _merge_heads(o) * jax.nn.silu(g)


def _short_conv(x, w):
    K = w.shape[0]
    p = K // 2
    T = x.shape[1]
    xp = jnp.pad(x, ((0, 0), (p, p), (0, 0)))
    acc = xp[:, 0:T] * w[0]
    for j in range(1, K):
        acc = acc + xp[:, j:j + T] * w[j]
    return acc


def _gated_delta_chunk(q, k, v, g, beta):
    B, H, T, dk = q.shape
    dv = v.shape[-1]
    C = DN_CHUNK
    N = T // C
    q, k, v = [a.reshape(B, H, N, C, a.shape[-1]) for a in (q, k, v)]
    g = g.reshape(B, H, N, C)
    beta = beta.reshape(B, H, N, C)
    b = jnp.cumsum(g, axis=-1)
    tri = jnp.tril(jnp.ones((C, C), dtype=bool))
    strict = jnp.tril(jnp.ones((C, C), dtype=bool), -1)
    L = jnp.exp(jnp.where(tri, b[..., :, None] - b[..., None, :], -jnp.inf))
    kb = k * beta[..., None]
    M = jnp.where(strict, jnp.einsum('bhnid,bhnjd->bhnij', kb, k) * L, 0.0)
    Amat = M + jnp.eye(C, dtype=M.dtype)
    rhs = jnp.concatenate([v * beta[..., None], kb * jnp.exp(b)[..., None]], axis=-1)
    sol = lax.linalg.triangular_solve(Amat, rhs, left_side=True, lower=True, unit_diagonal=True)
    u, w = sol[..., :dv], sol[..., dv:]
    Aqk = jnp.where(tri, jnp.einsum('bhnid,bhnjd->bhnij', q, k) * L, 0.0)
    q_dec = q * jnp.exp(b)[..., None]
    k_dec = k * jnp.exp(b[..., -1:] - b)[..., None]
    dec_last = jnp.exp(b[..., -1])
    xs = tuple(jnp.moveaxis(a, 2, 0) for a in (u, w, Aqk, q_dec, k_dec, dec_last))

    def step(S, inp):
        uc, wc, Ac, qdc, kdc, dl = inp
        v_new = uc - jnp.einsum('bhcd,bhdv->bhcv', wc, S)
        o = jnp.einsum('bhcd,bhdv->bhcv', qdc, S) + jnp.einsum('bhij,bhjv->bhiv', Ac, v_new)
        S = dl[..., None, None] * S + jnp.einsum('bhcd,bhcv->bhdv', kdc, v_new)
        return S, o

    _, o = lax.scan(step, jnp.zeros((B, H, dk, dv), q.dtype), xs)
    return jnp.moveaxis(o, 0, 2).reshape(B, H, T, dv)


def _deltanet_mixer(qkv, z, beta_raw, a_raw, conv_w, a_log, dt_bias, norm_g):
    B, T, _ = qkv.shape
    qkv = jax.nn.silu(_short_conv(qkv, conv_w))
    q, k, v = jnp.split(qkv, [DN_HEADS * DN_DK, 2 * DN_HEADS * DN_DK], axis=-1)
    qh = _l2norm(_split_heads(q, DN_HEADS)) * (DN_DK ** -0.5)
    kh = _l2norm(_split_heads(k, DN_HEADS))
    vh = _split_heads(v, DN_HEADS)
    beta = jax.nn.sigmoid(beta_raw.reshape(B, T, 2, DN_HEADS)).transpose(2, 0, 3, 1)
    a = a_raw.reshape(B, T, 2, DN_HEADS).transpose(2, 0, 1, 3)
    g = -jnp.exp(a_log)[:, None, None, :] * jax.nn.softplus(a + dt_bias[:, None, None, :])
    g = g.transpose(0, 1, 3, 2)
    o_f = _gated_delta_chunk(qh, kh, vh, g[0], beta[0])
    fl = lambda x: jnp.flip(x, axis=2)
    o_b = fl(_gated_delta_chunk(fl(qh), fl(kh), fl(vh), fl(g[1]), fl(beta[1])))
    o = _rmsnorm(o_f + o_b, norm_g)
    return _merge_heads(o) * jax.nn.silu(z)


def _trunk(x, p):
    B, T, D = x.shape
    bias = p["band_bias"]
    x2 = x.reshape(B * T, D)
    for l in range(DEPTH):
        dn_qkvn, small, swa_q, swa_kv, gla_qk, gla_v, gla_g, dn_z, dn_gates = in_proj(
            x2, p["mix_pre_g"][l], p["w_in_packed"][l], p["dn_conv_w"][l], p["dn_a_log"][l], p["dn_dt_bias"][l], T)
        o_swa = swa_attention(swa_q, swa_kv, p["swa_sink"][l], bias, B, T)
        gla_f, gla_b, dn_f, dn_b = seq_mixers(gla_qk, gla_v, small, p["gla_w_lr2"][l], p["gla_b_lr"][l],
                                              dn_qkvn, dn_gates, B, T)
        x2 = out_mlp(gla_f, gla_b, gla_g, o_swa, dn_f, dn_b, dn_z, x2, p["w_out_bf16"][l], p["gla_norm_g"][l],
                     p["dn_norm_g"][l], p["mix_post_g"][l], p["mlp_pre_g"][l],
                     p["mlp_w1_bf16"][l], p["mlp_w2_bf16"][l], p["mlp_post_g"][l])
    return x2.reshape(B, T, D)


def _prepare_params(rel_bias_table, mix_pre_g, w_in, gla_w_lr2, gla_b_lr, gla_norm_g, swa_sink, dn_conv_w, dn_a_log,
                    dn_dt_bias, dn_norm_g, w_out, mix_post_g, mlp_pre_g, mlp_w1, mlp_w2, mlp_post_g):
    return dict(mix_pre_g=mix_pre_g, gla_w_lr2=gla_w_lr2, gla_b_lr=gla_b_lr,
                gla_norm_g=gla_norm_g, swa_sink=swa_sink, dn_conv_w=dn_conv_w, dn_a_log=dn_a_log,
                dn_dt_bias=dn_dt_bias, dn_norm_g=dn_norm_g, mix_post_g=mix_post_g, mlp_pre_g=mlp_pre_g,
                mlp_post_g=mlp_post_g,
                band_bias=band_bias(rel_bias_table),
                w_in_packed=jax.vmap(pack_w_in)(w_in), w_out_bf16=w_out.astype(jnp.bfloat16),
                mlp_w1_bf16=mlp_w1.astype(jnp.bfloat16), mlp_w2_bf16=mlp_w2.astype(jnp.bfloat16))


def kernel(x_prompt, x_sample, rel_bias_table, mix_pre_g, w_in, gla_w_lr2, gla_b_lr, gla_norm_g, swa_sink, dn_conv_w, dn_a_log, dn_dt_bias, dn_norm_g, w_out, mix_post_g, mlp_pre_g, mlp_w1, mlp_w2, mlp_post_g):
    p = _prepare_params(rel_bias_table, mix_pre_g, w_in, gla_w_lr2, gla_b_lr, gla_norm_g, swa_sink, dn_conv_w,
                        dn_a_log, dn_dt_bias, dn_norm_g, w_out, mix_post_g, mlp_pre_g, mlp_w1, mlp_w2, mlp_post_g)
    return (_trunk(x_prompt, p), _trunk(x_sample, p))
```

```python
import functools
import math

import numpy as np

import jax
import jax.numpy as jnp
from jax import lax
from jax.experimental import pallas as pl
from jax.experimental.pallas import tpu as pltpu

D_MODEL = 1024
DEPTH = 4
HEAD_DIM = 64
EPS = 1e-6
GLA_HEADS = 4
GLA_DK = 32
GLA_DV = 64
GLA_LOWRANK = 16
GLA_GATE_NORMALIZER = 16.0
GLA_CHUNK = 32
SWA_HEADS = 8
SWA_KV_HEADS = 2
SWA_WINDOW = 128
SWA_BLOCK = 128
NUM_BUCKETS = 32
MAX_DISTANCE = 128
DN_HEADS = 4
DN_DK = 64
DN_DV = 64
DN_CONV = 5
DN_CHUNK = 64
GLA_WIDTH = GLA_HEADS * GLA_DV
SWA_WIDTH = SWA_HEADS * HEAD_DIM
DN_WIDTH = DN_HEADS * DN_DV
D_MIX = GLA_WIDTH + SWA_WIDTH + DN_WIDTH
D_FF = 4 * D_MODEL
DN_QKV = DN_HEADS * (2 * DN_DK + DN_DV)
IN_SIZES = (GLA_HEADS * GLA_DK, GLA_HEADS * GLA_DK, GLA_WIDTH, GLA_WIDTH, 2 * GLA_LOWRANK,
            SWA_WIDTH, SWA_KV_HEADS * HEAD_DIM, SWA_KV_HEADS * HEAD_DIM,
            DN_QKV, DN_WIDTH, 2 * DN_HEADS, 2 * DN_HEADS)
D_IN = sum(IN_SIZES)

VMEM_LIMIT_BYTES = 56 * 1024 * 1024
ROW_TILE = 512
FF_CHUNK = 1024


def _rms(xf, g):
    return xf * lax.rsqrt(jnp.mean(xf * xf, axis=-1, keepdims=True) + EPS) * g


def _const_spec(shape):
    return pl.BlockSpec(shape, lambda i: (0,) * len(shape), pipeline_mode=pl.Buffered(1))


_O = {}
_acc = 0
for _name, _width in (("dn_qkv", DN_QKV), ("small", 128), ("swa_q", SWA_WIDTH), ("swa_kv", 4 * 2 * HEAD_DIM),
                      ("gla_qk", 2 * GLA_HEADS * GLA_DK), ("gla_v", GLA_WIDTH), ("gla_g", GLA_WIDTH),
                      ("dn_z", DN_WIDTH)):
    _O[_name] = (_acc, _width)
    _acc += _width
D_IN_PACKED = _acc
D_IN_LEAD = _O["small"][0] + _O["small"][1]
_OUT_DTYPES = {"swa_q": jnp.bfloat16, "swa_kv": jnp.bfloat16}


def pack_w_in(w_in):
    w_in = w_in.astype(jnp.bfloat16)
    offs, acc = [], 0
    for s in IN_SIZES:
        offs.append(acc)
        acc += s
    gq, gk, gv, gg, glr, sq, sk, sv, dqkv, dz, dbeta, da = [w_in[:, o:o + s] for o, s in zip(offs, IN_SIZES)]
    hd = HEAD_DIM
    kv = [sk[:, 0:hd], sk[:, 0:hd], sk[:, hd:], sk[:, hd:], sv[:, 0:hd], sv[:, 0:hd], sv[:, hd:], sv[:, hd:]]
    small = jnp.concatenate([glr, dbeta, da], axis=1)
    small = jnp.pad(small, ((0, 0), (0, 128 - small.shape[1])))
    pieces = dict(dn_qkv=dqkv, small=small, swa_q=sq, swa_kv=jnp.concatenate(kv, axis=1),
                  gla_qk=jnp.concatenate([gq, gk], axis=1), gla_v=gv, gla_g=gg, dn_z=dz)
    return jnp.concatenate([pieces[name] for name in _O], axis=1)


def _in_proj_kernel(tiles_per_seq, xp_ref, xc_ref, xn_ref, g_ref, w_ref, cw_ref, na_ref, dt_ref, *refs):
    o_refs, gate_ref, ext_ref = refs[:len(_O)], refs[len(_O)], refs[len(_O) + 1]
    tm, H = xc_ref.shape[0], DN_HALO
    i = pl.program_id(0)
    first = (i % tiles_per_seq) == 0
    last = (i % tiles_per_seq) == tiles_per_seq - 1
    x = jnp.concatenate([xp_ref[...], xc_ref[...], xn_ref[...]], axis=0)
    h = _rms(x, g_ref[...])
    h_tile = h[H:H + tm].astype(jnp.bfloat16)
    out = dict(zip(_O, o_refs))
    y = jnp.dot(h.astype(jnp.bfloat16), w_ref[:, :D_IN_LEAD], preferred_element_type=jnp.float32)
    off, width = _O["dn_qkv"]
    ext_ref[0:H, :] = jnp.where(first, 0.0, y[0:H, off:off + width])
    ext_ref[H:H + tm, :] = y[H:H + tm, off:off + width]
    ext_ref[H + tm:, :] = jnp.where(last, 0.0, y[H + tm:, off:off + width])
    half = DN_CONV // 2
    w = DN_HEADS * DN_DK
    ind = _group_indicator(w, DN_DK)
    sumsq = lambda a: jnp.dot((a * a).astype(jnp.bfloat16), ind, preferred_element_type=jnp.float32)

    def project(names):
        lo, hi = _O[names[0]][0], _O[names[-1]][0] + _O[names[-1]][1]
        part = jnp.dot(h_tile, w_ref[:, lo:hi], preferred_element_type=jnp.float32)
        for name in names:
            o, wd = _O[name]
            out[name][...] = part[:, o - lo:o - lo + wd].astype(out[name].dtype)

    def deltanet_rows(r0, r1):
        acc = ext_ref[H - half + r0:H - half + r1, :] * cw_ref[0:1, :]
        for j in range(1, DN_CONV):
            acc = acc + ext_ref[H - half + j + r0:H - half + j + r1, :] * cw_ref[j:j + 1, :]
        c = acc * (1.0 / (1.0 + jnp.exp(-acc)))
        q, k = c[:, :w], c[:, w:2 * w]
        out["dn_qkv"][r0:r1, :w] = q * lax.rsqrt(sumsq(q) + EPS) * (DN_DK ** -0.5)
        out["dn_qkv"][r0:r1, w:2 * w] = k * lax.rsqrt(sumsq(k) + EPS)
        out["dn_qkv"][r0:r1, 2 * w:] = c[:, 2 * w:]

    groups = (["swa_q", "swa_kv"], ["gla_qk", "gla_v", "gla_g", "dn_z"])
    for n, names in enumerate(groups):
        project(names)
        deltanet_rows(n * tm // len(groups), (n + 1) * tm // len(groups))
    off, width = _O["small"]
    s = y[H:H + tm, off:off + width]
    out["small"][...] = s
    lane = lax.broadcasted_iota(jnp.int32, s.shape, 1)
    beta = 1.0 / (1.0 + jnp.exp(-s))
    a = s + dt_ref[...]
    gdec = na_ref[...] * (jnp.maximum(a, 0.0) + jnp.log1p(jnp.exp(-jnp.abs(a))))
    gate_ref[...] = jnp.where((lane >= DN_BETA_LANE) & (lane < DN_A_LANE), beta,
                              jnp.where((lane >= DN_A_LANE) & (lane < DN_A_LANE + 2 * DN_HEADS), gdec, 0.0))


def in_proj(x2d, g, w_packed, conv_w, a_log, dt_bias, seq):
    n, d = x2d.shape
    tm = min(ROW_TILE, seq)
    hb = tm // DN_HALO
    nh = n // DN_HALO
    lanes = lambda v: jnp.zeros((1, 128), jnp.float32).at[0, DN_A_LANE:DN_A_LANE + 2 * DN_HEADS].set(v.reshape(-1))
    row = lambda w: pl.BlockSpec((tm, w), lambda i: (i, 0))
    return pl.pallas_call(
        functools.partial(_in_proj_kernel, seq // tm),
        grid=(n // tm,),
        in_specs=[pl.BlockSpec((DN_HALO, d), lambda i: (jnp.maximum(i * hb - 1, 0), 0)),
                  row(d),
                  pl.BlockSpec((DN_HALO, d), lambda i: (jnp.minimum((i + 1) * hb, nh - 1), 0)),
                  _const_spec((1, d)),
                  _const_spec((d, D_IN_PACKED)),
                  _const_spec((DN_CONV, DN_QKV)), _const_spec((1, 128)), _const_spec((1, 128))],
        out_specs=[row(w) for _, w in _O.values()] + [row(128)],
        out_shape=[jax.ShapeDtypeStruct((n, w), _OUT_DTYPES.get(name, jnp.float32))
                   for name, (_, w) in _O.items()] + [jax.ShapeDtypeStruct((n, 128), jnp.float32)],
        scratch_shapes=[pltpu.VMEM((tm + 2 * DN_HALO, DN_QKV), jnp.float32)],
        compiler_params=pltpu.CompilerParams(dimension_semantics=("arbitrary",),
                                             vmem_limit_bytes=VMEM_LIMIT_BYTES),
        name="in_proj",
    )(x2d, x2d, x2d, g.reshape(1, d), w_packed, conv_w.astype(jnp.float32),
      lanes(-jnp.exp(a_log.astype(jnp.float32))), lanes(dt_bias.astype(jnp.float32)))


SWA_Q_TILE = 512
SWA_GROUP = SWA_HEADS // SWA_KV_HEADS


def _t5_bucket(rel):
    nb = NUM_BUCKETS // 2
    max_exact = nb // 2
    base = jnp.where(rel > 0, nb, 0)
    n = jnp.abs(rel)
    nf = jnp.maximum(n, 1).astype(jnp.float32)
    large = max_exact + (jnp.log(nf / max_exact) / math.log(MAX_DISTANCE / max_exact)
                         * (nb - max_exact)).astype(jnp.int32)
    large = jnp.minimum(large, nb - 1)
    return base + jnp.where(n < max_exact, n, large)


def _band_bias_kernel(tab_ref, idx_ref, o_ref):
    W = SWA_BLOCK
    idx = idx_ref[...]
    i = lax.broadcasted_iota(jnp.int32, idx.shape, 0)
    j = lax.broadcasted_iota(jnp.int32, idx.shape, 1)
    in_band = jnp.abs(j - W - i) <= SWA_WINDOW
    for h in range(SWA_HEADS):
        acc = jnp.zeros(idx.shape, jnp.float32)
        for b in range(NUM_BUCKETS):
            acc = jnp.where(idx == b, tab_ref[b, h], acc)
        o_ref[h] = jnp.where(in_band, acc, -jnp.inf)


def band_bias(table):
    W = SWA_BLOCK
    rel = jnp.arange(3 * W)[None, :] - W - jnp.arange(W)[:, None]
    idx = _t5_bucket(rel).astype(jnp.int32)
    return pl.pallas_call(
        _band_bias_kernel,
        in_specs=[pl.BlockSpec(memory_space=pltpu.SMEM), pl.BlockSpec(memory_space=pltpu.VMEM)],
        out_specs=pl.BlockSpec(memory_space=pltpu.VMEM),
        out_shape=jax.ShapeDtypeStruct((SWA_HEADS, W, 3 * W), jnp.float32),
        name="band_bias",
    )(table.astype(jnp.float32), idx)


def _swa_kernel(sink_ref, q_ref, kvp_ref, kvc_ref, kvn_ref, bias_ref, o_ref, kv_buf):
    W = SWA_BLOCK
    n_sub = q_ref.shape[0] // W
    n = pl.program_id(1)
    last = pl.num_programs(1) - 1
    kv_buf[0:W, :] = kvp_ref[...]
    kv_buf[W:W + n_sub * W, :] = kvc_ref[...]
    kv_buf[W + n_sub * W:, :] = kvn_ref[...]
    lane = lax.broadcasted_iota(jnp.int32, (W, 2 * HEAD_DIM), 1)
    lo = lane < HEAD_DIM
    key_blk = lax.broadcasted_iota(jnp.int32, (1, 3 * W), 1) // W
    neg = jnp.float32(-jnp.inf)
    for s in range(n_sub):
        edge = jnp.zeros((1, 3 * W), jnp.float32)
        if s == 0:
            edge = jnp.where((key_blk == 0) & (n == 0), neg, edge)
        if s == n_sub - 1:
            edge = jnp.where((key_blk == 2) & (n == last), neg, edge)
        rows = slice(s * W, (s + 1) * W)
        for g in range(SWA_KV_HEADS):
            kd = kv_buf[s * W:(s + 3) * W, g * 128:(g + 1) * 128]
            vd = kv_buf[s * W:(s + 3) * W, 256 + g * 128:256 + (g + 1) * 128]
            vd1 = jnp.concatenate([vd, jnp.ones_like(vd)], axis=1)
            qs = []
            for pair in range(SWA_GROUP // 2):
                col = (g * SWA_GROUP // 2 + pair) * 128
                qp = q_ref[rows, col:col + 128] * (HEAD_DIM ** -0.5)
                qs += [jnp.where(lo, qp, 0), jnp.where(lo, 0, qp)]
            q_stack = jnp.concatenate(qs, axis=0)
            s_all = lax.dot_general(q_stack, kd, (((1,), (1,)), ((), ())),
                                    preferred_element_type=jnp.float32)
            ps, sink_terms = [], []
            for hh in range(SWA_GROUP):
                h = g * SWA_GROUP + hh
                sc = s_all[hh * W:(hh + 1) * W] + bias_ref[h]
                if s == 0 or s == n_sub - 1:
                    sc = sc + edge
                sink = sink_ref[h]
                m = jnp.maximum(jnp.max(sc, axis=-1, keepdims=True), sink)
                ps.append(jnp.exp((sc - m).astype(jnp.bfloat16)))
                sink_terms.append(jnp.exp(sink - m))
            o_all = jnp.dot(jnp.concatenate(ps, axis=0), vd1, preferred_element_type=jnp.float32)
            outs = []
            for hh in range(SWA_GROUP):
                blk = o_all[hh * W:(hh + 1) * W]
                outs.append(blk[:, :128] * (1.0 / (blk[:, 128:] + sink_terms[hh])))
            for pair in range(SWA_GROUP // 2):
                col = (g * SWA_GROUP // 2 + pair) * 128
                o_ref[rows, col:col + 128] = jnp.where(lo, outs[2 * pair], outs[2 * pair + 1]).astype(o_ref.dtype)


def swa_attention(q, kv, sink, bias, batch, seq):
    W = SWA_BLOCK
    tq = min(SWA_Q_TILE, seq)
    n_sub = tq // W
    nq = seq // tq
    nb = seq // W
    return pl.pallas_call(
        _swa_kernel,
        grid=(batch, nq),
        in_specs=[pl.BlockSpec(memory_space=pltpu.SMEM),
                  pl.BlockSpec((tq, SWA_WIDTH), lambda b, i: (b * nq + i, 0)),
                  pl.BlockSpec((W, 512), lambda b, i: (b * nb + jnp.maximum(i * n_sub - 1, 0), 0)),
                  pl.BlockSpec((tq, 512), lambda b, i: (b * nq + i, 0)),
                  pl.BlockSpec((W, 512), lambda b, i: (b * nb + jnp.minimum((i + 1) * n_sub, nb - 1), 0)),
                  pl.BlockSpec((SWA_HEADS, W, 3 * W), lambda b, i: (0, 0, 0), pipeline_mode=pl.Buffered(1))],
        out_specs=pl.BlockSpec((tq, SWA_WIDTH), lambda b, i: (b * nq + i, 0)),
        out_shape=jax.ShapeDtypeStruct((batch * seq, SWA_WIDTH), jnp.bfloat16),
        scratch_shapes=[pltpu.VMEM((tq + 2 * W, 512), jnp.bfloat16)],
        compiler_params=pltpu.CompilerParams(dimension_semantics=("arbitrary", "arbitrary"),
                                             vmem_limit_bytes=VMEM_LIMIT_BYTES),
        name="swa",
    )(sink.astype(jnp.float32), q, kv, kv, kv, bias)


GLA_TILE = 256
GLA_SUB = 16
_NT = (((1,), (1,)), ((), ()))
_TN = (((0,), (0,)), ((), ()))


def _seg_cumsum(x, seg, reverse):
    n = x.shape[0]
    rowmod = lax.broadcasted_iota(jnp.int32, x.shape, 0) % seg
    sh = 1
    while sh < seg:
        if reverse:
            x = x + jnp.where(rowmod < seg - sh, pltpu.roll(x, n - sh, 0), 0.0)
        else:
            x = x + jnp.where(rowmod >= sh, pltpu.roll(x, sh, 0), 0.0)
        sh *= 2
    return x


def _log_sigmoid(x):
    return jnp.minimum(x, 0.0) - jnp.log1p(jnp.exp(-jnp.abs(x)))


def _gla_constants(tb):
    S, hk = GLA_SUB, GLA_HEADS * GLA_DK
    r = np.arange(tb)
    same = (r[:, None] // S) == (r[None, :] // S)
    tri = np.stack([same & (r[None, :] <= r[:, None]), same & (r[None, :] >= r[:, None])])
    src = np.arange(S * hk)
    d, h = src // hk, (src % hk) // GLA_DK
    lane = np.arange(128)
    place = np.stack([lane[None, :] == (32 * h + S - d)[:, None], lane[None, :] == (32 * h + d)[:, None]])
    return jnp.asarray(tri, jnp.bfloat16), jnp.asarray(place, jnp.bfloat16)


def _gla_direction(qk_ref, v_ref, sm_ref, wg_ref, bg_ref, tri_ref, place_ref, st_ref, o_ref, reverse):
    tb, S = qk_ref.shape[0], GLA_SUB
    hk = GLA_HEADS * GLA_DK
    logits = jnp.dot(sm_ref[...].astype(jnp.bfloat16), wg_ref[...], preferred_element_type=jnp.float32)
    g = _log_sigmoid(logits + bg_ref[...]) * (1.0 / GLA_GATE_NORMALIZER)
    g_hi = g.astype(jnp.bfloat16)
    g_lo = (g - g_hi.astype(jnp.float32)).astype(jnp.bfloat16)
    cs = (jnp.dot(tri_ref[...], g_hi, preferred_element_type=jnp.float32) +
          jnp.dot(tri_ref[...], g_lo, preferred_element_type=jnp.float32))
    yield
    a = jnp.exp(g)
    q = qk_ref[:, :hk] * (GLA_DK ** -0.5)
    k = qk_ref[:, hk:]
    v = v_ref[...]
    rowmod = lax.broadcasted_iota(jnp.int32, (tb, hk), 0) % S
    has_prev = (rowmod < S - 1) if reverse else (rowmod >= 1)
    kd = k
    ps = [(q * kd).astype(jnp.bfloat16)]
    for d in range(1, S):
        kd = a * jnp.where(has_prev, pltpu.roll(kd, tb - 1 if reverse else 1, 0), 0.0)
        ps.append((q * kd).astype(jnp.bfloat16))
    scores = jnp.dot(jnp.concatenate(ps, axis=1), place_ref[...], preferred_element_type=jnp.float32)
    yield
    head_of_lane = lax.broadcasted_iota(jnp.int32, (S, GLA_WIDTH), 1) // GLA_DV
    zeros = jnp.zeros((S, GLA_WIDTH), jnp.float32)
    acc = {}
    for s in range(tb // S):
        rows = slice(s * S, (s + 1) * S)
        band = pltpu.roll(scores[rows], 0, 1, stride=1, stride_axis=0).astype(jnp.bfloat16)
        v_blk = v[rows]
        pieces = []
        for h in range(GLA_HEADS):
            vh = jnp.where(head_of_lane == h, v_blk, 0.0)
            pieces += [vh, zeros] if reverse else [zeros, vh]
        acc[s] = jnp.dot(band, jnp.concatenate(pieces, axis=0).astype(jnp.bfloat16),
                         preferred_element_type=jnp.float32)
        if s % 2 == 1:
            yield
    qd = (q * jnp.exp(cs)).astype(jnp.bfloat16)
    blockdiag = (lax.broadcasted_iota(jnp.int32, (hk, GLA_WIDTH), 0) // GLA_DK ==
                 lax.broadcasted_iota(jnp.int32, (hk, GLA_WIDTH), 1) // GLA_DV)
    n_blk = tb // S
    order = range(n_blk - 1, -1, -1) if reverse else range(n_blk)
    edge_row = 0 if reverse else S - 1
    edges = jnp.concatenate([cs[s * S + edge_row:s * S + edge_row + 1] for s in range(n_blk)], axis=0)
    dec_cols = jnp.exp(edges).T
    upd = {}
    for s in order:
        rows = slice(s * S, (s + 1) * S)
        kt = (k[rows] * jnp.exp(edges[s:s + 1] - cs[rows])).astype(jnp.bfloat16)
        upd[s] = jnp.where(blockdiag, lax.dot_general(kt, v[rows].astype(jnp.bfloat16), _TN,
                                                      preferred_element_type=jnp.float32), 0.0)
        yield
    st = st_ref[...]
    for s in order:
        rows = slice(s * S, (s + 1) * S)
        o_ref[rows, :] = acc[s] + jnp.dot(qd[rows], st.astype(jnp.bfloat16), preferred_element_type=jnp.float32)
        st = st * dec_cols[:, s:s + 1] + upd[s]
        yield
    st_ref[...] = st


def _gla_kernel(qkf_ref, vf_ref, smf_ref, qkb_ref, vb_ref, smb_ref, wg_ref, bg_ref, tri_ref, place_ref,
                of_ref, ob_ref, stf_ref, stb_ref):
    @pl.when(pl.program_id(1) == 0)
    def _():
        stf_ref[...] = jnp.zeros_like(stf_ref)
        stb_ref[...] = jnp.zeros_like(stb_ref)

    _lockstep([_gla_direction(qkf_ref, vf_ref, smf_ref, wg_ref.at[0], bg_ref.at[0], tri_ref.at[0], place_ref.at[0],
                              stf_ref, of_ref, False),
               _gla_direction(qkb_ref, vb_ref, smb_ref, wg_ref.at[1], bg_ref.at[1], tri_ref.at[1], place_ref.at[1],
                              stb_ref, ob_ref, True)])


def gla_scan(gla_qk, gla_v, small, w_lr2, b_lr, batch, seq):
    hk = GLA_HEADS * GLA_DK
    tb = min(GLA_TILE, seq)
    nt = seq // tb
    wg = jnp.zeros((2, 128, hk), jnp.float32)
    for z in range(2):
        wg = wg.at[z, z * GLA_LOWRANK:(z + 1) * GLA_LOWRANK].set(w_lr2[z])
    tri, place = _gla_constants(tb)
    fwd = lambda w: pl.BlockSpec((tb, w), lambda b, i: (b * nt + i, 0))
    bwd = lambda w: pl.BlockSpec((tb, w), lambda b, i: (b * nt + nt - 1 - i, 0))
    const = lambda a: pl.BlockSpec(a.shape, lambda b, i: (0,) * a.ndim)
    out = jax.ShapeDtypeStruct((batch * seq, GLA_WIDTH), jnp.float32)
    return pl.pallas_call(
        _gla_kernel,
        grid=(batch, nt),
        in_specs=[fwd(2 * hk), fwd(GLA_WIDTH), fwd(128), bwd(2 * hk), bwd(GLA_WIDTH), bwd(128),
                  pl.BlockSpec((2, 128, hk), lambda b, i: (0, 0, 0)),
                  pl.BlockSpec((2, 1, hk), lambda b, i: (0, 0, 0)), const(tri), const(place)],
        out_specs=[fwd(GLA_WIDTH), bwd(GLA_WIDTH)],
        out_shape=[out, out],
        scratch_shapes=[pltpu.VMEM((hk, GLA_WIDTH), jnp.float32), pltpu.VMEM((hk, GLA_WIDTH), jnp.float32)],
        compiler_params=pltpu.CompilerParams(dimension_semantics=("arbitrary", "arbitrary"),
                                             vmem_limit_bytes=VMEM_LIMIT_BYTES),
        name="gla",
    )(gla_qk, gla_v, small, gla_qk, gla_v, small, wg.astype(jnp.bfloat16), b_lr.reshape(2, 1, hk).astype(jnp.float32),
      tri, place)


DN_PREP_TILE = 512
DN_HALO = 8
DN_TILE = 256
DN_BETA_LANE = 2 * GLA_LOWRANK
DN_A_LANE = DN_BETA_LANE + 2 * DN_HEADS


def _split_dot(x, w_bf16):
    hi = x.astype(jnp.bfloat16)
    lo = (x - hi.astype(jnp.float32)).astype(jnp.bfloat16)
    return (jnp.dot(hi, w_bf16, preferred_element_type=jnp.float32) +
            jnp.dot(lo, w_bf16, preferred_element_type=jnp.float32))


def _group_indicator(n, group):
    r = lax.broadcasted_iota(jnp.int32, (n, n), 0) // group
    c = lax.broadcasted_iota(jnp.int32, (n, n), 1) // group
    return (r == c).astype(jnp.bfloat16)


def _dn_prep_kernel(xp_ref, xc_ref, xn_ref, sm_ref, cw_ref, na_ref, dt_ref, qkv_ref, gate_ref, ext_ref):
    tm = xc_ref.shape[0]
    i = pl.program_id(1)
    last = pl.num_programs(1) - 1
    H = DN_HALO
    ext_ref[0:H, :] = jnp.where(i == 0, 0.0, xp_ref[...])
    ext_ref[H:H + tm, :] = xc_ref[...]
    ext_ref[H + tm:, :] = jnp.where(i == last, 0.0, xn_ref[...])
    half = DN_CONV // 2
    acc = ext_ref[H - half:H - half + tm, :] * cw_ref[0:1, :]
    for j in range(1, DN_CONV):
        acc = acc + ext_ref[H - half + j:H - half + j + tm, :] * cw_ref[j:j + 1, :]
    y = acc * (1.0 / (1.0 + jnp.exp(-acc)))
    ind = _group_indicator(DN_HEADS * DN_DK, DN_DK)
    w = DN_HEADS * DN_DK
    q, k = y[:, :w], y[:, w:2 * w]
    qkv_ref[:, :w] = q * lax.rsqrt(_split_dot(q * q, ind) + EPS) * (DN_DK ** -0.5)
    qkv_ref[:, w:2 * w] = k * lax.rsqrt(_split_dot(k * k, ind) + EPS)
    qkv_ref[:, 2 * w:] = y[:, 2 * w:]
    s = sm_ref[...]
    lane = lax.broadcasted_iota(jnp.int32, s.shape, 1)
    beta = 1.0 / (1.0 + jnp.exp(-s))
    a = s + dt_ref[...]
    g = na_ref[...] * (jnp.maximum(a, 0.0) + jnp.log1p(jnp.exp(-jnp.abs(a))))
    gate_ref[...] = jnp.where((lane >= DN_BETA_LANE) & (lane < DN_A_LANE), beta,
                              jnp.where((lane >= DN_A_LANE) & (lane < DN_A_LANE + 2 * DN_HEADS), g, 0.0))


def dn_prep(dn_qkv, small, conv_w, a_log, dt_bias, batch, seq):
    tm = min(DN_PREP_TILE, seq)
    nt = seq // tm
    hb = tm // DN_HALO
    nh = seq // DN_HALO
    pad = lambda x: jnp.zeros((1, 128), jnp.float32).at[0, DN_A_LANE:DN_A_LANE + 2 * DN_HEADS].set(x.reshape(-1))
    row = lambda w: pl.BlockSpec((tm, w), lambda b, i: (b * nt + i, 0))
    return pl.pallas_call(
        _dn_prep_kernel,
        grid=(batch, nt),
        in_specs=[pl.BlockSpec((DN_HALO, DN_QKV), lambda b, i: (b * nh + jnp.maximum(i * hb - 1, 0), 0)),
                  row(DN_QKV),
                  pl.BlockSpec((DN_HALO, DN_QKV), lambda b, i: (b * nh + jnp.minimum((i + 1) * hb, nh - 1), 0)),
                  row(128),
                  pl.BlockSpec((DN_CONV, DN_QKV), lambda b, i: (0, 0)),
                  pl.BlockSpec((1, 128), lambda b, i: (0, 0)),
                  pl.BlockSpec((1, 128), lambda b, i: (0, 0))],
        out_specs=[row(DN_QKV), row(128)],
        out_shape=[jax.ShapeDtypeStruct((batch * seq, DN_QKV), jnp.float32),
                   jax.ShapeDtypeStruct((batch * seq, 128), jnp.float32)],
        scratch_shapes=[pltpu.VMEM((tm + 2 * DN_HALO, DN_QKV), jnp.float32)],
        compiler_params=pltpu.CompilerParams(dimension_semantics=("arbitrary", "arbitrary"),
                                             vmem_limit_bytes=VMEM_LIMIT_BYTES),
        name="dn_prep",
    )(dn_qkv, dn_qkv, dn_qkv, small, conv_w.astype(jnp.float32), pad(-jnp.exp(a_log.astype(jnp.float32))),
      pad(dt_bias.astype(jnp.float32)))


def _mm(a, b, dims=None):
    a, b = a.astype(jnp.bfloat16), b.astype(jnp.bfloat16)
    if dims is None:
        return jnp.dot(a, b, preferred_element_type=jnp.float32)
    return lax.dot_general(a, b, dims, preferred_element_type=jnp.float32)


def _head_blockdiag(x):
    c, hw = x.shape
    w = hw // DN_HEADS
    stacked = jnp.concatenate([x] * DN_HEADS, axis=0)
    r = lax.broadcasted_iota(jnp.int32, stacked.shape, 0) // c
    l = lax.broadcasted_iota(jnp.int32, stacked.shape, 1) // w
    return jnp.where(r == l, stacked, 0.0)


def _dn_chunk_prepare(qkv, gate, direction):
    C, W = DN_CHUNK, DN_HEADS * DN_DK
    reverse = direction == 1
    q, k, v = qkv[:, :W], qkv[:, W:2 * W], qkv[:, 2 * W:]
    lane = lax.broadcasted_iota(jnp.int32, (C, 128), 1)
    g_lo = DN_A_LANE + direction * DN_HEADS
    b_lo = DN_BETA_LANE + direction * DN_HEADS
    cs = _seg_cumsum(jnp.where((lane >= g_lo) & (lane < g_lo + DN_HEADS), gate, 0.0), C, reverse)
    beta = jnp.where((lane >= b_lo) & (lane < b_lo + DN_HEADS), gate, 0.0)
    src = lax.broadcasted_iota(jnp.int32, (128, W), 0)
    dst = lax.broadcasted_iota(jnp.int32, (128, W), 1) // DN_DK
    expand = ((src - g_lo == dst) | (src - b_lo == dst)).astype(jnp.bfloat16)
    both = _split_dot(jnp.concatenate([cs, beta], axis=0), expand)
    bcol, bexp = both[:C], both[C:]
    yield
    row =lax.broadcasted_iota(jnp.int32, (C, W), 0)
    col = lax.broadcasted_iota(jnp.int32, (C, W), 1) % C
    diag = jnp.where(row == col, bcol, 0.0)
    diag_hi = diag.astype(jnp.bfloat16)
    diag_lo = (diag - diag_hi.astype(jnp.float32)).astype(jnp.bfloat16)
    ones = jnp.ones((C, C), jnp.bfloat16)
    brow = (jnp.dot(ones, diag_hi, preferred_element_type=jnp.float32) +
            jnp.dot(ones, diag_lo, preferred_element_type=jnp.float32))
    incl = (col >= row) if reverse else (col <= row)
    strict = (col > row) if reverse else (col < row)
    lmat = jnp.exp(jnp.where(incl, bcol - brow, -jnp.inf))
    yield
    kb = k * bexp
    kk = _mm(jnp.concatenate([kb, q], axis=0), _head_blockdiag(k), _NT)
    m = jnp.where(strict, kk[:C] * lmat, 0.0)
    aqk = jnp.where(incl, kk[C:] * lmat, 0.0)
    yield
    t = jnp.where(row == col, 1.0, 0.0)
    s = 1
    while s < C:
        same = (row // (2 * s)) == (col // (2 * s))
        r_hi, c_hi = (row % (2 * s)) >= s, (col % (2 * s)) >= s
        off = same & ((c_hi & ~r_hi) if reverse else (r_hi & ~c_hi))
        m_off = jnp.where(off, m, 0.0)
        if s == 1:
            t = t - m_off
        else:
            x = _mm(m_off, _head_blockdiag(t))
            yield
            t = t - _mm(t, _head_blockdiag(x))
            yield
        s *= 2
    ecs = jnp.exp(bcol)
    u = _mm(t, _head_blockdiag(v * bexp))
    w = _mm(t, _head_blockdiag(kb * ecs))
    yield
    edge = bcol[0:1] if reverse else bcol[C - 1:C]
    return dict(u=u, wq=jnp.concatenate([w, q * ecs], axis=0), aqk=aqk, k_dec=k * jnp.exp(edge - bcol),
                dl=jnp.exp(edge))


def _lockstep_stages(generators):
    results = [None] * len(generators)
    live = list(range(len(generators)))
    while live:
        still = []
        for i in live:
            try:
                next(generators[i])
                still.append(i)
            except StopIteration as stop:
                results[i] = stop.value
        live = still
        yield
    return results


def _lockstep(generators):
    stages = _lockstep_stages(generators)
    while True:
        try:
            next(stages)
        except StopIteration as stop:
            return stop.value


def _dn_state_chain(prepared, s_ref, o_ref, reverse):
    C, W = DN_CHUNK, DN_HEADS * DN_DK
    n_chunks = len(prepared)
    bd = (lax.broadcasted_iota(jnp.int32, (W, W), 0) // DN_DK ==
          lax.broadcasted_iota(jnp.int32, (W, W), 1) // DN_DV)
    s = s_ref[...]
    for c in (range(n_chunks - 1, -1, -1) if reverse else range(n_chunks)):
        pc = prepared[c]
        both = _mm(pc["wq"], s)
        v_new = pc["u"] - both[:C]
        yield
        o_ref[c * C:(c + 1) * C, :] = both[C:] + _mm(pc["aqk"], _head_blockdiag(v_new))
        s = s * pc["dl"] + jnp.where(bd, _mm(pc["k_dec"], v_new, _TN), 0.0)
        yield
    s_ref[...] = s


def _dn_tile(qkvf_ref, gf_ref, qkvb_ref, gb_ref, of_ref, ob_ref, sf_ref, sb_ref):
    C = DN_CHUNK
    n_chunks = qkvf_ref.shape[0] // C
    rows = lambda c: slice(c * C, (c + 1) * C)
    prepared = yield from _lockstep_stages(
        [_dn_chunk_prepare(qkvf_ref[rows(c), :], gf_ref[rows(c), :], 0) for c in range(n_chunks)] +
        [_dn_chunk_prepare(qkvb_ref[rows(c), :], gb_ref[rows(c), :], 1) for c in range(n_chunks)])
    yield from _lockstep_stages([_dn_state_chain(prepared[:n_chunks], sf_ref, of_ref, False),
                                 _dn_state_chain(prepared[n_chunks:], sb_ref, ob_ref, True)])


def _seq_mixers_kernel(qkf_ref, vf_ref, smf_ref, qkb_ref, vb_ref, smb_ref, wg_ref, bg_ref, tri_ref, place_ref,
                       dqkvf_ref, dgf_ref, dqkvb_ref, dgb_ref,
                       gof_ref, gob_ref, dof_ref, dob_ref, stf_ref, stb_ref, sf_ref, sb_ref):
    @pl.when(pl.program_id(1) == 0)
    def _():
        for ref in (stf_ref, stb_ref, sf_ref, sb_ref):
            ref[...] = jnp.zeros_like(ref)

    _lockstep([_dn_tile(dqkvf_ref, dgf_ref, dqkvb_ref, dgb_ref, dof_ref, dob_ref, sf_ref, sb_ref),
               _gla_direction(qkf_ref, vf_ref, smf_ref, wg_ref.at[0], bg_ref.at[0], tri_ref.at[0], place_ref.at[0],
                              stf_ref, gof_ref, False),
               _gla_direction(qkb_ref, vb_ref, smb_ref, wg_ref.at[1], bg_ref.at[1], tri_ref.at[1], place_ref.at[1],
                              stb_ref, gob_ref, True)])


def seq_mixers(gla_qk, gla_v, small, w_lr2, b_lr, dn_qkvn, dn_gates, batch, seq):
    hk = GLA_HEADS * GLA_DK
    W = DN_HEADS * DN_DK
    tb = min(GLA_TILE, seq)
    nt = seq // tb
    wg = jnp.zeros((2, 128, hk), jnp.float32)
    for z in range(2):
        wg = wg.at[z, z * GLA_LOWRANK:(z + 1) * GLA_LOWRANK].set(w_lr2[z])
    tri, place = _gla_constants(tb)
    fwd = lambda w: pl.BlockSpec((tb, w), lambda b, i: (b * nt + i, 0))
    bwd = lambda w: pl.BlockSpec((tb, w), lambda b, i: (b * nt + nt - 1 - i, 0))
    const = lambda a: pl.BlockSpec(a.shape, lambda b, i: (0,) * a.ndim)
    out = jax.ShapeDtypeStruct((batch * seq, GLA_WIDTH), jnp.float32)
    bg = b_lr.reshape(2, 1, hk).astype(jnp.float32)
    wg = wg.astype(jnp.bfloat16)
    return pl.pallas_call(
        _seq_mixers_kernel,
        grid=(batch, nt),
        in_specs=[fwd(2 * hk), fwd(GLA_WIDTH), fwd(128), bwd(2 * hk), bwd(GLA_WIDTH), bwd(128),
                  const(wg), const(bg), const(tri), const(place),
                  fwd(DN_QKV), fwd(128), bwd(DN_QKV), bwd(128)],
        out_specs=[fwd(GLA_WIDTH), bwd(GLA_WIDTH), fwd(DN_WIDTH), bwd(DN_WIDTH)],
        out_shape=[out, out, out, out],
        scratch_shapes=[pltpu.VMEM((hk, GLA_WIDTH), jnp.float32), pltpu.VMEM((hk, GLA_WIDTH), jnp.float32),
                        pltpu.VMEM((W, DN_WIDTH), jnp.float32), pltpu.VMEM((W, DN_WIDTH), jnp.float32)],
        compiler_params=pltpu.CompilerParams(dimension_semantics=("arbitrary", "arbitrary"),
                                             vmem_limit_bytes=VMEM_LIMIT_BYTES),
        name="seq_mixers",
    )(gla_qk, gla_v, small, gla_qk, gla_v, small, wg, bg, tri, place, dn_qkvn, dn_gates, dn_qkvn, dn_gates)


def _dn_kernel(qkvf_ref, gf_ref, qkvb_ref, gb_ref, of_ref, ob_ref, sf_ref, sb_ref):
    @pl.when(pl.program_id(1) == 0)
    def _():
        sf_ref[...] = jnp.zeros_like(sf_ref)
        sb_ref[...] = jnp.zeros_like(sb_ref)

    _lockstep([_dn_tile(qkvf_ref, gf_ref, qkvb_ref, gb_ref, of_ref, ob_ref, sf_ref, sb_ref)])


def dn_scan(qkv, gates, batch, seq):
    tb = min(DN_TILE, seq)
    nt = seq // tb
    W = DN_HEADS * DN_DK
    fwd = lambda w: pl.BlockSpec((tb, w), lambda b, i: (b * nt + i, 0))
    bwd = lambda w: pl.BlockSpec((tb, w), lambda b, i: (b * nt + nt - 1 - i, 0))
    out = jax.ShapeDtypeStruct((batch * seq, DN_WIDTH), jnp.float32)
    return pl.pallas_call(
        _dn_kernel,
        grid=(batch, nt),
        in_specs=[fwd(DN_QKV), fwd(128), bwd(DN_QKV), bwd(128)],
        out_specs=[fwd(DN_WIDTH), bwd(DN_WIDTH)],
        out_shape=[out, out],
        scratch_shapes=[pltpu.VMEM((W, DN_WIDTH), jnp.float32), pltpu.VMEM((W, DN_WIDTH), jnp.float32)],
        compiler_params=pltpu.CompilerParams(dimension_semantics=("arbitrary", "arbitrary"),
                                             vmem_limit_bytes=VMEM_LIMIT_BYTES),
        name="dn_scan",
    )(qkv, gates, qkv, gates)


def _gated_group_norm(of_ref, ob_ref, gate_ref, gn_ref, ind):
    o = of_ref[...] + ob_ref[...]
    ms = _split_dot(o * o, ind) * (1.0 / HEAD_DIM)
    gate = gate_ref[...]
    return (o * lax.rsqrt(ms + EPS) * gn_ref[...] * (gate * (1.0 / (1.0 + jnp.exp(-gate))))).astype(jnp.bfloat16)


def _out_mlp_kernel(glaf_ref, glab_ref, glag_ref, swa_ref, dnf_ref, dnb_ref, dnz_ref, x_ref, wo_ref,
                    gng_ref, gnd_ref, g1_ref, g2_ref, w1_ref, w2_ref, g3_ref, o_ref):
    ind = _group_indicator(GLA_WIDTH, HEAD_DIM)
    y_gla = _gated_group_norm(glaf_ref, glab_ref, glag_ref, gng_ref, ind)
    y_dn = _gated_group_norm(dnf_ref, dnb_ref, dnz_ref, gnd_ref, ind)
    a, b = GLA_WIDTH, GLA_WIDTH + SWA_WIDTH
    m = (jnp.dot(y_gla, wo_ref[0:a, :], preferred_element_type=jnp.float32) +
         jnp.dot(swa_ref[...], wo_ref[a:b, :], preferred_element_type=jnp.float32) +
         jnp.dot(y_dn, wo_ref[b:, :], preferred_element_type=jnp.float32))
    x1 = x_ref[...] + _rms(m, g1_ref[...])
    h = _rms(x1, g2_ref[...]).astype(jnp.bfloat16)
    acc = jnp.zeros_like(x1)
    for c in range(D_FF // FF_CHUNK):
        f = jnp.dot(h, w1_ref[:, c * FF_CHUNK:(c + 1) * FF_CHUNK], preferred_element_type=jnp.float32)
        f = jnp.square(jnp.maximum(f, 0.0)).astype(jnp.bfloat16)
        acc = acc + jnp.dot(f, w2_ref[c * FF_CHUNK:(c + 1) * FF_CHUNK, :], preferred_element_type=jnp.float32)
    o_ref[...] = x1 + _rms(acc, g3_ref[...])


def out_mlp(gla_f, gla_b, gla_g, swa_o, dn_f, dn_b, dn_z, x2d, wo, gla_norm_g, dn_norm_g, g1, g2, w1, w2, g3):
    n, d = x2d.shape
    tm = min(ROW_TILE, n)
    row = lambda w: pl.BlockSpec((tm, w), lambda i: (i, 0))
    vec = _const_spec((1, d))
    hvec = _const_spec((1, GLA_WIDTH))
    tile_gain = lambda g, h: jnp.tile(g.astype(jnp.float32), h).reshape(1, -1)
    return pl.pallas_call(
        _out_mlp_kernel,
        grid=(n // tm,),
        in_specs=[row(GLA_WIDTH), row(GLA_WIDTH), row(GLA_WIDTH), row(SWA_WIDTH), row(DN_WIDTH), row(DN_WIDTH),
                  row(DN_WIDTH), row(d), _const_spec((d, d)), hvec, hvec, vec, vec,
                  _const_spec((d, D_FF)), _const_spec((D_FF, d)), vec],
        out_specs=row(d),
        out_shape=jax.ShapeDtypeStruct((n, d), jnp.float32),
        compiler_params=pltpu.CompilerParams(dimension_semantics=("arbitrary",),
                                             vmem_limit_bytes=VMEM_LIMIT_BYTES),
        name="out_mlp",
    )(gla_f, gla_b, gla_g, swa_o, dn_f, dn_b, dn_z, x2d, wo, tile_gain(gla_norm_g, GLA_HEADS),
      tile_gain(dn_norm_g, DN_HEADS), g1.reshape(1, d), g2.reshape(1, d), w1, w2, g3.reshape(1, d))


def _rmsnorm(x, g):
    xf = x.astype(jnp.float32)
    y = xf * lax.rsqrt(jnp.mean(xf * xf, axis=-1, keepdims=True) + EPS)
    return (y * g.astype(jnp.float32)).astype(x.dtype)


def _l2norm(x):
    return x * lax.rsqrt(jnp.sum(x * x, axis=-1, keepdims=True) + EPS)


def _split_heads(x, h):
    B, T, _ = x.shape
    return x.reshape(B, T, h, -1).transpose(0, 2, 1, 3)


def _merge_heads(x):
    B, H, T, d = x.shape
    return x.transpose(0, 2, 1, 3).reshape(B, T, H * d)


def _split_columns(proj):
    idx, acc = [], 0
    for s in IN_SIZES[:-1]:
        acc += s
        idx.append(acc)
    return jnp.split(proj, idx, axis=-1)


def _gla_chunk_scan(q, k, v, gk):
    B, H, T, dk = q.shape
    dv = v.shape[-1]
    C = GLA_CHUNK
    N = T // C
    q, k, v, gk = [a.reshape(B, H, N, C, a.shape[-1]) for a in (q, k, v, gk)]
    b = jnp.cumsum(gk, axis=3)
    causal = jnp.tril(jnp.ones((C, C), dtype=bool))
    diff = jnp.where(causal[:, :, None], b[..., :, None, :] - b[..., None, :, :], -jnp.inf)
    A = jnp.einsum('bhnid,bhnjd,bhnijd->bhnij', q, k, jnp.exp(diff))
    o_intra = jnp.einsum('bhnij,bhnjv->bhniv', A, v)
    b_last = b[..., -1, :]
    k_dec = k * jnp.exp(b_last[..., None, :] - b)
    upd = jnp.einsum('bhncd,bhncv->nbhdv', k_dec, v)
    dec = jnp.moveaxis(jnp.exp(b_last), 2, 0)

    def step(S, inp):
        d, u = inp
        return d[..., None] * S + u, S

    _, S_prev = lax.scan(step, jnp.zeros((B, H, dk, dv), q.dtype), (dec, upd))
    o_inter = jnp.einsum('bhncd,nbhdv->bhncv', q * jnp.exp(b), S_prev)
    return (o_intra + o_inter).reshape(B, H, T, dv)


def _gla_mixer(q, k, v, g, lr, w_lr2, b_lr, norm_g):
    B, T, _ = q.shape
    lr = lr.reshape(B, T, 2, GLA_LOWRANK)
    logits = jnp.einsum('btzr,zrk->zbtk', lr, w_lr2) + b_lr[:, None, None, :]
    gk = jax.nn.log_sigmoid(logits) / GLA_GATE_NORMALIZER
    gk = gk.reshape(2, B, T, GLA_HEADS, GLA_DK).transpose(0, 1, 3, 2, 4)
    qh = _split_heads(q, GLA_HEADS) * (GLA_DK ** -0.5)
    kh = _split_heads(k, GLA_HEADS)
    vh = _split_heads(v, GLA_HEADS)
    o_f = _gla_chunk_scan(qh, kh, vh, gk[0])
    fl = lambda a: jnp.flip(a, axis=2)
    o_b = fl(_gla_chunk_scan(fl(qh), fl(kh), fl(vh), fl(gk[1])))
    o = _rmsnorm(o_f + o_b, norm_g)
    return _merge_heads(o) * jax.nn.silu(g)


def _short_conv(x, w):
    K = w.shape[0]
    p = K // 2
    T = x.shape[1]
    xp = jnp.pad(x, ((0, 0), (p, p), (0, 0)))
    acc = xp[:, 0:T] * w[0]
    for j in range(1, K):
        acc = acc + xp[:, j:j + T] * w[j]
    return acc


def _gated_delta_chunk(q, k, v, g, beta):
    B, H, T, dk = q.shape
    dv = v.shape[-1]
    C = DN_CHUNK
    N = T // C
    q, k, v = [a.reshape(B, H, N, C, a.shape[-1]) for a in (q, k, v)]
    g = g.reshape(B, H, N, C)
    beta = beta.reshape(B, H, N, C)
    b = jnp.cumsum(g, axis=-1)
    tri = jnp.tril(jnp.ones((C, C), dtype=bool))
    strict = jnp.tril(jnp.ones((C, C), dtype=bool), -1)
    L = jnp.exp(jnp.where(tri, b[..., :, None] - b[..., None, :], -jnp.inf))
    kb = k * beta[..., None]
    M = jnp.where(strict, jnp.einsum('bhnid,bhnjd->bhnij', kb, k) * L, 0.0)
    Amat = M + jnp.eye(C, dtype=M.dtype)
    rhs = jnp.concatenate([v * beta[..., None], kb * jnp.exp(b)[..., None]], axis=-1)
    sol = lax.linalg.triangular_solve(Amat, rhs, left_side=True, lower=True, unit_diagonal=True)
    u, w = sol[..., :dv], sol[..., dv:]
    Aqk = jnp.where(tri, jnp.einsum('bhnid,bhnjd->bhnij', q, k) * L, 0.0)
    q_dec = q * jnp.exp(b)[..., None]
    k_dec = k * jnp.exp(b[..., -1:] - b)[..., None]
    dec_last = jnp.exp(b[..., -1])
    xs = tuple(jnp.moveaxis(a, 2, 0) for a in (u, w, Aqk, q_dec, k_dec, dec_last))

    def step(S, inp):
        uc, wc, Ac, qdc, kdc, dl = inp
        v_new = uc - jnp.einsum('bhcd,bhdv->bhcv', wc, S)
        o = jnp.einsum('bhcd,bhdv->bhcv', qdc, S) + jnp.einsum('bhij,bhjv->bhiv', Ac, v_new)
        S = dl[..., None, None] * S + jnp.einsum('bhcd,bhcv->bhdv', kdc, v_new)
        return S, o

    _, o = lax.scan(step, jnp.zeros((B, H, dk, dv), q.dtype), xs)
    return jnp.moveaxis(o, 0, 2).reshape(B, H, T, dv)


def _deltanet_mixer(qkv, z, beta_raw, a_raw, conv_w, a_log, dt_bias, norm_g):
    B, T, _ = qkv.shape
    qkv = jax.nn.silu(_short_conv(qkv, conv_w))
    q, k, v = jnp.split(qkv, [DN_HEADS * DN_DK, 2 * DN_HEADS * DN_DK], axis=-1)
    qh = _l2norm(_split_heads(q, DN_HEADS)) * (DN_DK ** -0.5)
    kh = _l2norm(_split_heads(k, DN_HEADS))
    vh = _split_heads(v, DN_HEADS)
    beta = jax.nn.sigmoid(beta_raw.reshape(B, T, 2, DN_HEADS)).transpose(2, 0, 3, 1)
    a = a_raw.reshape(B, T, 2, DN_HEADS).transpose(2, 0, 1, 3)
    g = -jnp.exp(a_log)[:, None, None, :] * jax.nn.softplus(a + dt_bias[:, None, None, :])
    g = g.transpose(0, 1, 3, 2)
    o_f = _gated_delta_chunk(qh, kh, vh, g[0], beta[0])
    fl = lambda x: jnp.flip(x, axis=2)
    o_b = fl(_gated_delta_chunk(fl(qh), fl(kh), fl(vh), fl(g[1]), fl(beta[1])))
    o = _rmsnorm(o_f + o_b, norm_g)
    return _merge_heads(o) * jax.nn.silu(z)


def _trunk(x, p):
    B, T, D = x.shape
    bias = p["band_bias"]
    x2 = x.reshape(B * T, D)
    for l in range(DEPTH):
        dn_qkvn, small, swa_q, swa_kv, gla_qk, gla_v, gla_g, dn_z, dn_gates = in_proj(
            x2, p["mix_pre_g"][l], p["w_in_packed"][l], p["dn_conv_w"][l], p["dn_a_log"][l], p["dn_dt_bias"][l], T)
        o_swa = swa_attention(swa_q, swa_kv, p["swa_sink"][l], bias, B, T)
        gla_f, gla_b, dn_f, dn_b = seq_mixers(gla_qk, gla_v, small, p["gla_w_lr2"][l], p["gla_b_lr"][l],
                                              dn_qkvn, dn_gates, B, T)
        x2 = out_mlp(gla_f, gla_b, gla_g, o_swa, dn_f, dn_b, dn_z, x2, p["w_out_bf16"][l], p["gla_norm_g"][l],
                     p["dn_norm_g"][l], p["mix_post_g"][l], p["mlp_pre_g"][l],
                     p["mlp_w1_bf16"][l], p["mlp_w2_bf16"][l], p["mlp_post_g"][l])
    return x2.reshape(B, T, D)


def _prepare_params(rel_bias_table, mix_pre_g, w_in, gla_w_lr2, gla_b_lr, gla_norm_g, swa_sink, dn_conv_w, dn_a_log,
                    dn_dt_bias, dn_norm_g, w_out, mix_post_g, mlp_pre_g, mlp_w1, mlp_w2, mlp_post_g):
    return dict(mix_pre_g=mix_pre_g, gla_w_lr2=gla_w_lr2, gla_b_lr=gla_b_lr,
                gla_norm_g=gla_norm_g, swa_sink=swa_sink, dn_conv_w=dn_conv_w, dn_a_log=dn_a_log,
                dn_dt_bias=dn_dt_bias, dn_norm_g=dn_norm_g, mix_post_g=mix_post_g, mlp_pre_g=mlp_pre_g,
                mlp_post_g=mlp_post_g,
                band_bias=band_bias(rel_bias_table),
                w_in_packed=jax.vmap(pack_w_in)(w_in), w_out_bf16=w_out.astype(jnp.bfloat16),
                mlp_w1_bf16=mlp_w1.astype(jnp.bfloat16), mlp_w2_bf16=mlp_w2.astype(jnp.bfloat16))


def kernel(x_prompt, x_sample, rel_bias_table, mix_pre_g, w_in, gla_w_lr2, gla_b_lr, gla_norm_g, swa_sink, dn_conv_w, dn_a_log, dn_dt_bias, dn_norm_g, w_out, mix_post_g, mlp_pre_g, mlp_w1, mlp_w2, mlp_post_g):
    p = _prepare_params(rel_bias_table, mix_pre_g, w_in, gla_w_lr2, gla_b_lr, gla_norm_g, swa_sink, dn_conv_w,
                        dn_a_log, dn_dt_bias, dn_norm_g, w_out, mix_post_g, mlp_pre_g, mlp_w1, mlp_w2, mlp_post_g)
    return (_trunk(x_prompt, p), _trunk(x_sample, p))
```

```python
import functools
import math

import numpy as np

import jax
import jax.numpy as jnp
from jax import lax
from jax.experimental import pallas as pl
from jax.experimental.pallas import tpu as pltpu

D_MODEL = 1024
DEPTH = 4
HEAD_DIM = 64
EPS = 1e-6
GLA_HEADS = 4
GLA_DK = 32
GLA_DV = 64
GLA_LOWRANK = 16
GLA_GATE_NORMALIZER = 16.0
GLA_CHUNK = 32
SWA_HEADS = 8
SWA_KV_HEADS = 2
SWA_WINDOW = 128
SWA_BLOCK = 128
NUM_BUCKETS = 32
MAX_DISTANCE = 128
DN_HEADS = 4
DN_DK = 64
DN_DV = 64
DN_CONV = 5
DN_CHUNK = 64
GLA_WIDTH = GLA_HEADS * GLA_DV
SWA_WIDTH = SWA_HEADS * HEAD_DIM
DN_WIDTH = DN_HEADS * DN_DV
D_MIX = GLA_WIDTH + SWA_WIDTH + DN_WIDTH
D_FF = 4 * D_MODEL
DN_QKV = DN_HEADS * (2 * DN_DK + DN_DV)
IN_SIZES = (GLA_HEADS * GLA_DK, GLA_HEADS * GLA_DK, GLA_WIDTH, GLA_WIDTH, 2 * GLA_LOWRANK,
            SWA_WIDTH, SWA_KV_HEADS * HEAD_DIM, SWA_KV_HEADS * HEAD_DIM,
            DN_QKV, DN_WIDTH, 2 * DN_HEADS, 2 * DN_HEADS)
D_IN = sum(IN_SIZES)

VMEM_LIMIT_BYTES = 56 * 1024 * 1024
ROW_TILE = 512
FF_CHUNK = 1024


def _rms(xf, g):
    return xf * lax.rsqrt(jnp.mean(xf * xf, axis=-1, keepdims=True) + EPS) * g


def _const_spec(shape):
    return pl.BlockSpec(shape, lambda i: (0,) * len(shape), pipeline_mode=pl.Buffered(1))


_O = {}
_acc = 0
for _name, _width in (("dn_qkv", DN_QKV), ("small", 128), ("swa_q", SWA_WIDTH), ("swa_kv", 4 * 2 * HEAD_DIM),
                      ("gla_qk", 2 * GLA_HEADS * GLA_DK), ("gla_v", GLA_WIDTH), ("gla_g", GLA_WIDTH),
                      ("dn_z", DN_WIDTH)):
    _O[_name] = (_acc, _width)
    _acc += _width
D_IN_PACKED = _acc
D_IN_LEAD = _O["small"][0] + _O["small"][1]
_OUT_DTYPES = {"swa_q": jnp.bfloat16, "swa_kv": jnp.bfloat16}


def pack_w_in(w_in):
    w_in = w_in.astype(jnp.bfloat16)
    offs, acc = [], 0
    for s in IN_SIZES:
        offs.append(acc)
        acc += s
    gq, gk, gv, gg, glr, sq, sk, sv, dqkv, dz, dbeta, da = [w_in[:, o:o + s] for o, s in zip(offs, IN_SIZES)]
    hd = HEAD_DIM
    kv = [sk[:, 0:hd], sk[:, 0:hd], sk[:, hd:], sk[:, hd:], sv[:, 0:hd], sv[:, 0:hd], sv[:, hd:], sv[:, hd:]]
    small = jnp.concatenate([glr, dbeta, da], axis=1)
    small = jnp.pad(small, ((0, 0), (0, 128 - small.shape[1])))
    pieces = dict(dn_qkv=dqkv, small=small, swa_q=sq, swa_kv=jnp.concatenate(kv, axis=1),
                  gla_qk=jnp.concatenate([gq, gk], axis=1), gla_v=gv, gla_g=gg, dn_z=dz)
    return jnp.concatenate([pieces[name] for name in _O], axis=1)


def _in_proj_kernel(tiles_per_seq, xp_ref, xc_ref, xn_ref, g_ref, w_ref, cw_ref, na_ref, dt_ref, *refs):
    o_refs, gate_ref, ext_ref = refs[:len(_O)], refs[len(_O)], refs[len(_O) + 1]
    tm, H = xc_ref.shape[0], DN_HALO
    i = pl.program_id(0)
    first = (i % tiles_per_seq) == 0
    last = (i % tiles_per_seq) == tiles_per_seq - 1
    x = jnp.concatenate([xp_ref[...], xc_ref[...], xn_ref[...]], axis=0)
    h = _rms(x, g_ref[...])
    h_tile = h[H:H + tm].astype(jnp.bfloat16)
    out = dict(zip(_O, o_refs))
    y = jnp.dot(h.astype(jnp.bfloat16), w_ref[:, :D_IN_LEAD], preferred_element_type=jnp.float32)
    off, width = _O["dn_qkv"]
    ext_ref[0:H, :] = jnp.where(first, 0.0, y[0:H, off:off + width])
    ext_ref[H:H + tm, :] = y[H:H + tm, off:off + width]
    ext_ref[H + tm:, :] = jnp.where(last, 0.0, y[H + tm:, off:off + width])
    half = DN_CONV // 2
    w = DN_HEADS * DN_DK
    ind = _group_indicator(w, DN_DK)
    sumsq = lambda a: jnp.dot((a * a).astype(jnp.bfloat16), ind, preferred_element_type=jnp.float32)

    def project(names):
        lo, hi = _O[names[0]][0], _O[names[-1]][0] + _O[names[-1]][1]
        part = jnp.dot(h_tile, w_ref[:, lo:hi], preferred_element_type=jnp.float32)
        for name in names:
            o, wd = _O[name]
            out[name][...] = part[:, o - lo:o - lo + wd].astype(out[name].dtype)

    def deltanet_rows(r0, r1):
        acc = ext_ref[H - half + r0:H - half + r1, :] * cw_ref[0:1, :]
        for j in range(1, DN_CONV):
            acc = acc + ext_ref[H - half + j + r0:H - half + j + r1, :] * cw_ref[j:j + 1, :]
        c = acc * (1.0 / (1.0 + jnp.exp(-acc)))
        q, k = c[:, :w], c[:, w:2 * w]
        out["dn_qkv"][r0:r1, :w] = q * lax.rsqrt(sumsq(q) + EPS) * (DN_DK ** -0.5)
        out["dn_qkv"][r0:r1, w:2 * w] = k * lax.rsqrt(sumsq(k) + EPS)
        out["dn_qkv"][r0:r1, 2 * w:] = c[:, 2 * w:]

    groups = (["swa_q", "swa_kv"], ["gla_qk", "gla_v", "gla_g", "dn_z"])
    for n, names in enumerate(groups):
        project(names)
        deltanet_rows(n * tm // len(groups), (n + 1) * tm // len(groups))
    off, width = _O["small"]
    s = y[H:H + tm, off:off + width]
    out["small"][...] = s
    lane = lax.broadcasted_iota(jnp.int32, s.shape, 1)
    beta = 1.0 / (1.0 + jnp.exp(-s))
    a = s + dt_ref[...]
    gdec = na_ref[...] * (jnp.maximum(a, 0.0) + jnp.log1p(jnp.exp(-jnp.abs(a))))
    gate_ref[...] = jnp.where((lane >= DN_BETA_LANE) & (lane < DN_A_LANE), beta,
                              jnp.where((lane >= DN_A_LANE) & (lane < DN_A_LANE + 2 * DN_HEADS), gdec, 0.0))


def in_proj(x2d, g, w_packed, conv_w, a_log, dt_bias, seq):
    n, d = x2d.shape
    tm = min(ROW_TILE, seq)
    hb = tm // DN_HALO
    nh = n // DN_HALO
    lanes = lambda v: jnp.zeros((1, 128), jnp.float32).at[0, DN_A_LANE:DN_A_LANE + 2 * DN_HEADS].set(v.reshape(-1))
    row = lambda w: pl.BlockSpec((tm, w), lambda i: (i, 0))
    return pl.pallas_call(
        functools.partial(_in_proj_kernel, seq // tm),
        grid=(n // tm,),
        in_specs=[pl.BlockSpec((DN_HALO, d), lambda i: (jnp.maximum(i * hb - 1, 0), 0)),
                  row(d),
                  pl.BlockSpec((DN_HALO, d), lambda i: (jnp.minimum((i + 1) * hb, nh - 1), 0)),
                  _const_spec((1, d)),
                  _const_spec((d, D_IN_PACKED)),
                  _const_spec((DN_CONV, DN_QKV)), _const_spec((1, 128)), _const_spec((1, 128))],
        out_specs=[row(w) for _, w in _O.values()] + [row(128)],
        out_shape=[jax.ShapeDtypeStruct((n, w), _OUT_DTYPES.get(name, jnp.float32))
                   for name, (_, w) in _O.items()] + [jax.ShapeDtypeStruct((n, 128), jnp.float32)],
        scratch_shapes=[pltpu.VMEM((tm + 2 * DN_HALO, DN_QKV), jnp.float32)],
        compiler_params=pltpu.CompilerParams(dimension_semantics=("arbitrary",),
                                             vmem_limit_bytes=VMEM_LIMIT_BYTES),
        name="in_proj",
    )(x2d, x2d, x2d, g.reshape(1, d), w_packed, conv_w.astype(jnp.float32),
      lanes(-jnp.exp(a_log.astype(jnp.float32))), lanes(dt_bias.astype(jnp.float32)))


SWA_Q_TILE = 512
SWA_GROUP = SWA_HEADS // SWA_KV_HEADS


def _t5_bucket(rel):
    nb = NUM_BUCKETS // 2
    max_exact = nb // 2
    base = jnp.where(rel > 0, nb, 0)
    n = jnp.abs(rel)
    nf = jnp.maximum(n, 1).astype(jnp.float32)
    large = max_exact + (jnp.log(nf / max_exact) / math.log(MAX_DISTANCE / max_exact)
                         * (nb - max_exact)).astype(jnp.int32)
    large = jnp.minimum(large, nb - 1)
    return base + jnp.where(n < max_exact, n, large)


def _band_bias_kernel(tab_ref, idx_ref, o_ref):
    W = SWA_BLOCK
    idx = idx_ref[...]
    i = lax.broadcasted_iota(jnp.int32, idx.shape, 0)
    j = lax.broadcasted_iota(jnp.int32, idx.shape, 1)
    in_band = jnp.abs(j - W - i) <= SWA_WINDOW
    for h in range(SWA_HEADS):
        acc = jnp.zeros(idx.shape, jnp.float32)
        for b in range(NUM_BUCKETS):
            acc = jnp.where(idx == b, tab_ref[b, h], acc)
        o_ref[h] = jnp.where(in_band, acc, -jnp.inf)


def band_bias(table):
    W = SWA_BLOCK
    rel = jnp.arange(3 * W)[None, :] - W - jnp.arange(W)[:, None]
    idx = _t5_bucket(rel).astype(jnp.int32)
    return pl.pallas_call(
        _band_bias_kernel,
        in_specs=[pl.BlockSpec(memory_space=pltpu.SMEM), pl.BlockSpec(memory_space=pltpu.VMEM)],
        out_specs=pl.BlockSpec(memory_space=pltpu.VMEM),
        out_shape=jax.ShapeDtypeStruct((SWA_HEADS, W, 3 * W), jnp.float32),
        name="band_bias",
    )(table.astype(jnp.float32), idx)


def _swa_kernel(sink_ref, q_ref, kvp_ref, kvc_ref, kvn_ref, bias_ref, o_ref, kv_buf):
    W = SWA_BLOCK
    n_sub = q_ref.shape[0] // W
    n = pl.program_id(1)
    last = pl.num_programs(1) - 1
    kv_buf[0:W, :] = kvp_ref[...]
    kv_buf[W:W + n_sub * W, :] = kvc_ref[...]
    kv_buf[W + n_sub * W:, :] = kvn_ref[...]
    lane = lax.broadcasted_iota(jnp.int32, (W, 2 * HEAD_DIM), 1)
    lo = lane < HEAD_DIM
    lo3 = (lax.broadcasted_iota(jnp.int32, (3 * W, 2 * HEAD_DIM), 1) < HEAD_DIM).astype(jnp.float32).astype(jnp.bfloat16)
    hi3 = 1 - lo3
    key_blk = lax.broadcasted_iota(jnp.int32, (1, 3 * W), 1) // W
    neg = jnp.float32(-jnp.inf)
    for s in range(n_sub):
        edge = jnp.zeros((1, 3 * W), jnp.float32)
        if s == 0:
            edge = jnp.where((key_blk == 0) & (n == 0), neg, edge)
        if s == n_sub - 1:
            edge = jnp.where((key_blk == 2) & (n == last), neg, edge)
        rows = slice(s * W, (s + 1) * W)
        for g in range(SWA_KV_HEADS):
            kd = kv_buf[s * W:(s + 3) * W, g * 128:(g + 1) * 128]
            vd = kv_buf[s * W:(s + 3) * W, 256 + g * 128:256 + (g + 1) * 128]
            k2 = jnp.concatenate([kd * lo3, kd * hi3], axis=0)
            v2 = jnp.concatenate([jnp.concatenate([vd * lo3, lo3], axis=1),
                                  jnp.concatenate([vd * hi3, hi3], axis=1)], axis=0)
            for pair in range(SWA_GROUP // 2):
                col = (g * SWA_GROUP // 2 + pair) * 128
                qp = q_ref[rows, col:col + 128] * (HEAD_DIM ** -0.5)
                s2 = lax.dot_general(qp, k2, (((1,), (1,)), ((), ())),
                                     preferred_element_type=jnp.float32)
                ps, sink_terms = [], []
                for par in range(2):
                    h = g * SWA_GROUP + 2 * pair + par
                    sc = s2[:, par * 3 * W:(par + 1) * 3 * W] + bias_ref[h]
                    if s == 0 or s == n_sub - 1:
                        sc = sc + edge
                    sink = sink_ref[h]
                    m = jnp.maximum(jnp.max(sc, axis=-1, keepdims=True), sink)
                    ps.append(jnp.exp((sc - m).astype(jnp.bfloat16)))
                    sink_terms.append(jnp.exp(sink - m))
                o2 = jnp.dot(jnp.concatenate(ps, axis=1), v2, preferred_element_type=jnp.float32)
                denom = o2[:, 128:] + jnp.where(lo, sink_terms[0], sink_terms[1])
                o_ref[rows, col:col + 128] = (o2[:, :128] * (1.0 / denom)).astype(o_ref.dtype)


def swa_attention(q, kv, sink, bias, batch, seq):
    W = SWA_BLOCK
    tq = min(SWA_Q_TILE, seq)
    n_sub = tq // W
    nq = seq // tq
    nb = seq // W
    return pl.pallas_call(
        _swa_kernel,
        grid=(batch, nq),
        in_specs=[pl.BlockSpec(memory_space=pltpu.SMEM),
                  pl.BlockSpec((tq, SWA_WIDTH), lambda b, i: (b * nq + i, 0)),
                  pl.BlockSpec((W, 512), lambda b, i: (b * nb + jnp.maximum(i * n_sub - 1, 0), 0)),
                  pl.BlockSpec((tq, 512), lambda b, i: (b * nq + i, 0)),
                  pl.BlockSpec((W, 512), lambda b, i: (b * nb + jnp.minimum((i + 1) * n_sub, nb - 1), 0)),
                  pl.BlockSpec((SWA_HEADS, W, 3 * W), lambda b, i: (0, 0, 0), pipeline_mode=pl.Buffered(1))],
        out_specs=pl.BlockSpec((tq, SWA_WIDTH), lambda b, i: (b * nq + i, 0)),
        out_shape=jax.ShapeDtypeStruct((batch * seq, SWA_WIDTH), jnp.bfloat16),
        scratch_shapes=[pltpu.VMEM((tq + 2 * W, 512), jnp.bfloat16)],
        compiler_params=pltpu.CompilerParams(dimension_semantics=("arbitrary", "arbitrary"),
                                             vmem_limit_bytes=VMEM_LIMIT_BYTES),
        name="swa",
    )(sink.astype(jnp.float32), q, kv, kv, kv, bias)


GLA_TILE = 256
GLA_SUB = 16
_NT = (((1,), (1,)), ((), ()))
_TN = (((0,), (0,)), ((), ()))


def _seg_cumsum(x, seg, reverse):
    n = x.shape[0]
    rowmod = lax.broadcasted_iota(jnp.int32, x.shape, 0) % seg
    sh = 1
    while sh < seg:
        if reverse:
            x = x + jnp.where(rowmod < seg - sh, pltpu.roll(x, n - sh, 0), 0.0)
        else:
            x = x + jnp.where(rowmod >= sh, pltpu.roll(x, sh, 0), 0.0)
        sh *= 2
    return x


def _log_sigmoid(x):
    return jnp.minimum(x, 0.0) - jnp.log1p(jnp.exp(-jnp.abs(x)))


def _gla_constants(tb):
    S, hk = GLA_SUB, GLA_HEADS * GLA_DK
    r = np.arange(tb)
    same = (r[:, None] // S) == (r[None, :] // S)
    tri = np.stack([same & (r[None, :] <= r[:, None]), same & (r[None, :] >= r[:, None])])
    src = np.arange(S * hk)
    d, h = src // hk, (src % hk) // GLA_DK
    lane = np.arange(128)
    place = np.stack([lane[None, :] == (32 * h + S - d)[:, None], lane[None, :] == (32 * h + d)[:, None]])
    return jnp.asarray(tri, jnp.bfloat16), jnp.asarray(place, jnp.bfloat16)


def _gla_direction(qk_ref, v_ref, sm_ref, wg_ref, bg_ref, tri_ref, place_ref, st_ref, o_ref, reverse):
    tb, S = qk_ref.shape[0], GLA_SUB
    hk = GLA_HEADS * GLA_DK
    logits = jnp.dot(sm_ref[...].astype(jnp.bfloat16), wg_ref[...], preferred_element_type=jnp.float32)
    g = _log_sigmoid(logits + bg_ref[...]) * (1.0 / GLA_GATE_NORMALIZER)
    g_hi = g.astype(jnp.bfloat16)
    g_lo = (g - g_hi.astype(jnp.float32)).astype(jnp.bfloat16)
    cs = (jnp.dot(tri_ref[...], g_hi, preferred_element_type=jnp.float32) +
          jnp.dot(tri_ref[...], g_lo, preferred_element_type=jnp.float32))
    yield
    a = jnp.exp(g)
    q = qk_ref[:, :hk] * (GLA_DK ** -0.5)
    k = qk_ref[:, hk:]
    v = v_ref[...]
    rowmod = lax.broadcasted_iota(jnp.int32, (tb, hk), 0) % S
    has_prev = (rowmod < S - 1) if reverse else (rowmod >= 1)
    kd = k
    ps = [(q * kd).astype(jnp.bfloat16)]
    for d in range(1, S):
        kd = a * jnp.where(has_prev, pltpu.roll(kd, tb - 1 if reverse else 1, 0), 0.0)
        ps.append((q * kd).astype(jnp.bfloat16))
    scores = jnp.dot(jnp.concatenate(ps, axis=1), place_ref[...], preferred_element_type=jnp.float32)
    yield
    head_of_lane = lax.broadcasted_iota(jnp.int32, (S, GLA_WIDTH), 1) // GLA_DV
    zeros = jnp.zeros((S, GLA_WIDTH), jnp.float32)
    acc = {}
    for s in range(tb // S):
        rows = slice(s * S, (s + 1) * S)
        band = pltpu.roll(scores[rows], 0, 1, stride=1, stride_axis=0).astype(jnp.bfloat16)
        v_blk = v[rows]
        pieces = []
        for h in range(GLA_HEADS):
            vh = jnp.where(head_of_lane == h, v_blk, 0.0)
            pieces += [vh, zeros] if reverse else [zeros, vh]
        acc[s] = jnp.dot(band, jnp.concatenate(pieces, axis=0).astype(jnp.bfloat16),
                         preferred_element_type=jnp.float32)
        if s % 2 == 1:
            yield
    qd = (q * jnp.exp(cs)).astype(jnp.bfloat16)
    blockdiag = (lax.broadcasted_iota(jnp.int32, (hk, GLA_WIDTH), 0) // GLA_DK ==
                 lax.broadcasted_iota(jnp.int32, (hk, GLA_WIDTH), 1) // GLA_DV)
    n_blk = tb // S
    order = range(n_blk - 1, -1, -1) if reverse else range(n_blk)
    edge_row = 0 if reverse else S - 1
    edges = jnp.concatenate([cs[s * S + edge_row:s * S + edge_row + 1] for s in range(n_blk)], axis=0)
    dec_cols = jnp.exp(edges).T
    upd = {}
    for s in order:
        rows = slice(s * S, (s + 1) * S)
        kt = (k[rows] * jnp.exp(edges[s:s + 1] - cs[rows])).astype(jnp.bfloat16)
        upd[s] = jnp.where(blockdiag, lax.dot_general(kt, v[rows].astype(jnp.bfloat16), _TN,
                                                      preferred_element_type=jnp.float32), 0.0)
        yield
    st = st_ref[...]
    for s in order:
        rows = slice(s * S, (s + 1) * S)
        o_ref[rows, :] = acc[s] + jnp.dot(qd[rows], st.astype(jnp.bfloat16), preferred_element_type=jnp.float32)
        st = st * dec_cols[:, s:s + 1] + upd[s]
        yield
    st_ref[...] = st


def _gla_kernel(qkf_ref, vf_ref, smf_ref, qkb_ref, vb_ref, smb_ref, wg_ref, bg_ref, tri_ref, place_ref,
                of_ref, ob_ref, stf_ref, stb_ref):
    @pl.when(pl.program_id(1) == 0)
    def _():
        stf_ref[...] = jnp.zeros_like(stf_ref)
        stb_ref[...] = jnp.zeros_like(stb_ref)

    _lockstep([_gla_direction(qkf_ref, vf_ref, smf_ref, wg_ref.at[0], bg_ref.at[0], tri_ref.at[0], place_ref.at[0],
                              stf_ref, of_ref, False),
               _gla_direction(qkb_ref, vb_ref, smb_ref, wg_ref.at[1], bg_ref.at[1], tri_ref.at[1], place_ref.at[1],
                              stb_ref, ob_ref, True)])


def gla_scan(gla_qk, gla_v, small, w_lr2, b_lr, batch, seq):
    hk = GLA_HEADS * GLA_DK
    tb = min(GLA_TILE, seq)
    nt = seq // tb
    wg = jnp.zeros((2, 128, hk), jnp.float32)
    for z in range(2):
        wg = wg.at[z, z * GLA_LOWRANK:(z + 1) * GLA_LOWRANK].set(w_lr2[z])
    tri, place = _gla_constants(tb)
    fwd = lambda w: pl.BlockSpec((tb, w), lambda b, i: (b * nt + i, 0))
    bwd = lambda w: pl.BlockSpec((tb, w), lambda b, i: (b * nt + nt - 1 - i, 0))
    const = lambda a: pl.BlockSpec(a.shape, lambda b, i: (0,) * a.ndim)
    out = jax.ShapeDtypeStruct((batch * seq, GLA_WIDTH), jnp.float32)
    return pl.pallas_call(
        _gla_kernel,
        grid=(batch, nt),
        in_specs=[fwd(2 * hk), fwd(GLA_WIDTH), fwd(128), bwd(2 * hk), bwd(GLA_WIDTH), bwd(128),
                  pl.BlockSpec((2, 128, hk), lambda b, i: (0, 0, 0)),
                  pl.BlockSpec((2, 1, hk), lambda b, i: (0, 0, 0)), const(tri), const(place)],
        out_specs=[fwd(GLA_WIDTH), bwd(GLA_WIDTH)],
        out_shape=[out, out],
        scratch_shapes=[pltpu.VMEM((hk, GLA_WIDTH), jnp.float32), pltpu.VMEM((hk, GLA_WIDTH), jnp.float32)],
        compiler_params=pltpu.CompilerParams(dimension_semantics=("arbitrary", "arbitrary"),
                                             vmem_limit_bytes=VMEM_LIMIT_BYTES),
        name="gla",
    )(gla_qk, gla_v, small, gla_qk, gla_v, small, wg.astype(jnp.bfloat16), b_lr.reshape(2, 1, hk).astype(jnp.float32),
      tri, place)


DN_PREP_TILE = 512
DN_HALO = 8
DN_TILE = 256
DN_BETA_LANE = 2 * GLA_LOWRANK
DN_A_LANE = DN_BETA_LANE + 2 * DN_HEADS


def _split_dot(x, w_bf16):
    hi = x.astype(jnp.bfloat16)
    lo = (x - hi.astype(jnp.float32)).astype(jnp.bfloat16)
    return (jnp.dot(hi, w_bf16, preferred_element_type=jnp.float32) +
            jnp.dot(lo, w_bf16, preferred_element_type=jnp.float32))


def _group_indicator(n, group):
    r = lax.broadcasted_iota(jnp.int32, (n, n), 0) // group
    c = lax.broadcasted_iota(jnp.int32, (n, n), 1) // group
    return (r == c).astype(jnp.bfloat16)


def _dn_prep_kernel(xp_ref, xc_ref, xn_ref, sm_ref, cw_ref, na_ref, dt_ref, qkv_ref, gate_ref, ext_ref):
    tm = xc_ref.shape[0]
    i = pl.program_id(1)
    last = pl.num_programs(1) - 1
    H = DN_HALO
    ext_ref[0:H, :] = jnp.where(i == 0, 0.0, xp_ref[...])
    ext_ref[H:H + tm, :] = xc_ref[...]
    ext_ref[H + tm:, :] = jnp.where(i == last, 0.0, xn_ref[...])
    half = DN_CONV // 2
    acc = ext_ref[H - half:H - half + tm, :] * cw_ref[0:1, :]
    for j in range(1, DN_CONV):
        acc = acc + ext_ref[H - half + j:H - half + j + tm, :] * cw_ref[j:j + 1, :]
    y = acc * (1.0 / (1.0 + jnp.exp(-acc)))
    ind = _group_indicator(DN_HEADS * DN_DK, DN_DK)
    w = DN_HEADS * DN_DK
    q, k = y[:, :w], y[:, w:2 * w]
    qkv_ref[:, :w] = q * lax.rsqrt(_split_dot(q * q, ind) + EPS) * (DN_DK ** -0.5)
    qkv_ref[:, w:2 * w] = k * lax.rsqrt(_split_dot(k * k, ind) + EPS)
    qkv_ref[:, 2 * w:] = y[:, 2 * w:]
    s = sm_ref[...]
    lane = lax.broadcasted_iota(jnp.int32, s.shape, 1)
    beta = 1.0 / (1.0 + jnp.exp(-s))
    a = s + dt_ref[...]
    g = na_ref[...] * (jnp.maximum(a, 0.0) + jnp.log1p(jnp.exp(-jnp.abs(a))))
    gate_ref[...] = jnp.where((lane >= DN_BETA_LANE) & (lane < DN_A_LANE), beta,
                              jnp.where((lane >= DN_A_LANE) & (lane < DN_A_LANE + 2 * DN_HEADS), g, 0.0))


def dn_prep(dn_qkv, small, conv_w, a_log, dt_bias, batch, seq):
    tm = min(DN_PREP_TILE, seq)
    nt = seq // tm
    hb = tm // DN_HALO
    nh = seq // DN_HALO
    pad = lambda x: jnp.zeros((1, 128), jnp.float32).at[0, DN_A_LANE:DN_A_LANE + 2 * DN_HEADS].set(x.reshape(-1))
    row = lambda w: pl.BlockSpec((tm, w), lambda b, i: (b * nt + i, 0))
    return pl.pallas_call(
        _dn_prep_kernel,
        grid=(batch, nt),
        in_specs=[pl.BlockSpec((DN_HALO, DN_QKV), lambda b, i: (b * nh + jnp.maximum(i * hb - 1, 0), 0)),
                  row(DN_QKV),
                  pl.BlockSpec((DN_HALO, DN_QKV), lambda b, i: (b * nh + jnp.minimum((i + 1) * hb, nh - 1), 0)),
                  row(128),
                  pl.BlockSpec((DN_CONV, DN_QKV), lambda b, i: (0, 0)),
                  pl.BlockSpec((1, 128), lambda b, i: (0, 0)),
                  pl.BlockSpec((1, 128), lambda b, i: (0, 0))],
        out_specs=[row(DN_QKV), row(128)],
        out_shape=[jax.ShapeDtypeStruct((batch * seq, DN_QKV), jnp.float32),
                   jax.ShapeDtypeStruct((batch * seq, 128), jnp.float32)],
        scratch_shapes=[pltpu.VMEM((tm + 2 * DN_HALO, DN_QKV), jnp.float32)],
        compiler_params=pltpu.CompilerParams(dimension_semantics=("arbitrary", "arbitrary"),
                                             vmem_limit_bytes=VMEM_LIMIT_BYTES),
        name="dn_prep",
    )(dn_qkv, dn_qkv, dn_qkv, small, conv_w.astype(jnp.float32), pad(-jnp.exp(a_log.astype(jnp.float32))),
      pad(dt_bias.astype(jnp.float32)))


def _mm(a, b, dims=None):
    a, b = a.astype(jnp.bfloat16), b.astype(jnp.bfloat16)
    if dims is None:
        return jnp.dot(a, b, preferred_element_type=jnp.float32)
    return lax.dot_general(a, b, dims, preferred_element_type=jnp.float32)


def _head_blockdiag(x):
    c, hw = x.shape
    w = hw // DN_HEADS
    stacked = jnp.concatenate([x] * DN_HEADS, axis=0)
    r = lax.broadcasted_iota(jnp.int32, stacked.shape, 0) // c
    l = lax.broadcasted_iota(jnp.int32, stacked.shape, 1) // w
    return jnp.where(r == l, stacked, 0.0)


def _dn_chunk_prepare(qkv, gate, direction):
    C, W = DN_CHUNK, DN_HEADS * DN_DK
    reverse = direction == 1
    q, k, v = qkv[:, :W], qkv[:, W:2 * W], qkv[:, 2 * W:]
    lane = lax.broadcasted_iota(jnp.int32, (C, 128), 1)
    g_lo = DN_A_LANE + direction * DN_HEADS
    b_lo = DN_BETA_LANE + direction * DN_HEADS
    cs = _seg_cumsum(jnp.where((lane >= g_lo) & (lane < g_lo + DN_HEADS), gate, 0.0), C, reverse)
    beta = jnp.where((lane >= b_lo) & (lane < b_lo + DN_HEADS), gate, 0.0)
    src = lax.broadcasted_iota(jnp.int32, (128, W), 0)
    dst = lax.broadcasted_iota(jnp.int32, (128, W), 1) // DN_DK
    expand = ((src - g_lo == dst) | (src - b_lo == dst)).astype(jnp.bfloat16)
    both = _split_dot(jnp.concatenate([cs, beta], axis=0), expand)
    bcol, bexp = both[:C], both[C:]
    yield
    row =lax.broadcasted_iota(jnp.int32, (C, W), 0)
    col = lax.broadcasted_iota(jnp.int32, (C, W), 1) % C
    diag = jnp.where(row == col, bcol, 0.0)
    diag_hi = diag.astype(jnp.bfloat16)
    diag_lo = (diag - diag_hi.astype(jnp.float32)).astype(jnp.bfloat16)
    ones = jnp.ones((C, C), jnp.bfloat16)
    brow = (jnp.dot(ones, diag_hi, preferred_element_type=jnp.float32) +
            jnp.dot(ones, diag_lo, preferred_element_type=jnp.float32))
    incl = (col >= row) if reverse else (col <= row)
    strict = (col > row) if reverse else (col < row)
    lmat = jnp.exp(jnp.where(incl, bcol - brow, -jnp.inf))
    yield
    kb = k * bexp
    kk = _mm(jnp.concatenate([kb, q], axis=0), _head_blockdiag(k), _NT)
    m = jnp.where(strict, kk[:C] * lmat, 0.0)
    aqk = jnp.where(incl, kk[C:] * lmat, 0.0)
    yield
    t = jnp.where(row == col, 1.0, 0.0)
    s = 1
    while s < C:
        same = (row // (2 * s)) == (col // (2 * s))
        r_hi, c_hi = (row % (2 * s)) >= s, (col % (2 * s)) >= s
        off = same & ((c_hi & ~r_hi) if reverse else (r_hi & ~c_hi))
        m_off = jnp.where(off, m, 0.0)
        if s == 1:
            t = t - m_off
        else:
            x = _mm(m_off, _head_blockdiag(t))
            yield
            t = t - _mm(t, _head_blockdiag(x))
            yield
        s *= 2
    ecs = jnp.exp(bcol)
    u = _mm(t, _head_blockdiag(v * bexp))
    w = _mm(t, _head_blockdiag(kb * ecs))
    yield
    edge = bcol[0:1] if reverse else bcol[C - 1:C]
    return dict(u=u, wq=jnp.concatenate([w, q * ecs], axis=0), aqk=aqk, k_dec=k * jnp.exp(edge - bcol),
                dl=jnp.exp(edge))


def _lockstep_stages(generators):
    results = [None] * len(generators)
    live = list(range(len(generators)))
    while live:
        still = []
        for i in live:
            try:
                next(generators[i])
                still.append(i)
            except StopIteration as stop:
                results[i] = stop.value
        live = still
        yield
    return results


def _lockstep(generators):
    stages = _lockstep_stages(generators)
    while True:
        try:
            next(stages)
        except StopIteration as stop:
            return stop.value


def _dn_state_chain(prepared, s_ref, o_ref, reverse):
    C, W = DN_CHUNK, DN_HEADS * DN_DK
    n_chunks = len(prepared)
    bd = (lax.broadcasted_iota(jnp.int32, (W, W), 0) // DN_DK ==
          lax.broadcasted_iota(jnp.int32, (W, W), 1) // DN_DV)
    s = s_ref[...]
    for c in (range(n_chunks - 1, -1, -1) if reverse else range(n_chunks)):
        pc = prepared[c]
        both = _mm(pc["wq"], s)
        v_new = pc["u"] - both[:C]
        yield
        o_ref[c * C:(c + 1) * C, :] = both[C:] + _mm(pc["aqk"], _head_blockdiag(v_new))
        s = s * pc["dl"] + jnp.where(bd, _mm(pc["k_dec"], v_new, _TN), 0.0)
        yield
    s_ref[...] = s


def _dn_tile(qkvf_ref, gf_ref, qkvb_ref, gb_ref, of_ref, ob_ref, sf_ref, sb_ref):
    C = DN_CHUNK
    n_chunks = qkvf_ref.shape[0] // C
    rows = lambda c: slice(c * C, (c + 1) * C)
    prepared = yield from _lockstep_stages(
        [_dn_chunk_prepare(qkvf_ref[rows(c), :], gf_ref[rows(c), :], 0) for c in range(n_chunks)] +
        [_dn_chunk_prepare(qkvb_ref[rows(c), :], gb_ref[rows(c), :], 1) for c in range(n_chunks)])
    yield from _lockstep_stages([_dn_state_chain(prepared[:n_chunks], sf_ref, of_ref, False),
                                 _dn_state_chain(prepared[n_chunks:], sb_ref, ob_ref, True)])


def _seq_mixers_kernel(qkf_ref, vf_ref, smf_ref, qkb_ref, vb_ref, smb_ref, wg_ref, bg_ref, tri_ref, place_ref,
                       dqkvf_ref, dgf_ref, dqkvb_ref, dgb_ref,
                       gof_ref, gob_ref, dof_ref, dob_ref, stf_ref, stb_ref, sf_ref, sb_ref):
    @pl.when(pl.program_id(1) == 0)
    def _():
        for ref in (stf_ref, stb_ref, sf_ref, sb_ref):
            ref[...] = jnp.zeros_like(ref)

    _lockstep([_dn_tile(dqkvf_ref, dgf_ref, dqkvb_ref, dgb_ref, dof_ref, dob_ref, sf_ref, sb_ref),
               _gla_direction(qkf_ref, vf_ref, smf_ref, wg_ref.at[0], bg_ref.at[0], tri_ref.at[0], place_ref.at[0],
                              stf_ref, gof_ref, False),
               _gla_direction(qkb_ref, vb_ref, smb_ref, wg_ref.at[1], bg_ref.at[1], tri_ref.at[1], place_ref.at[1],
                              stb_ref, gob_ref, True)])


def seq_mixers(gla_qk, gla_v, small, w_lr2, b_lr, dn_qkvn, dn_gates, batch, seq):
    hk = GLA_HEADS * GLA_DK
    W = DN_HEADS * DN_DK
    tb = min(GLA_TILE, seq)
    nt = seq // tb
    wg = jnp.zeros((2, 128, hk), jnp.float32)
    for z in range(2):
        wg = wg.at[z, z * GLA_LOWRANK:(z + 1) * GLA_LOWRANK].set(w_lr2[z])
    tri, place = _gla_constants(tb)
    fwd = lambda w: pl.BlockSpec((tb, w), lambda b, i: (b * nt + i, 0))
    bwd = lambda w: pl.BlockSpec((tb, w), lambda b, i: (b * nt + nt - 1 - i, 0))
    const = lambda a: pl.BlockSpec(a.shape, lambda b, i: (0,) * a.ndim)
    out = jax.ShapeDtypeStruct((batch * seq, GLA_WIDTH), jnp.float32)
    bg = b_lr.reshape(2, 1, hk).astype(jnp.float32)
    wg = wg.astype(jnp.bfloat16)
    return pl.pallas_call(
        _seq_mixers_kernel,
        grid=(batch, nt),
        in_specs=[fwd(2 * hk), fwd(GLA_WIDTH), fwd(128), bwd(2 * hk), bwd(GLA_WIDTH), bwd(128),
                  const(wg), const(bg), const(tri), const(place),
                  fwd(DN_QKV), fwd(128), bwd(DN_QKV), bwd(128)],
        out_specs=[fwd(GLA_WIDTH), bwd(GLA_WIDTH), fwd(DN_WIDTH), bwd(DN_WIDTH)],
        out_shape=[out, out, out, out],
        scratch_shapes=[pltpu.VMEM((hk, GLA_WIDTH), jnp.float32), pltpu.VMEM((hk, GLA_WIDTH), jnp.float32),
                        pltpu.VMEM((W, DN_WIDTH), jnp.float32), pltpu.VMEM((W, DN_WIDTH), jnp.float32)],
        compiler_params=pltpu.CompilerParams(dimension_semantics=("arbitrary", "arbitrary"),
                                             vmem_limit_bytes=VMEM_LIMIT_BYTES),
        name="seq_mixers",
    )(gla_qk, gla_v, small, gla_qk, gla_v, small, wg, bg, tri, place, dn_qkvn, dn_gates, dn_qkvn, dn_gates)


def _dn_kernel(qkvf_ref, gf_ref, qkvb_ref, gb_ref, of_ref, ob_ref, sf_ref, sb_ref):
    @pl.when(pl.program_id(1) == 0)
    def _():
        sf_ref[...] = jnp.zeros_like(sf_ref)
        sb_ref[...] = jnp.zeros_like(sb_ref)

    _lockstep([_dn_tile(qkvf_ref, gf_ref, qkvb_ref, gb_ref, of_ref, ob_ref, sf_ref, sb_ref)])


def dn_scan(qkv, gates, batch, seq):
    tb = min(DN_TILE, seq)
    nt = seq // tb
    W = DN_HEADS * DN_DK
    fwd = lambda w: pl.BlockSpec((tb, w), lambda b, i: (b * nt + i, 0))
    bwd = lambda w: pl.BlockSpec((tb, w), lambda b, i: (b * nt + nt - 1 - i, 0))
    out = jax.ShapeDtypeStruct((batch * seq, DN_WIDTH), jnp.float32)
    return pl.pallas_call(
        _dn_kernel,
        grid=(batch, nt),
        in_specs=[fwd(DN_QKV), fwd(128), bwd(DN_QKV), bwd(128)],
        out_specs=[fwd(DN_WIDTH), bwd(DN_WIDTH)],
        out_shape=[out, out],
        scratch_shapes=[pltpu.VMEM((W, DN_WIDTH), jnp.float32), pltpu.VMEM((W, DN_WIDTH), jnp.float32)],
        compiler_params=pltpu.CompilerParams(dimension_semantics=("arbitrary", "arbitrary"),
                                             vmem_limit_bytes=VMEM_LIMIT_BYTES),
        name="dn_scan",
    )(qkv, gates, qkv, gates)


def _gated_group_norm(o, gate, gn, ind):
    ms = _split_dot(o * o, ind) * (1.0 / HEAD_DIM)
    return (o * lax.rsqrt(ms + EPS) * gn * (gate * (1.0 / (1.0 + jnp.exp(-gate))))).astype(jnp.bfloat16)


def _out_mlp_kernel(glaf_ref, glab_ref, glag_ref, swa_ref, dnf_ref, dnb_ref, dnz_ref, x_ref, wo_ref,
                    gng_ref, gnd_ref, g1_ref, g2_ref, w1_ref, w2_ref, g3_ref, o_ref):
    ind = _group_indicator(GLA_WIDTH, HEAD_DIM)
    y_gla = _gated_group_norm(glaf_ref[...] + glab_ref[...], glag_ref[...], gng_ref[...], ind)
    y_dn = _gated_group_norm(dnf_ref[...] + dnb_ref[...], dnz_ref[...], gnd_ref[...], ind)
    a, b = GLA_WIDTH, GLA_WIDTH + SWA_WIDTH
    m = (jnp.dot(y_gla, wo_ref[0:a, :], preferred_element_type=jnp.float32) +
         jnp.dot(swa_ref[...], wo_ref[a:b, :], preferred_element_type=jnp.float32) +
         jnp.dot(y_dn, wo_ref[b:, :], preferred_element_type=jnp.float32))
    x1 = x_ref[...] + _rms(m, g1_ref[...])
    h = _rms(x1, g2_ref[...]).astype(jnp.bfloat16)
    acc = jnp.zeros_like(x1)
    for c in range(D_FF // FF_CHUNK):
        f = jnp.dot(h, w1_ref[:, c * FF_CHUNK:(c + 1) * FF_CHUNK], preferred_element_type=jnp.float32)
        f = jnp.square(jnp.maximum(f, 0.0)).astype(jnp.bfloat16)
        acc = acc + jnp.dot(f, w2_ref[c * FF_CHUNK:(c + 1) * FF_CHUNK, :], preferred_element_type=jnp.float32)
    o_ref[...] = x1 + _rms(acc, g3_ref[...])


def out_mlp(gla_f, gla_b, gla_g, swa_o, dn_f, dn_b, dn_z, x2d, wo, gla_norm_g, dn_norm_g, g1, g2, w1, w2, g3):
    n, d = x2d.shape
    tm = min(ROW_TILE, n)
    row = lambda w: pl.BlockSpec((tm, w), lambda i: (i, 0))
    vec = _const_spec((1, d))
    hvec = _const_spec((1, GLA_WIDTH))
    tile_gain = lambda g, h: jnp.tile(g.astype(jnp.float32), h).reshape(1, -1)
    return pl.pallas_call(
        _out_mlp_kernel,
        grid=(n // tm,),
        in_specs=[row(GLA_WIDTH), row(GLA_WIDTH), row(GLA_WIDTH), row(SWA_WIDTH), row(DN_WIDTH), row(DN_WIDTH),
                  row(DN_WIDTH), row(d), _const_spec((d, d)), hvec, hvec, vec, vec,
                  _const_spec((d, D_FF)), _const_spec((D_FF, d)), vec],
        out_specs=row(d),
        out_shape=jax.ShapeDtypeStruct((n, d), jnp.float32),
        compiler_params=pltpu.CompilerParams(dimension_semantics=("arbitrary",),
                                             vmem_limit_bytes=VMEM_LIMIT_BYTES),
        name="out_mlp",
    )(gla_f, gla_b, gla_g, swa_o, dn_f, dn_b, dn_z, x2d, wo, tile_gain(gla_norm_g, GLA_HEADS),
      tile_gain(dn_norm_g, DN_HEADS), g1.reshape(1, d), g2.reshape(1, d), w1, w2, g3.reshape(1, d))


def _rmsnorm(x, g):
    xf = x.astype(jnp.float32)
    y = xf * lax.rsqrt(jnp.mean(xf * xf, axis=-1, keepdims=True) + EPS)
    return (y * g.astype(jnp.float32)).astype(x.dtype)


def _l2norm(x):
    return x * lax.rsqrt(jnp.sum(x * x, axis=-1, keepdims=True) + EPS)


def _split_heads(x, h):
    B, T, _ = x.shape
    return x.reshape(B, T, h, -1).transpose(0, 2, 1, 3)


def _merge_heads(x):
    B, H, T, d = x.shape
    return x.transpose(0, 2, 1, 3).reshape(B, T, H * d)


def _split_columns(proj):
    idx, acc = [], 0
    for s in IN_SIZES[:-1]:
        acc += s
        idx.append(acc)
    return jnp.split(proj, idx, axis=-1)


def _gla_chunk_scan(q, k, v, gk):
    B, H, T, dk = q.shape
    dv = v.shape[-1]
    C = GLA_CHUNK
    N = T // C
    q, k, v, gk = [a.reshape(B, H, N, C, a.shape[-1]) for a in (q, k, v, gk)]
    b = jnp.cumsum(gk, axis=3)
    causal = jnp.tril(jnp.ones((C, C), dtype=bool))
    diff = jnp.where(causal[:, :, None], b[..., :, None, :] - b[..., None, :, :], -jnp.inf)
    A = jnp.einsum('bhnid,bhnjd,bhnijd->bhnij', q, k, jnp.exp(diff))
    o_intra = jnp.einsum('bhnij,bhnjv->bhniv', A, v)
    b_last = b[..., -1, :]
    k_dec = k * jnp.exp(b_last[..., None, :] - b)
    upd = jnp.einsum('bhncd,bhncv->nbhdv', k_dec, v)
    dec = jnp.moveaxis(jnp.exp(b_last), 2, 0)

    def step(S, inp):
        d, u = inp
        return d[..., None] * S + u, S

    _, S_prev = lax.scan(step, jnp.zeros((B, H, dk, dv), q.dtype), (dec, upd))
    o_inter = jnp.einsum('bhncd,nbhdv->bhncv', q * jnp.exp(b), S_prev)
    return (o_intra + o_inter).reshape(B, H, T, dv)


def _gla_mixer(q, k, v, g, lr, w_lr2, b_lr, norm_g):
    B, T, _ = q.shape
    lr = lr.reshape(B, T, 2, GLA_LOWRANK)
    logits = jnp.einsum('btzr,zrk->zbtk', lr, w_lr2) + b_lr[:, None, None, :]
    gk = jax.nn.log_sigmoid(logits) / GLA_GATE_NORMALIZER
    gk = gk.reshape(2, B, T, GLA_HEADS, GLA_DK).transpose(0, 1, 3, 2, 4)
    qh = _split_heads(q, GLA_HEADS) * (GLA_DK ** -0.5)
    kh = _split_heads(k, GLA_HEADS)
    vh = _split_heads(v, GLA_HEADS)
    o_f = _gla_chunk_scan(qh, kh, vh, gk[0])
    fl = lambda a: jnp.flip(a, axis=2)
    o_b = fl(_gla_chunk_scan(fl(qh), fl(kh), fl(vh), fl(gk[1])))
    o = _rmsnorm(o_f + o_b, norm_g)
    return _merge_heads(o) * jax.nn.silu(g)


def _short_conv(x, w):
    K = w.shape[0]
    p = K // 2
    T = x.shape[1]
    xp = jnp.pad(x, ((0, 0), (p, p), (0, 0)))
    acc = xp[:, 0:T] * w[0]
    for j in range(1, K):
        acc = acc + xp[:, j:j + T] * w[j]
    return acc


def _gated_delta_chunk(q, k, v, g, beta):
    B, H, T, dk = q.shape
    dv = v.shape[-1]
    C = DN_CHUNK
    N = T // C
    q, k, v = [a.reshape(B, H, N, C, a.shape[-1]) for a in (q, k, v)]
    g = g.reshape(B, H, N, C)
    beta = beta.reshape(B, H, N, C)
    b = jnp.cumsum(g, axis=-1)
    tri = jnp.tril(jnp.ones((C, C), dtype=bool))
    strict = jnp.tril(jnp.ones((C, C), dtype=bool), -1)
    L = jnp.exp(jnp.where(tri, b[..., :, None] - b[..., None, :], -jnp.inf))
    kb = k * beta[..., None]
    M = jnp.where(strict, jnp.einsum('bhnid,bhnjd->bhnij', kb, k) * L, 0.0)
    Amat = M + jnp.eye(C, dtype=M.dtype)
    rhs = jnp.concatenate([v * beta[..., None], kb * jnp.exp(b)[..., None]], axis=-1)
    sol = lax.linalg.triangular_solve(Amat, rhs, left_side=True, lower=True, unit_diagonal=True)
    u, w = sol[..., :dv], sol[..., dv:]
    Aqk = jnp.where(tri, jnp.einsum('bhnid,bhnjd->bhnij', q, k) * L, 0.0)
    q_dec = q * jnp.exp(b)[..., None]
    k_dec = k * jnp.exp(b[..., -1:] - b)[..., None]
    dec_last = jnp.exp(b[..., -1])
    xs = tuple(jnp.moveaxis(a, 2, 0) for a in (u, w, Aqk, q_dec, k_dec, dec_last))

    def step(S, inp):
        uc, wc, Ac, qdc, kdc, dl = inp
        v_new = uc - jnp.einsum('bhcd,bhdv->bhcv', wc, S)
        o = jnp.einsum('bhcd,bhdv->bhcv', qdc, S) + jnp.einsum('bhij,bhjv->bhiv', Ac, v_new)
        S = dl[..., None, None] * S + jnp.einsum('bhcd,bhcv->bhdv', kdc, v_new)
        return S, o

    _, o = lax.scan(step, jnp.zeros((B, H, dk, dv), q.dtype), xs)
    return jnp.moveaxis(o, 0, 2).reshape(B, H, T, dv)


def _deltanet_mixer(qkv, z, beta_raw, a_raw, conv_w, a_log, dt_bias, norm_g):
    B, T, _ = qkv.shape
    qkv = jax.nn.silu(_short_conv(qkv, conv_w))
    q, k, v = jnp.split(qkv, [DN_HEADS * DN_DK, 2 * DN_HEADS * DN_DK], axis=-1)
    qh = _l2norm(_split_heads(q, DN_HEADS)) * (DN_DK ** -0.5)
    kh = _l2norm(_split_heads(k, DN_HEADS))
    vh = _split_heads(v, DN_HEADS)
    beta = jax.nn.sigmoid(beta_raw.reshape(B, T, 2, DN_HEADS)).transpose(2, 0, 3, 1)
    a = a_raw.reshape(B, T, 2, DN_HEADS).transpose(2, 0, 1, 3)
    g = -jnp.exp(a_log)[:, None, None, :] * jax.nn.softplus(a + dt_bias[:, None, None, :])
    g = g.transpose(0, 1, 3, 2)
    o_f = _gated_delta_chunk(qh, kh, vh, g[0], beta[0])
    fl = lambda x: jnp.flip(x, axis=2)
    o_b = fl(_gated_delta_chunk(fl(qh), fl(kh), fl(vh), fl(g[1]), fl(beta[1])))
    o = _rmsnorm(o_f + o_b, norm_g)
    return _merge_heads(o) * jax.nn.silu(z)


def _trunk(x, p):
    B, T, D = x.shape
    bias = p["band_bias"]
    x2 = x.reshape(B * T, D)
    for l in range(DEPTH):
        dn_qkvn, small, swa_q, swa_kv, gla_qk, gla_v, gla_g, dn_z, dn_gates = in_proj(
            x2, p["mix_pre_g"][l], p["w_in_packed"][l], p["dn_conv_w"][l], p["dn_a_log"][l], p["dn_dt_bias"][l], T)
        o_swa = swa_attention(swa_q, swa_kv, p["swa_sink"][l], bias, B, T)
        gla_f, gla_b, dn_f, dn_b = seq_mixers(gla_qk, gla_v, small, p["gla_w_lr2"][l], p["gla_b_lr"][l],
                                              dn_qkvn, dn_gates, B, T)
        x2 = out_mlp(gla_f, gla_b, gla_g, o_swa, dn_f, dn_b, dn_z, x2, p["w_out_bf16"][l], p["gla_norm_g"][l],
                     p["dn_norm_g"][l], p["mix_post_g"][l], p["mlp_pre_g"][l],
                     p["mlp_w1_bf16"][l], p["mlp_w2_bf16"][l], p["mlp_post_g"][l])
    return x2.reshape(B, T, D)


def _prepare_params(rel_bias_table, mix_pre_g, w_in, gla_w_lr2, gla_b_lr, gla_norm_g, swa_sink, dn_conv_w, dn_a_log,
                    dn_dt_bias, dn_norm_g, w_out, mix_post_g, mlp_pre_g, mlp_w1, mlp_w2, mlp_post_g):
    return dict(mix_pre_g=mix_pre_g, gla_w_lr2=gla_w_lr2, gla_b_lr=gla_b_lr,
                gla_norm_g=gla_norm_g, swa_sink=swa_sink, dn_conv_w=dn_conv_w, dn_a_log=dn_a_log,
                dn_dt_bias=dn_dt_bias, dn_norm_g=dn_norm_g, mix_post_g=mix_post_g, mlp_pre_g=mlp_pre_g,
                mlp_post_g=mlp_post_g,
                band_bias=band_bias(rel_bias_table),
                w_in_packed=jax.vmap(pack_w_in)(w_in), w_out_bf16=w_out.astype(jnp.bfloat16),
                mlp_w1_bf16=mlp_w1.astype(jnp.bfloat16), mlp_w2_bf16=mlp_w2.astype(jnp.bfloat16))


def kernel(x_prompt, x_sample, rel_bias_table, mix_pre_g, w_in, gla_w_lr2, gla_b_lr, gla_norm_g, swa_sink, dn_conv_w, dn_a_log, dn_dt_bias, dn_norm_g, w_out, mix_post_g, mlp_pre_g, mlp_w1, mlp_w2, mlp_post_g):
    p = _prepare_params(rel_bias_table, mix_pre_g, w_in, gla_w_lr2, gla_b_lr, gla_norm_g, swa_sink, dn_conv_w,
                        dn_a_log, dn_dt_bias, dn_norm_g, w_out, mix_post_g, mlp_pre_g, mlp_w1, mlp_w2, mlp_post_g)
    return (_trunk(x_prompt, p), _trunk(x_sample, p))
```

```python
import functools
import math

import numpy as np

import jax
import jax.numpy as jnp
from jax import lax
from jax.experimental import pallas as pl
from jax.experimental.pallas import tpu as pltpu

D_MODEL = 1024
DEPTH = 4
HEAD_DIM = 64
EPS = 1e-6
GLA_HEADS = 4
GLA_DK = 32
GLA_DV = 64
GLA_LOWRANK = 16
GLA_GATE_NORMALIZER = 16.0
SWA_HEADS = 8
SWA_KV_HEADS = 2
SWA_WINDOW = 128
SWA_BLOCK = 128
NUM_BUCKETS = 32
MAX_DISTANCE = 128
DN_HEADS = 4
DN_DK = 64
DN_DV = 64
DN_CONV = 5
DN_CHUNK = 64
GLA_WIDTH = GLA_HEADS * GLA_DV
SWA_WIDTH = SWA_HEADS * HEAD_DIM
DN_WIDTH = DN_HEADS * DN_DV
D_FF = 4 * D_MODEL
DN_QKV = DN_HEADS * (2 * DN_DK + DN_DV)
IN_SIZES = (GLA_HEADS * GLA_DK, GLA_HEADS * GLA_DK, GLA_WIDTH, GLA_WIDTH, 2 * GLA_LOWRANK,
            SWA_WIDTH, SWA_KV_HEADS * HEAD_DIM, SWA_KV_HEADS * HEAD_DIM,
            DN_QKV, DN_WIDTH, 2 * DN_HEADS, 2 * DN_HEADS)
D_IN = sum(IN_SIZES)

VMEM_LIMIT_BYTES = 56 * 1024 * 1024
ROW_TILE = 512
IN_ROW_TILE = 1024
FF_CHUNK = 1024


def _rms(xf, g):
    return xf * lax.rsqrt(jnp.mean(xf * xf, axis=-1, keepdims=True) + EPS) * g


def _const_spec(shape):
    return pl.BlockSpec(shape, lambda i: (0,) * len(shape), pipeline_mode=pl.Buffered(1))


_O = {}
_acc = 0
for _name, _width in (("dn_qkv", DN_QKV), ("small", 128), ("swa_q", SWA_WIDTH), ("swa_kv", 4 * 2 * HEAD_DIM),
                      ("gla_qk", 2 * GLA_HEADS * GLA_DK), ("gla_v", GLA_WIDTH), ("gla_g", GLA_WIDTH),
                      ("dn_z", DN_WIDTH)):
    _O[_name] = (_acc, _width)
    _acc += _width
D_IN_PACKED = _acc
D_IN_LEAD = _O["small"][0] + _O["small"][1]
_OUT_DTYPES = {"swa_q": jnp.bfloat16, "swa_kv": jnp.bfloat16}


def pack_w_in(w_in):
    w_in = w_in.astype(jnp.bfloat16)
    offs, acc = [], 0
    for s in IN_SIZES:
        offs.append(acc)
        acc += s
    gq, gk, gv, gg, glr, sq, sk, sv, dqkv, dz, dbeta, da = [w_in[:, o:o + s] for o, s in zip(offs, IN_SIZES)]
    hd = HEAD_DIM
    kv = [sk[:, 0:hd], sk[:, 0:hd], sk[:, hd:], sk[:, hd:], sv[:, 0:hd], sv[:, 0:hd], sv[:, hd:], sv[:, hd:]]
    small = jnp.concatenate([glr, dbeta, da], axis=1)
    small = jnp.pad(small, ((0, 0), (0, 128 - small.shape[1])))
    pieces = dict(dn_qkv=dqkv, small=small, swa_q=sq, swa_kv=jnp.concatenate(kv, axis=1),
                  gla_qk=jnp.concatenate([gq, gk], axis=1), gla_v=gv, gla_g=gg, dn_z=dz)
    return jnp.concatenate([pieces[name] for name in _O], axis=1)


def _in_proj_kernel(tiles_per_seq, xp_ref, xc_ref, xn_ref, g_ref, w_ref, cw_ref, na_ref, dt_ref, *refs):
    o_refs, gate_ref, ext_ref = refs[:len(_O)], refs[len(_O)], refs[len(_O) + 1]
    tm, H = xc_ref.shape[0], DN_HALO
    i = pl.program_id(0)
    first = (i % tiles_per_seq) == 0
    last = (i % tiles_per_seq) == tiles_per_seq - 1
    x = jnp.concatenate([xp_ref[...], xc_ref[...], xn_ref[...]], axis=0)
    h = _rms(x, g_ref[...])
    h_tile = h[H:H + tm].astype(jnp.bfloat16)
    out = dict(zip(_O, o_refs))
    y = jnp.dot(h.astype(jnp.bfloat16), w_ref[:, :D_IN_LEAD], preferred_element_type=jnp.float32)
    off, width = _O["dn_qkv"]
    ext_ref[0:H, :] = jnp.where(first, 0.0, y[0:H, off:off + width])
    ext_ref[H:H + tm, :] = y[H:H + tm, off:off + width]
    ext_ref[H + tm:, :] = jnp.where(last, 0.0, y[H + tm:, off:off + width])
    half = DN_CONV // 2
    w = DN_HEADS * DN_DK
    ind = _group_indicator(w, DN_DK)
    sumsq = lambda a: jnp.dot((a * a).astype(jnp.bfloat16), ind, preferred_element_type=jnp.float32)

    def project(names):
        lo, hi = _O[names[0]][0], _O[names[-1]][0] + _O[names[-1]][1]
        part = jnp.dot(h_tile, w_ref[:, lo:hi], preferred_element_type=jnp.float32)
        for name in names:
            o, wd = _O[name]
            out[name][...] = part[:, o - lo:o - lo + wd].astype(out[name].dtype)

    def deltanet_rows(r0, r1):
        acc = ext_ref[H - half + r0:H - half + r1, :] * cw_ref[0:1, :]
        for j in range(1, DN_CONV):
            acc = acc + ext_ref[H - half + j + r0:H - half + j + r1, :] * cw_ref[j:j + 1, :]
        c = acc * (1.0 / (1.0 + jnp.exp(-acc)))
        q, k = c[:, :w], c[:, w:2 * w]
        out["dn_qkv"][r0:r1, :w] = q * lax.rsqrt(sumsq(q) + EPS) * (DN_DK ** -0.5)
        out["dn_qkv"][r0:r1, w:2 * w] = k * lax.rsqrt(sumsq(k) + EPS)
        out["dn_qkv"][r0:r1, 2 * w:] = c[:, 2 * w:]

    groups = (["swa_q", "swa_kv"], ["gla_qk", "gla_v", "gla_g", "dn_z"])
    for n, names in enumerate(groups):
        project(names)
        deltanet_rows(n * tm // len(groups), (n + 1) * tm // len(groups))
    off, width = _O["small"]
    s = y[H:H + tm, off:off + width]
    out["small"][...] = s
    lane = lax.broadcasted_iota(jnp.int32, s.shape, 1)
    beta = 1.0 / (1.0 + jnp.exp(-s))
    a = s + dt_ref[...]
    gdec = na_ref[...] * (jnp.maximum(a, 0.0) + jnp.log1p(jnp.exp(-jnp.abs(a))))
    gate_ref[...] = jnp.where((lane >= DN_BETA_LANE) & (lane < DN_A_LANE), beta,
                              jnp.where((lane >= DN_A_LANE) & (lane < DN_A_LANE + 2 * DN_HEADS), gdec, 0.0))


def in_proj(x2d, g, w_packed, conv_w, a_log, dt_bias, seq):
    n, d = x2d.shape
    tm = min(IN_ROW_TILE, seq)
    hb = tm // DN_HALO
    nh = n // DN_HALO
    lanes = lambda v: jnp.zeros((1, 128), jnp.float32).at[0, DN_A_LANE:DN_A_LANE + 2 * DN_HEADS].set(v.reshape(-1))
    row = lambda w: pl.BlockSpec((tm, w), lambda i: (i, 0))
    return pl.pallas_call(
        functools.partial(_in_proj_kernel, seq // tm),
        grid=(n // tm,),
        in_specs=[pl.BlockSpec((DN_HALO, d), lambda i: (jnp.maximum(i * hb - 1, 0), 0)),
                  row(d),
                  pl.BlockSpec((DN_HALO, d), lambda i: (jnp.minimum((i + 1) * hb, nh - 1), 0)),
                  _const_spec((1, d)),
                  _const_spec((d, D_IN_PACKED)),
                  _const_spec((DN_CONV, DN_QKV)), _const_spec((1, 128)), _const_spec((1, 128))],
        out_specs=[row(w) for _, w in _O.values()] + [row(128)],
        out_shape=[jax.ShapeDtypeStruct((n, w), _OUT_DTYPES.get(name, jnp.float32))
                   for name, (_, w) in _O.items()] + [jax.ShapeDtypeStruct((n, 128), jnp.float32)],
        scratch_shapes=[pltpu.VMEM((tm + 2 * DN_HALO, DN_QKV), jnp.float32)],
        compiler_params=pltpu.CompilerParams(dimension_semantics=("arbitrary",),
                                             vmem_limit_bytes=VMEM_LIMIT_BYTES),
        name="in_proj",
    )(x2d, x2d, x2d, g.reshape(1, d), w_packed, conv_w.astype(jnp.float32),
      lanes(-jnp.exp(a_log.astype(jnp.float32))), lanes(dt_bias.astype(jnp.float32)))


SWA_Q_TILE = 512
SWA_GROUP = SWA_HEADS // SWA_KV_HEADS


def _t5_bucket(rel):
    nb = NUM_BUCKETS // 2
    max_exact = nb // 2
    base = jnp.where(rel > 0, nb, 0)
    n = jnp.abs(rel)
    nf = jnp.maximum(n, 1).astype(jnp.float32)
    large = max_exact + (jnp.log(nf / max_exact) / math.log(MAX_DISTANCE / max_exact)
                         * (nb - max_exact)).astype(jnp.int32)
    large = jnp.minimum(large, nb - 1)
    return base + jnp.where(n < max_exact, n, large)


def _band_bias_kernel(tab_ref, idx_ref, o_ref):
    W = SWA_BLOCK
    idx = idx_ref[...]
    i = lax.broadcasted_iota(jnp.int32, idx.shape, 0)
    j = lax.broadcasted_iota(jnp.int32, idx.shape, 1)
    in_band = jnp.abs(j - W - i) <= SWA_WINDOW
    for h in range(SWA_HEADS):
        acc = jnp.zeros(idx.shape, jnp.float32)
        for b in range(NUM_BUCKETS):
            acc = jnp.where(idx == b, tab_ref[b, h], acc)
        o_ref[h] = jnp.where(in_band, acc, -jnp.inf)


def band_bias(table):
    W = SWA_BLOCK
    rel = jnp.arange(3 * W)[None, :] - W - jnp.arange(W)[:, None]
    idx = _t5_bucket(rel).astype(jnp.int32)
    return pl.pallas_call(
        _band_bias_kernel,
        in_specs=[pl.BlockSpec(memory_space=pltpu.SMEM), pl.BlockSpec(memory_space=pltpu.VMEM)],
        out_specs=pl.BlockSpec(memory_space=pltpu.VMEM),
        out_shape=jax.ShapeDtypeStruct((SWA_HEADS, W, 3 * W), jnp.float32),
        name="band_bias",
    )(table.astype(jnp.float32), idx)


def _swa_kernel(sink_ref, q_ref, kvp_ref, kvc_ref, kvn_ref, bias_ref, o_ref, kv_buf):
    W = SWA_BLOCK
    n_sub = q_ref.shape[0] // W
    n = pl.program_id(1)
    last = pl.num_programs(1) - 1
    kv_buf[0:W, :] = kvp_ref[...]
    kv_buf[W:W + n_sub * W, :] = kvc_ref[...]
    kv_buf[W + n_sub * W:, :] = kvn_ref[...]
    lane = lax.broadcasted_iota(jnp.int32, (W, 2 * HEAD_DIM), 1)
    lo = lane < HEAD_DIM
    lo3 = (lax.broadcasted_iota(jnp.int32, (3 * W, 2 * HEAD_DIM), 1) < HEAD_DIM).astype(jnp.float32).astype(jnp.bfloat16)
    hi3 = 1 - lo3
    key_blk = lax.broadcasted_iota(jnp.int32, (1, 3 * W), 1) // W
    neg = jnp.float32(-jnp.inf)
    for s in range(n_sub):
        edge = jnp.zeros((1, 3 * W), jnp.float32)
        if s == 0:
            edge = jnp.where((key_blk == 0) & (n == 0), neg, edge)
        if s == n_sub - 1:
            edge = jnp.where((key_blk == 2) & (n == last), neg, edge)
        rows = slice(s * W, (s + 1) * W)
        for g in range(SWA_KV_HEADS):
            kd = kv_buf[s * W:(s + 3) * W, g * 128:(g + 1) * 128]
            vd = kv_buf[s * W:(s + 3) * W, 256 + g * 128:256 + (g + 1) * 128]
            k2 = jnp.concatenate([kd * lo3, kd * hi3], axis=0)
            v2 = jnp.concatenate([jnp.concatenate([vd * lo3, lo3], axis=1),
                                  jnp.concatenate([vd * hi3, hi3], axis=1)], axis=0)
            for pair in range(SWA_GROUP // 2):
                col = (g * SWA_GROUP // 2 + pair) * 128
                qp = q_ref[rows, col:col + 128] * (HEAD_DIM ** -0.5)
                s2 = lax.dot_general(qp, k2, (((1,), (1,)), ((), ())),
                                     preferred_element_type=jnp.float32)
                ps, sink_terms = [], []
                for par in range(2):
                    h = g * SWA_GROUP + 2 * pair + par
                    sc = s2[:, par * 3 * W:(par + 1) * 3 * W] + bias_ref[h]
                    if s == 0 or s == n_sub - 1:
                        sc = sc + edge
                    sink = sink_ref[h]
                    m = jnp.maximum(jnp.max(sc, axis=-1, keepdims=True), sink)
                    ps.append(jnp.exp((sc - m).astype(jnp.bfloat16)))
                    sink_terms.append(jnp.exp(sink - m))
                o2 = jnp.dot(jnp.concatenate(ps, axis=1), v2, preferred_element_type=jnp.float32)
                denom = o2[:, 128:] + jnp.where(lo, sink_terms[0], sink_terms[1])
                o_ref[rows, col:col + 128] = (o2[:, :128] * (1.0 / denom)).astype(o_ref.dtype)


def swa_attention(q, kv, sink, bias, batch, seq):
    W = SWA_BLOCK
    tq = min(SWA_Q_TILE, seq)
    n_sub = tq // W
    nq = seq // tq
    nb = seq // W
    return pl.pallas_call(
        _swa_kernel,
        grid=(batch, nq),
        in_specs=[pl.BlockSpec(memory_space=pltpu.SMEM),
                  pl.BlockSpec((tq, SWA_WIDTH), lambda b, i: (b * nq + i, 0)),
                  pl.BlockSpec((W, 512), lambda b, i: (b * nb + jnp.maximum(i * n_sub - 1, 0), 0)),
                  pl.BlockSpec((tq, 512), lambda b, i: (b * nq + i, 0)),
                  pl.BlockSpec((W, 512), lambda b, i: (b * nb + jnp.minimum((i + 1) * n_sub, nb - 1), 0)),
                  pl.BlockSpec((SWA_HEADS, W, 3 * W), lambda b, i: (0, 0, 0), pipeline_mode=pl.Buffered(1))],
        out_specs=pl.BlockSpec((tq, SWA_WIDTH), lambda b, i: (b * nq + i, 0)),
        out_shape=jax.ShapeDtypeStruct((batch * seq, SWA_WIDTH), jnp.bfloat16),
        scratch_shapes=[pltpu.VMEM((tq + 2 * W, 512), jnp.bfloat16)],
        compiler_params=pltpu.CompilerParams(dimension_semantics=("arbitrary", "arbitrary"),
                                             vmem_limit_bytes=VMEM_LIMIT_BYTES),
        name="swa",
    )(sink.astype(jnp.float32), q, kv, kv, kv, bias)


GLA_TILE = 256
GLA_SUB = 16
_NT = (((1,), (1,)), ((), ()))
_TN = (((0,), (0,)), ((), ()))


def _seg_cumsum(x, seg, reverse):
    n = x.shape[0]
    rowmod = lax.broadcasted_iota(jnp.int32, x.shape, 0) % seg
    sh = 1
    while sh < seg:
        if reverse:
            x = x + jnp.where(rowmod < seg - sh, pltpu.roll(x, n - sh, 0), 0.0)
        else:
            x = x + jnp.where(rowmod >= sh, pltpu.roll(x, sh, 0), 0.0)
        sh *= 2
    return x


def _log_sigmoid(x):
    return jnp.minimum(x, 0.0) - jnp.log1p(jnp.exp(-jnp.abs(x)))


def _gla_constants(tb):
    S, hk = GLA_SUB, GLA_HEADS * GLA_DK
    r = np.arange(tb)
    same = (r[:, None] // S) == (r[None, :] // S)
    tri = np.stack([same & (r[None, :] <= r[:, None]), same & (r[None, :] >= r[:, None])])
    src = np.arange(S * hk)
    d, h = src // hk, (src % hk) // GLA_DK
    lane = np.arange(128)
    place = np.stack([lane[None, :] == (32 * h + S - d)[:, None], lane[None, :] == (32 * h + d)[:, None]])
    return jnp.asarray(tri, jnp.bfloat16), jnp.asarray(place, jnp.bfloat16)


def _gla_direction(qk_ref, v_ref, sm_ref, wg_ref, bg_ref, tri_ref, place_ref, st_ref, o_ref, reverse):
    tb, S = qk_ref.shape[0], GLA_SUB
    hk = GLA_HEADS * GLA_DK
    logits = jnp.dot(sm_ref[...].astype(jnp.bfloat16), wg_ref[...], preferred_element_type=jnp.float32)
    g = _log_sigmoid(logits + bg_ref[...]) * (1.0 / GLA_GATE_NORMALIZER)
    g_hi = g.astype(jnp.bfloat16)
    g_lo = (g - g_hi.astype(jnp.float32)).astype(jnp.bfloat16)
    cs = (jnp.dot(tri_ref[...], g_hi, preferred_element_type=jnp.float32) +
          jnp.dot(tri_ref[...], g_lo, preferred_element_type=jnp.float32))
    yield
    a = jnp.exp(g)
    q = qk_ref[:, :hk] * (GLA_DK ** -0.5)
    k = qk_ref[:, hk:]
    v = v_ref[...]
    rowmod = lax.broadcasted_iota(jnp.int32, (tb, hk), 0) % S
    has_prev = (rowmod < S - 1) if reverse else (rowmod >= 1)
    kd = k
    ps = [(q * kd).astype(jnp.bfloat16)]
    for d in range(1, S):
        kd = a * jnp.where(has_prev, pltpu.roll(kd, tb - 1 if reverse else 1, 0), 0.0)
        ps.append((q * kd).astype(jnp.bfloat16))
    scores = jnp.dot(jnp.concatenate(ps, axis=1), place_ref[...], preferred_element_type=jnp.float32)
    yield
    head_of_lane = lax.broadcasted_iota(jnp.int32, (S, GLA_WIDTH), 1) // GLA_DV
    zeros = jnp.zeros((S, GLA_WIDTH), jnp.float32)
    acc = {}
    for s in range(tb // S):
        rows = slice(s * S, (s + 1) * S)
        band = pltpu.roll(scores[rows], 0, 1, stride=1, stride_axis=0).astype(jnp.bfloat16)
        v_blk = v[rows]
        pieces = []
        for h in range(GLA_HEADS):
            vh = jnp.where(head_of_lane == h, v_blk, 0.0)
            pieces += [vh, zeros] if reverse else [zeros, vh]
        acc[s] = jnp.dot(band, jnp.concatenate(pieces, axis=0).astype(jnp.bfloat16),
                         preferred_element_type=jnp.float32)
        if s % 2 == 1:
            yield
    qd = (q * jnp.exp(cs)).astype(jnp.bfloat16)
    blockdiag = (lax.broadcasted_iota(jnp.int32, (hk, GLA_WIDTH), 0) // GLA_DK ==
                 lax.broadcasted_iota(jnp.int32, (hk, GLA_WIDTH), 1) // GLA_DV)
    n_blk = tb // S
    order = range(n_blk - 1, -1, -1) if reverse else range(n_blk)
    edge_row = 0 if reverse else S - 1
    edges = jnp.concatenate([cs[s * S + edge_row:s * S + edge_row + 1] for s in range(n_blk)], axis=0)
    dec_cols = jnp.exp(edges).T
    upd = {}
    for s in order:
        rows = slice(s * S, (s + 1) * S)
        kt = (k[rows] * jnp.exp(edges[s:s + 1] - cs[rows])).astype(jnp.bfloat16)
        upd[s] = jnp.where(blockdiag, lax.dot_general(kt, v[rows].astype(jnp.bfloat16), _TN,
                                                      preferred_element_type=jnp.float32), 0.0)
        yield
    st = st_ref[...]
    for s in order:
        rows = slice(s * S, (s + 1) * S)
        o_ref[rows, :] = acc[s] + jnp.dot(qd[rows], st.astype(jnp.bfloat16), preferred_element_type=jnp.float32)
        st = st * dec_cols[:, s:s + 1] + upd[s]
        yield
    st_ref[...] = st


DN_HALO = 8
DN_BETA_LANE = 2 * GLA_LOWRANK
DN_A_LANE = DN_BETA_LANE + 2 * DN_HEADS


def _split_dot(x, w_bf16):
    hi = x.astype(jnp.bfloat16)
    lo = (x - hi.astype(jnp.float32)).astype(jnp.bfloat16)
    return (jnp.dot(hi, w_bf16, preferred_element_type=jnp.float32) +
            jnp.dot(lo, w_bf16, preferred_element_type=jnp.float32))


def _group_indicator(n, group):
    r = lax.broadcasted_iota(jnp.int32, (n, n), 0) // group
    c = lax.broadcasted_iota(jnp.int32, (n, n), 1) // group
    return (r == c).astype(jnp.bfloat16)


def _mm(a, b, dims=None):
    a, b = a.astype(jnp.bfloat16), b.astype(jnp.bfloat16)
    if dims is None:
        return jnp.dot(a, b, preferred_element_type=jnp.float32)
    return lax.dot_general(a, b, dims, preferred_element_type=jnp.float32)


def _head_blockdiag(x):
    c, hw = x.shape
    w = hw // DN_HEADS
    stacked = jnp.concatenate([x] * DN_HEADS, axis=0)
    r = lax.broadcasted_iota(jnp.int32, stacked.shape, 0) // c
    l = lax.broadcasted_iota(jnp.int32, stacked.shape, 1) // w
    return jnp.where(r == l, stacked, 0.0)


def _dn_chunk_prepare(qkv, gate, direction):
    C, W = DN_CHUNK, DN_HEADS * DN_DK
    reverse = direction == 1
    q, k, v = qkv[:, :W], qkv[:, W:2 * W], qkv[:, 2 * W:]
    lane = lax.broadcasted_iota(jnp.int32, (C, 128), 1)
    g_lo = DN_A_LANE + direction * DN_HEADS
    b_lo = DN_BETA_LANE + direction * DN_HEADS
    cs = _seg_cumsum(jnp.where((lane >= g_lo) & (lane < g_lo + DN_HEADS), gate, 0.0), C, reverse)
    beta = jnp.where((lane >= b_lo) & (lane < b_lo + DN_HEADS), gate, 0.0)
    src = lax.broadcasted_iota(jnp.int32, (128, W), 0)
    dst = lax.broadcasted_iota(jnp.int32, (128, W), 1) // DN_DK
    expand = ((src - g_lo == dst) | (src - b_lo == dst)).astype(jnp.bfloat16)
    both = _split_dot(jnp.concatenate([cs, beta], axis=0), expand)
    bcol, bexp = both[:C], both[C:]
    yield
    row = lax.broadcasted_iota(jnp.int32, (C, W), 0)
    col = lax.broadcasted_iota(jnp.int32, (C, W), 1) % C
    diag = jnp.where(row == col, bcol, 0.0)
    diag_hi = diag.astype(jnp.bfloat16)
    diag_lo = (diag - diag_hi.astype(jnp.float32)).astype(jnp.bfloat16)
    ones = jnp.ones((C, C), jnp.bfloat16)
    brow = (jnp.dot(ones, diag_hi, preferred_element_type=jnp.float32) +
            jnp.dot(ones, diag_lo, preferred_element_type=jnp.float32))
    incl = (col >= row) if reverse else (col <= row)
    strict = (col > row) if reverse else (col < row)
    lmat = jnp.exp(jnp.where(incl, bcol - brow, -jnp.inf))
    yield
    kb = k * bexp
    kk = _mm(jnp.concatenate([kb, q], axis=0), _head_blockdiag(k), _NT)
    m = jnp.where(strict, kk[:C] * lmat, 0.0)
    aqk = jnp.where(incl, kk[C:] * lmat, 0.0)
    yield
    t = jnp.where(row == col, 1.0, 0.0)
    s = 1
    while s < C:
        same = (row // (2 * s)) == (col // (2 * s))
        r_hi, c_hi = (row % (2 * s)) >= s, (col % (2 * s)) >= s
        off = same & ((c_hi & ~r_hi) if reverse else (r_hi & ~c_hi))
        m_off = jnp.where(off, m, 0.0)
        if s == 1:
            t = t - m_off
        else:
            x = _mm(m_off, _head_blockdiag(t))
            yield
            t = t - _mm(t, _head_blockdiag(x))
            yield
        s *= 2
    ecs = jnp.exp(bcol)
    u = _mm(t, _head_blockdiag(v * bexp))
    w = _mm(t, _head_blockdiag(kb * ecs))
    yield
    edge = bcol[0:1] if reverse else bcol[C - 1:C]
    return dict(u=u, wq=jnp.concatenate([w, q * ecs], axis=0), aqk=aqk, k_dec=k * jnp.exp(edge - bcol),
                dl=jnp.exp(edge))


def _lockstep_stages(generators):
    results = [None] * len(generators)
    live = list(range(len(generators)))
    while live:
        still = []
        for i in live:
            try:
                next(generators[i])
                still.append(i)
            except StopIteration as stop:
                results[i] = stop.value
        live = still
        yield
    return results


def _lockstep(generators):
    stages = _lockstep_stages(generators)
    while True:
        try:
            next(stages)
        except StopIteration as stop:
            return stop.value


def _dn_state_chain(prepared, s_ref, o_ref, reverse):
    C, W = DN_CHUNK, DN_HEADS * DN_DK
    n_chunks = len(prepared)
    bd = (lax.broadcasted_iota(jnp.int32, (W, W), 0) // DN_DK ==
          lax.broadcasted_iota(jnp.int32, (W, W), 1) // DN_DV)
    s = s_ref[...]
    for c in (range(n_chunks - 1, -1, -1) if reverse else range(n_chunks)):
        pc = prepared[c]
        both = _mm(pc["wq"], s)
        v_new = pc["u"] - both[:C]
        yield
        o_ref[c * C:(c + 1) * C, :] = both[C:] + _mm(pc["aqk"], _head_blockdiag(v_new))
        s = s * pc["dl"] + jnp.where(bd, _mm(pc["k_dec"], v_new, _TN), 0.0)
        yield
    s_ref[...] = s


def _dn_tile(qkvf_ref, gf_ref, qkvb_ref, gb_ref, of_ref, ob_ref, sf_ref, sb_ref):
    C = DN_CHUNK
    n_chunks = qkvf_ref.shape[0] // C
    rows = lambda c: slice(c * C, (c + 1) * C)
    prepared = yield from _lockstep_stages(
        [_dn_chunk_prepare(qkvf_ref[rows(c), :], gf_ref[rows(c), :], 0) for c in range(n_chunks)] +
        [_dn_chunk_prepare(qkvb_ref[rows(c), :], gb_ref[rows(c), :], 1) for c in range(n_chunks)])
    yield from _lockstep_stages([_dn_state_chain(prepared[:n_chunks], sf_ref, of_ref, False),
                                 _dn_state_chain(prepared[n_chunks:], sb_ref, ob_ref, True)])


def _seq_mixers_kernel(qkf_ref, vf_ref, smf_ref, qkb_ref, vb_ref, smb_ref, wg_ref, bg_ref, tri_ref, place_ref,
                       dqkvf_ref, dgf_ref, dqkvb_ref, dgb_ref,
                       gof_ref, gob_ref, dof_ref, dob_ref, stf_ref, stb_ref, sf_ref, sb_ref):
    @pl.when(pl.program_id(1) == 0)
    def _():
        for ref in (stf_ref, stb_ref, sf_ref, sb_ref):
            ref[...] = jnp.zeros_like(ref)

    _lockstep([_dn_tile(dqkvf_ref, dgf_ref, dqkvb_ref, dgb_ref, dof_ref, dob_ref, sf_ref, sb_ref),
               _gla_direction(qkf_ref, vf_ref, smf_ref, wg_ref.at[0], bg_ref.at[0], tri_ref.at[0], place_ref.at[0],
                              stf_ref, gof_ref, False),
               _gla_direction(qkb_ref, vb_ref, smb_ref, wg_ref.at[1], bg_ref.at[1], tri_ref.at[1], place_ref.at[1],
                              stb_ref, gob_ref, True)])


def seq_mixers(gla_qk, gla_v, small, w_lr2, b_lr, dn_qkvn, dn_gates, batch, seq):
    hk = GLA_HEADS * GLA_DK
    W = DN_HEADS * DN_DK
    tb = min(GLA_TILE, seq)
    nt = seq // tb
    wg = jnp.zeros((2, 128, hk), jnp.float32)
    for z in range(2):
        wg = wg.at[z, z * GLA_LOWRANK:(z + 1) * GLA_LOWRANK].set(w_lr2[z])
    tri, place = _gla_constants(tb)
    fwd = lambda w: pl.BlockSpec((tb, w), lambda b, i: (b * nt + i, 0))
    bwd = lambda w: pl.BlockSpec((tb, w), lambda b, i: (b * nt + nt - 1 - i, 0))
    const = lambda a: pl.BlockSpec(a.shape, lambda b, i: (0,) * a.ndim)
    out = jax.ShapeDtypeStruct((batch * seq, GLA_WIDTH), jnp.float32)
    bg = b_lr.reshape(2, 1, hk).astype(jnp.float32)
    wg = wg.astype(jnp.bfloat16)
    return pl.pallas_call(
        _seq_mixers_kernel,
        grid=(batch, nt),
        in_specs=[fwd(2 * hk), fwd(GLA_WIDTH), fwd(128), bwd(2 * hk), bwd(GLA_WIDTH), bwd(128),
                  const(wg), const(bg), const(tri), const(place),
                  fwd(DN_QKV), fwd(128), bwd(DN_QKV), bwd(128)],
        out_specs=[fwd(GLA_WIDTH), bwd(GLA_WIDTH), fwd(DN_WIDTH), bwd(DN_WIDTH)],
        out_shape=[out, out, out, out],
        scratch_shapes=[pltpu.VMEM((hk, GLA_WIDTH), jnp.float32), pltpu.VMEM((hk, GLA_WIDTH), jnp.float32),
                        pltpu.VMEM((W, DN_WIDTH), jnp.float32), pltpu.VMEM((W, DN_WIDTH), jnp.float32)],
        compiler_params=pltpu.CompilerParams(dimension_semantics=("arbitrary", "arbitrary"),
                                             vmem_limit_bytes=VMEM_LIMIT_BYTES),
        name="seq_mixers",
    )(gla_qk, gla_v, small, gla_qk, gla_v, small, wg, bg, tri, place, dn_qkvn, dn_gates, dn_qkvn, dn_gates)


def _gated_group_norm(o, gate, gn, ind):
    ms = _split_dot(o * o, ind) * (1.0 / HEAD_DIM)
    return (o * lax.rsqrt(ms + EPS) * gn * (gate * (1.0 / (1.0 + jnp.exp(-gate))))).astype(jnp.bfloat16)


def _out_mlp_kernel(glaf_ref, glab_ref, glag_ref, swa_ref, dnf_ref, dnb_ref, dnz_ref, x_ref, wo_ref,
                    gng_ref, gnd_ref, g1_ref, g2_ref, w1_ref, w2_ref, g3_ref, o_ref):
    ind = _group_indicator(GLA_WIDTH, HEAD_DIM)
    y_gla = _gated_group_norm(glaf_ref[...] + glab_ref[...], glag_ref[...], gng_ref[...], ind)
    y_dn = _gated_group_norm(dnf_ref[...] + dnb_ref[...], dnz_ref[...], gnd_ref[...], ind)
    a, b = GLA_WIDTH, GLA_WIDTH + SWA_WIDTH
    m = (jnp.dot(y_gla, wo_ref[0:a, :], preferred_element_type=jnp.float32) +
         jnp.dot(swa_ref[...], wo_ref[a:b, :], preferred_element_type=jnp.float32) +
         jnp.dot(y_dn, wo_ref[b:, :], preferred_element_type=jnp.float32))
    x1 = x_ref[...] + _rms(m, g1_ref[...])
    h = _rms(x1, g2_ref[...]).astype(jnp.bfloat16)
    acc = jnp.zeros_like(x1)
    for c in range(D_FF // FF_CHUNK):
        f = jnp.dot(h, w1_ref[:, c * FF_CHUNK:(c + 1) * FF_CHUNK], preferred_element_type=jnp.float32)
        f = jnp.square(jnp.maximum(f, 0.0)).astype(jnp.bfloat16)
        acc = acc + jnp.dot(f, w2_ref[c * FF_CHUNK:(c + 1) * FF_CHUNK, :], preferred_element_type=jnp.float32)
    o_ref[...] = x1 + _rms(acc, g3_ref[...])


def out_mlp(gla_f, gla_b, gla_g, swa_o, dn_f, dn_b, dn_z, x2d, wo, gla_norm_g, dn_norm_g, g1, g2, w1, w2, g3):
    n, d = x2d.shape
    tm = min(ROW_TILE, n)
    row = lambda w: pl.BlockSpec((tm, w), lambda i: (i, 0))
    vec = _const_spec((1, d))
    hvec = _const_spec((1, GLA_WIDTH))
    tile_gain = lambda g, h: jnp.tile(g.astype(jnp.float32), h).reshape(1, -1)
    return pl.pallas_call(
        _out_mlp_kernel,
        grid=(n // tm,),
        in_specs=[row(GLA_WIDTH), row(GLA_WIDTH), row(GLA_WIDTH), row(SWA_WIDTH), row(DN_WIDTH), row(DN_WIDTH),
                  row(DN_WIDTH), row(d), _const_spec((d, d)), hvec, hvec, vec, vec,
                  _const_spec((d, D_FF)), _const_spec((D_FF, d)), vec],
        out_specs=row(d),
        out_shape=jax.ShapeDtypeStruct((n, d), jnp.float32),
        compiler_params=pltpu.CompilerParams(dimension_semantics=("arbitrary",),
                                             vmem_limit_bytes=VMEM_LIMIT_BYTES),
        name="out_mlp",
    )(gla_f, gla_b, gla_g, swa_o, dn_f, dn_b, dn_z, x2d, wo, tile_gain(gla_norm_g, GLA_HEADS),
      tile_gain(dn_norm_g, DN_HEADS), g1.reshape(1, d), g2.reshape(1, d), w1, w2, g3.reshape(1, d))


def _trunk(x, p):
    B, T, D = x.shape
    bias = p["band_bias"]
    x2 = x.reshape(B * T, D)
    for l in range(DEPTH):
        dn_qkvn, small, swa_q, swa_kv, gla_qk, gla_v, gla_g, dn_z, dn_gates = in_proj(
            x2, p["mix_pre_g"][l], p["w_in_packed"][l], p["dn_conv_w"][l], p["dn_a_log"][l], p["dn_dt_bias"][l], T)
        o_swa = swa_attention(swa_q, swa_kv, p["swa_sink"][l], bias, B, T)
        gla_f, gla_b, dn_f, dn_b = seq_mixers(gla_qk, gla_v, small, p["gla_w_lr2"][l], p["gla_b_lr"][l],
                                              dn_qkvn, dn_gates, B, T)
        x2 = out_mlp(gla_f, gla_b, gla_g, o_swa, dn_f, dn_b, dn_z, x2, p["w_out_bf16"][l], p["gla_norm_g"][l],
                     p["dn_norm_g"][l], p["mix_post_g"][l], p["mlp_pre_g"][l],
                     p["mlp_w1_bf16"][l], p["mlp_w2_bf16"][l], p["mlp_post_g"][l])
    return x2.reshape(B, T, D)


def _prepare_params(rel_bias_table, mix_pre_g, w_in, gla_w_lr2, gla_b_lr, gla_norm_g, swa_sink, dn_conv_w, dn_a_log,
                    dn_dt_bias, dn_norm_g, w_out, mix_post_g, mlp_pre_g, mlp_w1, mlp_w2, mlp_post_g):
    return dict(mix_pre_g=mix_pre_g, gla_w_lr2=gla_w_lr2, gla_b_lr=gla_b_lr,
                gla_norm_g=gla_norm_g, swa_sink=swa_sink, dn_conv_w=dn_conv_w, dn_a_log=dn_a_log,
                dn_dt_bias=dn_dt_bias, dn_norm_g=dn_norm_g, mix_post_g=mix_post_g, mlp_pre_g=mlp_pre_g,
                mlp_post_g=mlp_post_g,
                band_bias=band_bias(rel_bias_table),
                w_in_packed=jax.vmap(pack_w_in)(w_in), w_out_bf16=w_out.astype(jnp.bfloat16),
                mlp_w1_bf16=mlp_w1.astype(jnp.bfloat16), mlp_w2_bf16=mlp_w2.astype(jnp.bfloat16))


def kernel(x_prompt, x_sample, rel_bias_table, mix_pre_g, w_in, gla_w_lr2, gla_b_lr, gla_norm_g, swa_sink, dn_conv_w, dn_a_log, dn_dt_bias, dn_norm_g, w_out, mix_post_g, mlp_pre_g, mlp_w1, mlp_w2, mlp_post_g):
    p = _prepare_params(rel_bias_table, mix_pre_g, w_in, gla_w_lr2, gla_b_lr, gla_norm_g, swa_sink, dn_conv_w,
                        dn_a_log, dn_dt_bias, dn_norm_g, w_out, mix_post_g, mlp_pre_g, mlp_w1, mlp_w2, mlp_post_g)
    return (_trunk(x_prompt, p), _trunk(x_sample, p))
```

```python
import functools
import math

import numpy as np

import jax
import jax.numpy as jnp
from jax import lax
from jax.experimental import pallas as pl
from jax.experimental.pallas import tpu as pltpu

D_MODEL = 1024
DEPTH = 4
HEAD_DIM = 64
EPS = 1e-6
GLA_HEADS = 4
GLA_DK = 32
GLA_DV = 64
GLA_LOWRANK = 16
GLA_GATE_NORMALIZER = 16.0
SWA_HEADS = 8
SWA_KV_HEADS = 2
SWA_WINDOW = 128
SWA_BLOCK = 128
NUM_BUCKETS = 32
MAX_DISTANCE = 128
DN_HEADS = 4
DN_DK = 64
DN_DV = 64
DN_CONV = 5
DN_CHUNK = 64
GLA_WIDTH = GLA_HEADS * GLA_DV
SWA_WIDTH = SWA_HEADS * HEAD_DIM
DN_WIDTH = DN_HEADS * DN_DV
D_FF = 4 * D_MODEL
DN_QKV = DN_HEADS * (2 * DN_DK + DN_DV)
IN_SIZES = (GLA_HEADS * GLA_DK, GLA_HEADS * GLA_DK, GLA_WIDTH, GLA_WIDTH, 2 * GLA_LOWRANK,
            SWA_WIDTH, SWA_KV_HEADS * HEAD_DIM, SWA_KV_HEADS * HEAD_DIM,
            DN_QKV, DN_WIDTH, 2 * DN_HEADS, 2 * DN_HEADS)
D_IN = sum(IN_SIZES)

VMEM_LIMIT_BYTES = 56 * 1024 * 1024
ROW_TILE = 512
IN_ROW_TILE = 1024
FF_CHUNK = 1024


def _rms(xf, g):
    return xf * lax.rsqrt(jnp.mean(xf * xf, axis=-1, keepdims=True) + EPS) * g


def _const_spec(shape):
    return pl.BlockSpec(shape, lambda i: (0,) * len(shape), pipeline_mode=pl.Buffered(1))


_O = {}
_acc = 0
for _name, _width in (("dn_qkv", DN_QKV), ("small", 128), ("swa_q", SWA_WIDTH), ("swa_kv", 4 * 2 * HEAD_DIM),
                      ("gla_qk", 2 * GLA_HEADS * GLA_DK), ("gla_v", GLA_WIDTH), ("gla_g", GLA_WIDTH),
                      ("dn_z", DN_WIDTH)):
    _O[_name] = (_acc, _width)
    _acc += _width
D_IN_PACKED = _acc
D_IN_LEAD = _O["small"][0] + _O["small"][1]
_OUT_DTYPES = {"swa_q": jnp.bfloat16, "swa_kv": jnp.bfloat16}


def pack_w_in(w_in):
    w_in = w_in.astype(jnp.bfloat16)
    offs, acc = [], 0
    for s in IN_SIZES:
        offs.append(acc)
        acc += s
    gq, gk, gv, gg, glr, sq, sk, sv, dqkv, dz, dbeta, da = [w_in[:, o:o + s] for o, s in zip(offs, IN_SIZES)]
    hd = HEAD_DIM
    kv = [sk[:, 0:hd], sk[:, 0:hd], sk[:, hd:], sk[:, hd:], sv[:, 0:hd], sv[:, 0:hd], sv[:, hd:], sv[:, hd:]]
    small = jnp.concatenate([glr, dbeta, da], axis=1)
    small = jnp.pad(small, ((0, 0), (0, 128 - small.shape[1])))
    pieces = dict(dn_qkv=dqkv, small=small, swa_q=sq, swa_kv=jnp.concatenate(kv, axis=1),
                  gla_qk=jnp.concatenate([gq, gk], axis=1), gla_v=gv, gla_g=gg, dn_z=dz)
    return jnp.concatenate([pieces[name] for name in _O], axis=1)


def _in_proj_kernel(tiles_per_seq, xp_ref, xc_ref, xn_ref, g_ref, w_ref, cw_ref, na_ref, dt_ref, *refs):
    o_refs, gate_ref, ext_ref = refs[:len(_O)], refs[len(_O)], refs[len(_O) + 1]
    tm, H = xc_ref.shape[0], DN_HALO
    i = pl.program_id(0)
    first = (i % tiles_per_seq) == 0
    last = (i % tiles_per_seq) == tiles_per_seq - 1
    x = jnp.concatenate([xp_ref[...], xc_ref[...], xn_ref[...]], axis=0)
    h = _rms(x, g_ref[...])
    h_tile = h[H:H + tm].astype(jnp.bfloat16)
    out = dict(zip(_O, o_refs))
    y = jnp.dot(h.astype(jnp.bfloat16), w_ref[:, :D_IN_LEAD], preferred_element_type=jnp.float32)
    off, width = _O["dn_qkv"]
    ext_ref[0:H, :] = jnp.where(first, 0.0, y[0:H, off:off + width])
    ext_ref[H:H + tm, :] = y[H:H + tm, off:off + width]
    ext_ref[H + tm:, :] = jnp.where(last, 0.0, y[H + tm:, off:off + width])
    half = DN_CONV // 2
    w = DN_HEADS * DN_DK
    ind = _group_indicator(w, DN_DK)
    sumsq = lambda a: jnp.dot((a * a).astype(jnp.bfloat16), ind, preferred_element_type=jnp.float32)

    def project(names):
        lo, hi = _O[names[0]][0], _O[names[-1]][0] + _O[names[-1]][1]
        part = jnp.dot(h_tile, w_ref[:, lo:hi], preferred_element_type=jnp.float32)
        for name in names:
            o, wd = _O[name]
            out[name][...] = part[:, o - lo:o - lo + wd].astype(out[name].dtype)

    def deltanet_rows(r0, r1):
        acc = ext_ref[H - half + r0:H - half + r1, :] * cw_ref[0:1, :]
        for j in range(1, DN_CONV):
            acc = acc + ext_ref[H - half + j + r0:H - half + j + r1, :] * cw_ref[j:j + 1, :]
        c = acc * (1.0 / (1.0 + jnp.exp(-acc)))
        q, k = c[:, :w], c[:, w:2 * w]
        out["dn_qkv"][r0:r1, :w] = q * lax.rsqrt(sumsq(q) + EPS) * (DN_DK ** -0.5)
        out["dn_qkv"][r0:r1, w:2 * w] = k * lax.rsqrt(sumsq(k) + EPS)
        out["dn_qkv"][r0:r1, 2 * w:] = c[:, 2 * w:]

    groups = (["swa_q", "swa_kv"], ["gla_qk", "gla_v", "gla_g", "dn_z"])
    for n, names in enumerate(groups):
        project(names)
        deltanet_rows(n * tm // len(groups), (n + 1) * tm // len(groups))
    off, width = _O["small"]
    s = y[H:H + tm, off:off + width]
    out["small"][...] = s
    lane = lax.broadcasted_iota(jnp.int32, s.shape, 1)
    beta = 1.0 / (1.0 + jnp.exp(-s))
    a = s + dt_ref[...]
    gdec = na_ref[...] * (jnp.maximum(a, 0.0) + jnp.log1p(jnp.exp(-jnp.abs(a))))
    gate_ref[...] = jnp.where((lane >= DN_BETA_LANE) & (lane < DN_A_LANE), beta,
                              jnp.where((lane >= DN_A_LANE) & (lane < DN_A_LANE + 2 * DN_HEADS), gdec, 0.0))


def in_proj(x2d, g, w_packed, conv_w, a_log, dt_bias, seq):
    n, d = x2d.shape
    tm = min(IN_ROW_TILE, seq)
    hb = tm // DN_HALO
    nh = n // DN_HALO
    lanes = lambda v: jnp.zeros((1, 128), jnp.float32).at[0, DN_A_LANE:DN_A_LANE + 2 * DN_HEADS].set(v.reshape(-1))
    row = lambda w: pl.BlockSpec((tm, w), lambda i: (i, 0))
    return pl.pallas_call(
        functools.partial(_in_proj_kernel, seq // tm),
        grid=(n // tm,),
        in_specs=[pl.BlockSpec((DN_HALO, d), lambda i: (jnp.maximum(i * hb - 1, 0), 0)),
                  row(d),
                  pl.BlockSpec((DN_HALO, d), lambda i: (jnp.minimum((i + 1) * hb, nh - 1), 0)),
                  _const_spec((1, d)),
                  _const_spec((d, D_IN_PACKED)),
                  _const_spec((DN_CONV, DN_QKV)), _const_spec((1, 128)), _const_spec((1, 128))],
        out_specs=[row(w) for _, w in _O.values()] + [row(128)],
        out_shape=[jax.ShapeDtypeStruct((n, w), _OUT_DTYPES.get(name, jnp.float32))
                   for name, (_, w) in _O.items()] + [jax.ShapeDtypeStruct((n, 128), jnp.float32)],
        scratch_shapes=[pltpu.VMEM((tm + 2 * DN_HALO, DN_QKV), jnp.float32)],
        compiler_params=pltpu.CompilerParams(dimension_semantics=("arbitrary",),
                                             vmem_limit_bytes=VMEM_LIMIT_BYTES),
        name="in_proj",
    )(x2d, x2d, x2d, g.reshape(1, d), w_packed, conv_w.astype(jnp.float32),
      lanes(-jnp.exp(a_log.astype(jnp.float32))), lanes(dt_bias.astype(jnp.float32)))


SWA_Q_TILE = 512
SWA_GROUP = SWA_HEADS // SWA_KV_HEADS


def _t5_bucket(rel):
    nb = NUM_BUCKETS // 2
    max_exact = nb // 2
    base = jnp.where(rel > 0, nb, 0)
    n = jnp.abs(rel)
    nf = jnp.maximum(n, 1).astype(jnp.float32)
    large = max_exact + (jnp.log(nf / max_exact) / math.log(MAX_DISTANCE / max_exact)
                         * (nb - max_exact)).astype(jnp.int32)
    large = jnp.minimum(large, nb - 1)
    return base + jnp.where(n < max_exact, n, large)


def _band_bias_kernel(tab_ref, idx_ref, o_ref):
    W = SWA_BLOCK
    idx = idx_ref[...]
    i = lax.broadcasted_iota(jnp.int32, idx.shape, 0)
    j = lax.broadcasted_iota(jnp.int32, idx.shape, 1)
    in_band = jnp.abs(j - W - i) <= SWA_WINDOW
    for h in range(SWA_HEADS):
        acc = jnp.zeros(idx.shape, jnp.float32)
        for b in range(NUM_BUCKETS):
            acc = jnp.where(idx == b, tab_ref[b, h], acc)
        o_ref[h] = jnp.where(in_band, acc, -jnp.inf)


def band_bias(table):
    W = SWA_BLOCK
    rel = jnp.arange(3 * W)[None, :] - W - jnp.arange(W)[:, None]
    idx = _t5_bucket(rel).astype(jnp.int32)
    return pl.pallas_call(
        _band_bias_kernel,
        in_specs=[pl.BlockSpec(memory_space=pltpu.SMEM), pl.BlockSpec(memory_space=pltpu.VMEM)],
        out_specs=pl.BlockSpec(memory_space=pltpu.VMEM),
        out_shape=jax.ShapeDtypeStruct((SWA_HEADS, W, 3 * W), jnp.float32),
        name="band_bias",
    )(table.astype(jnp.float32), idx)


def _swa_kernel(sink_ref, q_ref, kvp_ref, kvc_ref, kvn_ref, bias_ref, o_ref, kv_buf):
    W = SWA_BLOCK
    n_sub = q_ref.shape[0] // W
    n = pl.program_id(1)
    last = pl.num_programs(1) - 1
    kv_buf[0:W, :] = kvp_ref[...]
    kv_buf[W:W + n_sub * W, :] = kvc_ref[...]
    kv_buf[W + n_sub * W:, :] = kvn_ref[...]
    lane = lax.broadcasted_iota(jnp.int32, (W, 2 * HEAD_DIM), 1)
    lo = lane < HEAD_DIM
    lo3 = (lax.broadcasted_iota(jnp.int32, (3 * W, 2 * HEAD_DIM), 1) < HEAD_DIM).astype(jnp.float32).astype(jnp.bfloat16)
    hi3 = 1 - lo3
    key_blk = lax.broadcasted_iota(jnp.int32, (1, 3 * W), 1) // W
    neg = jnp.float32(-jnp.inf)
    for s in range(n_sub):
        edge = jnp.zeros((1, 3 * W), jnp.float32)
        if s == 0:
            edge = jnp.where((key_blk == 0) & (n == 0), neg, edge)
        if s == n_sub - 1:
            edge = jnp.where((key_blk == 2) & (n == last), neg, edge)
        rows = slice(s * W, (s + 1) * W)
        for g in range(SWA_KV_HEADS):
            kd = kv_buf[s * W:(s + 3) * W, g * 128:(g + 1) * 128]
            vd = kv_buf[s * W:(s + 3) * W, 256 + g * 128:256 + (g + 1) * 128]
            k2 = jnp.concatenate([kd * lo3, kd * hi3], axis=0)
            v2 = jnp.concatenate([jnp.concatenate([vd * lo3, lo3], axis=1),
                                  jnp.concatenate([vd * hi3, hi3], axis=1)], axis=0)
            for pair in range(SWA_GROUP // 2):
                col = (g * SWA_GROUP // 2 + pair) * 128
                qp = q_ref[rows, col:col + 128] * (HEAD_DIM ** -0.5)
                s2 = lax.dot_general(qp, k2, (((1,), (1,)), ((), ())),
                                     preferred_element_type=jnp.float32)
                ps, sink_terms = [], []
                for par in range(2):
                    h = g * SWA_GROUP + 2 * pair + par
                    sc = s2[:, par * 3 * W:(par + 1) * 3 * W] + bias_ref[h]
                    if s == 0 or s == n_sub - 1:
                        sc = sc + edge
                    sink = sink_ref[h]
                    m = jnp.maximum(jnp.max(sc, axis=-1, keepdims=True), sink)
                    ps.append(jnp.exp((sc - m).astype(jnp.bfloat16)))
                    sink_terms.append(jnp.exp(sink - m))
                o2 = jnp.dot(jnp.concatenate(ps, axis=1), v2, preferred_element_type=jnp.float32)
                denom = o2[:, 128:] + jnp.where(lo, sink_terms[0], sink_terms[1])
                o_ref[rows, col:col + 128] = (o2[:, :128] * (1.0 / denom)).astype(o_ref.dtype)


def swa_attention(q, kv, sink, bias, batch, seq):
    W = SWA_BLOCK
    tq = min(SWA_Q_TILE, seq)
    n_sub = tq // W
    nq = seq // tq
    nb = seq // W
    return pl.pallas_call(
        _swa_kernel,
        grid=(batch, nq),
        in_specs=[pl.BlockSpec(memory_space=pltpu.SMEM),
                  pl.BlockSpec((tq, SWA_WIDTH), lambda b, i: (b * nq + i, 0)),
                  pl.BlockSpec((W, 512), lambda b, i: (b * nb + jnp.maximum(i * n_sub - 1, 0), 0)),
                  pl.BlockSpec((tq, 512), lambda b, i: (b * nq + i, 0)),
                  pl.BlockSpec((W, 512), lambda b, i: (b * nb + jnp.minimum((i + 1) * n_sub, nb - 1), 0)),
                  pl.BlockSpec((SWA_HEADS, W, 3 * W), lambda b, i: (0, 0, 0), pipeline_mode=pl.Buffered(1))],
        out_specs=pl.BlockSpec((tq, SWA_WIDTH), lambda b, i: (b * nq + i, 0)),
        out_shape=jax.ShapeDtypeStruct((batch * seq, SWA_WIDTH), jnp.bfloat16),
        scratch_shapes=[pltpu.VMEM((tq + 2 * W, 512), jnp.bfloat16)],
        compiler_params=pltpu.CompilerParams(dimension_semantics=("arbitrary", "arbitrary"),
                                             vmem_limit_bytes=VMEM_LIMIT_BYTES),
        name="swa",
    )(sink.astype(jnp.float32), q, kv, kv, kv, bias)


GLA_TILE = 512
GLA_SUB = 16
_NT = (((1,), (1,)), ((), ()))
_TN = (((0,), (0,)), ((), ()))


def _seg_cumsum(x, seg, reverse):
    n = x.shape[0]
    rowmod = lax.broadcasted_iota(jnp.int32, x.shape, 0) % seg
    sh = 1
    while sh < seg:
        if reverse:
            x = x + jnp.where(rowmod < seg - sh, pltpu.roll(x, n - sh, 0), 0.0)
        else:
            x = x + jnp.where(rowmod >= sh, pltpu.roll(x, sh, 0), 0.0)
        sh *= 2
    return x


def _log_sigmoid(x):
    return jnp.minimum(x, 0.0) - jnp.log1p(jnp.exp(-jnp.abs(x)))


def _gla_constants(tb):
    S, hk = GLA_SUB, GLA_HEADS * GLA_DK
    r = np.arange(tb)
    same = (r[:, None] // S) == (r[None, :] // S)
    tri = np.stack([same & (r[None, :] <= r[:, None]), same & (r[None, :] >= r[:, None])])
    src = np.arange(S * hk)
    d, h = src // hk, (src % hk) // GLA_DK
    lane = np.arange(128)
    place = np.stack([lane[None, :] == (32 * h + S - d)[:, None], lane[None, :] == (32 * h + d)[:, None]])
    return jnp.asarray(tri, jnp.bfloat16), jnp.asarray(place, jnp.bfloat16)


def _gla_direction(qk_ref, v_ref, sm_ref, wg_ref, bg_ref, tri_ref, place_ref, st_ref, o_ref, reverse):
    tb, S = qk_ref.shape[0], GLA_SUB
    hk = GLA_HEADS * GLA_DK
    logits = jnp.dot(sm_ref[...].astype(jnp.bfloat16), wg_ref[...], preferred_element_type=jnp.float32)
    g = _log_sigmoid(logits + bg_ref[...]) * (1.0 / GLA_GATE_NORMALIZER)
    g_hi = g.astype(jnp.bfloat16)
    g_lo = (g - g_hi.astype(jnp.float32)).astype(jnp.bfloat16)
    cs = (jnp.dot(tri_ref[...], g_hi, preferred_element_type=jnp.float32) +
          jnp.dot(tri_ref[...], g_lo, preferred_element_type=jnp.float32))
    yield
    a = jnp.exp(g)
    q = qk_ref[:, :hk] * (GLA_DK ** -0.5)
    k = qk_ref[:, hk:]
    v = v_ref[...]
    rowmod = lax.broadcasted_iota(jnp.int32, (tb, hk), 0) % S
    has_prev = (rowmod < S - 1) if reverse else (rowmod >= 1)
    kd = k
    ps = [(q * kd).astype(jnp.bfloat16)]
    for d in range(1, S):
        kd = a * jnp.where(has_prev, pltpu.roll(kd, tb - 1 if reverse else 1, 0), 0.0)
        ps.append((q * kd).astype(jnp.bfloat16))
    scores = jnp.dot(jnp.concatenate(ps, axis=1), place_ref[...], preferred_element_type=jnp.float32)
    yield
    head_of_lane = lax.broadcasted_iota(jnp.int32, (S, GLA_WIDTH), 1) // GLA_DV
    zeros = jnp.zeros((S, GLA_WIDTH), jnp.float32)
    acc = {}
    for s in range(tb // S):
        rows = slice(s * S, (s + 1) * S)
        band = pltpu.roll(scores[rows], 0, 1, stride=1, stride_axis=0).astype(jnp.bfloat16)
        v_blk = v[rows]
        pieces = []
        for h in range(GLA_HEADS):
            vh = jnp.where(head_of_lane == h, v_blk, 0.0)
            pieces += [vh, zeros] if reverse else [zeros, vh]
        acc[s] = jnp.dot(band, jnp.concatenate(pieces, axis=0).astype(jnp.bfloat16),
                         preferred_element_type=jnp.float32)
        if s % 2 == 1:
            yield
    qd = (q * jnp.exp(cs)).astype(jnp.bfloat16)
    blockdiag = (lax.broadcasted_iota(jnp.int32, (hk, GLA_WIDTH), 0) // GLA_DK ==
                 lax.broadcasted_iota(jnp.int32, (hk, GLA_WIDTH), 1) // GLA_DV)
    n_blk = tb // S
    order = range(n_blk - 1, -1, -1) if reverse else range(n_blk)
    edge_row = 0 if reverse else S - 1
    edges = jnp.concatenate([cs[s * S + edge_row:s * S + edge_row + 1] for s in range(n_blk)], axis=0)
    dec_cols = jnp.exp(edges).T
    upd = {}
    for s in order:
        rows = slice(s * S, (s + 1) * S)
        kt = (k[rows] * jnp.exp(edges[s:s + 1] - cs[rows])).astype(jnp.bfloat16)
        upd[s] = jnp.where(blockdiag, lax.dot_general(kt, v[rows].astype(jnp.bfloat16), _TN,
                                                      preferred_element_type=jnp.float32), 0.0)
        yield
    st = st_ref[...]
    for s in order:
        rows = slice(s * S, (s + 1) * S)
        o_ref[rows, :] = acc[s] + jnp.dot(qd[rows], st.astype(jnp.bfloat16), preferred_element_type=jnp.float32)
        st = st * dec_cols[:, s:s + 1] + upd[s]
        yield
    st_ref[...] = st


DN_HALO = 8
DN_BETA_LANE = 2 * GLA_LOWRANK
DN_A_LANE = DN_BETA_LANE + 2 * DN_HEADS


def _split_dot(x, w_bf16):
    hi = x.astype(jnp.bfloat16)
    lo = (x - hi.astype(jnp.float32)).astype(jnp.bfloat16)
    return (jnp.dot(hi, w_bf16, preferred_element_type=jnp.float32) +
            jnp.dot(lo, w_bf16, preferred_element_type=jnp.float32))


def _group_indicator(n, group):
    r = lax.broadcasted_iota(jnp.int32, (n, n), 0) // group
    c = lax.broadcasted_iota(jnp.int32, (n, n), 1) // group
    return (r == c).astype(jnp.bfloat16)


def _mm(a, b, dims=None):
    a, b = a.astype(jnp.bfloat16), b.astype(jnp.bfloat16)
    if dims is None:
        return jnp.dot(a, b, preferred_element_type=jnp.float32)
    return lax.dot_general(a, b, dims, preferred_element_type=jnp.float32)


def _head_blockdiag(x):
    c, hw = x.shape
    w = hw // DN_HEADS
    stacked = jnp.concatenate([x] * DN_HEADS, axis=0)
    r = lax.broadcasted_iota(jnp.int32, stacked.shape, 0) // c
    l = lax.broadcasted_iota(jnp.int32, stacked.shape, 1) // w
    return jnp.where(r == l, stacked, 0.0)


def _dn_chunk_prepare(qkv, gate, direction):
    C, W = DN_CHUNK, DN_HEADS * DN_DK
    reverse = direction == 1
    q, k, v = qkv[:, :W], qkv[:, W:2 * W], qkv[:, 2 * W:]
    g_lo = DN_A_LANE + direction * DN_HEADS
    b_lo = DN_BETA_LANE + direction * DN_HEADS
    cs = _seg_cumsum(gate, C, reverse)
    beta = gate
    head_of_lane = lax.broadcasted_iota(jnp.int32, (C, W), 1) // DN_DK

    def expand(x, lo):
        out = jnp.broadcast_to(x[:, lo:lo + 1], (C, W))
        for h in range(1, DN_HEADS):
            out = jnp.where(head_of_lane == h, jnp.broadcast_to(x[:, lo + h:lo + h + 1], (C, W)), out)
        return out

    bcol, bexp = expand(cs, g_lo), expand(beta, b_lo)
    yield
    row = lax.broadcasted_iota(jnp.int32, (C, W), 0)
    col = lax.broadcasted_iota(jnp.int32, (C, W), 1) % C
    diag = jnp.where(row == col, bcol, 0.0)
    diag_hi = diag.astype(jnp.bfloat16)
    diag_lo = (diag - diag_hi.astype(jnp.float32)).astype(jnp.bfloat16)
    ones = jnp.ones((C, C), jnp.bfloat16)
    brow = (jnp.dot(ones, diag_hi, preferred_element_type=jnp.float32) +
            jnp.dot(ones, diag_lo, preferred_element_type=jnp.float32))
    incl = (col >= row) if reverse else (col <= row)
    strict = (col > row) if reverse else (col < row)
    lmat = jnp.exp(jnp.where(incl, bcol - brow, -jnp.inf))
    yield
    kb = k * bexp
    kk = _mm(jnp.concatenate([kb, q], axis=0), _head_blockdiag(k), _NT)
    m = jnp.where(strict, kk[:C] * lmat, 0.0)
    aqk = jnp.where(incl, kk[C:] * lmat, 0.0)
    yield
    t = jnp.where(row == col, 1.0, 0.0)
    s = 1
    while s < C:
        same = (row // (2 * s)) == (col // (2 * s))
        r_hi, c_hi = (row % (2 * s)) >= s, (col % (2 * s)) >= s
        off = same & ((c_hi & ~r_hi) if reverse else (r_hi & ~c_hi))
        m_off = jnp.where(off, m, 0.0)
        if s == 1:
            t = t - m_off
        else:
            x = _mm(m_off, _head_blockdiag(t))
            yield
            t = t - _mm(t, _head_blockdiag(x))
            yield
        s *= 2
    ecs = jnp.exp(bcol)
    u = _mm(t, _head_blockdiag(v * bexp))
    w = _mm(t, _head_blockdiag(kb * ecs))
    yield
    edge = bcol[0:1] if reverse else bcol[C - 1:C]
    return dict(u=u, wq=jnp.concatenate([w, q * ecs], axis=0), aqk=aqk, k_dec=k * jnp.exp(edge - bcol),
                dl=jnp.exp(edge))


def _lockstep_stages(generators):
    results = [None] * len(generators)
    live = list(range(len(generators)))
    while live:
        still = []
        for i in live:
            try:
                next(generators[i])
                still.append(i)
            except StopIteration as stop:
                results[i] = stop.value
        live = still
        yield
    return results


def _lockstep(generators):
    stages = _lockstep_stages(generators)
    while True:
        try:
            next(stages)
        except StopIteration as stop:
            return stop.value


def _dn_state_chain(prepared, s_ref, o_ref, reverse):
    C, W = DN_CHUNK, DN_HEADS * DN_DK
    n_chunks = len(prepared)
    bd = (lax.broadcasted_iota(jnp.int32, (W, W), 0) // DN_DK ==
          lax.broadcasted_iota(jnp.int32, (W, W), 1) // DN_DV)
    s = s_ref[...]
    for c in (range(n_chunks - 1, -1, -1) if reverse else range(n_chunks)):
        pc = prepared[c]
        both = _mm(pc["wq"], s)
        v_new = pc["u"] - both[:C]
        yield
        o_ref[c * C:(c + 1) * C, :] = both[C:] + _mm(pc["aqk"], _head_blockdiag(v_new))
        s = s * pc["dl"] + jnp.where(bd, _mm(pc["k_dec"], v_new, _TN), 0.0)
        yield
    s_ref[...] = s


def _dn_tile(qkvf_ref, gf_ref, qkvb_ref, gb_ref, of_ref, ob_ref, sf_ref, sb_ref):
    C = DN_CHUNK
    n_chunks = qkvf_ref.shape[0] // C
    rows = lambda c: slice(c * C, (c + 1) * C)
    prepared = yield from _lockstep_stages(
        [_dn_chunk_prepare(qkvf_ref[rows(c), :], gf_ref[rows(c), :], 0) for c in range(n_chunks)] +
        [_dn_chunk_prepare(qkvb_ref[rows(c), :], gb_ref[rows(c), :], 1) for c in range(n_chunks)])
    yield from _lockstep_stages([_dn_state_chain(prepared[:n_chunks], sf_ref, of_ref, False),
                                 _dn_state_chain(prepared[n_chunks:], sb_ref, ob_ref, True)])


def _seq_mixers_kernel(qkf_ref, vf_ref, smf_ref, qkb_ref, vb_ref, smb_ref, wg_ref, bg_ref, tri_ref, place_ref,
                       dqkvf_ref, dgf_ref, dqkvb_ref, dgb_ref,
                       gof_ref, gob_ref, dof_ref, dob_ref, stf_ref, stb_ref, sf_ref, sb_ref):
    @pl.when(pl.program_id(1) == 0)
    def _():
        for ref in (stf_ref, stb_ref, sf_ref, sb_ref):
            ref[...] = jnp.zeros_like(ref)

    _lockstep([_dn_tile(dqkvf_ref, dgf_ref, dqkvb_ref, dgb_ref, dof_ref, dob_ref, sf_ref, sb_ref),
               _gla_direction(qkf_ref, vf_ref, smf_ref, wg_ref.at[0], bg_ref.at[0], tri_ref.at[0], place_ref.at[0],
                              stf_ref, gof_ref, False),
               _gla_direction(qkb_ref, vb_ref, smb_ref, wg_ref.at[1], bg_ref.at[1], tri_ref.at[1], place_ref.at[1],
                              stb_ref, gob_ref, True)])


def seq_mixers(gla_qk, gla_v, small, w_lr2, b_lr, dn_qkvn, dn_gates, batch, seq):
    hk = GLA_HEADS * GLA_DK
    W = DN_HEADS * DN_DK
    tb = min(GLA_TILE, seq)
    nt = seq // tb
    wg = jnp.zeros((2, 128, hk), jnp.float32)
    for z in range(2):
        wg = wg.at[z, z * GLA_LOWRANK:(z + 1) * GLA_LOWRANK].set(w_lr2[z])
    tri, place = _gla_constants(tb)
    fwd = lambda w: pl.BlockSpec((tb, w), lambda b, i: (b * nt + i, 0))
    bwd = lambda w: pl.BlockSpec((tb, w), lambda b, i: (b * nt + nt - 1 - i, 0))
    const = lambda a: pl.BlockSpec(a.shape, lambda b, i: (0,) * a.ndim)
    out = jax.ShapeDtypeStruct((batch * seq, GLA_WIDTH), jnp.float32)
    bg = b_lr.reshape(2, 1, hk).astype(jnp.float32)
    wg = wg.astype(jnp.bfloat16)
    return pl.pallas_call(
        _seq_mixers_kernel,
        grid=(batch, nt),
        in_specs=[fwd(2 * hk), fwd(GLA_WIDTH), fwd(128), bwd(2 * hk), bwd(GLA_WIDTH), bwd(128),
                  const(wg), const(bg), const(tri), const(place),
                  fwd(DN_QKV), fwd(128), bwd(DN_QKV), bwd(128)],
        out_specs=[fwd(GLA_WIDTH), bwd(GLA_WIDTH), fwd(DN_WIDTH), bwd(DN_WIDTH)],
        out_shape=[out, out, out, out],
        scratch_shapes=[pltpu.VMEM((hk, GLA_WIDTH), jnp.float32), pltpu.VMEM((hk, GLA_WIDTH), jnp.float32),
                        pltpu.VMEM((W, DN_WIDTH), jnp.float32), pltpu.VMEM((W, DN_WIDTH), jnp.float32)],
        compiler_params=pltpu.CompilerParams(dimension_semantics=("arbitrary", "arbitrary"),
                                             vmem_limit_bytes=VMEM_LIMIT_BYTES),
        name="seq_mixers",
    )(gla_qk, gla_v, small, gla_qk, gla_v, small, wg, bg, tri, place, dn_qkvn, dn_gates, dn_qkvn, dn_gates)


def _gated_group_norm(o, gate, gn, ind):
    ms = _split_dot(o * o, ind) * (1.0 / HEAD_DIM)
    return (o * lax.rsqrt(ms + EPS) * gn * (gate * (1.0 / (1.0 + jnp.exp(-gate))))).astype(jnp.bfloat16)


def _out_mlp_kernel(glaf_ref, glab_ref, glag_ref, swa_ref, dnf_ref, dnb_ref, dnz_ref, x_ref, wo_ref,
                    gng_ref, gnd_ref, g1_ref, g2_ref, w1_ref, w2_ref, g3_ref, o_ref):
    ind = _group_indicator(GLA_WIDTH, HEAD_DIM)
    y_gla = _gated_group_norm(glaf_ref[...] + glab_ref[...], glag_ref[...], gng_ref[...], ind)
    y_dn = _gated_group_norm(dnf_ref[...] + dnb_ref[...], dnz_ref[...], gnd_ref[...], ind)
    a, b = GLA_WIDTH, GLA_WIDTH + SWA_WIDTH
    m = (jnp.dot(y_gla, wo_ref[0:a, :], preferred_element_type=jnp.float32) +
         jnp.dot(swa_ref[...], wo_ref[a:b, :], preferred_element_type=jnp.float32) +
         jnp.dot(y_dn, wo_ref[b:, :], preferred_element_type=jnp.float32))
    x1 = x_ref[...] + _rms(m, g1_ref[...])
    h = _rms(x1, g2_ref[...]).astype(jnp.bfloat16)
    acc = jnp.zeros_like(x1)
    for c in range(D_FF // FF_CHUNK):
        f = jnp.dot(h, w1_ref[:, c * FF_CHUNK:(c + 1) * FF_CHUNK], preferred_element_type=jnp.float32)
        f = jnp.square(jnp.maximum(f, 0.0)).astype(jnp.bfloat16)
        acc = acc + jnp.dot(f, w2_ref[c * FF_CHUNK:(c + 1) * FF_CHUNK, :], preferred_element_type=jnp.float32)
    o_ref[...] = x1 + _rms(acc, g3_ref[...])


def out_mlp(gla_f, gla_b, gla_g, swa_o, dn_f, dn_b, dn_z, x2d, wo, gla_norm_g, dn_norm_g, g1, g2, w1, w2, g3):
    n, d = x2d.shape
    tm = min(ROW_TILE, n)
    row = lambda w: pl.BlockSpec((tm, w), lambda i: (i, 0))
    vec = _const_spec((1, d))
    hvec = _const_spec((1, GLA_WIDTH))
    tile_gain = lambda g, h: jnp.tile(g.astype(jnp.float32), h).reshape(1, -1)
    return pl.pallas_call(
        _out_mlp_kernel,
        grid=(n // tm,),
        in_specs=[row(GLA_WIDTH), row(GLA_WIDTH), row(GLA_WIDTH), row(SWA_WIDTH), row(DN_WIDTH), row(DN_WIDTH),
                  row(DN_WIDTH), row(d), _const_spec((d, d)), hvec, hvec, vec, vec,
                  _const_spec((d, D_FF)), _const_spec((D_FF, d)), vec],
        out_specs=row(d),
        out_shape=jax.ShapeDtypeStruct((n, d), jnp.float32),
        compiler_params=pltpu.CompilerParams(dimension_semantics=("arbitrary",),
                                             vmem_limit_bytes=VMEM_LIMIT_BYTES),
        name="out_mlp",
    )(gla_f, gla_b, gla_g, swa_o, dn_f, dn_b, dn_z, x2d, wo, tile_gain(gla_norm_g, GLA_HEADS),
      tile_gain(dn_norm_g, DN_HEADS), g1.reshape(1, d), g2.reshape(1, d), w1, w2, g3.reshape(1, d))


def _trunk(x, p):
    B, T, D = x.shape
    bias = p["band_bias"]
    x2 = x.reshape(B * T, D)
    for l in range(DEPTH):
        dn_qkvn, small, swa_q, swa_kv, gla_qk, gla_v, gla_g, dn_z, dn_gates = in_proj(
            x2, p["mix_pre_g"][l], p["w_in_packed"][l], p["dn_conv_w"][l], p["dn_a_log"][l], p["dn_dt_bias"][l], T)
        o_swa = swa_attention(swa_q, swa_kv, p["swa_sink"][l], bias, B, T)
        gla_f, gla_b, dn_f, dn_b = seq_mixers(gla_qk, gla_v, small, p["gla_w_lr2"][l], p["gla_b_lr"][l],
                                              dn_qkvn, dn_gates, B, T)
        x2 = out_mlp(gla_f, gla_b, gla_g, o_swa, dn_f, dn_b, dn_z, x2, p["w_out_bf16"][l], p["gla_norm_g"][l],
                     p["dn_norm_g"][l], p["mix_post_g"][l], p["mlp_pre_g"][l],
                     p["mlp_w1_bf16"][l], p["mlp_w2_bf16"][l], p["mlp_post_g"][l])
    return x2.reshape(B, T, D)


def _prepare_params(rel_bias_table, mix_pre_g, w_in, gla_w_lr2, gla_b_lr, gla_norm_g, swa_sink, dn_conv_w, dn_a_log,
                    dn_dt_bias, dn_norm_g, w_out, mix_post_g, mlp_pre_g, mlp_w1, mlp_w2, mlp_post_g):
    return dict(mix_pre_g=mix_pre_g, gla_w_lr2=gla_w_lr2, gla_b_lr=gla_b_lr,
                gla_norm_g=gla_norm_g, swa_sink=swa_sink, dn_conv_w=dn_conv_w, dn_a_log=dn_a_log,
                dn_dt_bias=dn_dt_bias, dn_norm_g=dn_norm_g, mix_post_g=mix_post_g, mlp_pre_g=mlp_pre_g,
                mlp_post_g=mlp_post_g,
                band_bias=band_bias(rel_bias_table),
                w_in_packed=jax.vmap(pack_w_in)(w_in), w_out_bf16=w_out.astype(jnp.bfloat16),
                mlp_w1_bf16=mlp_w1.astype(jnp.bfloat16), mlp_w2_bf16=mlp_w2.astype(jnp.bfloat16))


def kernel(x_prompt, x_sample, rel_bias_table, mix_pre_g, w_in, gla_w_lr2, gla_b_lr, gla_norm_g, swa_sink, dn_conv_w, dn_a_log, dn_dt_bias, dn_norm_g, w_out, mix_post_g, mlp_pre_g, mlp_w1, mlp_w2, mlp_post_g):
    p = _prepare_params(rel_bias_table, mix_pre_g, w_in, gla_w_lr2, gla_b_lr, gla_norm_g, swa_sink, dn_conv_w,
                        dn_a_log, dn_dt_bias, dn_norm_g, w_out, mix_post_g, mlp_pre_g, mlp_w1, mlp_w2, mlp_post_g)
    return (_trunk(x_prompt, p), _trunk(x_sample, p))
```

```python
import functools
import math

import numpy as np

import jax
import jax.numpy as jnp
from jax import lax
from jax.experimental import pallas as pl
from jax.experimental.pallas import tpu as pltpu

D_MODEL = 1024
DEPTH = 4
HEAD_DIM = 64
EPS = 1e-6
GLA_HEADS = 4
GLA_DK = 32
GLA_DV = 64
GLA_LOWRANK = 16
GLA_GATE_NORMALIZER = 16.0
SWA_HEADS = 8
SWA_KV_HEADS = 2
SWA_WINDOW = 128
SWA_BLOCK = 128
NUM_BUCKETS = 32
MAX_DISTANCE = 128
DN_HEADS = 4
DN_DK = 64
DN_DV = 64
DN_CONV = 5
DN_CHUNK = 64
GLA_WIDTH = GLA_HEADS * GLA_DV
SWA_WIDTH = SWA_HEADS * HEAD_DIM
DN_WIDTH = DN_HEADS * DN_DV
D_FF = 4 * D_MODEL
DN_QKV = DN_HEADS * (2 * DN_DK + DN_DV)
IN_SIZES = (GLA_HEADS * GLA_DK, GLA_HEADS * GLA_DK, GLA_WIDTH, GLA_WIDTH, 2 * GLA_LOWRANK,
            SWA_WIDTH, SWA_KV_HEADS * HEAD_DIM, SWA_KV_HEADS * HEAD_DIM,
            DN_QKV, DN_WIDTH, 2 * DN_HEADS, 2 * DN_HEADS)
D_IN = sum(IN_SIZES)

VMEM_LIMIT_BYTES = 56 * 1024 * 1024
ROW_TILE = 512
IN_ROW_TILE = 1024
FF_CHUNK = 1024


def _rms(xf, g):
    return xf * lax.rsqrt(jnp.mean(xf * xf, axis=-1, keepdims=True) + EPS) * g


def _const_spec(shape):
    return pl.BlockSpec(shape, lambda i: (0,) * len(shape), pipeline_mode=pl.Buffered(1))


_O = {}
_acc = 0
for _name, _width in (("dn_qkv", DN_QKV), ("small", 128), ("swa_q", SWA_WIDTH), ("swa_kv", 4 * 2 * HEAD_DIM),
                      ("gla_qk", 2 * GLA_HEADS * GLA_DK), ("gla_v", GLA_WIDTH), ("gla_g", GLA_WIDTH),
                      ("dn_z", DN_WIDTH)):
    _O[_name] = (_acc, _width)
    _acc += _width
D_IN_PACKED = _acc
D_IN_LEAD = _O["small"][0] + _O["small"][1]
_OUT_DTYPES = {"swa_q": jnp.bfloat16, "swa_kv": jnp.bfloat16}


PACK_ROWS = 256


def _pack_w_in_kernel(w_ref, o_ref):
    w = w_ref[0]
    offs, acc = [], 0
    for s in IN_SIZES:
        offs.append(acc)
        acc += s
    gq, gk, gv, gg, glr, sq, sk, sv, dqkv, dz, dbeta, da = [w[:, o:o + s] for o, s in zip(offs, IN_SIZES)]
    hd = HEAD_DIM
    kv = [sk[:, 0:hd], sk[:, 0:hd], sk[:, hd:], sk[:, hd:], sv[:, 0:hd], sv[:, 0:hd], sv[:, hd:], sv[:, hd:]]
    used = glr.shape[1] + dbeta.shape[1] + da.shape[1]
    small = jnp.concatenate([glr, dbeta, da, jnp.zeros((w.shape[0], 128 - used), w.dtype)], axis=1)
    pieces = dict(dn_qkv=dqkv, small=small, swa_q=sq, swa_kv=jnp.concatenate(kv, axis=1),
                  gla_qk=jnp.concatenate([gq, gk], axis=1), gla_v=gv, gla_g=gg, dn_z=dz)
    for name, (off, width) in _O.items():
        o_ref[0, :, off:off + width] = pieces[name].astype(o_ref.dtype)


def pack_w_in(w_in):
    depth, d, d_in = w_in.shape
    return pl.pallas_call(
        _pack_w_in_kernel,
        grid=(depth, d // PACK_ROWS),
        in_specs=[pl.BlockSpec((1, PACK_ROWS, d_in), lambda l, i: (l, i, 0))],
        out_specs=pl.BlockSpec((1, PACK_ROWS, D_IN_PACKED), lambda l, i: (l, i, 0)),
        out_shape=jax.ShapeDtypeStruct((depth, d, D_IN_PACKED), jnp.bfloat16),
        compiler_params=pltpu.CompilerParams(dimension_semantics=("arbitrary", "arbitrary"),
                                             vmem_limit_bytes=VMEM_LIMIT_BYTES),
        name="pack_w_in",
    )(w_in)


def _in_proj_kernel(tiles_per_seq, xp_ref, xc_ref, xn_ref, g_ref, w_ref, cw_ref, na_ref, dt_ref, *refs):
    o_refs, gate_ref, ext_ref = refs[:len(_O)], refs[len(_O)], refs[len(_O) + 1]
    tm, H = xc_ref.shape[0], DN_HALO
    i = pl.program_id(0)
    first = (i % tiles_per_seq) == 0
    last = (i % tiles_per_seq) == tiles_per_seq - 1
    x = jnp.concatenate([xp_ref[...], xc_ref[...], xn_ref[...]], axis=0)
    h = _rms(x, g_ref[...])
    h_tile = h[H:H + tm].astype(jnp.bfloat16)
    out = dict(zip(_O, o_refs))
    y = jnp.dot(h.astype(jnp.bfloat16), w_ref[:, :D_IN_LEAD], preferred_element_type=jnp.float32)
    off, width = _O["dn_qkv"]
    ext_ref[0:H, :] = jnp.where(first, 0.0, y[0:H, off:off + width])
    ext_ref[H:H + tm, :] = y[H:H + tm, off:off + width]
    ext_ref[H + tm:, :] = jnp.where(last, 0.0, y[H + tm:, off:off + width])
    half = DN_CONV // 2
    w = DN_HEADS * DN_DK
    ind = _group_indicator(w, DN_DK)
    sumsq = lambda a: jnp.dot((a * a).astype(jnp.bfloat16), ind, preferred_element_type=jnp.float32)

    def project(names):
        lo, hi = _O[names[0]][0], _O[names[-1]][0] + _O[names[-1]][1]
        part = jnp.dot(h_tile, w_ref[:, lo:hi], preferred_element_type=jnp.float32)
        for name in names:
            o, wd = _O[name]
            out[name][...] = part[:, o - lo:o - lo + wd].astype(out[name].dtype)

    def deltanet_rows(r0, r1):
        acc = ext_ref[H - half + r0:H - half + r1, :] * cw_ref[0:1, :]
        for j in range(1, DN_CONV):
            acc = acc + ext_ref[H - half + j + r0:H - half + j + r1, :] * cw_ref[j:j + 1, :]
        c = acc * (1.0 / (1.0 + jnp.exp(-acc)))
        q, k = c[:, :w], c[:, w:2 * w]
        out["dn_qkv"][r0:r1, :w] = q * lax.rsqrt(sumsq(q) + EPS) * (DN_DK ** -0.5)
        out["dn_qkv"][r0:r1, w:2 * w] = k * lax.rsqrt(sumsq(k) + EPS)
        out["dn_qkv"][r0:r1, 2 * w:] = c[:, 2 * w:]

    groups = (["swa_q", "swa_kv"], ["gla_qk", "gla_v", "gla_g", "dn_z"])
    for n, names in enumerate(groups):
        project(names)
        deltanet_rows(n * tm // len(groups), (n + 1) * tm // len(groups))
    off, width = _O["small"]
    s = y[H:H + tm, off:off + width]
    out["small"][...] = s
    lane = lax.broadcasted_iota(jnp.int32, s.shape, 1)
    beta = 1.0 / (1.0 + jnp.exp(-s))
    a = s + dt_ref[...]
    gdec = na_ref[...] * (jnp.maximum(a, 0.0) + jnp.log1p(jnp.exp(-jnp.abs(a))))
    gate_ref[...] = jnp.where((lane >= DN_BETA_LANE) & (lane < DN_A_LANE), beta,
                              jnp.where((lane >= DN_A_LANE) & (lane < DN_A_LANE + 2 * DN_HEADS), gdec, 0.0))


def in_proj(x2d, g, w_packed, conv_w, a_log, dt_bias, seq):
    n, d = x2d.shape
    tm = min(IN_ROW_TILE, seq)
    hb = tm // DN_HALO
    nh = n // DN_HALO
    lanes = lambda v: jnp.zeros((1, 128), jnp.float32).at[0, DN_A_LANE:DN_A_LANE + 2 * DN_HEADS].set(v.reshape(-1))
    row = lambda w: pl.BlockSpec((tm, w), lambda i: (i, 0))
    return pl.pallas_call(
        functools.partial(_in_proj_kernel, seq // tm),
        grid=(n // tm,),
        in_specs=[pl.BlockSpec((DN_HALO, d), lambda i: (jnp.maximum(i * hb - 1, 0), 0)),
                  row(d),
                  pl.BlockSpec((DN_HALO, d), lambda i: (jnp.minimum((i + 1) * hb, nh - 1), 0)),
                  _const_spec((1, d)),
                  _const_spec((d, D_IN_PACKED)),
                  _const_spec((DN_CONV, DN_QKV)), _const_spec((1, 128)), _const_spec((1, 128))],
        out_specs=[row(w) for _, w in _O.values()] + [row(128)],
        out_shape=[jax.ShapeDtypeStruct((n, w), _OUT_DTYPES.get(name, jnp.float32))
                   for name, (_, w) in _O.items()] + [jax.ShapeDtypeStruct((n, 128), jnp.float32)],
        scratch_shapes=[pltpu.VMEM((tm + 2 * DN_HALO, DN_QKV), jnp.float32)],
        compiler_params=pltpu.CompilerParams(dimension_semantics=("arbitrary",),
                                             vmem_limit_bytes=VMEM_LIMIT_BYTES),
        name="in_proj",
    )(x2d, x2d, x2d, g.reshape(1, d), w_packed, conv_w.astype(jnp.float32),
      lanes(-jnp.exp(a_log.astype(jnp.float32))), lanes(dt_bias.astype(jnp.float32)))


SWA_Q_TILE = 512
SWA_GROUP = SWA_HEADS // SWA_KV_HEADS


def _t5_bucket(rel):
    nb = NUM_BUCKETS // 2
    max_exact = nb // 2
    base = jnp.where(rel > 0, nb, 0)
    n = jnp.abs(rel)
    nf = jnp.maximum(n, 1).astype(jnp.float32)
    large = max_exact + (jnp.log(nf / max_exact) / math.log(MAX_DISTANCE / max_exact)
                         * (nb - max_exact)).astype(jnp.int32)
    large = jnp.minimum(large, nb - 1)
    return base + jnp.where(n < max_exact, n, large)


def _band_bias_kernel(tab_ref, idx_ref, o_ref):
    W = SWA_BLOCK
    idx = idx_ref[...]
    i = lax.broadcasted_iota(jnp.int32, idx.shape, 0)
    j = lax.broadcasted_iota(jnp.int32, idx.shape, 1)
    in_band = jnp.abs(j - W - i) <= SWA_WINDOW
    for h in range(SWA_HEADS):
        acc = jnp.zeros(idx.shape, jnp.float32)
        for b in range(NUM_BUCKETS):
            acc = jnp.where(idx == b, tab_ref[b, h], acc)
        o_ref[h] = jnp.where(in_band, acc, -jnp.inf)


def band_bias(table):
    W = SWA_BLOCK
    rel = jnp.arange(3 * W)[None, :] - W - jnp.arange(W)[:, None]
    idx = _t5_bucket(rel).astype(jnp.int32)
    return pl.pallas_call(
        _band_bias_kernel,
        in_specs=[pl.BlockSpec(memory_space=pltpu.SMEM), pl.BlockSpec(memory_space=pltpu.VMEM)],
        out_specs=pl.BlockSpec(memory_space=pltpu.VMEM),
        out_shape=jax.ShapeDtypeStruct((SWA_HEADS, W, 3 * W), jnp.float32),
        name="band_bias",
    )(table.astype(jnp.float32), idx)


def _swa_kernel(sink_ref, q_ref, kvp_ref, kvc_ref, kvn_ref, bias_ref, o_ref, kv_buf):
    W = SWA_BLOCK
    n_sub = q_ref.shape[0] // W
    n = pl.program_id(1)
    last = pl.num_programs(1) - 1
    kv_buf[0:W, :] = kvp_ref[...]
    kv_buf[W:W + n_sub * W, :] = kvc_ref[...]
    kv_buf[W + n_sub * W:, :] = kvn_ref[...]
    lane = lax.broadcasted_iota(jnp.int32, (W, 2 * HEAD_DIM), 1)
    lo = lane < HEAD_DIM
    lo3 = (lax.broadcasted_iota(jnp.int32, (3 * W, 2 * HEAD_DIM), 1) < HEAD_DIM).astype(jnp.float32).astype(jnp.bfloat16)
    hi3 = 1 - lo3
    key_blk = lax.broadcasted_iota(jnp.int32, (1, 3 * W), 1) // W
    neg = jnp.float32(-jnp.inf)
    for s in range(n_sub):
        edge = jnp.zeros((1, 3 * W), jnp.float32)
        if s == 0:
            edge = jnp.where((key_blk == 0) & (n == 0), neg, edge)
        if s == n_sub - 1:
            edge = jnp.where((key_blk == 2) & (n == last), neg, edge)
        rows = slice(s * W, (s + 1) * W)
        for g in range(SWA_KV_HEADS):
            kd = kv_buf[s * W:(s + 3) * W, g * 128:(g + 1) * 128]
            vd = kv_buf[s * W:(s + 3) * W, 256 + g * 128:256 + (g + 1) * 128]
            k2 = jnp.concatenate([kd * lo3, kd * hi3], axis=0)
            v2 = jnp.concatenate([jnp.concatenate([vd * lo3, lo3], axis=1),
                                  jnp.concatenate([vd * hi3, hi3], axis=1)], axis=0)
            for pair in range(SWA_GROUP // 2):
                col = (g * SWA_GROUP // 2 + pair) * 128
                qp = q_ref[rows, col:col + 128] * (HEAD_DIM ** -0.5)
                s2 = lax.dot_general(qp, k2, (((1,), (1,)), ((), ())),
                                     preferred_element_type=jnp.float32)
                ps, sink_terms = [], []
                for par in range(2):
                    h = g * SWA_GROUP + 2 * pair + par
                    sc = s2[:, par * 3 * W:(par + 1) * 3 * W] + bias_ref[h]
                    if s == 0 or s == n_sub - 1:
                        sc = sc + edge
                    sink = sink_ref[h]
                    m = jnp.maximum(jnp.max(sc, axis=-1, keepdims=True), sink)
                    ps.append(jnp.exp((sc - m).astype(jnp.bfloat16)))
                    sink_terms.append(jnp.exp(sink - m))
                o2 = jnp.dot(jnp.concatenate(ps, axis=1), v2, preferred_element_type=jnp.float32)
                denom = o2[:, 128:] + jnp.where(lo, sink_terms[0], sink_terms[1])
                o_ref[rows, col:col + 128] = (o2[:, :128] * (1.0 / denom)).astype(o_ref.dtype)


def swa_attention(q, kv, sink, bias, batch, seq):
    W = SWA_BLOCK
    tq = min(SWA_Q_TILE, seq)
    n_sub = tq // W
    nq = seq // tq
    nb = seq // W
    return pl.pallas_call(
        _swa_kernel,
        grid=(batch, nq),
        in_specs=[pl.BlockSpec(memory_space=pltpu.SMEM),
                  pl.BlockSpec((tq, SWA_WIDTH), lambda b, i: (b * nq + i, 0)),
                  pl.BlockSpec((W, 512), lambda b, i: (b * nb + jnp.maximum(i * n_sub - 1, 0), 0)),
                  pl.BlockSpec((tq, 512), lambda b, i: (b * nq + i, 0)),
                  pl.BlockSpec((W, 512), lambda b, i: (b * nb + jnp.minimum((i + 1) * n_sub, nb - 1), 0)),
                  pl.BlockSpec((SWA_HEADS, W, 3 * W), lambda b, i: (0, 0, 0), pipeline_mode=pl.Buffered(1))],
        out_specs=pl.BlockSpec((tq, SWA_WIDTH), lambda b, i: (b * nq + i, 0)),
        out_shape=jax.ShapeDtypeStruct((batch * seq, SWA_WIDTH), jnp.bfloat16),
        scratch_shapes=[pltpu.VMEM((tq + 2 * W, 512), jnp.bfloat16)],
        compiler_params=pltpu.CompilerParams(dimension_semantics=("arbitrary", "arbitrary"),
                                             vmem_limit_bytes=VMEM_LIMIT_BYTES),
        name="swa",
    )(sink.astype(jnp.float32), q, kv, kv, kv, bias)


GLA_TILE = 512
GLA_SUB = 16
_NT = (((1,), (1,)), ((), ()))
_TN = (((0,), (0,)), ((), ()))


def _seg_cumsum(x, seg, reverse):
    n = x.shape[0]
    rowmod = lax.broadcasted_iota(jnp.int32, x.shape, 0) % seg
    sh = 1
    while sh < seg:
        if reverse:
            x = x + jnp.where(rowmod < seg - sh, pltpu.roll(x, n - sh, 0), 0.0)
        else:
            x = x + jnp.where(rowmod >= sh, pltpu.roll(x, sh, 0), 0.0)
        sh *= 2
    return x


def _log_sigmoid(x):
    return jnp.minimum(x, 0.0) - jnp.log1p(jnp.exp(-jnp.abs(x)))


def _gla_constants(tb):
    S, hk = GLA_SUB, GLA_HEADS * GLA_DK
    r = np.arange(tb)
    same = (r[:, None] // S) == (r[None, :] // S)
    tri = np.stack([same & (r[None, :] <= r[:, None]), same & (r[None, :] >= r[:, None])])
    src = np.arange(S * hk)
    d, h = src // hk, (src % hk) // GLA_DK
    lane = np.arange(128)
    place = np.stack([lane[None, :] == (32 * h + S - d)[:, None], lane[None, :] == (32 * h + d)[:, None]])
    return jnp.asarray(tri, jnp.bfloat16), jnp.asarray(place, jnp.bfloat16)


def _gla_direction(qk_ref, v_ref, sm_ref, wg_ref, bg_ref, tri_ref, place_ref, st_ref, o_ref, reverse):
    tb, S = qk_ref.shape[0], GLA_SUB
    hk = GLA_HEADS * GLA_DK
    logits = jnp.dot(sm_ref[...].astype(jnp.bfloat16), wg_ref[...], preferred_element_type=jnp.float32)
    g = _log_sigmoid(logits + bg_ref[...]) * (1.0 / GLA_GATE_NORMALIZER)
    g_hi = g.astype(jnp.bfloat16)
    g_lo = (g - g_hi.astype(jnp.float32)).astype(jnp.bfloat16)
    cs = (jnp.dot(tri_ref[...], g_hi, preferred_element_type=jnp.float32) +
          jnp.dot(tri_ref[...], g_lo, preferred_element_type=jnp.float32))
    yield
    a = jnp.exp(g)
    q = qk_ref[:, :hk] * (GLA_DK ** -0.5)
    k = qk_ref[:, hk:]
    v = v_ref[...]
    rowmod = lax.broadcasted_iota(jnp.int32, (tb, hk), 0) % S
    has_prev = (rowmod < S - 1) if reverse else (rowmod >= 1)
    kd = k
    ps = [(q * kd).astype(jnp.bfloat16)]
    for d in range(1, S):
        kd = a * jnp.where(has_prev, pltpu.roll(kd, tb - 1 if reverse else 1, 0), 0.0)
        ps.append((q * kd).astype(jnp.bfloat16))
    scores = jnp.dot(jnp.concatenate(ps, axis=1), place_ref[...], preferred_element_type=jnp.float32)
    yield
    head_of_lane = lax.broadcasted_iota(jnp.int32, (S, GLA_WIDTH), 1) // GLA_DV
    zeros = jnp.zeros((S, GLA_WIDTH), jnp.float32)
    acc = {}
    for s in range(tb // S):
        rows = slice(s * S, (s + 1) * S)
        band = pltpu.roll(scores[rows], 0, 1, stride=1, stride_axis=0).astype(jnp.bfloat16)
        v_blk = v[rows]
        pieces = []
        for h in range(GLA_HEADS):
            vh = jnp.where(head_of_lane == h, v_blk, 0.0)
            pieces += [vh, zeros] if reverse else [zeros, vh]
        acc[s] = jnp.dot(band, jnp.concatenate(pieces, axis=0).astype(jnp.bfloat16),
                         preferred_element_type=jnp.float32)
        if s % 2 == 1:
            yield
    qd = (q * jnp.exp(cs)).astype(jnp.bfloat16)
    blockdiag = (lax.broadcasted_iota(jnp.int32, (hk, GLA_WIDTH), 0) // GLA_DK ==
                 lax.broadcasted_iota(jnp.int32, (hk, GLA_WIDTH), 1) // GLA_DV)
    n_blk = tb // S
    order = range(n_blk - 1, -1, -1) if reverse else range(n_blk)
    edge_row = 0 if reverse else S - 1
    edges = jnp.concatenate([cs[s * S + edge_row:s * S + edge_row + 1] for s in range(n_blk)], axis=0)
    dec_cols = jnp.exp(edges).T
    upd = {}
    for s in order:
        rows = slice(s * S, (s + 1) * S)
        kt = (k[rows] * jnp.exp(edges[s:s + 1] - cs[rows])).astype(jnp.bfloat16)
        upd[s] = jnp.where(blockdiag, lax.dot_general(kt, v[rows].astype(jnp.bfloat16), _TN,
                                                      preferred_element_type=jnp.float32), 0.0)
        yield
    st = st_ref[...]
    for s in order:
        rows = slice(s * S, (s + 1) * S)
        o_ref[rows, :] = acc[s] + jnp.dot(qd[rows], st.astype(jnp.bfloat16), preferred_element_type=jnp.float32)
        st = st * dec_cols[:, s:s + 1] + upd[s]
        yield
    st_ref[...] = st


DN_HALO = 8
DN_BETA_LANE = 2 * GLA_LOWRANK
DN_A_LANE = DN_BETA_LANE + 2 * DN_HEADS


def _split_dot(x, w_bf16):
    hi = x.astype(jnp.bfloat16)
    lo = (x - hi.astype(jnp.float32)).astype(jnp.bfloat16)
    return (jnp.dot(hi, w_bf16, preferred_element_type=jnp.float32) +
            jnp.dot(lo, w_bf16, preferred_element_type=jnp.float32))


def _group_indicator(n, group):
    r = lax.broadcasted_iota(jnp.int32, (n, n), 0) // group
    c = lax.broadcasted_iota(jnp.int32, (n, n), 1) // group
    return (r == c).astype(jnp.bfloat16)


def _mm(a, b, dims=None):
    a, b = a.astype(jnp.bfloat16), b.astype(jnp.bfloat16)
    if dims is None:
        return jnp.dot(a, b, preferred_element_type=jnp.float32)
    return lax.dot_general(a, b, dims, preferred_element_type=jnp.float32)


def _head_blockdiag(x):
    c, hw = x.shape
    w = hw // DN_HEADS
    stacked = jnp.concatenate([x] * DN_HEADS, axis=0)
    r = lax.broadcasted_iota(jnp.int32, stacked.shape, 0) // c
    l = lax.broadcasted_iota(jnp.int32, stacked.shape, 1) // w
    return jnp.where(r == l, stacked, 0.0)


def _dn_chunk_prepare(qkv, gate, direction):
    C, W = DN_CHUNK, DN_HEADS * DN_DK
    reverse = direction == 1
    q, k, v = qkv[:, :W], qkv[:, W:2 * W], qkv[:, 2 * W:]
    g_lo = DN_A_LANE + direction * DN_HEADS
    b_lo = DN_BETA_LANE + direction * DN_HEADS
    cs = _seg_cumsum(gate, C, reverse)
    beta = gate
    head_of_lane = lax.broadcasted_iota(jnp.int32, (C, W), 1) // DN_DK

    def expand(x, lo):
        out = jnp.broadcast_to(x[:, lo:lo + 1], (C, W))
        for h in range(1, DN_HEADS):
            out = jnp.where(head_of_lane == h, jnp.broadcast_to(x[:, lo + h:lo + h + 1], (C, W)), out)
        return out

    bcol, bexp = expand(cs, g_lo), expand(beta, b_lo)
    yield
    row = lax.broadcasted_iota(jnp.int32, (C, W), 0)
    col = lax.broadcasted_iota(jnp.int32, (C, W), 1) % C
    diag = jnp.where(row == col, bcol, 0.0)
    diag_hi = diag.astype(jnp.bfloat16)
    diag_lo = (diag - diag_hi.astype(jnp.float32)).astype(jnp.bfloat16)
    ones = jnp.ones((C, C), jnp.bfloat16)
    brow = (jnp.dot(ones, diag_hi, preferred_element_type=jnp.float32) +
            jnp.dot(ones, diag_lo, preferred_element_type=jnp.float32))
    incl = (col >= row) if reverse else (col <= row)
    strict = (col > row) if reverse else (col < row)
    lmat = jnp.exp(jnp.where(incl, bcol - brow, -jnp.inf))
    yield
    kb = k * bexp
    kk = _mm(jnp.concatenate([kb, q], axis=0), _head_blockdiag(k), _NT)
    m = jnp.where(strict, kk[:C] * lmat, 0.0)
    aqk = jnp.where(incl, kk[C:] * lmat, 0.0)
    yield
    t = jnp.where(row == col, 1.0, 0.0)
    s = 1
    while s < C:
        same = (row // (2 * s)) == (col // (2 * s))
        r_hi, c_hi = (row % (2 * s)) >= s, (col % (2 * s)) >= s
        off = same & ((c_hi & ~r_hi) if reverse else (r_hi & ~c_hi))
        m_off = jnp.where(off, m, 0.0)
        if s == 1:
            t = t - m_off
        else:
            x = _mm(m_off, _head_blockdiag(t))
            yield
            t = t - _mm(t, _head_blockdiag(x))
            yield
        s *= 2
    ecs = jnp.exp(bcol)
    u = _mm(t, _head_blockdiag(v * bexp))
    w = _mm(t, _head_blockdiag(kb * ecs))
    yield
    edge = bcol[0:1] if reverse else bcol[C - 1:C]
    return dict(u=u, wq=jnp.concatenate([w, q * ecs], axis=0), aqk=aqk, k_dec=k * jnp.exp(edge - bcol),
                dl=jnp.exp(edge))


def _lockstep_stages(generators):
    results = [None] * len(generators)
    live = list(range(len(generators)))
    while live:
        still = []
        for i in live:
            try:
                next(generators[i])
                still.append(i)
            except StopIteration as stop:
                results[i] = stop.value
        live = still
        yield
    return results


def _lockstep(generators):
    stages = _lockstep_stages(generators)
    while True:
        try:
            next(stages)
        except StopIteration as stop:
            return stop.value


def _dn_state_chain(prepared, s_ref, o_ref, reverse):
    C, W = DN_CHUNK, DN_HEADS * DN_DK
    n_chunks = len(prepared)
    bd = (lax.broadcasted_iota(jnp.int32, (W, W), 0) // DN_DK ==
          lax.broadcasted_iota(jnp.int32, (W, W), 1) // DN_DV)
    s = s_ref[...]
    for c in (range(n_chunks - 1, -1, -1) if reverse else range(n_chunks)):
        pc = prepared[c]
        both = _mm(pc["wq"], s)
        v_new = pc["u"] - both[:C]
        yield
        o_ref[c * C:(c + 1) * C, :] = both[C:] + _mm(pc["aqk"], _head_blockdiag(v_new))
        s = s * pc["dl"] + jnp.where(bd, _mm(pc["k_dec"], v_new, _TN), 0.0)
        yield
    s_ref[...] = s


def _dn_tile(qkvf_ref, gf_ref, qkvb_ref, gb_ref, of_ref, ob_ref, sf_ref, sb_ref):
    C = DN_CHUNK
    n_chunks = qkvf_ref.shape[0] // C
    rows = lambda c: slice(c * C, (c + 1) * C)
    prepared = yield from _lockstep_stages(
        [_dn_chunk_prepare(qkvf_ref[rows(c), :], gf_ref[rows(c), :], 0) for c in range(n_chunks)] +
        [_dn_chunk_prepare(qkvb_ref[rows(c), :], gb_ref[rows(c), :], 1) for c in range(n_chunks)])
    yield from _lockstep_stages([_dn_state_chain(prepared[:n_chunks], sf_ref, of_ref, False),
                                 _dn_state_chain(prepared[n_chunks:], sb_ref, ob_ref, True)])


def _seq_mixers_kernel(qkf_ref, vf_ref, smf_ref, qkb_ref, vb_ref, smb_ref, wg_ref, bg_ref, tri_ref, place_ref,
                       dqkvf_ref, dgf_ref, dqkvb_ref, dgb_ref,
                       gof_ref, gob_ref, dof_ref, dob_ref, stf_ref, stb_ref, sf_ref, sb_ref):
    @pl.when(pl.program_id(1) == 0)
    def _():
        for ref in (stf_ref, stb_ref, sf_ref, sb_ref):
            ref[...] = jnp.zeros_like(ref)

    _lockstep([_dn_tile(dqkvf_ref, dgf_ref, dqkvb_ref, dgb_ref, dof_ref, dob_ref, sf_ref, sb_ref),
               _gla_direction(qkf_ref, vf_ref, smf_ref, wg_ref.at[0], bg_ref.at[0], tri_ref.at[0], place_ref.at[0],
                              stf_ref, gof_ref, False),
               _gla_direction(qkb_ref, vb_ref, smb_ref, wg_ref.at[1], bg_ref.at[1], tri_ref.at[1], place_ref.at[1],
                              stb_ref, gob_ref, True)])


def seq_mixers(gla_qk, gla_v, small, w_lr2, b_lr, dn_qkvn, dn_gates, batch, seq):
    hk = GLA_HEADS * GLA_DK
    W = DN_HEADS * DN_DK
    tb = min(GLA_TILE, seq)
    nt = seq // tb
    wg = jnp.zeros((2, 128, hk), jnp.float32)
    for z in range(2):
        wg = wg.at[z, z * GLA_LOWRANK:(z + 1) * GLA_LOWRANK].set(w_lr2[z])
    tri, place = _gla_constants(tb)
    fwd = lambda w: pl.BlockSpec((tb, w), lambda b, i: (b * nt + i, 0))
    bwd = lambda w: pl.BlockSpec((tb, w), lambda b, i: (b * nt + nt - 1 - i, 0))
    const = lambda a: pl.BlockSpec(a.shape, lambda b, i: (0,) * a.ndim)
    out = jax.ShapeDtypeStruct((batch * seq, GLA_WIDTH), jnp.float32)
    bg = b_lr.reshape(2, 1, hk).astype(jnp.float32)
    wg = wg.astype(jnp.bfloat16)
    return pl.pallas_call(
        _seq_mixers_kernel,
        grid=(batch, nt),
        in_specs=[fwd(2 * hk), fwd(GLA_WIDTH), fwd(128), bwd(2 * hk), bwd(GLA_WIDTH), bwd(128),
                  const(wg), const(bg), const(tri), const(place),
                  fwd(DN_QKV), fwd(128), bwd(DN_QKV), bwd(128)],
        out_specs=[fwd(GLA_WIDTH), bwd(GLA_WIDTH), fwd(DN_WIDTH), bwd(DN_WIDTH)],
        out_shape=[out, out, out, out],
        scratch_shapes=[pltpu.VMEM((hk, GLA_WIDTH), jnp.float32), pltpu.VMEM((hk, GLA_WIDTH), jnp.float32),
                        pltpu.VMEM((W, DN_WIDTH), jnp.float32), pltpu.VMEM((W, DN_WIDTH), jnp.float32)],
        compiler_params=pltpu.CompilerParams(dimension_semantics=("arbitrary", "arbitrary"),
                                             vmem_limit_bytes=VMEM_LIMIT_BYTES),
        name="seq_mixers",
    )(gla_qk, gla_v, small, gla_qk, gla_v, small, wg, bg, tri, place, dn_qkvn, dn_gates, dn_qkvn, dn_gates)


def _gated_group_norm(o, gate, gn, ind):
    ms = _split_dot(o * o, ind) * (1.0 / HEAD_DIM)
    return (o * lax.rsqrt(ms + EPS) * gn * (gate * (1.0 / (1.0 + jnp.exp(-gate))))).astype(jnp.bfloat16)


def _out_mlp_kernel(glaf_ref, glab_ref, glag_ref, swa_ref, dnf_ref, dnb_ref, dnz_ref, x_ref, wo_ref,
                    gng_ref, gnd_ref, g1_ref, g2_ref, w1_ref, w2_ref, g3_ref, o_ref):
    ind = _group_indicator(GLA_WIDTH, HEAD_DIM)
    y_gla = _gated_group_norm(glaf_ref[...] + glab_ref[...], glag_ref[...], gng_ref[...], ind)
    y_dn = _gated_group_norm(dnf_ref[...] + dnb_ref[...], dnz_ref[...], gnd_ref[...], ind)
    a, b = GLA_WIDTH, GLA_WIDTH + SWA_WIDTH
    m = (jnp.dot(y_gla, wo_ref[0:a, :], preferred_element_type=jnp.float32) +
         jnp.dot(swa_ref[...], wo_ref[a:b, :], preferred_element_type=jnp.float32) +
         jnp.dot(y_dn, wo_ref[b:, :], preferred_element_type=jnp.float32))
    x1 = x_ref[...] + _rms(m, g1_ref[...])
    h = _rms(x1, g2_ref[...]).astype(jnp.bfloat16)
    acc = jnp.zeros_like(x1)
    for c in range(D_FF // FF_CHUNK):
        f = jnp.dot(h, w1_ref[:, c * FF_CHUNK:(c + 1) * FF_CHUNK], preferred_element_type=jnp.float32)
        f = jnp.square(jnp.maximum(f, 0.0)).astype(jnp.bfloat16)
        acc = acc + jnp.dot(f, w2_ref[c * FF_CHUNK:(c + 1) * FF_CHUNK, :], preferred_element_type=jnp.float32)
    o_ref[...] = x1 + _rms(acc, g3_ref[...])


def out_mlp(gla_f, gla_b, gla_g, swa_o, dn_f, dn_b, dn_z, x2d, wo, gla_norm_g, dn_norm_g, g1, g2, w1, w2, g3):
    n, d = x2d.shape
    tm = min(ROW_TILE, n)
    row = lambda w: pl.BlockSpec((tm, w), lambda i: (i, 0))
    vec = _const_spec((1, d))
    hvec = _const_spec((1, GLA_WIDTH))
    tile_gain = lambda g, h: jnp.tile(g.astype(jnp.float32), h).reshape(1, -1)
    return pl.pallas_call(
        _out_mlp_kernel,
        grid=(n // tm,),
        in_specs=[row(GLA_WIDTH), row(GLA_WIDTH), row(GLA_WIDTH), row(SWA_WIDTH), row(DN_WIDTH), row(DN_WIDTH),
                  row(DN_WIDTH), row(d), _const_spec((d, d)), hvec, hvec, vec, vec,
                  _const_spec((d, D_FF)), _const_spec((D_FF, d)), vec],
        out_specs=row(d),
        out_shape=jax.ShapeDtypeStruct((n, d), jnp.float32),
        compiler_params=pltpu.CompilerParams(dimension_semantics=("arbitrary",),
                                             vmem_limit_bytes=VMEM_LIMIT_BYTES),
        name="out_mlp",
    )(gla_f, gla_b, gla_g, swa_o, dn_f, dn_b, dn_z, x2d, wo, tile_gain(gla_norm_g, GLA_HEADS),
      tile_gain(dn_norm_g, DN_HEADS), g1.reshape(1, d), g2.reshape(1, d), w1, w2, g3.reshape(1, d))


def _trunk(x, p):
    B, T, D = x.shape
    bias = p["band_bias"]
    x2 = x.reshape(B * T, D)
    for l in range(DEPTH):
        dn_qkvn, small, swa_q, swa_kv, gla_qk, gla_v, gla_g, dn_z, dn_gates = in_proj(
            x2, p["mix_pre_g"][l], p["w_in_packed"][l], p["dn_conv_w"][l], p["dn_a_log"][l], p["dn_dt_bias"][l], T)
        o_swa = swa_attention(swa_q, swa_kv, p["swa_sink"][l], bias, B, T)
        gla_f, gla_b, dn_f, dn_b = seq_mixers(gla_qk, gla_v, small, p["gla_w_lr2"][l], p["gla_b_lr"][l],
                                              dn_qkvn, dn_gates, B, T)
        x2 = out_mlp(gla_f, gla_b, gla_g, o_swa, dn_f, dn_b, dn_z, x2, p["w_out_bf16"][l], p["gla_norm_g"][l],
                     p["dn_norm_g"][l], p["mix_post_g"][l], p["mlp_pre_g"][l],
                     p["mlp_w1_bf16"][l], p["mlp_w2_bf16"][l], p["mlp_post_g"][l])
    return x2.reshape(B, T, D)


def _prepare_params(rel_bias_table, mix_pre_g, w_in, gla_w_lr2, gla_b_lr, gla_norm_g, swa_sink, dn_conv_w, dn_a_log,
                    dn_dt_bias, dn_norm_g, w_out, mix_post_g, mlp_pre_g, mlp_w1, mlp_w2, mlp_post_g):
    return dict(mix_pre_g=mix_pre_g, gla_w_lr2=gla_w_lr2, gla_b_lr=gla_b_lr,
                gla_norm_g=gla_norm_g, swa_sink=swa_sink, dn_conv_w=dn_conv_w, dn_a_log=dn_a_log,
                dn_dt_bias=dn_dt_bias, dn_norm_g=dn_norm_g, mix_post_g=mix_post_g, mlp_pre_g=mlp_pre_g,
                mlp_post_g=mlp_post_g,
                band_bias=band_bias(rel_bias_table),
                w_in_packed=pack_w_in(w_in), w_out_bf16=w_out.astype(jnp.bfloat16),
                mlp_w1_bf16=mlp_w1.astype(jnp.bfloat16), mlp_w2_bf16=mlp_w2.astype(jnp.bfloat16))


def kernel(x_prompt, x_sample, rel_bias_table, mix_pre_g, w_in, gla_w_lr2, gla_b_lr, gla_norm_g, swa_sink, dn_conv_w, dn_a_log, dn_dt_bias, dn_norm_g, w_out, mix_post_g, mlp_pre_g, mlp_w1, mlp_w2, mlp_post_g):
    p = _prepare_params(rel_bias_table, mix_pre_g, w_in, gla_w_lr2, gla_b_lr, gla_norm_g, swa_sink, dn_conv_w,
                        dn_a_log, dn_dt_bias, dn_norm_g, w_out, mix_post_g, mlp_pre_g, mlp_w1, mlp_w2, mlp_post_g)
    return (_trunk(x_prompt, p), _trunk(x_sample, p))
```

```python
import functools
import math

import numpy as np

import jax
import jax.numpy as jnp
from jax import lax
from jax.experimental import pallas as pl
from jax.experimental.pallas import tpu as pltpu

D_MODEL = 1024
DEPTH = 4
HEAD_DIM = 64
EPS = 1e-6
GLA_HEADS = 4
GLA_DK = 32
GLA_DV = 64
GLA_LOWRANK = 16
GLA_GATE_NORMALIZER = 16.0
SWA_HEADS = 8
SWA_KV_HEADS = 2
SWA_WINDOW = 128
SWA_BLOCK = 128
NUM_BUCKETS = 32
MAX_DISTANCE = 128
DN_HEADS = 4
DN_DK = 64
DN_DV = 64
DN_CONV = 5
DN_CHUNK = 64
GLA_WIDTH = GLA_HEADS * GLA_DV
SWA_WIDTH = SWA_HEADS * HEAD_DIM
DN_WIDTH = DN_HEADS * DN_DV
D_FF = 4 * D_MODEL
DN_QKV = DN_HEADS * (2 * DN_DK + DN_DV)
IN_SIZES = (GLA_HEADS * GLA_DK, GLA_HEADS * GLA_DK, GLA_WIDTH, GLA_WIDTH, 2 * GLA_LOWRANK,
            SWA_WIDTH, SWA_KV_HEADS * HEAD_DIM, SWA_KV_HEADS * HEAD_DIM,
            DN_QKV, DN_WIDTH, 2 * DN_HEADS, 2 * DN_HEADS)
D_IN = sum(IN_SIZES)

VMEM_LIMIT_BYTES = 56 * 1024 * 1024
ROW_TILE = 512
IN_ROW_TILE = 1024
FF_CHUNK = 1024


def _rms(xf, g):
    return xf * lax.rsqrt(jnp.mean(xf * xf, axis=-1, keepdims=True) + EPS) * g


def _const_spec(shape):
    return pl.BlockSpec(shape, lambda i: (0,) * len(shape), pipeline_mode=pl.Buffered(1))


def _layer_spec(layer, shape):
    return pl.BlockSpec((None,) + shape, lambda i: (layer,) + (0,) * len(shape), pipeline_mode=pl.Buffered(1))


_O = {}
_acc = 0
for _name, _width in (("dn_qkv", DN_QKV), ("small", 128), ("swa_q", SWA_WIDTH), ("swa_kv", 4 * 2 * HEAD_DIM),
                      ("gla_qk", 2 * GLA_HEADS * GLA_DK), ("gla_v", GLA_WIDTH), ("gla_g", GLA_WIDTH),
                      ("dn_z", DN_WIDTH)):
    _O[_name] = (_acc, _width)
    _acc += _width
D_IN_PACKED = _acc
D_IN_LEAD = _O["small"][0] + _O["small"][1]
_OUT_DTYPES = {"swa_q": jnp.bfloat16, "swa_kv": jnp.bfloat16}


PACK_ROWS = 256


def _pack_w_in_kernel(w_ref, o_ref):
    w = w_ref[0]
    offs, acc = [], 0
    for s in IN_SIZES:
        offs.append(acc)
        acc += s
    gq, gk, gv, gg, glr, sq, sk, sv, dqkv, dz, dbeta, da = [w[:, o:o + s] for o, s in zip(offs, IN_SIZES)]
    hd = HEAD_DIM
    kv = [sk[:, 0:hd], sk[:, 0:hd], sk[:, hd:], sk[:, hd:], sv[:, 0:hd], sv[:, 0:hd], sv[:, hd:], sv[:, hd:]]
    used = glr.shape[1] + dbeta.shape[1] + da.shape[1]
    small = jnp.concatenate([glr, dbeta, da, jnp.zeros((w.shape[0], 128 - used), w.dtype)], axis=1)
    pieces = dict(dn_qkv=dqkv, small=small, swa_q=sq, swa_kv=jnp.concatenate(kv, axis=1),
                  gla_qk=jnp.concatenate([gq, gk], axis=1), gla_v=gv, gla_g=gg, dn_z=dz)
    for name, (off, width) in _O.items():
        o_ref[0, :, off:off + width] = pieces[name].astype(o_ref.dtype)


def pack_w_in(w_in):
    depth, d, d_in = w_in.shape
    return pl.pallas_call(
        _pack_w_in_kernel,
        grid=(depth, d // PACK_ROWS),
        in_specs=[pl.BlockSpec((1, PACK_ROWS, d_in), lambda l, i: (l, i, 0))],
        out_specs=pl.BlockSpec((1, PACK_ROWS, D_IN_PACKED), lambda l, i: (l, i, 0)),
        out_shape=jax.ShapeDtypeStruct((depth, d, D_IN_PACKED), jnp.bfloat16),
        compiler_params=pltpu.CompilerParams(dimension_semantics=("arbitrary", "arbitrary"),
                                             vmem_limit_bytes=VMEM_LIMIT_BYTES),
        name="pack_w_in",
    )(w_in)


def _in_proj_kernel(tiles_per_seq, xp_ref, xc_ref, xn_ref, g_ref, w_ref, cw_ref, na_ref, dt_ref, *refs):
    o_refs, gate_ref, ext_ref = refs[:len(_O)], refs[len(_O)], refs[len(_O) + 1]
    tm, H = xc_ref.shape[0], DN_HALO
    i = pl.program_id(0)
    first = (i % tiles_per_seq) == 0
    last = (i % tiles_per_seq) == tiles_per_seq - 1
    x = jnp.concatenate([xp_ref[...], xc_ref[...], xn_ref[...]], axis=0)
    h = _rms(x, g_ref[...])
    h_tile = h[H:H + tm].astype(jnp.bfloat16)
    out = dict(zip(_O, o_refs))
    y = jnp.dot(h.astype(jnp.bfloat16), w_ref[:, :D_IN_LEAD], preferred_element_type=jnp.float32)
    off, width = _O["dn_qkv"]
    ext_ref[0:H, :] = jnp.where(first, 0.0, y[0:H, off:off + width])
    ext_ref[H:H + tm, :] = y[H:H + tm, off:off + width]
    ext_ref[H + tm:, :] = jnp.where(last, 0.0, y[H + tm:, off:off + width])
    half = DN_CONV // 2
    w = DN_HEADS * DN_DK
    ind = _group_indicator(w, DN_DK)
    sumsq = lambda a: jnp.dot((a * a).astype(jnp.bfloat16), ind, preferred_element_type=jnp.float32)

    def project(names):
        lo, hi = _O[names[0]][0], _O[names[-1]][0] + _O[names[-1]][1]
        part = jnp.dot(h_tile, w_ref[:, lo:hi], preferred_element_type=jnp.float32)
        for name in names:
            o, wd = _O[name]
            out[name][...] = part[:, o - lo:o - lo + wd].astype(out[name].dtype)

    def deltanet_rows(r0, r1):
        acc = ext_ref[H - half + r0:H - half + r1, :] * cw_ref[0:1, :]
        for j in range(1, DN_CONV):
            acc = acc + ext_ref[H - half + j + r0:H - half + j + r1, :] * cw_ref[j:j + 1, :]
        c = acc * (1.0 / (1.0 + jnp.exp(-acc)))
        q, k = c[:, :w], c[:, w:2 * w]
        out["dn_qkv"][r0:r1, :w] = q * lax.rsqrt(sumsq(q) + EPS) * (DN_DK ** -0.5)
        out["dn_qkv"][r0:r1, w:2 * w] = k * lax.rsqrt(sumsq(k) + EPS)
        out["dn_qkv"][r0:r1, 2 * w:] = c[:, 2 * w:]

    groups = (["swa_q", "swa_kv"], ["gla_qk", "gla_v", "gla_g", "dn_z"])
    for n, names in enumerate(groups):
        project(names)
        deltanet_rows(n * tm // len(groups), (n + 1) * tm // len(groups))
    off, width = _O["small"]
    s = y[H:H + tm, off:off + width]
    out["small"][...] = s
    lane = lax.broadcasted_iota(jnp.int32, s.shape, 1)
    beta = 1.0 / (1.0 + jnp.exp(-s))
    a = s + dt_ref[...]
    gdec = na_ref[...] * (jnp.maximum(a, 0.0) + jnp.log1p(jnp.exp(-jnp.abs(a))))
    gate_ref[...] = jnp.where((lane >= DN_BETA_LANE) & (lane < DN_A_LANE), beta,
                              jnp.where((lane >= DN_A_LANE) & (lane < DN_A_LANE + 2 * DN_HEADS), gdec, 0.0))


def in_proj(x2d, g, w_packed, layer, conv_w, a_log, dt_bias, seq):
    n, d = x2d.shape
    tm = min(IN_ROW_TILE, seq)
    hb = tm // DN_HALO
    nh = n // DN_HALO
    lanes = lambda v: jnp.zeros((1, 128), jnp.float32).at[0, DN_A_LANE:DN_A_LANE + 2 * DN_HEADS].set(v.reshape(-1))
    row = lambda w: pl.BlockSpec((tm, w), lambda i: (i, 0))
    return pl.pallas_call(
        functools.partial(_in_proj_kernel, seq // tm),
        grid=(n // tm,),
        in_specs=[pl.BlockSpec((DN_HALO, d), lambda i: (jnp.maximum(i * hb - 1, 0), 0)),
                  row(d),
                  pl.BlockSpec((DN_HALO, d), lambda i: (jnp.minimum((i + 1) * hb, nh - 1), 0)),
                  _const_spec((1, d)),
                  _layer_spec(layer, (d, D_IN_PACKED)),
                  _const_spec((DN_CONV, DN_QKV)), _const_spec((1, 128)), _const_spec((1, 128))],
        out_specs=[row(w) for _, w in _O.values()] + [row(128)],
        out_shape=[jax.ShapeDtypeStruct((n, w), _OUT_DTYPES.get(name, jnp.float32))
                   for name, (_, w) in _O.items()] + [jax.ShapeDtypeStruct((n, 128), jnp.float32)],
        scratch_shapes=[pltpu.VMEM((tm + 2 * DN_HALO, DN_QKV), jnp.float32)],
        compiler_params=pltpu.CompilerParams(dimension_semantics=("arbitrary",),
                                             vmem_limit_bytes=VMEM_LIMIT_BYTES),
        name="in_proj",
    )(x2d, x2d, x2d, g.reshape(1, d), w_packed, conv_w.astype(jnp.float32),
      lanes(-jnp.exp(a_log.astype(jnp.float32))), lanes(dt_bias.astype(jnp.float32)))


SWA_Q_TILE = 512
SWA_GROUP = SWA_HEADS // SWA_KV_HEADS


def _t5_bucket(rel):
    nb = NUM_BUCKETS // 2
    max_exact = nb // 2
    base = jnp.where(rel > 0, nb, 0)
    n = jnp.abs(rel)
    nf = jnp.maximum(n, 1).astype(jnp.float32)
    large = max_exact + (jnp.log(nf / max_exact) / math.log(MAX_DISTANCE / max_exact)
                         * (nb - max_exact)).astype(jnp.int32)
    large = jnp.minimum(large, nb - 1)
    return base + jnp.where(n < max_exact, n, large)


def _band_bias_kernel(tab_ref, idx_ref, o_ref):
    W = SWA_BLOCK
    idx = idx_ref[...]
    i = lax.broadcasted_iota(jnp.int32, idx.shape, 0)
    j = lax.broadcasted_iota(jnp.int32, idx.shape, 1)
    in_band = jnp.abs(j - W - i) <= SWA_WINDOW
    for h in range(SWA_HEADS):
        acc = jnp.zeros(idx.shape, jnp.float32)
        for b in range(NUM_BUCKETS):
            acc = jnp.where(idx == b, tab_ref[b, h], acc)
        o_ref[h] = jnp.where(in_band, acc, -jnp.inf)


def band_bias(table):
    W = SWA_BLOCK
    rel = jnp.arange(3 * W)[None, :] - W - jnp.arange(W)[:, None]
    idx = _t5_bucket(rel).astype(jnp.int32)
    return pl.pallas_call(
        _band_bias_kernel,
        in_specs=[pl.BlockSpec(memory_space=pltpu.SMEM), pl.BlockSpec(memory_space=pltpu.VMEM)],
        out_specs=pl.BlockSpec(memory_space=pltpu.VMEM),
        out_shape=jax.ShapeDtypeStruct((SWA_HEADS, W, 3 * W), jnp.float32),
        name="band_bias",
    )(table.astype(jnp.float32), idx)


def _swa_kernel(sink_ref, q_ref, kvp_ref, kvc_ref, kvn_ref, bias_ref, o_ref, kv_buf):
    W = SWA_BLOCK
    n_sub = q_ref.shape[0] // W
    n = pl.program_id(1)
    last = pl.num_programs(1) - 1
    kv_buf[0:W, :] = kvp_ref[...]
    kv_buf[W:W + n_sub * W, :] = kvc_ref[...]
    kv_buf[W + n_sub * W:, :] = kvn_ref[...]
    lane = lax.broadcasted_iota(jnp.int32, (W, 2 * HEAD_DIM), 1)
    lo = lane < HEAD_DIM
    lo3 = (lax.broadcasted_iota(jnp.int32, (3 * W, 2 * HEAD_DIM), 1) < HEAD_DIM).astype(jnp.float32).astype(jnp.bfloat16)
    hi3 = 1 - lo3
    key_blk = lax.broadcasted_iota(jnp.int32, (1, 3 * W), 1) // W
    neg = jnp.float32(-jnp.inf)
    for s in range(n_sub):
        edge = jnp.zeros((1, 3 * W), jnp.float32)
        if s == 0:
            edge = jnp.where((key_blk == 0) & (n == 0), neg, edge)
        if s == n_sub - 1:
            edge = jnp.where((key_blk == 2) & (n == last), neg, edge)
        rows = slice(s * W, (s + 1) * W)
        for g in range(SWA_KV_HEADS):
            kd = kv_buf[s * W:(s + 3) * W, g * 128:(g + 1) * 128]
            vd = kv_buf[s * W:(s + 3) * W, 256 + g * 128:256 + (g + 1) * 128]
            k2 = jnp.concatenate([kd * lo3, kd * hi3], axis=0)
            v2 = jnp.concatenate([jnp.concatenate([vd * lo3, lo3], axis=1),
                                  jnp.concatenate([vd * hi3, hi3], axis=1)], axis=0)
            for pair in range(SWA_GROUP // 2):
                col = (g * SWA_GROUP // 2 + pair) * 128
                qp = q_ref[rows, col:col + 128] * (HEAD_DIM ** -0.5)
                s2 = lax.dot_general(qp, k2, (((1,), (1,)), ((), ())),
                                     preferred_element_type=jnp.float32)
                ps, sink_terms = [], []
                for par in range(2):
                    h = g * SWA_GROUP + 2 * pair + par
                    sc = s2[:, par * 3 * W:(par + 1) * 3 * W] + bias_ref[h]
                    if s == 0 or s == n_sub - 1:
                        sc = sc + edge
                    sink = sink_ref[h]
                    m = jnp.maximum(jnp.max(sc, axis=-1, keepdims=True), sink)
                    ps.append(jnp.exp((sc - m).astype(jnp.bfloat16)))
                    sink_terms.append(jnp.exp(sink - m))
                o2 = jnp.dot(jnp.concatenate(ps, axis=1), v2, preferred_element_type=jnp.float32)
                denom = o2[:, 128:] + jnp.where(lo, sink_terms[0], sink_terms[1])
                o_ref[rows, col:col + 128] = (o2[:, :128] * (1.0 / denom)).astype(o_ref.dtype)


def swa_attention(q, kv, sink, bias, batch, seq):
    W = SWA_BLOCK
    tq = min(SWA_Q_TILE, seq)
    n_sub = tq // W
    nq = seq // tq
    nb = seq // W
    return pl.pallas_call(
        _swa_kernel,
        grid=(batch, nq),
        in_specs=[pl.BlockSpec(memory_space=pltpu.SMEM),
                  pl.BlockSpec((tq, SWA_WIDTH), lambda b, i: (b * nq + i, 0)),
                  pl.BlockSpec((W, 512), lambda b, i: (b * nb + jnp.maximum(i * n_sub - 1, 0), 0)),
                  pl.BlockSpec((tq, 512), lambda b, i: (b * nq + i, 0)),
                  pl.BlockSpec((W, 512), lambda b, i: (b * nb + jnp.minimum((i + 1) * n_sub, nb - 1), 0)),
                  pl.BlockSpec((SWA_HEADS, W, 3 * W), lambda b, i: (0, 0, 0), pipeline_mode=pl.Buffered(1))],
        out_specs=pl.BlockSpec((tq, SWA_WIDTH), lambda b, i: (b * nq + i, 0)),
        out_shape=jax.ShapeDtypeStruct((batch * seq, SWA_WIDTH), jnp.bfloat16),
        scratch_shapes=[pltpu.VMEM((tq + 2 * W, 512), jnp.bfloat16)],
        compiler_params=pltpu.CompilerParams(dimension_semantics=("arbitrary", "arbitrary"),
                                             vmem_limit_bytes=VMEM_LIMIT_BYTES),
        name="swa",
    )(sink.astype(jnp.float32), q, kv, kv, kv, bias)


GLA_TILE = 512
GLA_SUB = 16
_NT = (((1,), (1,)), ((), ()))
_TN = (((0,), (0,)), ((), ()))


def _seg_cumsum(x, seg, reverse):
    n = x.shape[0]
    rowmod = lax.broadcasted_iota(jnp.int32, x.shape, 0) % seg
    sh = 1
    while sh < seg:
        if reverse:
            x = x + jnp.where(rowmod < seg - sh, pltpu.roll(x, n - sh, 0), 0.0)
        else:
            x = x + jnp.where(rowmod >= sh, pltpu.roll(x, sh, 0), 0.0)
        sh *= 2
    return x


def _log_sigmoid(x):
    return jnp.minimum(x, 0.0) - jnp.log1p(jnp.exp(-jnp.abs(x)))


def _gla_constants(tb):
    S, hk = GLA_SUB, GLA_HEADS * GLA_DK
    r = np.arange(tb)
    same = (r[:, None] // S) == (r[None, :] // S)
    tri = np.stack([same & (r[None, :] <= r[:, None]), same & (r[None, :] >= r[:, None])])
    src = np.arange(S * hk)
    d, h = src // hk, (src % hk) // GLA_DK
    lane = np.arange(128)
    place = np.stack([lane[None, :] == (32 * h + S - d)[:, None], lane[None, :] == (32 * h + d)[:, None]])
    return jnp.asarray(tri, jnp.bfloat16), jnp.asarray(place, jnp.bfloat16)


def _gla_direction(qk_ref, v_ref, sm_ref, wg_ref, bg_ref, tri_ref, place_ref, st_ref, o_ref, reverse):
    tb, S = qk_ref.shape[0], GLA_SUB
    hk = GLA_HEADS * GLA_DK
    logits = jnp.dot(sm_ref[...].astype(jnp.bfloat16), wg_ref[...], preferred_element_type=jnp.float32)
    g = _log_sigmoid(logits + bg_ref[...]) * (1.0 / GLA_GATE_NORMALIZER)
    g_hi = g.astype(jnp.bfloat16)
    g_lo = (g - g_hi.astype(jnp.float32)).astype(jnp.bfloat16)
    cs = (jnp.dot(tri_ref[...], g_hi, preferred_element_type=jnp.float32) +
          jnp.dot(tri_ref[...], g_lo, preferred_element_type=jnp.float32))
    yield
    a = jnp.exp(g)
    q = qk_ref[:, :hk] * (GLA_DK ** -0.5)
    k = qk_ref[:, hk:]
    v = v_ref[...]
    rowmod = lax.broadcasted_iota(jnp.int32, (tb, hk), 0) % S
    has_prev = (rowmod < S - 1) if reverse else (rowmod >= 1)
    kd = k
    ps = [(q * kd).astype(jnp.bfloat16)]
    for d in range(1, S):
        kd = a * jnp.where(has_prev, pltpu.roll(kd, tb - 1 if reverse else 1, 0), 0.0)
        ps.append((q * kd).astype(jnp.bfloat16))
    scores = jnp.dot(jnp.concatenate(ps, axis=1), place_ref[...], preferred_element_type=jnp.float32)
    yield
    head_of_lane = lax.broadcasted_iota(jnp.int32, (S, GLA_WIDTH), 1) // GLA_DV
    zeros = jnp.zeros((S, GLA_WIDTH), jnp.float32)
    acc = {}
    for s in range(tb // S):
        rows = slice(s * S, (s + 1) * S)
        band = pltpu.roll(scores[rows], 0, 1, stride=1, stride_axis=0).astype(jnp.bfloat16)
        v_blk = v[rows]
        pieces = []
        for h in range(GLA_HEADS):
            vh = jnp.where(head_of_lane == h, v_blk, 0.0)
            pieces += [vh, zeros] if reverse else [zeros, vh]
        acc[s] = jnp.dot(band, jnp.concatenate(pieces, axis=0).astype(jnp.bfloat16),
                         preferred_element_type=jnp.float32)
        if s % 2 == 1:
            yield
    qd = (q * jnp.exp(cs)).astype(jnp.bfloat16)
    blockdiag = (lax.broadcasted_iota(jnp.int32, (hk, GLA_WIDTH), 0) // GLA_DK ==
                 lax.broadcasted_iota(jnp.int32, (hk, GLA_WIDTH), 1) // GLA_DV)
    n_blk = tb // S
    order = range(n_blk - 1, -1, -1) if reverse else range(n_blk)
    edge_row = 0 if reverse else S - 1
    edges = jnp.concatenate([cs[s * S + edge_row:s * S + edge_row + 1] for s in range(n_blk)], axis=0)
    dec_cols = jnp.exp(edges).T
    upd = {}
    for s in order:
        rows = slice(s * S, (s + 1) * S)
        kt = (k[rows] * jnp.exp(edges[s:s + 1] - cs[rows])).astype(jnp.bfloat16)
        upd[s] = jnp.where(blockdiag, lax.dot_general(kt, v[rows].astype(jnp.bfloat16), _TN,
                                                      preferred_element_type=jnp.float32), 0.0)
        yield
    st = st_ref[...]
    for s in order:
        rows = slice(s * S, (s + 1) * S)
        o_ref[rows, :] = acc[s] + jnp.dot(qd[rows], st.astype(jnp.bfloat16), preferred_element_type=jnp.float32)
        st = st * dec_cols[:, s:s + 1] + upd[s]
        yield
    st_ref[...] = st


DN_HALO = 8
DN_BETA_LANE = 2 * GLA_LOWRANK
DN_A_LANE = DN_BETA_LANE + 2 * DN_HEADS


def _split_dot(x, w_bf16):
    hi = x.astype(jnp.bfloat16)
    lo = (x - hi.astype(jnp.float32)).astype(jnp.bfloat16)
    return (jnp.dot(hi, w_bf16, preferred_element_type=jnp.float32) +
            jnp.dot(lo, w_bf16, preferred_element_type=jnp.float32))


def _group_indicator(n, group):
    r = lax.broadcasted_iota(jnp.int32, (n, n), 0) // group
    c = lax.broadcasted_iota(jnp.int32, (n, n), 1) // group
    return (r == c).astype(jnp.bfloat16)


def _mm(a, b, dims=None):
    a, b = a.astype(jnp.bfloat16), b.astype(jnp.bfloat16)
    if dims is None:
        return jnp.dot(a, b, preferred_element_type=jnp.float32)
    return lax.dot_general(a, b, dims, preferred_element_type=jnp.float32)


def _head_blockdiag(x):
    c, hw = x.shape
    w = hw // DN_HEADS
    stacked = jnp.concatenate([x] * DN_HEADS, axis=0)
    r = lax.broadcasted_iota(jnp.int32, stacked.shape, 0) // c
    l = lax.broadcasted_iota(jnp.int32, stacked.shape, 1) // w
    return jnp.where(r == l, stacked, 0.0)


def _dn_chunk_prepare(qkv, gate, direction):
    C, W = DN_CHUNK, DN_HEADS * DN_DK
    reverse = direction == 1
    q, k, v = qkv[:, :W], qkv[:, W:2 * W], qkv[:, 2 * W:]
    g_lo = DN_A_LANE + direction * DN_HEADS
    b_lo = DN_BETA_LANE + direction * DN_HEADS
    cs = _seg_cumsum(gate, C, reverse)
    beta = gate
    head_of_lane = lax.broadcasted_iota(jnp.int32, (C, W), 1) // DN_DK

    def expand(x, lo):
        out = jnp.broadcast_to(x[:, lo:lo + 1], (C, W))
        for h in range(1, DN_HEADS):
            out = jnp.where(head_of_lane == h, jnp.broadcast_to(x[:, lo + h:lo + h + 1], (C, W)), out)
        return out

    bcol, bexp = expand(cs, g_lo), expand(beta, b_lo)
    yield
    row = lax.broadcasted_iota(jnp.int32, (C, W), 0)
    col = lax.broadcasted_iota(jnp.int32, (C, W), 1) % C
    diag = jnp.where(row == col, bcol, 0.0)
    diag_hi = diag.astype(jnp.bfloat16)
    diag_lo = (diag - diag_hi.astype(jnp.float32)).astype(jnp.bfloat16)
    ones = jnp.ones((C, C), jnp.bfloat16)
    brow = (jnp.dot(ones, diag_hi, preferred_element_type=jnp.float32) +
            jnp.dot(ones, diag_lo, preferred_element_type=jnp.float32))
    incl = (col >= row) if reverse else (col <= row)
    strict = (col > row) if reverse else (col < row)
    lmat = jnp.exp(jnp.where(incl, bcol - brow, -jnp.inf))
    yield
    kb = k * bexp
    kk = _mm(jnp.concatenate([kb, q], axis=0), _head_blockdiag(k), _NT)
    m = jnp.where(strict, kk[:C] * lmat, 0.0)
    aqk = jnp.where(incl, kk[C:] * lmat, 0.0)
    yield
    t = jnp.where(row == col, 1.0, 0.0)
    s = 1
    while s < C:
        same = (row // (2 * s)) == (col // (2 * s))
        r_hi, c_hi = (row % (2 * s)) >= s, (col % (2 * s)) >= s
        off = same & ((c_hi & ~r_hi) if reverse else (r_hi & ~c_hi))
        m_off = jnp.where(off, m, 0.0)
        if s == 1:
            t = t - m_off
        else:
            x = _mm(m_off, _head_blockdiag(t))
            yield
            t = t - _mm(t, _head_blockdiag(x))
            yield
        s *= 2
    ecs = jnp.exp(bcol)
    u = _mm(t, _head_blockdiag(v * bexp))
    w = _mm(t, _head_blockdiag(kb * ecs))
    yield
    edge = bcol[0:1] if reverse else bcol[C - 1:C]
    return dict(u=u, wq=jnp.concatenate([w, q * ecs], axis=0), aqk=aqk, k_dec=k * jnp.exp(edge - bcol),
                dl=jnp.exp(edge))


def _lockstep_stages(generators):
    results = [None] * len(generators)
    live = list(range(len(generators)))
    while live:
        still = []
        for i in live:
            try:
                next(generators[i])
                still.append(i)
            except StopIteration as stop:
                results[i] = stop.value
        live = still
        yield
    return results


def _lockstep(generators):
    stages = _lockstep_stages(generators)
    while True:
        try:
            next(stages)
        except StopIteration as stop:
            return stop.value


def _dn_state_chain(prepared, s_ref, o_ref, reverse):
    C, W = DN_CHUNK, DN_HEADS * DN_DK
    n_chunks = len(prepared)
    bd = (lax.broadcasted_iota(jnp.int32, (W, W), 0) // DN_DK ==
          lax.broadcasted_iota(jnp.int32, (W, W), 1) // DN_DV)
    s = s_ref[...]
    for c in (range(n_chunks - 1, -1, -1) if reverse else range(n_chunks)):
        pc = prepared[c]
        both = _mm(pc["wq"], s)
        v_new = pc["u"] - both[:C]
        yield
        o_ref[c * C:(c + 1) * C, :] = both[C:] + _mm(pc["aqk"], _head_blockdiag(v_new))
        s = s * pc["dl"] + jnp.where(bd, _mm(pc["k_dec"], v_new, _TN), 0.0)
        yield
    s_ref[...] = s


def _dn_tile(qkvf_ref, gf_ref, qkvb_ref, gb_ref, of_ref, ob_ref, sf_ref, sb_ref):
    C = DN_CHUNK
    n_chunks = qkvf_ref.shape[0] // C
    rows = lambda c: slice(c * C, (c + 1) * C)
    prepared = yield from _lockstep_stages(
        [_dn_chunk_prepare(qkvf_ref[rows(c), :], gf_ref[rows(c), :], 0) for c in range(n_chunks)] +
        [_dn_chunk_prepare(qkvb_ref[rows(c), :], gb_ref[rows(c), :], 1) for c in range(n_chunks)])
    yield from _lockstep_stages([_dn_state_chain(prepared[:n_chunks], sf_ref, of_ref, False),
                                 _dn_state_chain(prepared[n_chunks:], sb_ref, ob_ref, True)])


def _seq_mixers_kernel(qkf_ref, vf_ref, smf_ref, qkb_ref, vb_ref, smb_ref, wg_ref, bg_ref, tri_ref, place_ref,
                       dqkvf_ref, dgf_ref, dqkvb_ref, dgb_ref,
                       gof_ref, gob_ref, dof_ref, dob_ref, stf_ref, stb_ref, sf_ref, sb_ref):
    @pl.when(pl.program_id(1) == 0)
    def _():
        for ref in (stf_ref, stb_ref, sf_ref, sb_ref):
            ref[...] = jnp.zeros_like(ref)

    _lockstep([_dn_tile(dqkvf_ref, dgf_ref, dqkvb_ref, dgb_ref, dof_ref, dob_ref, sf_ref, sb_ref),
               _gla_direction(qkf_ref, vf_ref, smf_ref, wg_ref.at[0], bg_ref.at[0], tri_ref.at[0], place_ref.at[0],
                              stf_ref, gof_ref, False),
               _gla_direction(qkb_ref, vb_ref, smb_ref, wg_ref.at[1], bg_ref.at[1], tri_ref.at[1], place_ref.at[1],
                              stb_ref, gob_ref, True)])


def seq_mixers(gla_qk, gla_v, small, w_lr2, b_lr, dn_qkvn, dn_gates, batch, seq):
    hk = GLA_HEADS * GLA_DK
    W = DN_HEADS * DN_DK
    tb = min(GLA_TILE, seq)
    nt = seq // tb
    wg = jnp.zeros((2, 128, hk), jnp.float32)
    for z in range(2):
        wg = wg.at[z, z * GLA_LOWRANK:(z + 1) * GLA_LOWRANK].set(w_lr2[z])
    tri, place = _gla_constants(tb)
    fwd = lambda w: pl.BlockSpec((tb, w), lambda b, i: (b * nt + i, 0))
    bwd = lambda w: pl.BlockSpec((tb, w), lambda b, i: (b * nt + nt - 1 - i, 0))
    const = lambda a: pl.BlockSpec(a.shape, lambda b, i: (0,) * a.ndim)
    out = jax.ShapeDtypeStruct((batch * seq, GLA_WIDTH), jnp.float32)
    bg = b_lr.reshape(2, 1, hk).astype(jnp.float32)
    wg = wg.astype(jnp.bfloat16)
    return pl.pallas_call(
        _seq_mixers_kernel,
        grid=(batch, nt),
        in_specs=[fwd(2 * hk), fwd(GLA_WIDTH), fwd(128), bwd(2 * hk), bwd(GLA_WIDTH), bwd(128),
                  const(wg), const(bg), const(tri), const(place),
                  fwd(DN_QKV), fwd(128), bwd(DN_QKV), bwd(128)],
        out_specs=[fwd(GLA_WIDTH), bwd(GLA_WIDTH), fwd(DN_WIDTH), bwd(DN_WIDTH)],
        out_shape=[out, out, out, out],
        scratch_shapes=[pltpu.VMEM((hk, GLA_WIDTH), jnp.float32), pltpu.VMEM((hk, GLA_WIDTH), jnp.float32),
                        pltpu.VMEM((W, DN_WIDTH), jnp.float32), pltpu.VMEM((W, DN_WIDTH), jnp.float32)],
        compiler_params=pltpu.CompilerParams(dimension_semantics=("arbitrary", "arbitrary"),
                                             vmem_limit_bytes=VMEM_LIMIT_BYTES),
        name="seq_mixers",
    )(gla_qk, gla_v, small, gla_qk, gla_v, small, wg, bg, tri, place, dn_qkvn, dn_gates, dn_qkvn, dn_gates)


def _gated_group_norm(o, gate, gn, ind):
    ms = _split_dot(o * o, ind) * (1.0 / HEAD_DIM)
    return (o * lax.rsqrt(ms + EPS) * gn * (gate * (1.0 / (1.0 + jnp.exp(-gate))))).astype(jnp.bfloat16)


def _out_mlp_kernel(glaf_ref, glab_ref, glag_ref, swa_ref, dnf_ref, dnb_ref, dnz_ref, x_ref, wo_ref,
                    gng_ref, gnd_ref, g1_ref, g2_ref, w1_ref, w2_ref, g3_ref, o_ref):
    ind = _group_indicator(GLA_WIDTH, HEAD_DIM)
    y_gla = _gated_group_norm(glaf_ref[...] + glab_ref[...], glag_ref[...], gng_ref[...], ind)
    y_dn = _gated_group_norm(dnf_ref[...] + dnb_ref[...], dnz_ref[...], gnd_ref[...], ind)
    a, b = GLA_WIDTH, GLA_WIDTH + SWA_WIDTH
    m = (jnp.dot(y_gla, wo_ref[0:a, :], preferred_element_type=jnp.float32) +
         jnp.dot(swa_ref[...], wo_ref[a:b, :], preferred_element_type=jnp.float32) +
         jnp.dot(y_dn, wo_ref[b:, :], preferred_element_type=jnp.float32))
    x1 = x_ref[...] + _rms(m, g1_ref[...])
    h = _rms(x1, g2_ref[...]).astype(jnp.bfloat16)
    acc = jnp.zeros_like(x1)
    for c in range(D_FF // FF_CHUNK):
        f = jnp.dot(h, w1_ref[:, c * FF_CHUNK:(c + 1) * FF_CHUNK], preferred_element_type=jnp.float32)
        f = jnp.square(jnp.maximum(f, 0.0)).astype(jnp.bfloat16)
        acc = acc + jnp.dot(f, w2_ref[c * FF_CHUNK:(c + 1) * FF_CHUNK, :], preferred_element_type=jnp.float32)
    o_ref[...] = x1 + _rms(acc, g3_ref[...])


def out_mlp(gla_f, gla_b, gla_g, swa_o, dn_f, dn_b, dn_z, x2d, layer, wo, gla_norm_g, dn_norm_g, g1, g2, w1, w2, g3):
    n, d = x2d.shape
    tm = min(ROW_TILE, n)
    row = lambda w: pl.BlockSpec((tm, w), lambda i: (i, 0))
    vec = _const_spec((1, d))
    hvec = _const_spec((1, GLA_WIDTH))
    tile_gain = lambda g, h: jnp.tile(g.astype(jnp.float32), h).reshape(1, -1)
    return pl.pallas_call(
        _out_mlp_kernel,
        grid=(n // tm,),
        in_specs=[row(GLA_WIDTH), row(GLA_WIDTH), row(GLA_WIDTH), row(SWA_WIDTH), row(DN_WIDTH), row(DN_WIDTH),
                  row(DN_WIDTH), row(d), _layer_spec(layer, (d, d)), hvec, hvec, vec, vec,
                  _layer_spec(layer, (d, D_FF)), _layer_spec(layer, (D_FF, d)), vec],
        out_specs=row(d),
        out_shape=jax.ShapeDtypeStruct((n, d), jnp.float32),
        compiler_params=pltpu.CompilerParams(dimension_semantics=("arbitrary",),
                                             vmem_limit_bytes=VMEM_LIMIT_BYTES),
        name="out_mlp",
    )(gla_f, gla_b, gla_g, swa_o, dn_f, dn_b, dn_z, x2d, wo, tile_gain(gla_norm_g, GLA_HEADS),
      tile_gain(dn_norm_g, DN_HEADS), g1.reshape(1, d), g2.reshape(1, d), w1, w2, g3.reshape(1, d))


def _trunk(x, p):
    B, T, D = x.shape
    bias = p["band_bias"]
    x2 = x.reshape(B * T, D)
    for l in range(DEPTH):
        dn_qkvn, small, swa_q, swa_kv, gla_qk, gla_v, gla_g, dn_z, dn_gates = in_proj(
            x2, p["mix_pre_g"][l], p["w_in_packed"], l, p["dn_conv_w"][l], p["dn_a_log"][l], p["dn_dt_bias"][l], T)
        o_swa = swa_attention(swa_q, swa_kv, p["swa_sink"][l], bias, B, T)
        gla_f, gla_b, dn_f, dn_b = seq_mixers(gla_qk, gla_v, small, p["gla_w_lr2"][l], p["gla_b_lr"][l],
                                              dn_qkvn, dn_gates, B, T)
        x2 = out_mlp(gla_f, gla_b, gla_g, o_swa, dn_f, dn_b, dn_z, x2, l, p["w_out_bf16"], p["gla_norm_g"][l],
                     p["dn_norm_g"][l], p["mix_post_g"][l], p["mlp_pre_g"][l],
                     p["mlp_w1_bf16"], p["mlp_w2_bf16"], p["mlp_post_g"][l])
    return x2.reshape(B, T, D)


def _prepare_params(rel_bias_table, mix_pre_g, w_in, gla_w_lr2, gla_b_lr, gla_norm_g, swa_sink, dn_conv_w, dn_a_log,
                    dn_dt_bias, dn_norm_g, w_out, mix_post_g, mlp_pre_g, mlp_w1, mlp_w2, mlp_post_g):
    return dict(mix_pre_g=mix_pre_g, gla_w_lr2=gla_w_lr2, gla_b_lr=gla_b_lr,
                gla_norm_g=gla_norm_g, swa_sink=swa_sink, dn_conv_w=dn_conv_w, dn_a_log=dn_a_log,
                dn_dt_bias=dn_dt_bias, dn_norm_g=dn_norm_g, mix_post_g=mix_post_g, mlp_pre_g=mlp_pre_g,
                mlp_post_g=mlp_post_g,
                band_bias=band_bias(rel_bias_table),
                w_in_packed=pack_w_in(w_in), w_out_bf16=w_out.astype(jnp.bfloat16),
                mlp_w1_bf16=mlp_w1.astype(jnp.bfloat16), mlp_w2_bf16=mlp_w2.astype(jnp.bfloat16))


def kernel(x_prompt, x_sample, rel_bias_table, mix_pre_g, w_in, gla_w_lr2, gla_b_lr, gla_norm_g, swa_sink, dn_conv_w, dn_a_log, dn_dt_bias, dn_norm_g, w_out, mix_post_g, mlp_pre_g, mlp_w1, mlp_w2, mlp_post_g):
    p = _prepare_params(rel_bias_table, mix_pre_g, w_in, gla_w_lr2, gla_b_lr, gla_norm_g, swa_sink, dn_conv_w,
                        dn_a_log, dn_dt_bias, dn_norm_g, w_out, mix_post_g, mlp_pre_g, mlp_w1, mlp_w2, mlp_post_g)
    return (_trunk(x_prompt, p), _trunk(x_sample, p))
```

```python
import functools
import math

import numpy as np

import jax
import jax.numpy as jnp
from jax import lax
from jax.experimental import pallas as pl
from jax.experimental.pallas import tpu as pltpu

D_MODEL = 1024
DEPTH = 4
HEAD_DIM = 64
EPS = 1e-6
GLA_HEADS = 4
GLA_DK = 32
GLA_DV = 64
GLA_LOWRANK = 16
GLA_GATE_NORMALIZER = 16.0
SWA_HEADS = 8
SWA_KV_HEADS = 2
SWA_WINDOW = 128
SWA_BLOCK = 128
NUM_BUCKETS = 32
MAX_DISTANCE = 128
DN_HEADS = 4
DN_DK = 64
DN_DV = 64
DN_CONV = 5
DN_CHUNK = 64
GLA_WIDTH = GLA_HEADS * GLA_DV
SWA_WIDTH = SWA_HEADS * HEAD_DIM
DN_WIDTH = DN_HEADS * DN_DV
D_FF = 4 * D_MODEL
DN_QKV = DN_HEADS * (2 * DN_DK + DN_DV)
IN_SIZES = (GLA_HEADS * GLA_DK, GLA_HEADS * GLA_DK, GLA_WIDTH, GLA_WIDTH, 2 * GLA_LOWRANK,
            SWA_WIDTH, SWA_KV_HEADS * HEAD_DIM, SWA_KV_HEADS * HEAD_DIM,
            DN_QKV, DN_WIDTH, 2 * DN_HEADS, 2 * DN_HEADS)
D_IN = sum(IN_SIZES)

VMEM_LIMIT_BYTES = 56 * 1024 * 1024
ROW_TILE = 512
IN_ROW_TILE = 1024
FF_CHUNK = 1024


def _rms(xf, g):
    return xf * lax.rsqrt(jnp.mean(xf * xf, axis=-1, keepdims=True) + EPS) * g


def _const_spec(shape):
    return pl.BlockSpec(shape, lambda i: (0,) * len(shape), pipeline_mode=pl.Buffered(1))


def _layer_spec(layer, shape):
    return pl.BlockSpec((None,) + shape, lambda i: (layer,) + (0,) * len(shape), pipeline_mode=pl.Buffered(1))


_O = {}
_acc = 0
for _name, _width in (("dn_qkv", DN_QKV), ("small", 128), ("swa_q", SWA_WIDTH), ("swa_kv", 4 * 2 * HEAD_DIM),
                      ("gla_qk", 2 * GLA_HEADS * GLA_DK), ("gla_v", GLA_WIDTH), ("gla_g", GLA_WIDTH),
                      ("dn_z", DN_WIDTH)):
    _O[_name] = (_acc, _width)
    _acc += _width
D_IN_PACKED = _acc
D_IN_LEAD = _O["small"][0] + _O["small"][1]
_OUT_DTYPES = {"swa_q": jnp.bfloat16, "swa_kv": jnp.bfloat16}


PACK_ROWS = 256


def _pack_w_in_kernel(w_ref, o_ref):
    w = w_ref[0]
    offs, acc = [], 0
    for s in IN_SIZES:
        offs.append(acc)
        acc += s
    gq, gk, gv, gg, glr, sq, sk, sv, dqkv, dz, dbeta, da = [w[:, o:o + s] for o, s in zip(offs, IN_SIZES)]
    hd = HEAD_DIM
    kv = [sk[:, 0:hd], sk[:, 0:hd], sk[:, hd:], sk[:, hd:], sv[:, 0:hd], sv[:, 0:hd], sv[:, hd:], sv[:, hd:]]
    used = glr.shape[1] + dbeta.shape[1] + da.shape[1]
    small = jnp.concatenate([glr, dbeta, da, jnp.zeros((w.shape[0], 128 - used), w.dtype)], axis=1)
    pieces = dict(dn_qkv=dqkv, small=small, swa_q=sq, swa_kv=jnp.concatenate(kv, axis=1),
                  gla_qk=jnp.concatenate([gq, gk], axis=1), gla_v=gv, gla_g=gg, dn_z=dz)
    for name, (off, width) in _O.items():
        o_ref[0, :, off:off + width] = pieces[name].astype(o_ref.dtype)


def pack_w_in(w_in):
    depth, d, d_in = w_in.shape
    return pl.pallas_call(
        _pack_w_in_kernel,
        grid=(depth, d // PACK_ROWS),
        in_specs=[pl.BlockSpec((1, PACK_ROWS, d_in), lambda l, i: (l, i, 0))],
        out_specs=pl.BlockSpec((1, PACK_ROWS, D_IN_PACKED), lambda l, i: (l, i, 0)),
        out_shape=jax.ShapeDtypeStruct((depth, d, D_IN_PACKED), jnp.bfloat16),
        compiler_params=pltpu.CompilerParams(dimension_semantics=("arbitrary", "arbitrary"),
                                             vmem_limit_bytes=VMEM_LIMIT_BYTES),
        name="pack_w_in",
    )(w_in)


def _in_proj_kernel(tiles_per_seq, xp_ref, xc_ref, xn_ref, g_ref, w_ref, cw_ref, na_ref, dt_ref, *refs):
    o_refs, gate_ref, ext_ref = refs[:len(_O)], refs[len(_O)], refs[len(_O) + 1]
    tm, H = xc_ref.shape[0], DN_HALO
    i = pl.program_id(0)
    first = (i % tiles_per_seq) == 0
    last = (i % tiles_per_seq) == tiles_per_seq - 1
    x = jnp.concatenate([xp_ref[...], xc_ref[...], xn_ref[...]], axis=0)
    h = _rms(x, g_ref[...])
    h_tile = h[H:H + tm].astype(jnp.bfloat16)
    out = dict(zip(_O, o_refs))
    y = jnp.dot(h.astype(jnp.bfloat16), w_ref[:, :D_IN_LEAD], preferred_element_type=jnp.float32)
    off, width = _O["dn_qkv"]
    ext_ref[0:H, :] = jnp.where(first, 0.0, y[0:H, off:off + width])
    ext_ref[H:H + tm, :] = y[H:H + tm, off:off + width]
    ext_ref[H + tm:, :] = jnp.where(last, 0.0, y[H + tm:, off:off + width])
    half = DN_CONV // 2
    w = DN_HEADS * DN_DK
    ind = _group_indicator(w, DN_DK)
    sumsq = lambda a: jnp.dot((a * a).astype(jnp.bfloat16), ind, preferred_element_type=jnp.float32)

    def project(names):
        lo, hi = _O[names[0]][0], _O[names[-1]][0] + _O[names[-1]][1]
        part = jnp.dot(h_tile, w_ref[:, lo:hi], preferred_element_type=jnp.float32)
        for name in names:
            o, wd = _O[name]
            out[name][...] = part[:, o - lo:o - lo + wd].astype(out[name].dtype)

    def deltanet_rows(r0, r1):
        acc = ext_ref[H - half + r0:H - half + r1, :] * cw_ref[0:1, :]
        for j in range(1, DN_CONV):
            acc = acc + ext_ref[H - half + j + r0:H - half + j + r1, :] * cw_ref[j:j + 1, :]
        c = acc * (1.0 / (1.0 + jnp.exp(-acc)))
        q, k = c[:, :w], c[:, w:2 * w]
        out["dn_qkv"][r0:r1, :w] = q * lax.rsqrt(sumsq(q) + EPS) * (DN_DK ** -0.5)
        out["dn_qkv"][r0:r1, w:2 * w] = k * lax.rsqrt(sumsq(k) + EPS)
        out["dn_qkv"][r0:r1, 2 * w:] = c[:, 2 * w:]

    groups = (["swa_q", "swa_kv"], ["gla_qk", "gla_v", "gla_g", "dn_z"])
    for n, names in enumerate(groups):
        project(names)
        deltanet_rows(n * tm // len(groups), (n + 1) * tm // len(groups))
    off, width = _O["small"]
    s = y[H:H + tm, off:off + width]
    out["small"][...] = s
    lane = lax.broadcasted_iota(jnp.int32, s.shape, 1)
    beta = 1.0 / (1.0 + jnp.exp(-s))
    a = s + dt_ref[...]
    gdec = na_ref[...] * (jnp.maximum(a, 0.0) + jnp.log1p(jnp.exp(-jnp.abs(a))))
    gate_ref[...] = jnp.where((lane >= DN_BETA_LANE) & (lane < DN_A_LANE), beta,
                              jnp.where((lane >= DN_A_LANE) & (lane < DN_A_LANE + 2 * DN_HEADS), gdec, 0.0))


def in_proj(x2d, g, w_packed, layer, conv_w, a_log, dt_bias, seq):
    n, d = x2d.shape
    tm = min(IN_ROW_TILE, seq)
    hb = tm // DN_HALO
    nh = n // DN_HALO
    lanes = lambda v: jnp.zeros((1, 128), jnp.float32).at[0, DN_A_LANE:DN_A_LANE + 2 * DN_HEADS].set(v.reshape(-1))
    row = lambda w: pl.BlockSpec((tm, w), lambda i: (i, 0))
    return pl.pallas_call(
        functools.partial(_in_proj_kernel, seq // tm),
        grid=(n // tm,),
        in_specs=[pl.BlockSpec((DN_HALO, d), lambda i: (jnp.maximum(i * hb - 1, 0), 0)),
                  row(d),
                  pl.BlockSpec((DN_HALO, d), lambda i: (jnp.minimum((i + 1) * hb, nh - 1), 0)),
                  _const_spec((1, d)),
                  _layer_spec(layer, (d, D_IN_PACKED)),
                  _const_spec((DN_CONV, DN_QKV)), _const_spec((1, 128)), _const_spec((1, 128))],
        out_specs=[row(w) for _, w in _O.values()] + [row(128)],
        out_shape=[jax.ShapeDtypeStruct((n, w), _OUT_DTYPES.get(name, jnp.float32))
                   for name, (_, w) in _O.items()] + [jax.ShapeDtypeStruct((n, 128), jnp.float32)],
        scratch_shapes=[pltpu.VMEM((tm + 2 * DN_HALO, DN_QKV), jnp.float32)],
        compiler_params=pltpu.CompilerParams(dimension_semantics=("arbitrary",),
                                             vmem_limit_bytes=VMEM_LIMIT_BYTES),
        name="in_proj",
    )(x2d, x2d, x2d, g.reshape(1, d), w_packed, conv_w.astype(jnp.float32),
      lanes(-jnp.exp(a_log.astype(jnp.float32))), lanes(dt_bias.astype(jnp.float32)))


SWA_Q_TILE = 512
SWA_GROUP = SWA_HEADS // SWA_KV_HEADS


def _t5_bucket(rel):
    nb = NUM_BUCKETS // 2
    max_exact = nb // 2
    base = jnp.where(rel > 0, nb, 0)
    n = jnp.abs(rel)
    nf = jnp.maximum(n, 1).astype(jnp.float32)
    large = max_exact + (jnp.log(nf / max_exact) / math.log(MAX_DISTANCE / max_exact)
                         * (nb - max_exact)).astype(jnp.int32)
    large = jnp.minimum(large, nb - 1)
    return base + jnp.where(n < max_exact, n, large)


def _band_bias_kernel(tab_ref, idx_ref, o_ref):
    W = SWA_BLOCK
    idx = idx_ref[...]
    i = lax.broadcasted_iota(jnp.int32, idx.shape, 0)
    j = lax.broadcasted_iota(jnp.int32, idx.shape, 1)
    in_band = jnp.abs(j - W - i) <= SWA_WINDOW
    for h in range(SWA_HEADS):
        acc = jnp.zeros(idx.shape, jnp.float32)
        for b in range(NUM_BUCKETS):
            acc = jnp.where(idx == b, tab_ref[b, h], acc)
        o_ref[h] = jnp.where(in_band, acc, -jnp.inf)


def band_bias(table):
    W = SWA_BLOCK
    rel = jnp.arange(3 * W)[None, :] - W - jnp.arange(W)[:, None]
    idx = _t5_bucket(rel).astype(jnp.int32)
    return pl.pallas_call(
        _band_bias_kernel,
        in_specs=[pl.BlockSpec(memory_space=pltpu.SMEM), pl.BlockSpec(memory_space=pltpu.VMEM)],
        out_specs=pl.BlockSpec(memory_space=pltpu.VMEM),
        out_shape=jax.ShapeDtypeStruct((SWA_HEADS, W, 3 * W), jnp.float32),
        name="band_bias",
    )(table.astype(jnp.float32), idx)


def _swa_kernel(sink_ref, q_ref, kvp_ref, kvc_ref, kvn_ref, bias_ref, o_ref, kv_buf):
    W = SWA_BLOCK
    n_sub = q_ref.shape[0] // W
    n = pl.program_id(1)
    last = pl.num_programs(1) - 1
    kv_buf[0:W, :] = kvp_ref[...]
    kv_buf[W:W + n_sub * W, :] = kvc_ref[...]
    kv_buf[W + n_sub * W:, :] = kvn_ref[...]
    lane = lax.broadcasted_iota(jnp.int32, (W, 2 * HEAD_DIM), 1)
    lo = lane < HEAD_DIM
    lo3 = (lax.broadcasted_iota(jnp.int32, (3 * W, 2 * HEAD_DIM), 1) < HEAD_DIM).astype(jnp.float32).astype(jnp.bfloat16)
    hi3 = 1 - lo3
    key_blk = lax.broadcasted_iota(jnp.int32, (1, 3 * W), 1) // W
    neg = jnp.float32(-jnp.inf)
    for s in range(n_sub):
        edge = jnp.zeros((1, 3 * W), jnp.float32)
        if s == 0:
            edge = jnp.where((key_blk == 0) & (n == 0), neg, edge)
        if s == n_sub - 1:
            edge = jnp.where((key_blk == 2) & (n == last), neg, edge)
        rows = slice(s * W, (s + 1) * W)
        for g in range(SWA_KV_HEADS):
            kd = kv_buf[s * W:(s + 3) * W, g * 128:(g + 1) * 128]
            vd = kv_buf[s * W:(s + 3) * W, 256 + g * 128:256 + (g + 1) * 128]
            k2 = jnp.concatenate([kd * lo3, kd * hi3], axis=0)
            v2 = jnp.concatenate([jnp.concatenate([vd * lo3, lo3], axis=1),
                                  jnp.concatenate([vd * hi3, hi3], axis=1)], axis=0)
            for pair in range(SWA_GROUP // 2):
                col = (g * SWA_GROUP // 2 + pair) * 128
                qp = q_ref[rows, col:col + 128] * (HEAD_DIM ** -0.5)
                s2 = lax.dot_general(qp, k2, (((1,), (1,)), ((), ())),
                                     preferred_element_type=jnp.float32)
                ps, sink_terms = [], []
                for par in range(2):
                    h = g * SWA_GROUP + 2 * pair + par
                    sc = s2[:, par * 3 * W:(par + 1) * 3 * W] + bias_ref[h]
                    if s == 0 or s == n_sub - 1:
                        sc = sc + edge
                    sink = sink_ref[h]
                    m = jnp.maximum(jnp.max(sc, axis=-1, keepdims=True), sink)
                    ps.append(jnp.exp((sc - m).astype(jnp.bfloat16)))
                    sink_terms.append(jnp.exp(sink - m))
                o2 = jnp.dot(jnp.concatenate(ps, axis=1), v2, preferred_element_type=jnp.float32)
                denom = o2[:, 128:] + jnp.where(lo, sink_terms[0], sink_terms[1])
                o_ref[rows, col:col + 128] = (o2[:, :128] * (1.0 / denom)).astype(o_ref.dtype)


def swa_attention(q, kv, sink, bias, batch, seq):
    W = SWA_BLOCK
    tq = min(SWA_Q_TILE, seq)
    n_sub = tq // W
    nq = seq // tq
    nb = seq // W
    return pl.pallas_call(
        _swa_kernel,
        grid=(batch, nq),
        in_specs=[pl.BlockSpec(memory_space=pltpu.SMEM),
                  pl.BlockSpec((tq, SWA_WIDTH), lambda b, i: (b * nq + i, 0)),
                  pl.BlockSpec((W, 512), lambda b, i: (b * nb + jnp.maximum(i * n_sub - 1, 0), 0)),
                  pl.BlockSpec((tq, 512), lambda b, i: (b * nq + i, 0)),
                  pl.BlockSpec((W, 512), lambda b, i: (b * nb + jnp.minimum((i + 1) * n_sub, nb - 1), 0)),
                  pl.BlockSpec((SWA_HEADS, W, 3 * W), lambda b, i: (0, 0, 0), pipeline_mode=pl.Buffered(1))],
        out_specs=pl.BlockSpec((tq, SWA_WIDTH), lambda b, i: (b * nq + i, 0)),
        out_shape=jax.ShapeDtypeStruct((batch * seq, SWA_WIDTH), jnp.bfloat16),
        scratch_shapes=[pltpu.VMEM((tq + 2 * W, 512), jnp.bfloat16)],
        compiler_params=pltpu.CompilerParams(dimension_semantics=("arbitrary", "arbitrary"),
                                             vmem_limit_bytes=VMEM_LIMIT_BYTES),
        name="swa",
    )(sink.astype(jnp.float32), q, kv, kv, kv, bias)


GLA_TILE = 512
GLA_SUB = 16
_NT = (((1,), (1,)), ((), ()))
_TN = (((0,), (0,)), ((), ()))


def _seg_cumsum(x, seg, reverse):
    n = x.shape[0]
    rowmod = lax.broadcasted_iota(jnp.int32, x.shape, 0) % seg
    sh = 1
    while sh < seg:
        if reverse:
            x = x + jnp.where(rowmod < seg - sh, pltpu.roll(x, n - sh, 0), 0.0)
        else:
            x = x + jnp.where(rowmod >= sh, pltpu.roll(x, sh, 0), 0.0)
        sh *= 2
    return x


def _log_sigmoid(x):
    return jnp.minimum(x, 0.0) - jnp.log1p(jnp.exp(-jnp.abs(x)))


def _gla_constants(tb):
    S, hk = GLA_SUB, GLA_HEADS * GLA_DK
    r = np.arange(tb)
    same = (r[:, None] // S) == (r[None, :] // S)
    tri = np.stack([same & (r[None, :] <= r[:, None]), same & (r[None, :] >= r[:, None])])
    src = np.arange(S * hk)
    d, h = src // hk, (src % hk) // GLA_DK
    lane = np.arange(128)
    place = np.stack([lane[None, :] == (32 * h + S - d)[:, None], lane[None, :] == (32 * h + d)[:, None]])
    return jnp.asarray(tri, jnp.bfloat16), jnp.asarray(place, jnp.bfloat16)


def _gla_direction(qk_ref, v_ref, sm_ref, wg_ref, bg_ref, tri_ref, place_ref, st_ref, o_ref, reverse):
    tb, S = qk_ref.shape[0], GLA_SUB
    hk = GLA_HEADS * GLA_DK
    logits = jnp.dot(sm_ref[...].astype(jnp.bfloat16), wg_ref[...], preferred_element_type=jnp.float32)
    g = _log_sigmoid(logits + bg_ref[...]) * (1.0 / GLA_GATE_NORMALIZER)
    g_hi = g.astype(jnp.bfloat16)
    g_lo = (g - g_hi.astype(jnp.float32)).astype(jnp.bfloat16)
    cs = (jnp.dot(tri_ref[...], g_hi, preferred_element_type=jnp.float32) +
          jnp.dot(tri_ref[...], g_lo, preferred_element_type=jnp.float32))
    yield
    a = jnp.exp(g)
    q = qk_ref[:, :hk] * (GLA_DK ** -0.5)
    k = qk_ref[:, hk:]
    v = v_ref[...]
    rowmod = lax.broadcasted_iota(jnp.int32, (tb, hk), 0) % S
    has_prev = (rowmod < S - 1) if reverse else (rowmod >= 1)
    kd = k
    ps = [(q * kd).astype(jnp.bfloat16)]
    for d in range(1, S):
        kd = a * jnp.where(has_prev, pltpu.roll(kd, tb - 1 if reverse else 1, 0), 0.0)
        ps.append((q * kd).astype(jnp.bfloat16))
    scores = jnp.dot(jnp.concatenate(ps, axis=1), place_ref[...], preferred_element_type=jnp.float32)
    yield
    head_of_lane = lax.broadcasted_iota(jnp.int32, (S, GLA_WIDTH), 1) // GLA_DV
    zeros = jnp.zeros((S, GLA_WIDTH), jnp.float32)
    acc = {}
    for s in range(tb // S):
        rows = slice(s * S, (s + 1) * S)
        band = pltpu.roll(scores[rows], 0, 1, stride=1, stride_axis=0).astype(jnp.bfloat16)
        v_blk = v[rows]
        pieces = []
        for h in range(GLA_HEADS):
            vh = jnp.where(head_of_lane == h, v_blk, 0.0)
            pieces += [vh, zeros] if reverse else [zeros, vh]
        acc[s] = jnp.dot(band, jnp.concatenate(pieces, axis=0).astype(jnp.bfloat16),
                         preferred_element_type=jnp.float32)
        if s % 2 == 1:
            yield
    qd = (q * jnp.exp(cs)).astype(jnp.bfloat16)
    blockdiag = (lax.broadcasted_iota(jnp.int32, (hk, GLA_WIDTH), 0) // GLA_DK ==
                 lax.broadcasted_iota(jnp.int32, (hk, GLA_WIDTH), 1) // GLA_DV)
    n_blk = tb // S
    order = range(n_blk - 1, -1, -1) if reverse else range(n_blk)
    edge_row = 0 if reverse else S - 1
    edges = jnp.concatenate([cs[s * S + edge_row:s * S + edge_row + 1] for s in range(n_blk)], axis=0)
    dec_cols = jnp.exp(edges).T
    upd = {}
    for s in order:
        rows = slice(s * S, (s + 1) * S)
        kt = (k[rows] * jnp.exp(edges[s:s + 1] - cs[rows])).astype(jnp.bfloat16)
        upd[s] = jnp.where(blockdiag, lax.dot_general(kt, v[rows].astype(jnp.bfloat16), _TN,
                                                      preferred_element_type=jnp.float32), 0.0)
        yield
    st = st_ref[...]
    for s in order:
        rows = slice(s * S, (s + 1) * S)
        o_ref[rows, :] = acc[s] + jnp.dot(qd[rows], st.astype(jnp.bfloat16), preferred_element_type=jnp.float32)
        st = st * dec_cols[:, s:s + 1] + upd[s]
        yield
    st_ref[...] = st


DN_HALO = 8
DN_BETA_LANE = 2 * GLA_LOWRANK
DN_A_LANE = DN_BETA_LANE + 2 * DN_HEADS


def _group_indicator(n, group):
    r = lax.broadcasted_iota(jnp.int32, (n, n), 0) // group
    c = lax.broadcasted_iota(jnp.int32, (n, n), 1) // group
    return (r == c).astype(jnp.bfloat16)


def _mm(a, b, dims=None):
    a, b = a.astype(jnp.bfloat16), b.astype(jnp.bfloat16)
    if dims is None:
        return jnp.dot(a, b, preferred_element_type=jnp.float32)
    return lax.dot_general(a, b, dims, preferred_element_type=jnp.float32)


def _head_blockdiag(x):
    c, hw = x.shape
    w = hw // DN_HEADS
    stacked = jnp.concatenate([x] * DN_HEADS, axis=0)
    r = lax.broadcasted_iota(jnp.int32, stacked.shape, 0) // c
    l = lax.broadcasted_iota(jnp.int32, stacked.shape, 1) // w
    return jnp.where(r == l, stacked, 0.0)


def _dn_chunk_prepare(qkv, gate, direction):
    C, W = DN_CHUNK, DN_HEADS * DN_DK
    reverse = direction == 1
    q, k, v = qkv[:, :W], qkv[:, W:2 * W], qkv[:, 2 * W:]
    g_lo = DN_A_LANE + direction * DN_HEADS
    b_lo = DN_BETA_LANE + direction * DN_HEADS
    cs = _seg_cumsum(gate, C, reverse)
    beta = gate
    head_of_lane = lax.broadcasted_iota(jnp.int32, (C, W), 1) // DN_DK

    def expand(x, lo):
        out = jnp.broadcast_to(x[:, lo:lo + 1], (C, W))
        for h in range(1, DN_HEADS):
            out = jnp.where(head_of_lane == h, jnp.broadcast_to(x[:, lo + h:lo + h + 1], (C, W)), out)
        return out

    bcol, bexp = expand(cs, g_lo), expand(beta, b_lo)
    yield
    row = lax.broadcasted_iota(jnp.int32, (C, W), 0)
    col = lax.broadcasted_iota(jnp.int32, (C, W), 1) % C
    cs_t = jnp.concatenate([cs, jnp.zeros_like(cs)], axis=0).T
    lane_t = lax.broadcasted_iota(jnp.int32, (1, 128), 1)
    halves = []
    for pair in range(DN_HEADS // 2):
        even = cs_t[g_lo + 2 * pair:g_lo + 2 * pair + 1, :]
        odd = pltpu.roll(cs_t[g_lo + 2 * pair + 1:g_lo + 2 * pair + 2, :], C, 1)
        halves.append(jnp.where(lane_t < C, even, odd))
    brow = jnp.broadcast_to(jnp.concatenate(halves, axis=1), (C, W))
    incl = (col >= row) if reverse else (col <= row)
    strict = (col > row) if reverse else (col < row)
    lmat = jnp.exp(jnp.where(incl, bcol - brow, -jnp.inf))
    yield
    kb = k * bexp
    kk = _mm(jnp.concatenate([kb, q], axis=0), _head_blockdiag(k), _NT)
    m = jnp.where(strict, kk[:C] * lmat, 0.0)
    aqk = jnp.where(incl, kk[C:] * lmat, 0.0)
    yield
    t = jnp.where(row == col, 1.0, 0.0)
    s = 1
    while s < C:
        same = (row // (2 * s)) == (col // (2 * s))
        r_hi, c_hi = (row % (2 * s)) >= s, (col % (2 * s)) >= s
        off = same & ((c_hi & ~r_hi) if reverse else (r_hi & ~c_hi))
        m_off = jnp.where(off, m, 0.0)
        if s == 1:
            t = t - m_off
        else:
            x = _mm(m_off, _head_blockdiag(t))
            yield
            t = t - _mm(t, _head_blockdiag(x))
            yield
        s *= 2
    ecs = jnp.exp(bcol)
    u = _mm(t, _head_blockdiag(v * bexp))
    w = _mm(t, _head_blockdiag(kb * ecs))
    yield
    edge = bcol[0:1] if reverse else bcol[C - 1:C]
    return dict(u=u, wq=jnp.concatenate([w, q * ecs], axis=0), aqk=aqk, k_dec=k * jnp.exp(edge - bcol),
                dl=jnp.exp(edge))


def _lockstep_stages(generators):
    results = [None] * len(generators)
    live = list(range(len(generators)))
    while live:
        still = []
        for i in live:
            try:
                next(generators[i])
                still.append(i)
            except StopIteration as stop:
                results[i] = stop.value
        live = still
        yield
    return results


def _lockstep(generators):
    stages = _lockstep_stages(generators)
    while True:
        try:
            next(stages)
        except StopIteration as stop:
            return stop.value


def _dn_state_chain(prepared, s_ref, o_ref, reverse):
    C, W = DN_CHUNK, DN_HEADS * DN_DK
    n_chunks = len(prepared)
    bd = (lax.broadcasted_iota(jnp.int32, (W, W), 0) // DN_DK ==
          lax.broadcasted_iota(jnp.int32, (W, W), 1) // DN_DV)
    s = s_ref[...]
    for c in (range(n_chunks - 1, -1, -1) if reverse else range(n_chunks)):
        pc = prepared[c]
        both = _mm(pc["wq"], s)
        v_new = pc["u"] - both[:C]
        yield
        o_ref[c * C:(c + 1) * C, :] = both[C:] + _mm(pc["aqk"], _head_blockdiag(v_new))
        s = s * pc["dl"] + jnp.where(bd, _mm(pc["k_dec"], v_new, _TN), 0.0)
        yield
    s_ref[...] = s


def _dn_tile(qkvf_ref, gf_ref, qkvb_ref, gb_ref, of_ref, ob_ref, sf_ref, sb_ref):
    C = DN_CHUNK
    n_chunks = qkvf_ref.shape[0] // C
    rows = lambda c: slice(c * C, (c + 1) * C)
    prepared = yield from _lockstep_stages(
        [_dn_chunk_prepare(qkvf_ref[rows(c), :], gf_ref[rows(c), :], 0) for c in range(n_chunks)] +
        [_dn_chunk_prepare(qkvb_ref[rows(c), :], gb_ref[rows(c), :], 1) for c in range(n_chunks)])
    yield from _lockstep_stages([_dn_state_chain(prepared[:n_chunks], sf_ref, of_ref, False),
                                 _dn_state_chain(prepared[n_chunks:], sb_ref, ob_ref, True)])


def _seq_mixers_kernel(qkf_ref, vf_ref, smf_ref, qkb_ref, vb_ref, smb_ref, wg_ref, bg_ref, tri_ref, place_ref,
                       dqkvf_ref, dgf_ref, dqkvb_ref, dgb_ref,
                       gof_ref, gob_ref, dof_ref, dob_ref, stf_ref, stb_ref, sf_ref, sb_ref):
    @pl.when(pl.program_id(1) == 0)
    def _():
        for ref in (stf_ref, stb_ref, sf_ref, sb_ref):
            ref[...] = jnp.zeros_like(ref)

    _lockstep([_dn_tile(dqkvf_ref, dgf_ref, dqkvb_ref, dgb_ref, dof_ref, dob_ref, sf_ref, sb_ref),
               _gla_direction(qkf_ref, vf_ref, smf_ref, wg_ref.at[0], bg_ref.at[0], tri_ref.at[0], place_ref.at[0],
                              stf_ref, gof_ref, False),
               _gla_direction(qkb_ref, vb_ref, smb_ref, wg_ref.at[1], bg_ref.at[1], tri_ref.at[1], place_ref.at[1],
                              stb_ref, gob_ref, True)])


def seq_mixers(gla_qk, gla_v, small, w_lr2, b_lr, dn_qkvn, dn_gates, batch, seq):
    hk = GLA_HEADS * GLA_DK
    W = DN_HEADS * DN_DK
    tb = min(GLA_TILE, seq)
    nt = seq // tb
    wg = jnp.zeros((2, 128, hk), jnp.float32)
    for z in range(2):
        wg = wg.at[z, z * GLA_LOWRANK:(z + 1) * GLA_LOWRANK].set(w_lr2[z])
    tri, place = _gla_constants(tb)
    fwd = lambda w: pl.BlockSpec((tb, w), lambda b, i: (b * nt + i, 0))
    bwd = lambda w: pl.BlockSpec((tb, w), lambda b, i: (b * nt + nt - 1 - i, 0))
    const = lambda a: pl.BlockSpec(a.shape, lambda b, i: (0,) * a.ndim)
    out = jax.ShapeDtypeStruct((batch * seq, GLA_WIDTH), jnp.float32)
    bg = b_lr.reshape(2, 1, hk).astype(jnp.float32)
    wg = wg.astype(jnp.bfloat16)
    return pl.pallas_call(
        _seq_mixers_kernel,
        grid=(batch, nt),
        in_specs=[fwd(2 * hk), fwd(GLA_WIDTH), fwd(128), bwd(2 * hk), bwd(GLA_WIDTH), bwd(128),
                  const(wg), const(bg), const(tri), const(place),
                  fwd(DN_QKV), fwd(128), bwd(DN_QKV), bwd(128)],
        out_specs=[fwd(GLA_WIDTH), bwd(GLA_WIDTH), fwd(DN_WIDTH), bwd(DN_WIDTH)],
        out_shape=[out, out, out, out],
        scratch_shapes=[pltpu.VMEM((hk, GLA_WIDTH), jnp.float32), pltpu.VMEM((hk, GLA_WIDTH), jnp.float32),
                        pltpu.VMEM((W, DN_WIDTH), jnp.float32), pltpu.VMEM((W, DN_WIDTH), jnp.float32)],
        compiler_params=pltpu.CompilerParams(dimension_semantics=("arbitrary", "arbitrary"),
                                             vmem_limit_bytes=VMEM_LIMIT_BYTES),
        name="seq_mixers",
    )(gla_qk, gla_v, small, gla_qk, gla_v, small, wg, bg, tri, place, dn_qkvn, dn_gates, dn_qkvn, dn_gates)


def _gated_group_norm(o, gate, gn, ind):
    ms = jnp.dot((o * o).astype(jnp.bfloat16), ind, preferred_element_type=jnp.float32) * (1.0 / HEAD_DIM)
    return (o * lax.rsqrt(ms + EPS) * gn * (gate * (1.0 / (1.0 + jnp.exp(-gate))))).astype(jnp.bfloat16)


def _out_mlp_kernel(glaf_ref, glab_ref, glag_ref, swa_ref, dnf_ref, dnb_ref, dnz_ref, x_ref, wo_ref,
                    gng_ref, gnd_ref, g1_ref, g2_ref, w1_ref, w2_ref, g3_ref, o_ref):
    ind = _group_indicator(GLA_WIDTH, HEAD_DIM)
    y_gla = _gated_group_norm(glaf_ref[...] + glab_ref[...], glag_ref[...], gng_ref[...], ind)
    y_dn = _gated_group_norm(dnf_ref[...] + dnb_ref[...], dnz_ref[...], gnd_ref[...], ind)
    a, b = GLA_WIDTH, GLA_WIDTH + SWA_WIDTH
    m = (jnp.dot(y_gla, wo_ref[0:a, :], preferred_element_type=jnp.float32) +
         jnp.dot(swa_ref[...], wo_ref[a:b, :], preferred_element_type=jnp.float32) +
         jnp.dot(y_dn, wo_ref[b:, :], preferred_element_type=jnp.float32))
    x1 = x_ref[...] + _rms(m, g1_ref[...])
    h = _rms(x1, g2_ref[...]).astype(jnp.bfloat16)
    acc = jnp.zeros_like(x1)
    for c in range(D_FF // FF_CHUNK):
        f = jnp.dot(h, w1_ref[:, c * FF_CHUNK:(c + 1) * FF_CHUNK], preferred_element_type=jnp.float32)
        f = jnp.square(jnp.maximum(f, 0.0)).astype(jnp.bfloat16)
        acc = acc + jnp.dot(f, w2_ref[c * FF_CHUNK:(c + 1) * FF_CHUNK, :], preferred_element_type=jnp.float32)
    o_ref[...] = x1 + _rms(acc, g3_ref[...])


def out_mlp(gla_f, gla_b, gla_g, swa_o, dn_f, dn_b, dn_z, x2d, layer, wo, gla_norm_g, dn_norm_g, g1, g2, w1, w2, g3):
    n, d = x2d.shape
    tm = min(ROW_TILE, n)
    row = lambda w: pl.BlockSpec((tm, w), lambda i: (i, 0))
    vec = _const_spec((1, d))
    hvec = _const_spec((1, GLA_WIDTH))
    tile_gain = lambda g, h: jnp.tile(g.astype(jnp.float32), h).reshape(1, -1)
    return pl.pallas_call(
        _out_mlp_kernel,
        grid=(n // tm,),
        in_specs=[row(GLA_WIDTH), row(GLA_WIDTH), row(GLA_WIDTH), row(SWA_WIDTH), row(DN_WIDTH), row(DN_WIDTH),
                  row(DN_WIDTH), row(d), _layer_spec(layer, (d, d)), hvec, hvec, vec, vec,
                  _layer_spec(layer, (d, D_FF)), _layer_spec(layer, (D_FF, d)), vec],
        out_specs=row(d),
        out_shape=jax.ShapeDtypeStruct((n, d), jnp.float32),
        compiler_params=pltpu.CompilerParams(dimension_semantics=("arbitrary",),
                                             vmem_limit_bytes=VMEM_LIMIT_BYTES),
        name="out_mlp",
    )(gla_f, gla_b, gla_g, swa_o, dn_f, dn_b, dn_z, x2d, wo, tile_gain(gla_norm_g, GLA_HEADS),
      tile_gain(dn_norm_g, DN_HEADS), g1.reshape(1, d), g2.reshape(1, d), w1, w2, g3.reshape(1, d))


def _trunk(x, p):
    B, T, D = x.shape
    bias = p["band_bias"]
    x2 = x.reshape(B * T, D)
    for l in range(DEPTH):
        dn_qkvn, small, swa_q, swa_kv, gla_qk, gla_v, gla_g, dn_z, dn_gates = in_proj(
            x2, p["mix_pre_g"][l], p["w_in_packed"], l, p["dn_conv_w"][l], p["dn_a_log"][l], p["dn_dt_bias"][l], T)
        o_swa = swa_attention(swa_q, swa_kv, p["swa_sink"][l], bias, B, T)
        gla_f, gla_b, dn_f, dn_b = seq_mixers(gla_qk, gla_v, small, p["gla_w_lr2"][l], p["gla_b_lr"][l],
                                              dn_qkvn, dn_gates, B, T)
        x2 = out_mlp(gla_f, gla_b, gla_g, o_swa, dn_f, dn_b, dn_z, x2, l, p["w_out_bf16"], p["gla_norm_g"][l],
                     p["dn_norm_g"][l], p["mix_post_g"][l], p["mlp_pre_g"][l],
                     p["mlp_w1_bf16"], p["mlp_w2_bf16"], p["mlp_post_g"][l])
    return x2.reshape(B, T, D)


def _prepare_params(rel_bias_table, mix_pre_g, w_in, gla_w_lr2, gla_b_lr, gla_norm_g, swa_sink, dn_conv_w, dn_a_log,
                    dn_dt_bias, dn_norm_g, w_out, mix_post_g, mlp_pre_g, mlp_w1, mlp_w2, mlp_post_g):
    return dict(mix_pre_g=mix_pre_g, gla_w_lr2=gla_w_lr2, gla_b_lr=gla_b_lr,
                gla_norm_g=gla_norm_g, swa_sink=swa_sink, dn_conv_w=dn_conv_w, dn_a_log=dn_a_log,
                dn_dt_bias=dn_dt_bias, dn_norm_g=dn_norm_g, mix_post_g=mix_post_g, mlp_pre_g=mlp_pre_g,
                mlp_post_g=mlp_post_g,
                band_bias=band_bias(rel_bias_table),
                w_in_packed=pack_w_in(w_in), w_out_bf16=w_out.astype(jnp.bfloat16),
                mlp_w1_bf16=mlp_w1.astype(jnp.bfloat16), mlp_w2_bf16=mlp_w2.astype(jnp.bfloat16))


def kernel(x_prompt, x_sample, rel_bias_table, mix_pre_g, w_in, gla_w_lr2, gla_b_lr, gla_norm_g, swa_sink, dn_conv_w, dn_a_log, dn_dt_bias, dn_norm_g, w_out, mix_post_g, mlp_pre_g, mlp_w1, mlp_w2, mlp_post_g):
    p = _prepare_params(rel_bias_table, mix_pre_g, w_in, gla_w_lr2, gla_b_lr, gla_norm_g, swa_sink, dn_conv_w,
                        dn_a_log, dn_dt_bias, dn_norm_g, w_out, mix_post_g, mlp_pre_g, mlp_w1, mlp_w2, mlp_post_g)
    return (_trunk(x_prompt, p), _trunk(x_sample, p))
```

```python
import functools
import math

import numpy as np

import jax
import jax.numpy as jnp
from jax import lax
from jax.experimental import pallas as pl
from jax.experimental.pallas import tpu as pltpu

D_MODEL = 1024
DEPTH = 4
HEAD_DIM = 64
EPS = 1e-6
GLA_HEADS = 4
GLA_DK = 32
GLA_DV = 64
GLA_LOWRANK = 16
GLA_GATE_NORMALIZER = 16.0
SWA_HEADS = 8
SWA_KV_HEADS = 2
SWA_WINDOW = 128
SWA_BLOCK = 128
NUM_BUCKETS = 32
MAX_DISTANCE = 128
DN_HEADS = 4
DN_DK = 64
DN_DV = 64
DN_CONV = 5
DN_CHUNK = 64
GLA_WIDTH = GLA_HEADS * GLA_DV
SWA_WIDTH = SWA_HEADS * HEAD_DIM
DN_WIDTH = DN_HEADS * DN_DV
D_FF = 4 * D_MODEL
DN_QKV = DN_HEADS * (2 * DN_DK + DN_DV)
IN_SIZES = (GLA_HEADS * GLA_DK, GLA_HEADS * GLA_DK, GLA_WIDTH, GLA_WIDTH, 2 * GLA_LOWRANK,
            SWA_WIDTH, SWA_KV_HEADS * HEAD_DIM, SWA_KV_HEADS * HEAD_DIM,
            DN_QKV, DN_WIDTH, 2 * DN_HEADS, 2 * DN_HEADS)
D_IN = sum(IN_SIZES)

VMEM_LIMIT_BYTES = 56 * 1024 * 1024
ROW_TILE = 512
IN_ROW_TILE = 1024
FF_CHUNK = 1024


def _rms(xf, g):
    return xf * lax.rsqrt(jnp.mean(xf * xf, axis=-1, keepdims=True) + EPS) * g


def _const_spec(shape):
    return pl.BlockSpec(shape, lambda i: (0,) * len(shape), pipeline_mode=pl.Buffered(1))


def _layer_spec(layer, shape):
    return pl.BlockSpec((None,) + shape, lambda i: (layer,) + (0,) * len(shape), pipeline_mode=pl.Buffered(1))


_O = {}
_acc = 0
for _name, _width in (("dn_qkv", DN_QKV), ("small", 128), ("swa_q", SWA_WIDTH), ("swa_kv", 4 * 2 * HEAD_DIM),
                      ("gla_qk", 2 * GLA_HEADS * GLA_DK), ("gla_v", GLA_WIDTH), ("gla_g", GLA_WIDTH),
                      ("dn_z", DN_WIDTH)):
    _O[_name] = (_acc, _width)
    _acc += _width
D_IN_PACKED = _acc
D_IN_LEAD = _O["small"][0] + _O["small"][1]
_OUT_DTYPES = {"swa_q": jnp.bfloat16, "swa_kv": jnp.bfloat16}


PACK_ROWS = 256


def _pack_w_in_kernel(w_ref, o_ref):
    w = w_ref[0]
    offs, acc = [], 0
    for s in IN_SIZES:
        offs.append(acc)
        acc += s
    gq, gk, gv, gg, glr, sq, sk, sv, dqkv, dz, dbeta, da = [w[:, o:o + s] for o, s in zip(offs, IN_SIZES)]
    hd = HEAD_DIM
    kv = [sk[:, 0:hd], sk[:, 0:hd], sk[:, hd:], sk[:, hd:], sv[:, 0:hd], sv[:, 0:hd], sv[:, hd:], sv[:, hd:]]
    used = glr.shape[1] + dbeta.shape[1] + da.shape[1]
    small = jnp.concatenate([glr, dbeta, da, jnp.zeros((w.shape[0], 128 - used), w.dtype)], axis=1)
    pieces = dict(dn_qkv=dqkv, small=small, swa_q=sq, swa_kv=jnp.concatenate(kv, axis=1),
                  gla_qk=jnp.concatenate([gq, gk], axis=1), gla_v=gv, gla_g=gg, dn_z=dz)
    for name, (off, width) in _O.items():
        o_ref[0, :, off:off + width] = pieces[name].astype(o_ref.dtype)


def pack_w_in(w_in):
    depth, d, d_in = w_in.shape
    return pl.pallas_call(
        _pack_w_in_kernel,
        grid=(depth, d // PACK_ROWS),
        in_specs=[pl.BlockSpec((1, PACK_ROWS, d_in), lambda l, i: (l, i, 0))],
        out_specs=pl.BlockSpec((1, PACK_ROWS, D_IN_PACKED), lambda l, i: (l, i, 0)),
        out_shape=jax.ShapeDtypeStruct((depth, d, D_IN_PACKED), jnp.bfloat16),
        compiler_params=pltpu.CompilerParams(dimension_semantics=("arbitrary", "arbitrary"),
                                             vmem_limit_bytes=VMEM_LIMIT_BYTES),
        name="pack_w_in",
    )(w_in)


def _in_proj_kernel(tiles_per_seq, xp_ref, xc_ref, xn_ref, g_ref, w_ref, cw_ref, na_ref, dt_ref, *refs):
    o_refs, gate_ref, ext_ref = refs[:len(_O)], refs[len(_O)], refs[len(_O) + 1]
    tm, H = xc_ref.shape[0], DN_HALO
    i = pl.program_id(0)
    first = (i % tiles_per_seq) == 0
    last = (i % tiles_per_seq) == tiles_per_seq - 1
    x = jnp.concatenate([xp_ref[...], xc_ref[...], xn_ref[...]], axis=0)
    h = _rms(x, g_ref[...])
    h_tile = h[H:H + tm].astype(jnp.bfloat16)
    out = dict(zip(_O, o_refs))
    y = jnp.dot(h.astype(jnp.bfloat16), w_ref[:, :D_IN_LEAD], preferred_element_type=jnp.float32)
    off, width = _O["dn_qkv"]
    ext_ref[0:H, :] = jnp.where(first, 0.0, y[0:H, off:off + width])
    ext_ref[H:H + tm, :] = y[H:H + tm, off:off + width]
    ext_ref[H + tm:, :] = jnp.where(last, 0.0, y[H + tm:, off:off + width])
    half = DN_CONV // 2
    w = DN_HEADS * DN_DK
    ind = _group_indicator(w, DN_DK)
    sumsq = lambda a: jnp.dot((a * a).astype(jnp.bfloat16), ind, preferred_element_type=jnp.float32)

    def project(names):
        lo, hi = _O[names[0]][0], _O[names[-1]][0] + _O[names[-1]][1]
        part = jnp.dot(h_tile, w_ref[:, lo:hi], preferred_element_type=jnp.float32)
        for name in names:
            o, wd = _O[name]
            out[name][...] = part[:, o - lo:o - lo + wd].astype(out[name].dtype)

    def deltanet_rows(r0, r1):
        acc = ext_ref[H - half + r0:H - half + r1, :] * cw_ref[0:1, :]
        for j in range(1, DN_CONV):
            acc = acc + ext_ref[H - half + j + r0:H - half + j + r1, :] * cw_ref[j:j + 1, :]
        c = acc * (1.0 / (1.0 + jnp.exp(-acc)))
        q, k = c[:, :w], c[:, w:2 * w]
        out["dn_qkv"][r0:r1, :w] = q * lax.rsqrt(sumsq(q) + EPS) * (DN_DK ** -0.5)
        out["dn_qkv"][r0:r1, w:2 * w] = k * lax.rsqrt(sumsq(k) + EPS)
        out["dn_qkv"][r0:r1, 2 * w:] = c[:, 2 * w:]

    groups = (["swa_q", "swa_kv"], ["gla_qk", "gla_v", "gla_g", "dn_z"])
    for n, names in enumerate(groups):
        project(names)
        deltanet_rows(n * tm // len(groups), (n + 1) * tm // len(groups))
    off, width = _O["small"]
    s = y[H:H + tm, off:off + width]
    out["small"][...] = s
    lane = lax.broadcasted_iota(jnp.int32, s.shape, 1)
    beta = 1.0 / (1.0 + jnp.exp(-s))
    a = s + dt_ref[...]
    gdec = na_ref[...] * (jnp.maximum(a, 0.0) + jnp.log1p(jnp.exp(-jnp.abs(a))))
    gate_ref[...] = jnp.where((lane >= DN_BETA_LANE) & (lane < DN_A_LANE), beta,
                              jnp.where((lane >= DN_A_LANE) & (lane < DN_A_LANE + 2 * DN_HEADS), gdec, 0.0))


def in_proj(x2d, g, w_packed, layer, conv_w, a_log, dt_bias, seq):
    n, d = x2d.shape
    tm = min(IN_ROW_TILE, seq)
    hb = tm // DN_HALO
    nh = n // DN_HALO
    lanes = lambda v: jnp.zeros((1, 128), jnp.float32).at[0, DN_A_LANE:DN_A_LANE + 2 * DN_HEADS].set(v.reshape(-1))
    row = lambda w: pl.BlockSpec((tm, w), lambda i: (i, 0))
    return pl.pallas_call(
        functools.partial(_in_proj_kernel, seq // tm),
        grid=(n // tm,),
        in_specs=[pl.BlockSpec((DN_HALO, d), lambda i: (jnp.maximum(i * hb - 1, 0), 0)),
                  row(d),
                  pl.BlockSpec((DN_HALO, d), lambda i: (jnp.minimum((i + 1) * hb, nh - 1), 0)),
                  _const_spec((1, d)),
                  _layer_spec(layer, (d, D_IN_PACKED)),
                  _const_spec((DN_CONV, DN_QKV)), _const_spec((1, 128)), _const_spec((1, 128))],
        out_specs=[row(w) for _, w in _O.values()] + [row(128)],
        out_shape=[jax.ShapeDtypeStruct((n, w), _OUT_DTYPES.get(name, jnp.float32))
                   for name, (_, w) in _O.items()] + [jax.ShapeDtypeStruct((n, 128), jnp.float32)],
        scratch_shapes=[pltpu.VMEM((tm + 2 * DN_HALO, DN_QKV), jnp.float32)],
        compiler_params=pltpu.CompilerParams(dimension_semantics=("arbitrary",),
                                             vmem_limit_bytes=VMEM_LIMIT_BYTES),
        name="in_proj",
    )(x2d, x2d, x2d, g.reshape(1, d), w_packed, conv_w.astype(jnp.float32),
      lanes(-jnp.exp(a_log.astype(jnp.float32))), lanes(dt_bias.astype(jnp.float32)))


SWA_Q_TILE = 512
SWA_GROUP = SWA_HEADS // SWA_KV_HEADS


def _t5_bucket(rel):
    nb = NUM_BUCKETS // 2
    max_exact = nb // 2
    base = jnp.where(rel > 0, nb, 0)
    n = jnp.abs(rel)
    nf = jnp.maximum(n, 1).astype(jnp.float32)
    large = max_exact + (jnp.log(nf / max_exact) / math.log(MAX_DISTANCE / max_exact)
                         * (nb - max_exact)).astype(jnp.int32)
    large = jnp.minimum(large, nb - 1)
    return base + jnp.where(n < max_exact, n, large)


def _band_bias_kernel(tab_ref, idx_ref, o_ref):
    W = SWA_BLOCK
    idx = idx_ref[...]
    i = lax.broadcasted_iota(jnp.int32, idx.shape, 0)
    j = lax.broadcasted_iota(jnp.int32, idx.shape, 1)
    in_band = jnp.abs(j - W - i) <= SWA_WINDOW
    for h in range(SWA_HEADS):
        acc = jnp.zeros(idx.shape, jnp.float32)
        for b in range(NUM_BUCKETS):
            acc = jnp.where(idx == b, tab_ref[b, h], acc)
        o_ref[h] = jnp.where(in_band, acc, -jnp.inf)


def band_bias(table):
    W = SWA_BLOCK
    rel = jnp.arange(3 * W)[None, :] - W - jnp.arange(W)[:, None]
    idx = _t5_bucket(rel).astype(jnp.int32)
    return pl.pallas_call(
        _band_bias_kernel,
        in_specs=[pl.BlockSpec(memory_space=pltpu.SMEM), pl.BlockSpec(memory_space=pltpu.VMEM)],
        out_specs=pl.BlockSpec(memory_space=pltpu.VMEM),
        out_shape=jax.ShapeDtypeStruct((SWA_HEADS, W, 3 * W), jnp.float32),
        name="band_bias",
    )(table.astype(jnp.float32), idx)


def _swa_kernel(sink_ref, q_ref, kvp_ref, kvc_ref, kvn_ref, bias_ref, o_ref, kv_buf):
    W = SWA_BLOCK
    n_sub = q_ref.shape[0] // W
    n = pl.program_id(1)
    last = pl.num_programs(1) - 1
    kv_buf[0:W, :] = kvp_ref[...]
    kv_buf[W:W + n_sub * W, :] = kvc_ref[...]
    kv_buf[W + n_sub * W:, :] = kvn_ref[...]
    lane = lax.broadcasted_iota(jnp.int32, (W, 2 * HEAD_DIM), 1)
    lo = lane < HEAD_DIM
    lo3 = (lax.broadcasted_iota(jnp.int32, (3 * W, 2 * HEAD_DIM), 1) < HEAD_DIM).astype(jnp.float32).astype(jnp.bfloat16)
    hi3 = 1 - lo3
    key_blk = lax.broadcasted_iota(jnp.int32, (1, 3 * W), 1) // W
    neg = jnp.float32(-jnp.inf)
    for s in range(n_sub):
        edge = jnp.zeros((1, 3 * W), jnp.float32)
        if s == 0:
            edge = jnp.where((key_blk == 0) & (n == 0), neg, edge)
        if s == n_sub - 1:
            edge = jnp.where((key_blk == 2) & (n == last), neg, edge)
        rows = slice(s * W, (s + 1) * W)
        for g in range(SWA_KV_HEADS):
            kd = kv_buf[s * W:(s + 3) * W, g * 128:(g + 1) * 128]
            vd = kv_buf[s * W:(s + 3) * W, 256 + g * 128:256 + (g + 1) * 128]
            k2 = jnp.concatenate([kd * lo3, kd * hi3], axis=0)
            v2 = jnp.concatenate([jnp.concatenate([vd * lo3, lo3], axis=1),
                                  jnp.concatenate([vd * hi3, hi3], axis=1)], axis=0)
            for pair in range(SWA_GROUP // 2):
                col = (g * SWA_GROUP // 2 + pair) * 128
                qp = q_ref[rows, col:col + 128] * (HEAD_DIM ** -0.5)
                s2 = lax.dot_general(qp, k2, (((1,), (1,)), ((), ())),
                                     preferred_element_type=jnp.float32)
                ps, sink_terms = [], []
                for par in range(2):
                    h = g * SWA_GROUP + 2 * pair + par
                    sc = s2[:, par * 3 * W:(par + 1) * 3 * W] + bias_ref[h]
                    if s == 0 or s == n_sub - 1:
                        sc = sc + edge
                    sink = sink_ref[h]
                    m = jnp.maximum(jnp.max(sc, axis=-1, keepdims=True), sink)
                    ps.append(jnp.exp((sc - m).astype(jnp.bfloat16)))
                    sink_terms.append(jnp.exp(sink - m))
                o2 = jnp.dot(jnp.concatenate(ps, axis=1), v2, preferred_element_type=jnp.float32)
                denom = o2[:, 128:] + jnp.where(lo, sink_terms[0], sink_terms[1])
                o_ref[rows, col:col + 128] = (o2[:, :128] * (1.0 / denom)).astype(o_ref.dtype)


def swa_attention(q, kv, sink, bias, batch, seq):
    W = SWA_BLOCK
    tq = min(SWA_Q_TILE, seq)
    n_sub = tq // W
    nq = seq // tq
    nb = seq // W
    return pl.pallas_call(
        _swa_kernel,
        grid=(batch, nq),
        in_specs=[pl.BlockSpec(memory_space=pltpu.SMEM),
                  pl.BlockSpec((tq, SWA_WIDTH), lambda b, i: (b * nq + i, 0)),
                  pl.BlockSpec((W, 512), lambda b, i: (b * nb + jnp.maximum(i * n_sub - 1, 0), 0)),
                  pl.BlockSpec((tq, 512), lambda b, i: (b * nq + i, 0)),
                  pl.BlockSpec((W, 512), lambda b, i: (b * nb + jnp.minimum((i + 1) * n_sub, nb - 1), 0)),
                  pl.BlockSpec((SWA_HEADS, W, 3 * W), lambda b, i: (0, 0, 0), pipeline_mode=pl.Buffered(1))],
        out_specs=pl.BlockSpec((tq, SWA_WIDTH), lambda b, i: (b * nq + i, 0)),
        out_shape=jax.ShapeDtypeStruct((batch * seq, SWA_WIDTH), jnp.bfloat16),
        scratch_shapes=[pltpu.VMEM((tq + 2 * W, 512), jnp.bfloat16)],
        compiler_params=pltpu.CompilerParams(dimension_semantics=("arbitrary", "arbitrary"),
                                             vmem_limit_bytes=VMEM_LIMIT_BYTES),
        name="swa",
    )(sink.astype(jnp.float32), q, kv, kv, kv, bias)


GLA_TILE = 512
GLA_SUB = 16
GLA_CUMSUM_ROWS = 128
_NT = (((1,), (1,)), ((), ()))
_TN = (((0,), (0,)), ((), ()))


def _seg_cumsum(x, seg, reverse):
    n = x.shape[0]
    rowmod = lax.broadcasted_iota(jnp.int32, x.shape, 0) % seg
    sh = 1
    while sh < seg:
        if reverse:
            x = x + jnp.where(rowmod < seg - sh, pltpu.roll(x, n - sh, 0), 0.0)
        else:
            x = x + jnp.where(rowmod >= sh, pltpu.roll(x, sh, 0), 0.0)
        sh *= 2
    return x


def _log_sigmoid(x):
    return jnp.minimum(x, 0.0) - jnp.log1p(jnp.exp(-jnp.abs(x)))


def _gla_constants():
    S, hk = GLA_SUB, GLA_HEADS * GLA_DK
    r = np.arange(GLA_CUMSUM_ROWS)
    same = (r[:, None] // S) == (r[None, :] // S)
    tri = np.stack([same & (r[None, :] <= r[:, None]), same & (r[None, :] >= r[:, None])])
    src = np.arange(S * hk)
    d, h = src // hk, (src % hk) // GLA_DK
    lane = np.arange(128)
    place = np.stack([lane[None, :] == ((S * h - d) % 128)[:, None], lane[None, :] == (S * h + d)[:, None]])
    return jnp.asarray(tri, jnp.bfloat16), jnp.asarray(place, jnp.bfloat16)


def _gla_direction(qk_ref, v_ref, sm_ref, wg_ref, bg_ref, tri_ref, place_ref, st_ref, o_ref, reverse):
    tb, S = qk_ref.shape[0], GLA_SUB
    hk = GLA_HEADS * GLA_DK
    logits = jnp.dot(sm_ref[...].astype(jnp.bfloat16), wg_ref[...], preferred_element_type=jnp.float32)
    g = _log_sigmoid(logits + bg_ref[...]) * (1.0 / GLA_GATE_NORMALIZER)
    g_hi = g.astype(jnp.bfloat16)
    g_lo = (g - g_hi.astype(jnp.float32)).astype(jnp.bfloat16)
    R = GLA_CUMSUM_ROWS
    cs = jnp.concatenate([jnp.dot(tri_ref[...], g_hi[r:r + R], preferred_element_type=jnp.float32) +
                          jnp.dot(tri_ref[...], g_lo[r:r + R], preferred_element_type=jnp.float32)
                          for r in range(0, tb, R)], axis=0)
    yield
    a = jnp.exp(g)
    q = qk_ref[:, :hk] * (GLA_DK ** -0.5)
    k = qk_ref[:, hk:]
    v = v_ref[...]
    rowmod = lax.broadcasted_iota(jnp.int32, (tb, hk), 0) % S
    has_prev = (rowmod < S - 1) if reverse else (rowmod >= 1)
    kd = k
    ps = [(q * kd).astype(jnp.bfloat16)]
    for d in range(1, S):
        kd = a * jnp.where(has_prev, pltpu.roll(kd, tb - 1 if reverse else 1, 0), 0.0)
        ps.append((q * kd).astype(jnp.bfloat16))
    scores = jnp.dot(jnp.concatenate(ps, axis=1), place_ref[...], preferred_element_type=jnp.float32)
    yield
    head_of_lane = lax.broadcasted_iota(jnp.int32, (S, GLA_WIDTH), 1) // GLA_DV
    acc = {}
    for s in range(tb // S):
        rows = slice(s * S, (s + 1) * S)
        band = pltpu.roll(scores[rows], 0, 1, stride=1, stride_axis=0)[:, :GLA_HEADS * S].astype(jnp.bfloat16)
        v_blk = v[rows]
        v_heads = jnp.concatenate([jnp.where(head_of_lane == h, v_blk, 0.0) for h in range(GLA_HEADS)], axis=0)
        acc[s] = jnp.dot(band, v_heads.astype(jnp.bfloat16), preferred_element_type=jnp.float32)
        if s % 2 == 1:
            yield
    qd = (q * jnp.exp(cs)).astype(jnp.bfloat16)
    blockdiag = (lax.broadcasted_iota(jnp.int32, (hk, GLA_WIDTH), 0) // GLA_DK ==
                 lax.broadcasted_iota(jnp.int32, (hk, GLA_WIDTH), 1) // GLA_DV)
    n_blk = tb // S
    order = range(n_blk - 1, -1, -1) if reverse else range(n_blk)
    edge_row = 0 if reverse else S - 1
    edges = jnp.concatenate([cs[s * S + edge_row:s * S + edge_row + 1] for s in range(n_blk)], axis=0)
    dec_cols = jnp.exp(edges).T
    upd = {}
    for s in order:
        rows = slice(s * S, (s + 1) * S)
        kt = (k[rows] * jnp.exp(edges[s:s + 1] - cs[rows])).astype(jnp.bfloat16)
        upd[s] = jnp.where(blockdiag, lax.dot_general(kt, v[rows].astype(jnp.bfloat16), _TN,
                                                      preferred_element_type=jnp.float32), 0.0)
        yield
    st = st_ref[...]
    for s in order:
        rows = slice(s * S, (s + 1) * S)
        o_ref[rows, :] = acc[s] + jnp.dot(qd[rows], st.astype(jnp.bfloat16), preferred_element_type=jnp.float32)
        st = st * dec_cols[:, s:s + 1] + upd[s]
        yield
    st_ref[...] = st


DN_HALO = 8
DN_BETA_LANE = 2 * GLA_LOWRANK
DN_A_LANE = DN_BETA_LANE + 2 * DN_HEADS


def _group_indicator(n, group):
    r = lax.broadcasted_iota(jnp.int32, (n, n), 0) // group
    c = lax.broadcasted_iota(jnp.int32, (n, n), 1) // group
    return (r == c).astype(jnp.bfloat16)


def _mm(a, b, dims=None):
    a, b = a.astype(jnp.bfloat16), b.astype(jnp.bfloat16)
    if dims is None:
        return jnp.dot(a, b, preferred_element_type=jnp.float32)
    return lax.dot_general(a, b, dims, preferred_element_type=jnp.float32)


def _head_blockdiag(x):
    c, hw = x.shape
    w = hw // DN_HEADS
    stacked = jnp.concatenate([x] * DN_HEADS, axis=0)
    r = lax.broadcasted_iota(jnp.int32, stacked.shape, 0) // c
    l = lax.broadcasted_iota(jnp.int32, stacked.shape, 1) // w
    return jnp.where(r == l, stacked, 0.0)


def _dn_chunk_prepare(qkv, gate, direction):
    C, W = DN_CHUNK, DN_HEADS * DN_DK
    reverse = direction == 1
    q, k, v = qkv[:, :W], qkv[:, W:2 * W], qkv[:, 2 * W:]
    g_lo = DN_A_LANE + direction * DN_HEADS
    b_lo = DN_BETA_LANE + direction * DN_HEADS
    cs = _seg_cumsum(gate, C, reverse)
    beta = gate
    head_of_lane = lax.broadcasted_iota(jnp.int32, (C, W), 1) // DN_DK

    def expand(x, lo):
        out = jnp.broadcast_to(x[:, lo:lo + 1], (C, W))
        for h in range(1, DN_HEADS):
            out = jnp.where(head_of_lane == h, jnp.broadcast_to(x[:, lo + h:lo + h + 1], (C, W)), out)
        return out

    bcol, bexp = expand(cs, g_lo), expand(beta, b_lo)
    yield
    row = lax.broadcasted_iota(jnp.int32, (C, W), 0)
    col = lax.broadcasted_iota(jnp.int32, (C, W), 1) % C
    cs_t = jnp.concatenate([cs, jnp.zeros_like(cs)], axis=0).T
    lane_t = lax.broadcasted_iota(jnp.int32, (1, 128), 1)
    halves = []
    for pair in range(DN_HEADS // 2):
        even = cs_t[g_lo + 2 * pair:g_lo + 2 * pair + 1, :]
        odd = pltpu.roll(cs_t[g_lo + 2 * pair + 1:g_lo + 2 * pair + 2, :], C, 1)
        halves.append(jnp.where(lane_t < C, even, odd))
    brow = jnp.broadcast_to(jnp.concatenate(halves, axis=1), (C, W))
    incl = (col >= row) if reverse else (col <= row)
    strict = (col > row) if reverse else (col < row)
    lmat = jnp.exp(jnp.where(incl, bcol - brow, -jnp.inf))
    yield
    kb = k * bexp
    kk = _mm(jnp.concatenate([kb, q], axis=0), _head_blockdiag(k), _NT)
    m = jnp.where(strict, kk[:C] * lmat, 0.0)
    aqk = jnp.where(incl, kk[C:] * lmat, 0.0)
    yield
    t = jnp.where(row == col, 1.0, 0.0)
    s = 1
    while s < C:
        same = (row // (2 * s)) == (col // (2 * s))
        r_hi, c_hi = (row % (2 * s)) >= s, (col % (2 * s)) >= s
        off = same & ((c_hi & ~r_hi) if reverse else (r_hi & ~c_hi))
        m_off = jnp.where(off, m, 0.0)
        if s == 1:
            t = t - m_off
        else:
            x = _mm(m_off, _head_blockdiag(t))
            yield
            t = t - _mm(t, _head_blockdiag(x))
            yield
        s *= 2
    ecs = jnp.exp(bcol)
    u = _mm(t, _head_blockdiag(v * bexp))
    w = _mm(t, _head_blockdiag(kb * ecs))
    yield
    edge = bcol[0:1] if reverse else bcol[C - 1:C]
    return dict(u=u, wq=jnp.concatenate([w, q * ecs], axis=0), aqk=aqk, k_dec=k * jnp.exp(edge - bcol),
                dl=jnp.exp(edge))


def _lockstep_stages(generators):
    results = [None] * len(generators)
    live = list(range(len(generators)))
    while live:
        still = []
        for i in live:
            try:
                next(generators[i])
                still.append(i)
            except StopIteration as stop:
                results[i] = stop.value
        live = still
        yield
    return results


def _lockstep(generators):
    stages = _lockstep_stages(generators)
    while True:
        try:
            next(stages)
        except StopIteration as stop:
            return stop.value


def _dn_state_chain(prepared, s_ref, o_ref, reverse):
    C, W = DN_CHUNK, DN_HEADS * DN_DK
    n_chunks = len(prepared)
    bd = (lax.broadcasted_iota(jnp.int32, (W, W), 0) // DN_DK ==
          lax.broadcasted_iota(jnp.int32, (W, W), 1) // DN_DV)
    s = s_ref[...]
    for c in (range(n_chunks - 1, -1, -1) if reverse else range(n_chunks)):
        pc = prepared[c]
        both = _mm(pc["wq"], s)
        v_new = pc["u"] - both[:C]
        yield
        o_ref[c * C:(c + 1) * C, :] = both[C:] + _mm(pc["aqk"], _head_blockdiag(v_new))
        s = s * pc["dl"] + jnp.where(bd, _mm(pc["k_dec"], v_new, _TN), 0.0)
        yield
    s_ref[...] = s


def _dn_tile(qkvf_ref, gf_ref, qkvb_ref, gb_ref, of_ref, ob_ref, sf_ref, sb_ref):
    C = DN_CHUNK
    n_chunks = qkvf_ref.shape[0] // C
    rows = lambda c: slice(c * C, (c + 1) * C)
    prepared = yield from _lockstep_stages(
        [_dn_chunk_prepare(qkvf_ref[rows(c), :], gf_ref[rows(c), :], 0) for c in range(n_chunks)] +
        [_dn_chunk_prepare(qkvb_ref[rows(c), :], gb_ref[rows(c), :], 1) for c in range(n_chunks)])
    yield from _lockstep_stages([_dn_state_chain(prepared[:n_chunks], sf_ref, of_ref, False),
                                 _dn_state_chain(prepared[n_chunks:], sb_ref, ob_ref, True)])


def _seq_mixers_kernel(qkf_ref, vf_ref, smf_ref, qkb_ref, vb_ref, smb_ref, wg_ref, bg_ref, tri_ref, place_ref,
                       dqkvf_ref, dgf_ref, dqkvb_ref, dgb_ref,
                       gof_ref, gob_ref, dof_ref, dob_ref, stf_ref, stb_ref, sf_ref, sb_ref):
    @pl.when(pl.program_id(1) == 0)
    def _():
        for ref in (stf_ref, stb_ref, sf_ref, sb_ref):
            ref[...] = jnp.zeros_like(ref)

    _lockstep([_dn_tile(dqkvf_ref, dgf_ref, dqkvb_ref, dgb_ref, dof_ref, dob_ref, sf_ref, sb_ref),
               _gla_direction(qkf_ref, vf_ref, smf_ref, wg_ref.at[0], bg_ref.at[0], tri_ref.at[0], place_ref.at[0],
                              stf_ref, gof_ref, False),
               _gla_direction(qkb_ref, vb_ref, smb_ref, wg_ref.at[1], bg_ref.at[1], tri_ref.at[1], place_ref.at[1],
                              stb_ref, gob_ref, True)])


def seq_mixers(gla_qk, gla_v, small, w_lr2, b_lr, dn_qkvn, dn_gates, batch, seq):
    hk = GLA_HEADS * GLA_DK
    W = DN_HEADS * DN_DK
    tb = min(GLA_TILE, seq)
    nt = seq // tb
    wg = jnp.zeros((2, 128, hk), jnp.float32)
    for z in range(2):
        wg = wg.at[z, z * GLA_LOWRANK:(z + 1) * GLA_LOWRANK].set(w_lr2[z])
    tri, place = _gla_constants()
    fwd = lambda w: pl.BlockSpec((tb, w), lambda b, i: (b * nt + i, 0))
    bwd = lambda w: pl.BlockSpec((tb, w), lambda b, i: (b * nt + nt - 1 - i, 0))
    const = lambda a: pl.BlockSpec(a.shape, lambda b, i: (0,) * a.ndim)
    out = jax.ShapeDtypeStruct((batch * seq, GLA_WIDTH), jnp.float32)
    bg = b_lr.reshape(2, 1, hk).astype(jnp.float32)
    wg = wg.astype(jnp.bfloat16)
    return pl.pallas_call(
        _seq_mixers_kernel,
        grid=(batch, nt),
        in_specs=[fwd(2 * hk), fwd(GLA_WIDTH), fwd(128), bwd(2 * hk), bwd(GLA_WIDTH), bwd(128),
                  const(wg), const(bg), const(tri), const(place),
                  fwd(DN_QKV), fwd(128), bwd(DN_QKV), bwd(128)],
        out_specs=[fwd(GLA_WIDTH), bwd(GLA_WIDTH), fwd(DN_WIDTH), bwd(DN_WIDTH)],
        out_shape=[out, out, out, out],
        scratch_shapes=[pltpu.VMEM((hk, GLA_WIDTH), jnp.float32), pltpu.VMEM((hk, GLA_WIDTH), jnp.float32),
                        pltpu.VMEM((W, DN_WIDTH), jnp.float32), pltpu.VMEM((W, DN_WIDTH), jnp.float32)],
        compiler_params=pltpu.CompilerParams(dimension_semantics=("arbitrary", "arbitrary"),
                                             vmem_limit_bytes=VMEM_LIMIT_BYTES),
        name="seq_mixers",
    )(gla_qk, gla_v, small, gla_qk, gla_v, small, wg, bg, tri, place, dn_qkvn, dn_gates, dn_qkvn, dn_gates)


def _gated_group_norm(o, gate, gn, ind):
    ms = jnp.dot((o * o).astype(jnp.bfloat16), ind, preferred_element_type=jnp.float32) * (1.0 / HEAD_DIM)
    return (o * lax.rsqrt(ms + EPS) * gn * (gate * (1.0 / (1.0 + jnp.exp(-gate))))).astype(jnp.bfloat16)


def _out_mlp_kernel(glaf_ref, glab_ref, glag_ref, swa_ref, dnf_ref, dnb_ref, dnz_ref, x_ref, wo_ref,
                    gng_ref, gnd_ref, g1_ref, g2_ref, w1_ref, w2_ref, g3_ref, o_ref):
    ind = _group_indicator(GLA_WIDTH, HEAD_DIM)
    y_gla = _gated_group_norm(glaf_ref[...] + glab_ref[...], glag_ref[...], gng_ref[...], ind)
    y_dn = _gated_group_norm(dnf_ref[...] + dnb_ref[...], dnz_ref[...], gnd_ref[...], ind)
    a, b = GLA_WIDTH, GLA_WIDTH + SWA_WIDTH
    m = (jnp.dot(y_gla, wo_ref[0:a, :], preferred_element_type=jnp.float32) +
         jnp.dot(swa_ref[...], wo_ref[a:b, :], preferred_element_type=jnp.float32) +
         jnp.dot(y_dn, wo_ref[b:, :], preferred_element_type=jnp.float32))
    x1 = x_ref[...] + _rms(m, g1_ref[...])
    h = _rms(x1, g2_ref[...]).astype(jnp.bfloat16)
    acc = jnp.zeros_like(x1)
    for c in range(D_FF // FF_CHUNK):
        f = jnp.dot(h, w1_ref[:, c * FF_CHUNK:(c + 1) * FF_CHUNK], preferred_element_type=jnp.float32)
        f = jnp.square(jnp.maximum(f, 0.0)).astype(jnp.bfloat16)
        acc = acc + jnp.dot(f, w2_ref[c * FF_CHUNK:(c + 1) * FF_CHUNK, :], preferred_element_type=jnp.float32)
    o_ref[...] = x1 + _rms(acc, g3_ref[...])


def out_mlp(gla_f, gla_b, gla_g, swa_o, dn_f, dn_b, dn_z, x2d, layer, wo, gla_norm_g, dn_norm_g, g1, g2, w1, w2, g3):
    n, d = x2d.shape
    tm = min(ROW_TILE, n)
    row = lambda w: pl.BlockSpec((tm, w), lambda i: (i, 0))
    vec = _const_spec((1, d))
    hvec = _const_spec((1, GLA_WIDTH))
    tile_gain = lambda g, h: jnp.tile(g.astype(jnp.float32), h).reshape(1, -1)
    return pl.pallas_call(
        _out_mlp_kernel,
        grid=(n // tm,),
        in_specs=[row(GLA_WIDTH), row(GLA_WIDTH), row(GLA_WIDTH), row(SWA_WIDTH), row(DN_WIDTH), row(DN_WIDTH),
                  row(DN_WIDTH), row(d), _layer_spec(layer, (d, d)), hvec, hvec, vec, vec,
                  _layer_spec(layer, (d, D_FF)), _layer_spec(layer, (D_FF, d)), vec],
        out_specs=row(d),
        out_shape=jax.ShapeDtypeStruct((n, d), jnp.float32),
        compiler_params=pltpu.CompilerParams(dimension_semantics=("arbitrary",),
                                             vmem_limit_bytes=VMEM_LIMIT_BYTES),
        name="out_mlp",
    )(gla_f, gla_b, gla_g, swa_o, dn_f, dn_b, dn_z, x2d, wo, tile_gain(gla_norm_g, GLA_HEADS),
      tile_gain(dn_norm_g, DN_HEADS), g1.reshape(1, d), g2.reshape(1, d), w1, w2, g3.reshape(1, d))


def _trunk(x, p):
    B, T, D = x.shape
    bias = p["band_bias"]
    x2 = x.reshape(B * T, D)
    for l in range(DEPTH):
        dn_qkvn, small, swa_q, swa_kv, gla_qk, gla_v, gla_g, dn_z, dn_gates = in_proj(
            x2, p["mix_pre_g"][l], p["w_in_packed"], l, p["dn_conv_w"][l], p["dn_a_log"][l], p["dn_dt_bias"][l], T)
        o_swa = swa_attention(swa_q, swa_kv, p["swa_sink"][l], bias, B, T)
        gla_f, gla_b, dn_f, dn_b = seq_mixers(gla_qk, gla_v, small, p["gla_w_lr2"][l], p["gla_b_lr"][l],
                                              dn_qkvn, dn_gates, B, T)
        x2 = out_mlp(gla_f, gla_b, gla_g, o_swa, dn_f, dn_b, dn_z, x2, l, p["w_out_bf16"], p["gla_norm_g"][l],
                     p["dn_norm_g"][l], p["mix_post_g"][l], p["mlp_pre_g"][l],
                     p["mlp_w1_bf16"], p["mlp_w2_bf16"], p["mlp_post_g"][l])
    return x2.reshape(B, T, D)


def _prepare_params(rel_bias_table, mix_pre_g, w_in, gla_w_lr2, gla_b_lr, gla_norm_g, swa_sink, dn_conv_w, dn_a_log,
                    dn_dt_bias, dn_norm_g, w_out, mix_post_g, mlp_pre_g, mlp_w1, mlp_w2, mlp_post_g):
    return dict(mix_pre_g=mix_pre_g, gla_w_lr2=gla_w_lr2, gla_b_lr=gla_b_lr,
                gla_norm_g=gla_norm_g, swa_sink=swa_sink, dn_conv_w=dn_conv_w, dn_a_log=dn_a_log,
                dn_dt_bias=dn_dt_bias, dn_norm_g=dn_norm_g, mix_post_g=mix_post_g, mlp_pre_g=mlp_pre_g,
                mlp_post_g=mlp_post_g,
                band_bias=band_bias(rel_bias_table),
                w_in_packed=pack_w_in(w_in), w_out_bf16=w_out.astype(jnp.bfloat16),
                mlp_w1_bf16=mlp_w1.astype(jnp.bfloat16), mlp_w2_bf16=mlp_w2.astype(jnp.bfloat16))


def kernel(x_prompt, x_sample, rel_bias_table, mix_pre_g, w_in, gla_w_lr2, gla_b_lr, gla_norm_g, swa_sink, dn_conv_w, dn_a_log, dn_dt_bias, dn_norm_g, w_out, mix_post_g, mlp_pre_g, mlp_w1, mlp_w2, mlp_post_g):
    p = _prepare_params(rel_bias_table, mix_pre_g, w_in, gla_w_lr2, gla_b_lr, gla_norm_g, swa_sink, dn_conv_w,
                        dn_a_log, dn_dt_bias, dn_norm_g, w_out, mix_post_g, mlp_pre_g, mlp_w1, mlp_w2, mlp_post_g)
    return (_trunk(x_prompt, p), _trunk(x_sample, p))
```

```python
import functools
import math

import numpy as np

import jax
import jax.numpy as jnp
from jax import lax
from jax.experimental import pallas as pl
from jax.experimental.pallas import tpu as pltpu

D_MODEL = 1024
DEPTH = 4
HEAD_DIM = 64
EPS = 1e-6
GLA_HEADS = 4
GLA_DK = 32
GLA_DV = 64
GLA_LOWRANK = 16
GLA_GATE_NORMALIZER = 16.0
SWA_HEADS = 8
SWA_KV_HEADS = 2
SWA_WINDOW = 128
SWA_BLOCK = 128
NUM_BUCKETS = 32
MAX_DISTANCE = 128
DN_HEADS = 4
DN_DK = 64
DN_DV = 64
DN_CONV = 5
DN_CHUNK = 64
GLA_WIDTH = GLA_HEADS * GLA_DV
SWA_WIDTH = SWA_HEADS * HEAD_DIM
DN_WIDTH = DN_HEADS * DN_DV
D_FF = 4 * D_MODEL
DN_QKV = DN_HEADS * (2 * DN_DK + DN_DV)
IN_SIZES = (GLA_HEADS * GLA_DK, GLA_HEADS * GLA_DK, GLA_WIDTH, GLA_WIDTH, 2 * GLA_LOWRANK,
            SWA_WIDTH, SWA_KV_HEADS * HEAD_DIM, SWA_KV_HEADS * HEAD_DIM,
            DN_QKV, DN_WIDTH, 2 * DN_HEADS, 2 * DN_HEADS)
D_IN = sum(IN_SIZES)

VMEM_LIMIT_BYTES = 56 * 1024 * 1024
ROW_TILE = 512
IN_ROW_TILE = 1024
FF_CHUNK = 1024


def _rms(xf, g):
    return xf * lax.rsqrt(jnp.mean(xf * xf, axis=-1, keepdims=True) + EPS) * g


def _const_spec(shape):
    return pl.BlockSpec(shape, lambda i: (0,) * len(shape), pipeline_mode=pl.Buffered(1))


def _layer_spec(layer, shape):
    return pl.BlockSpec((None,) + shape, lambda i: (layer,) + (0,) * len(shape), pipeline_mode=pl.Buffered(1))


_O = {}
_acc = 0
for _name, _width in (("dn_qkv", DN_QKV), ("small", 128), ("swa_q", SWA_WIDTH), ("swa_kv", 4 * 2 * HEAD_DIM),
                      ("gla_qk", 2 * GLA_HEADS * GLA_DK), ("gla_v", GLA_WIDTH), ("gla_g", GLA_WIDTH),
                      ("dn_z", DN_WIDTH)):
    _O[_name] = (_acc, _width)
    _acc += _width
D_IN_PACKED = _acc
D_IN_LEAD = _O["small"][0] + _O["small"][1]
_OUT_DTYPES = {"swa_q": jnp.bfloat16, "swa_kv": jnp.bfloat16}


PACK_ROWS = 256


def _pack_w_in_kernel(w_ref, o_ref):
    w = w_ref[0]
    offs, acc = [], 0
    for s in IN_SIZES:
        offs.append(acc)
        acc += s
    gq, gk, gv, gg, glr, sq, sk, sv, dqkv, dz, dbeta, da = [w[:, o:o + s] for o, s in zip(offs, IN_SIZES)]
    hd = HEAD_DIM
    kv = [sk[:, 0:hd], sk[:, 0:hd], sk[:, hd:], sk[:, hd:], sv[:, 0:hd], sv[:, 0:hd], sv[:, hd:], sv[:, hd:]]
    used = glr.shape[1] + dbeta.shape[1] + da.shape[1]
    small = jnp.concatenate([glr, dbeta, da, jnp.zeros((w.shape[0], 128 - used), w.dtype)], axis=1)
    pieces = dict(dn_qkv=dqkv, small=small, swa_q=sq, swa_kv=jnp.concatenate(kv, axis=1),
                  gla_qk=jnp.concatenate([gq, gk], axis=1), gla_v=gv, gla_g=gg, dn_z=dz)
    for name, (off, width) in _O.items():
        o_ref[0, :, off:off + width] = pieces[name].astype(o_ref.dtype)


def pack_w_in(w_in):
    depth, d, d_in = w_in.shape
    return pl.pallas_call(
        _pack_w_in_kernel,
        grid=(depth, d // PACK_ROWS),
        in_specs=[pl.BlockSpec((1, PACK_ROWS, d_in), lambda l, i: (l, i, 0))],
        out_specs=pl.BlockSpec((1, PACK_ROWS, D_IN_PACKED), lambda l, i: (l, i, 0)),
        out_shape=jax.ShapeDtypeStruct((depth, d, D_IN_PACKED), jnp.bfloat16),
        compiler_params=pltpu.CompilerParams(dimension_semantics=("arbitrary", "arbitrary"),
                                             vmem_limit_bytes=VMEM_LIMIT_BYTES),
        name="pack_w_in",
    )(w_in)


def _in_proj_kernel(tiles_per_seq, xp_ref, xc_ref, xn_ref, g_ref, w_ref, cw_ref, na_ref, dt_ref, *refs):
    o_refs, gate_ref, ext_ref = refs[:len(_O)], refs[len(_O)], refs[len(_O) + 1]
    tm, H = xc_ref.shape[0], DN_HALO
    i = pl.program_id(0)
    first = (i % tiles_per_seq) == 0
    last = (i % tiles_per_seq) == tiles_per_seq - 1
    x = jnp.concatenate([xp_ref[...], xc_ref[...], xn_ref[...]], axis=0)
    h = _rms(x, g_ref[...])
    h_tile = h[H:H + tm].astype(jnp.bfloat16)
    out = dict(zip(_O, o_refs))
    y = jnp.dot(h.astype(jnp.bfloat16), w_ref[:, :D_IN_LEAD], preferred_element_type=jnp.float32)
    off, width = _O["dn_qkv"]
    ext_ref[0:H, :] = jnp.where(first, 0.0, y[0:H, off:off + width])
    ext_ref[H:H + tm, :] = y[H:H + tm, off:off + width]
    ext_ref[H + tm:, :] = jnp.where(last, 0.0, y[H + tm:, off:off + width])
    half = DN_CONV // 2
    w = DN_HEADS * DN_DK
    ind = _group_indicator(w, DN_DK)
    sumsq = lambda a: jnp.dot((a * a).astype(jnp.bfloat16), ind, preferred_element_type=jnp.float32)

    def project(names):
        lo, hi = _O[names[0]][0], _O[names[-1]][0] + _O[names[-1]][1]
        part = jnp.dot(h_tile, w_ref[:, lo:hi], preferred_element_type=jnp.float32)
        for name in names:
            o, wd = _O[name]
            out[name][...] = part[:, o - lo:o - lo + wd].astype(out[name].dtype)

    def deltanet_rows(r0, r1):
        acc = ext_ref[H - half + r0:H - half + r1, :] * cw_ref[0:1, :]
        for j in range(1, DN_CONV):
            acc = acc + ext_ref[H - half + j + r0:H - half + j + r1, :] * cw_ref[j:j + 1, :]
        c = acc * (1.0 / (1.0 + jnp.exp(-acc)))
        q, k = c[:, :w], c[:, w:2 * w]
        out["dn_qkv"][r0:r1, :w] = q * lax.rsqrt(sumsq(q) + EPS) * (DN_DK ** -0.5)
        out["dn_qkv"][r0:r1, w:2 * w] = k * lax.rsqrt(sumsq(k) + EPS)
        out["dn_qkv"][r0:r1, 2 * w:] = c[:, 2 * w:]

    groups = (["swa_q", "swa_kv"], ["gla_qk", "gla_v", "gla_g", "dn_z"])
    for n, names in enumerate(groups):
        project(names)
        deltanet_rows(n * tm // len(groups), (n + 1) * tm // len(groups))
    off, width = _O["small"]
    s = y[H:H + tm, off:off + width]
    out["small"][...] = s
    lane = lax.broadcasted_iota(jnp.int32, s.shape, 1)
    beta = 1.0 / (1.0 + jnp.exp(-s))
    a = s + dt_ref[...]
    gdec = na_ref[...] * (jnp.maximum(a, 0.0) + jnp.log1p(jnp.exp(-jnp.abs(a))))
    gate_ref[...] = jnp.where((lane >= DN_BETA_LANE) & (lane < DN_A_LANE), beta,
                              jnp.where((lane >= DN_A_LANE) & (lane < DN_A_LANE + 2 * DN_HEADS), gdec, 0.0))


def in_proj(x2d, g, w_packed, layer, conv_w, a_log, dt_bias, seq):
    n, d = x2d.shape
    tm = min(IN_ROW_TILE, seq)
    hb = tm // DN_HALO
    nh = n // DN_HALO
    lanes = lambda v: jnp.zeros((1, 128), jnp.float32).at[0, DN_A_LANE:DN_A_LANE + 2 * DN_HEADS].set(v.reshape(-1))
    row = lambda w: pl.BlockSpec((tm, w), lambda i: (i, 0))
    return pl.pallas_call(
        functools.partial(_in_proj_kernel, seq // tm),
        grid=(n // tm,),
        in_specs=[pl.BlockSpec((DN_HALO, d), lambda i: (jnp.maximum(i * hb - 1, 0), 0)),
                  row(d),
                  pl.BlockSpec((DN_HALO, d), lambda i: (jnp.minimum((i + 1) * hb, nh - 1), 0)),
                  _const_spec((1, d)),
                  _layer_spec(layer, (d, D_IN_PACKED)),
                  _const_spec((DN_CONV, DN_QKV)), _const_spec((1, 128)), _const_spec((1, 128))],
        out_specs=[row(w) for _, w in _O.values()] + [row(128)],
        out_shape=[jax.ShapeDtypeStruct((n, w), _OUT_DTYPES.get(name, jnp.float32))
                   for name, (_, w) in _O.items()] + [jax.ShapeDtypeStruct((n, 128), jnp.float32)],
        scratch_shapes=[pltpu.VMEM((tm + 2 * DN_HALO, DN_QKV), jnp.float32)],
        compiler_params=pltpu.CompilerParams(dimension_semantics=("arbitrary",),
                                             vmem_limit_bytes=VMEM_LIMIT_BYTES),
        name="in_proj",
    )(x2d, x2d, x2d, g.reshape(1, d), w_packed, conv_w.astype(jnp.float32),
      lanes(-jnp.exp(a_log.astype(jnp.float32))), lanes(dt_bias.astype(jnp.float32)))


SWA_Q_TILE = 1024
SWA_GROUP = SWA_HEADS // SWA_KV_HEADS


def _t5_bucket(rel):
    nb = NUM_BUCKETS // 2
    max_exact = nb // 2
    base = jnp.where(rel > 0, nb, 0)
    n = jnp.abs(rel)
    nf = jnp.maximum(n, 1).astype(jnp.float32)
    large = max_exact + (jnp.log(nf / max_exact) / math.log(MAX_DISTANCE / max_exact)
                         * (nb - max_exact)).astype(jnp.int32)
    large = jnp.minimum(large, nb - 1)
    return base + jnp.where(n < max_exact, n, large)


def _band_bias_kernel(tab_ref, idx_ref, o_ref):
    W = SWA_BLOCK
    idx = idx_ref[...]
    i = lax.broadcasted_iota(jnp.int32, idx.shape, 0)
    j = lax.broadcasted_iota(jnp.int32, idx.shape, 1)
    in_band = jnp.abs(j - W - i) <= SWA_WINDOW
    for h in range(SWA_HEADS):
        acc = jnp.zeros(idx.shape, jnp.float32)
        for b in range(NUM_BUCKETS):
            acc = jnp.where(idx == b, tab_ref[b, h], acc)
        o_ref[h] = jnp.where(in_band, acc, -jnp.inf)


def band_bias(table):
    W = SWA_BLOCK
    rel = jnp.arange(3 * W)[None, :] - W - jnp.arange(W)[:, None]
    idx = _t5_bucket(rel).astype(jnp.int32)
    return pl.pallas_call(
        _band_bias_kernel,
        in_specs=[pl.BlockSpec(memory_space=pltpu.SMEM), pl.BlockSpec(memory_space=pltpu.VMEM)],
        out_specs=pl.BlockSpec(memory_space=pltpu.VMEM),
        out_shape=jax.ShapeDtypeStruct((SWA_HEADS, W, 3 * W), jnp.float32),
        name="band_bias",
    )(table.astype(jnp.float32), idx)


def _swa_kernel(sink_ref, q_ref, kvp_ref, kvc_ref, kvn_ref, bias_ref, o_ref, kv_buf):
    W = SWA_BLOCK
    n_sub = q_ref.shape[0] // W
    n = pl.program_id(1)
    last = pl.num_programs(1) - 1
    kv_buf[0:W, :] = kvp_ref[...]
    kv_buf[W:W + n_sub * W, :] = kvc_ref[...]
    kv_buf[W + n_sub * W:, :] = kvn_ref[...]
    lane = lax.broadcasted_iota(jnp.int32, (W, 2 * HEAD_DIM), 1)
    lo = lane < HEAD_DIM
    lo3 = (lax.broadcasted_iota(jnp.int32, (3 * W, 2 * HEAD_DIM), 1) < HEAD_DIM).astype(jnp.float32).astype(jnp.bfloat16)
    hi3 = 1 - lo3
    key_blk = lax.broadcasted_iota(jnp.int32, (1, 3 * W), 1) // W
    neg = jnp.float32(-jnp.inf)
    for s in range(n_sub):
        edge = jnp.zeros((1, 3 * W), jnp.float32)
        if s == 0:
            edge = jnp.where((key_blk == 0) & (n == 0), neg, edge)
        if s == n_sub - 1:
            edge = jnp.where((key_blk == 2) & (n == last), neg, edge)
        rows = slice(s * W, (s + 1) * W)
        for g in range(SWA_KV_HEADS):
            kd = kv_buf[s * W:(s + 3) * W, g * 128:(g + 1) * 128]
            vd = kv_buf[s * W:(s + 3) * W, 256 + g * 128:256 + (g + 1) * 128]
            k2 = jnp.concatenate([kd * lo3, kd * hi3], axis=0)
            v2 = jnp.concatenate([jnp.concatenate([vd * lo3, lo3], axis=1),
                                  jnp.concatenate([vd * hi3, hi3], axis=1)], axis=0)
            for pair in range(SWA_GROUP // 2):
                col = (g * SWA_GROUP // 2 + pair) * 128
                qp = q_ref[rows, col:col + 128] * (HEAD_DIM ** -0.5)
                s2 = lax.dot_general(qp, k2, (((1,), (1,)), ((), ())),
                                     preferred_element_type=jnp.float32)
                ps, sink_terms = [], []
                for par in range(2):
                    h = g * SWA_GROUP + 2 * pair + par
                    sc = s2[:, par * 3 * W:(par + 1) * 3 * W] + bias_ref[h]
                    if s == 0 or s == n_sub - 1:
                        sc = sc + edge
                    sink = sink_ref[h]
                    m = jnp.maximum(jnp.max(sc, axis=-1, keepdims=True), sink)
                    ps.append(jnp.exp((sc - m).astype(jnp.bfloat16)))
                    sink_terms.append(jnp.exp(sink - m))
                o2 = jnp.dot(jnp.concatenate(ps, axis=1), v2, preferred_element_type=jnp.float32)
                denom = o2[:, 128:] + jnp.where(lo, sink_terms[0], sink_terms[1])
                o_ref[rows, col:col + 128] = (o2[:, :128] * (1.0 / denom)).astype(o_ref.dtype)


def swa_attention(q, kv, sink, bias, batch, seq):
    W = SWA_BLOCK
    tq = min(SWA_Q_TILE, seq)
    n_sub = tq // W
    nq = seq // tq
    nb = seq // W
    return pl.pallas_call(
        _swa_kernel,
        grid=(batch, nq),
        in_specs=[pl.BlockSpec(memory_space=pltpu.SMEM),
                  pl.BlockSpec((tq, SWA_WIDTH), lambda b, i: (b * nq + i, 0)),
                  pl.BlockSpec((W, 512), lambda b, i: (b * nb + jnp.maximum(i * n_sub - 1, 0), 0)),
                  pl.BlockSpec((tq, 512), lambda b, i: (b * nq + i, 0)),
                  pl.BlockSpec((W, 512), lambda b, i: (b * nb + jnp.minimum((i + 1) * n_sub, nb - 1), 0)),
                  pl.BlockSpec((SWA_HEADS, W, 3 * W), lambda b, i: (0, 0, 0), pipeline_mode=pl.Buffered(1))],
        out_specs=pl.BlockSpec((tq, SWA_WIDTH), lambda b, i: (b * nq + i, 0)),
        out_shape=jax.ShapeDtypeStruct((batch * seq, SWA_WIDTH), jnp.bfloat16),
        scratch_shapes=[pltpu.VMEM((tq + 2 * W, 512), jnp.bfloat16)],
        compiler_params=pltpu.CompilerParams(dimension_semantics=("arbitrary", "arbitrary"),
                                             vmem_limit_bytes=VMEM_LIMIT_BYTES),
        name="swa",
    )(sink.astype(jnp.float32), q, kv, kv, kv, bias)


GLA_TILE = 512
GLA_SUB = 16
GLA_CUMSUM_ROWS = 128
_NT = (((1,), (1,)), ((), ()))
_TN = (((0,), (0,)), ((), ()))


def _seg_cumsum(x, seg, reverse):
    n = x.shape[0]
    rowmod = lax.broadcasted_iota(jnp.int32, x.shape, 0) % seg
    sh = 1
    while sh < seg:
        if reverse:
            x = x + jnp.where(rowmod < seg - sh, pltpu.roll(x, n - sh, 0), 0.0)
        else:
            x = x + jnp.where(rowmod >= sh, pltpu.roll(x, sh, 0), 0.0)
        sh *= 2
    return x


def _log_sigmoid(x):
    return jnp.minimum(x, 0.0) - jnp.log1p(jnp.exp(-jnp.abs(x)))


def _gla_constants():
    S, hk = GLA_SUB, GLA_HEADS * GLA_DK
    r = np.arange(GLA_CUMSUM_ROWS)
    same = (r[:, None] // S) == (r[None, :] // S)
    tri = np.stack([same & (r[None, :] <= r[:, None]), same & (r[None, :] >= r[:, None])])
    src = np.arange(S * hk)
    d, h = src // hk, (src % hk) // GLA_DK
    lane = np.arange(128)
    place = np.stack([lane[None, :] == ((S * h - d) % 128)[:, None], lane[None, :] == (S * h + d)[:, None]])
    return jnp.asarray(tri, jnp.bfloat16), jnp.asarray(place, jnp.bfloat16)


def _gla_direction(qk_ref, v_ref, sm_ref, wg_ref, bg_ref, tri_ref, place_ref, st_ref, o_ref, reverse):
    tb, S = qk_ref.shape[0], GLA_SUB
    hk = GLA_HEADS * GLA_DK
    logits = jnp.dot(sm_ref[...].astype(jnp.bfloat16), wg_ref[...], preferred_element_type=jnp.float32)
    g = _log_sigmoid(logits + bg_ref[...]) * (1.0 / GLA_GATE_NORMALIZER)
    g_hi = g.astype(jnp.bfloat16)
    g_lo = (g - g_hi.astype(jnp.float32)).astype(jnp.bfloat16)
    R = GLA_CUMSUM_ROWS
    cs = jnp.concatenate([jnp.dot(tri_ref[...], g_hi[r:r + R], preferred_element_type=jnp.float32) +
                          jnp.dot(tri_ref[...], g_lo[r:r + R], preferred_element_type=jnp.float32)
                          for r in range(0, tb, R)], axis=0)
    yield
    a = jnp.exp(g)
    q = qk_ref[:, :hk] * (GLA_DK ** -0.5)
    k = qk_ref[:, hk:]
    v = v_ref[...]
    rowmod = lax.broadcasted_iota(jnp.int32, (tb, hk), 0) % S
    has_prev = (rowmod < S - 1) if reverse else (rowmod >= 1)
    kd = k
    ps = [(q * kd).astype(jnp.bfloat16)]
    for d in range(1, S):
        kd = a * jnp.where(has_prev, pltpu.roll(kd, tb - 1 if reverse else 1, 0), 0.0)
        ps.append((q * kd).astype(jnp.bfloat16))
    scores = jnp.dot(jnp.concatenate(ps, axis=1), place_ref[...], preferred_element_type=jnp.float32)
    yield
    head_of_lane = lax.broadcasted_iota(jnp.int32, (S, GLA_WIDTH), 1) // GLA_DV
    acc = {}
    for s in range(tb // S):
        rows = slice(s * S, (s + 1) * S)
        band = pltpu.roll(scores[rows], 0, 1, stride=1, stride_axis=0)[:, :GLA_HEADS * S].astype(jnp.bfloat16)
        v_blk = v[rows]
        v_heads = jnp.concatenate([jnp.where(head_of_lane == h, v_blk, 0.0) for h in range(GLA_HEADS)], axis=0)
        acc[s] = jnp.dot(band, v_heads.astype(jnp.bfloat16), preferred_element_type=jnp.float32)
        if s % 2 == 1:
            yield
    qd = (q * jnp.exp(cs)).astype(jnp.bfloat16)
    blockdiag = (lax.broadcasted_iota(jnp.int32, (hk, GLA_WIDTH), 0) // GLA_DK ==
                 lax.broadcasted_iota(jnp.int32, (hk, GLA_WIDTH), 1) // GLA_DV)
    n_blk = tb // S
    order = range(n_blk - 1, -1, -1) if reverse else range(n_blk)
    edge_row = 0 if reverse else S - 1
    edges = jnp.concatenate([cs[s * S + edge_row:s * S + edge_row + 1] for s in range(n_blk)], axis=0)
    dec_cols = jnp.exp(edges).T
    upd = {}
    for s in order:
        rows = slice(s * S, (s + 1) * S)
        kt = (k[rows] * jnp.exp(edges[s:s + 1] - cs[rows])).astype(jnp.bfloat16)
        upd[s] = jnp.where(blockdiag, lax.dot_general(kt, v[rows].astype(jnp.bfloat16), _TN,
                                                      preferred_element_type=jnp.float32), 0.0)
        yield
    st = st_ref[...]
    for s in order:
        rows = slice(s * S, (s + 1) * S)
        o_ref[rows, :] = acc[s] + jnp.dot(qd[rows], st.astype(jnp.bfloat16), preferred_element_type=jnp.float32)
        st = st * dec_cols[:, s:s + 1] + upd[s]
        yield
    st_ref[...] = st


DN_HALO = 8
DN_BETA_LANE = 2 * GLA_LOWRANK
DN_A_LANE = DN_BETA_LANE + 2 * DN_HEADS


def _group_indicator(n, group):
    r = lax.broadcasted_iota(jnp.int32, (n, n), 0) // group
    c = lax.broadcasted_iota(jnp.int32, (n, n), 1) // group
    return (r == c).astype(jnp.bfloat16)


def _mm(a, b, dims=None):
    a, b = a.astype(jnp.bfloat16), b.astype(jnp.bfloat16)
    if dims is None:
        return jnp.dot(a, b, preferred_element_type=jnp.float32)
    return lax.dot_general(a, b, dims, preferred_element_type=jnp.float32)


def _head_blockdiag(x):
    c, hw = x.shape
    w = hw // DN_HEADS
    stacked = jnp.concatenate([x] * DN_HEADS, axis=0)
    r = lax.broadcasted_iota(jnp.int32, stacked.shape, 0) // c
    l = lax.broadcasted_iota(jnp.int32, stacked.shape, 1) // w
    return jnp.where(r == l, stacked, 0.0)


def _dn_chunk_prepare(qkv, gate, direction):
    C, W = DN_CHUNK, DN_HEADS * DN_DK
    reverse = direction == 1
    q, k, v = qkv[:, :W], qkv[:, W:2 * W], qkv[:, 2 * W:]
    g_lo = DN_A_LANE + direction * DN_HEADS
    b_lo = DN_BETA_LANE + direction * DN_HEADS
    cs = _seg_cumsum(gate, C, reverse)
    beta = gate
    head_of_lane = lax.broadcasted_iota(jnp.int32, (C, W), 1) // DN_DK

    def expand(x, lo):
        out = jnp.broadcast_to(x[:, lo:lo + 1], (C, W))
        for h in range(1, DN_HEADS):
            out = jnp.where(head_of_lane == h, jnp.broadcast_to(x[:, lo + h:lo + h + 1], (C, W)), out)
        return out

    bcol, bexp = expand(cs, g_lo), expand(beta, b_lo)
    yield
    row = lax.broadcasted_iota(jnp.int32, (C, W), 0)
    col = lax.broadcasted_iota(jnp.int32, (C, W), 1) % C
    cs_t = jnp.concatenate([cs, jnp.zeros_like(cs)], axis=0).T
    lane_t = lax.broadcasted_iota(jnp.int32, (1, 128), 1)
    halves = []
    for pair in range(DN_HEADS // 2):
        even = cs_t[g_lo + 2 * pair:g_lo + 2 * pair + 1, :]
        odd = pltpu.roll(cs_t[g_lo + 2 * pair + 1:g_lo + 2 * pair + 2, :], C, 1)
        halves.append(jnp.where(lane_t < C, even, odd))
    brow = jnp.broadcast_to(jnp.concatenate(halves, axis=1), (C, W))
    incl = (col >= row) if reverse else (col <= row)
    strict = (col > row) if reverse else (col < row)
    lmat = jnp.exp(jnp.where(incl, bcol - brow, -jnp.inf))
    yield
    kb = k * bexp
    kk = _mm(jnp.concatenate([kb, q], axis=0), _head_blockdiag(k), _NT)
    m = jnp.where(strict, kk[:C] * lmat, 0.0)
    aqk = jnp.where(incl, kk[C:] * lmat, 0.0)
    yield
    t = jnp.where(row == col, 1.0, 0.0)
    s = 1
    while s < C:
        same = (row // (2 * s)) == (col // (2 * s))
        r_hi, c_hi = (row % (2 * s)) >= s, (col % (2 * s)) >= s
        off = same & ((c_hi & ~r_hi) if reverse else (r_hi & ~c_hi))
        m_off = jnp.where(off, m, 0.0)
        if s == 1:
            t = t - m_off
        else:
            x = _mm(m_off, _head_blockdiag(t))
            yield
            t = t - _mm(t, _head_blockdiag(x))
            yield
        s *= 2
    ecs = jnp.exp(bcol)
    u = _mm(t, _head_blockdiag(v * bexp))
    w = _mm(t, _head_blockdiag(kb * ecs))
    yield
    edge = bcol[0:1] if reverse else bcol[C - 1:C]
    return dict(u=u, wq=jnp.concatenate([w, q * ecs], axis=0), aqk=aqk, k_dec=k * jnp.exp(edge - bcol),
                dl=jnp.exp(edge))


def _lockstep_stages(generators):
    results = [None] * len(generators)
    live = list(range(len(generators)))
    while live:
        still = []
        for i in live:
            try:
                next(generators[i])
                still.append(i)
            except StopIteration as stop:
                results[i] = stop.value
        live = still
        yield
    return results


def _lockstep(generators):
    stages = _lockstep_stages(generators)
    while True:
        try:
            next(stages)
        except StopIteration as stop:
            return stop.value


def _dn_state_chain(prepared, s_ref, o_ref, reverse):
    C, W = DN_CHUNK, DN_HEADS * DN_DK
    n_chunks = len(prepared)
    bd = (lax.broadcasted_iota(jnp.int32, (W, W), 0) // DN_DK ==
          lax.broadcasted_iota(jnp.int32, (W, W), 1) // DN_DV)
    s = s_ref[...]
    for c in (range(n_chunks - 1, -1, -1) if reverse else range(n_chunks)):
        pc = prepared[c]
        both = _mm(pc["wq"], s)
        v_new = pc["u"] - both[:C]
        yield
        o_ref[c * C:(c + 1) * C, :] = both[C:] + _mm(pc["aqk"], _head_blockdiag(v_new))
        s = s * pc["dl"] + jnp.where(bd, _mm(pc["k_dec"], v_new, _TN), 0.0)
        yield
    s_ref[...] = s


def _dn_tile(qkvf_ref, gf_ref, qkvb_ref, gb_ref, of_ref, ob_ref, sf_ref, sb_ref):
    C = DN_CHUNK
    n_chunks = qkvf_ref.shape[0] // C
    rows = lambda c: slice(c * C, (c + 1) * C)
    prepared = yield from _lockstep_stages(
        [_dn_chunk_prepare(qkvf_ref[rows(c), :], gf_ref[rows(c), :], 0) for c in range(n_chunks)] +
        [_dn_chunk_prepare(qkvb_ref[rows(c), :], gb_ref[rows(c), :], 1) for c in range(n_chunks)])
    yield from _lockstep_stages([_dn_state_chain(prepared[:n_chunks], sf_ref, of_ref, False),
                                 _dn_state_chain(prepared[n_chunks:], sb_ref, ob_ref, True)])


def _seq_mixers_kernel(qkf_ref, vf_ref, smf_ref, qkb_ref, vb_ref, smb_ref, wg_ref, bg_ref, tri_ref, place_ref,
                       dqkvf_ref, dgf_ref, dqkvb_ref, dgb_ref,
                       gof_ref, gob_ref, dof_ref, dob_ref, stf_ref, stb_ref, sf_ref, sb_ref):
    @pl.when(pl.program_id(1) == 0)
    def _():
        for ref in (stf_ref, stb_ref, sf_ref, sb_ref):
            ref[...] = jnp.zeros_like(ref)

    _lockstep([_dn_tile(dqkvf_ref, dgf_ref, dqkvb_ref, dgb_ref, dof_ref, dob_ref, sf_ref, sb_ref),
               _gla_direction(qkf_ref, vf_ref, smf_ref, wg_ref.at[0], bg_ref.at[0], tri_ref.at[0], place_ref.at[0],
                              stf_ref, gof_ref, False),
               _gla_direction(qkb_ref, vb_ref, smb_ref, wg_ref.at[1], bg_ref.at[1], tri_ref.at[1], place_ref.at[1],
                              stb_ref, gob_ref, True)])


def seq_mixers(gla_qk, gla_v, small, w_lr2, b_lr, dn_qkvn, dn_gates, batch, seq):
    hk = GLA_HEADS * GLA_DK
    W = DN_HEADS * DN_DK
    tb = min(GLA_TILE, seq)
    nt = seq // tb
    wg = jnp.zeros((2, 128, hk), jnp.float32)
    for z in range(2):
        wg = wg.at[z, z * GLA_LOWRANK:(z + 1) * GLA_LOWRANK].set(w_lr2[z])
    tri, place = _gla_constants()
    fwd = lambda w: pl.BlockSpec((tb, w), lambda b, i: (b * nt + i, 0))
    bwd = lambda w: pl.BlockSpec((tb, w), lambda b, i: (b * nt + nt - 1 - i, 0))
    const = lambda a: pl.BlockSpec(a.shape, lambda b, i: (0,) * a.ndim)
    out = jax.ShapeDtypeStruct((batch * seq, GLA_WIDTH), jnp.float32)
    bg = b_lr.reshape(2, 1, hk).astype(jnp.float32)
    wg = wg.astype(jnp.bfloat16)
    return pl.pallas_call(
        _seq_mixers_kernel,
        grid=(batch, nt),
        in_specs=[fwd(2 * hk), fwd(GLA_WIDTH), fwd(128), bwd(2 * hk), bwd(GLA_WIDTH), bwd(128),
                  const(wg), const(bg), const(tri), const(place),
                  fwd(DN_QKV), fwd(128), bwd(DN_QKV), bwd(128)],
        out_specs=[fwd(GLA_WIDTH), bwd(GLA_WIDTH), fwd(DN_WIDTH), bwd(DN_WIDTH)],
        out_shape=[out, out, out, out],
        scratch_shapes=[pltpu.VMEM((hk, GLA_WIDTH), jnp.float32), pltpu.VMEM((hk, GLA_WIDTH), jnp.float32),
                        pltpu.VMEM((W, DN_WIDTH), jnp.float32), pltpu.VMEM((W, DN_WIDTH), jnp.float32)],
        compiler_params=pltpu.CompilerParams(dimension_semantics=("arbitrary", "arbitrary"),
                                             vmem_limit_bytes=VMEM_LIMIT_BYTES),
        name="seq_mixers",
    )(gla_qk, gla_v, small, gla_qk, gla_v, small, wg, bg, tri, place, dn_qkvn, dn_gates, dn_qkvn, dn_gates)


def _gated_group_norm(o, gate, gn, ind):
    ms = jnp.dot((o * o).astype(jnp.bfloat16), ind, preferred_element_type=jnp.float32) * (1.0 / HEAD_DIM)
    return (o * lax.rsqrt(ms + EPS) * gn * (gate * (1.0 / (1.0 + jnp.exp(-gate))))).astype(jnp.bfloat16)


def _out_mlp_kernel(glaf_ref, glab_ref, glag_ref, swa_ref, dnf_ref, dnb_ref, dnz_ref, x_ref, wo_ref,
                    gng_ref, gnd_ref, g1_ref, g2_ref, w1_ref, w2_ref, g3_ref, o_ref):
    ind = _group_indicator(GLA_WIDTH, HEAD_DIM)
    y_gla = _gated_group_norm(glaf_ref[...] + glab_ref[...], glag_ref[...], gng_ref[...], ind)
    y_dn = _gated_group_norm(dnf_ref[...] + dnb_ref[...], dnz_ref[...], gnd_ref[...], ind)
    a, b = GLA_WIDTH, GLA_WIDTH + SWA_WIDTH
    m = (jnp.dot(y_gla, wo_ref[0:a, :], preferred_element_type=jnp.float32) +
         jnp.dot(swa_ref[...], wo_ref[a:b, :], preferred_element_type=jnp.float32) +
         jnp.dot(y_dn, wo_ref[b:, :], preferred_element_type=jnp.float32))
    x1 = x_ref[...] + _rms(m, g1_ref[...])
    h = _rms(x1, g2_ref[...]).astype(jnp.bfloat16)
    acc = jnp.zeros_like(x1)
    for c in range(D_FF // FF_CHUNK):
        f = jnp.dot(h, w1_ref[:, c * FF_CHUNK:(c + 1) * FF_CHUNK], preferred_element_type=jnp.float32)
        f = jnp.square(jnp.maximum(f, 0.0)).astype(jnp.bfloat16)
        acc = acc + jnp.dot(f, w2_ref[c * FF_CHUNK:(c + 1) * FF_CHUNK, :], preferred_element_type=jnp.float32)
    o_ref[...] = x1 + _rms(acc, g3_ref[...])


def out_mlp(gla_f, gla_b, gla_g, swa_o, dn_f, dn_b, dn_z, x2d, layer, wo, gla_norm_g, dn_norm_g, g1, g2, w1, w2, g3):
    n, d = x2d.shape
    tm = min(ROW_TILE, n)
    row = lambda w: pl.BlockSpec((tm, w), lambda i: (i, 0))
    vec = _const_spec((1, d))
    hvec = _const_spec((1, GLA_WIDTH))
    tile_gain = lambda g, h: jnp.tile(g.astype(jnp.float32), h).reshape(1, -1)
    return pl.pallas_call(
        _out_mlp_kernel,
        grid=(n // tm,),
        in_specs=[row(GLA_WIDTH), row(GLA_WIDTH), row(GLA_WIDTH), row(SWA_WIDTH), row(DN_WIDTH), row(DN_WIDTH),
                  row(DN_WIDTH), row(d), _layer_spec(layer, (d, d)), hvec, hvec, vec, vec,
                  _layer_spec(layer, (d, D_FF)), _layer_spec(layer, (D_FF, d)), vec],
        out_specs=row(d),
        out_shape=jax.ShapeDtypeStruct((n, d), jnp.float32),
        compiler_params=pltpu.CompilerParams(dimension_semantics=("arbitrary",),
                                             vmem_limit_bytes=VMEM_LIMIT_BYTES),
        name="out_mlp",
    )(gla_f, gla_b, gla_g, swa_o, dn_f, dn_b, dn_z, x2d, wo, tile_gain(gla_norm_g, GLA_HEADS),
      tile_gain(dn_norm_g, DN_HEADS), g1.reshape(1, d), g2.reshape(1, d), w1, w2, g3.reshape(1, d))


def _trunk(x, p):
    B, T, D = x.shape
    bias = p["band_bias"]
    x2 = x.reshape(B * T, D)
    for l in range(DEPTH):
        dn_qkvn, small, swa_q, swa_kv, gla_qk, gla_v, gla_g, dn_z, dn_gates = in_proj(
            x2, p["mix_pre_g"][l], p["w_in_packed"], l, p["dn_conv_w"][l], p["dn_a_log"][l], p["dn_dt_bias"][l], T)
        o_swa = swa_attention(swa_q, swa_kv, p["swa_sink"][l], bias, B, T)
        gla_f, gla_b, dn_f, dn_b = seq_mixers(gla_qk, gla_v, small, p["gla_w_lr2"][l], p["gla_b_lr"][l],
                                              dn_qkvn, dn_gates, B, T)
        x2 = out_mlp(gla_f, gla_b, gla_g, o_swa, dn_f, dn_b, dn_z, x2, l, p["w_out_bf16"], p["gla_norm_g"][l],
                     p["dn_norm_g"][l], p["mix_post_g"][l], p["mlp_pre_g"][l],
                     p["mlp_w1_bf16"], p["mlp_w2_bf16"], p["mlp_post_g"][l])
    return x2.reshape(B, T, D)


def _prepare_params(rel_bias_table, mix_pre_g, w_in, gla_w_lr2, gla_b_lr, gla_norm_g, swa_sink, dn_conv_w, dn_a_log,
                    dn_dt_bias, dn_norm_g, w_out, mix_post_g, mlp_pre_g, mlp_w1, mlp_w2, mlp_post_g):
    return dict(mix_pre_g=mix_pre_g, gla_w_lr2=gla_w_lr2, gla_b_lr=gla_b_lr,
                gla_norm_g=gla_norm_g, swa_sink=swa_sink, dn_conv_w=dn_conv_w, dn_a_log=dn_a_log,
                dn_dt_bias=dn_dt_bias, dn_norm_g=dn_norm_g, mix_post_g=mix_post_g, mlp_pre_g=mlp_pre_g,
                mlp_post_g=mlp_post_g,
                band_bias=band_bias(rel_bias_table),
                w_in_packed=pack_w_in(w_in), w_out_bf16=w_out.astype(jnp.bfloat16),
                mlp_w1_bf16=mlp_w1.astype(jnp.bfloat16), mlp_w2_bf16=mlp_w2.astype(jnp.bfloat16))


def kernel(x_prompt, x_sample, rel_bias_table, mix_pre_g, w_in, gla_w_lr2, gla_b_lr, gla_norm_g, swa_sink, dn_conv_w, dn_a_log, dn_dt_bias, dn_norm_g, w_out, mix_post_g, mlp_pre_g, mlp_w1, mlp_w2, mlp_post_g):
    p = _prepare_params(rel_bias_table, mix_pre_g, w_in, gla_w_lr2, gla_b_lr, gla_norm_g, swa_sink, dn_conv_w,
                        dn_a_log, dn_dt_bias, dn_norm_g, w_out, mix_post_g, mlp_pre_g, mlp_w1, mlp_w2, mlp_post_g)
    return (_trunk(x_prompt, p), _trunk(x_sample, p))
```

```python
import functools
import math

import numpy as np

import jax
import jax.numpy as jnp
from jax import lax
from jax.experimental import pallas as pl
from jax.experimental.pallas import tpu as pltpu

D_MODEL = 1024
DEPTH = 4
HEAD_DIM = 64
EPS = 1e-6
GLA_HEADS = 4
GLA_DK = 32
GLA_DV = 64
GLA_LOWRANK = 16
GLA_GATE_NORMALIZER = 16.0
SWA_HEADS = 8
SWA_KV_HEADS = 2
SWA_WINDOW = 128
SWA_BLOCK = 128
NUM_BUCKETS = 32
MAX_DISTANCE = 128
DN_HEADS = 4
DN_DK = 64
DN_DV = 64
DN_CONV = 5
DN_CHUNK = 64
GLA_WIDTH = GLA_HEADS * GLA_DV
SWA_WIDTH = SWA_HEADS * HEAD_DIM
DN_WIDTH = DN_HEADS * DN_DV
D_FF = 4 * D_MODEL
DN_QKV = DN_HEADS * (2 * DN_DK + DN_DV)
IN_SIZES = (GLA_HEADS * GLA_DK, GLA_HEADS * GLA_DK, GLA_WIDTH, GLA_WIDTH, 2 * GLA_LOWRANK,
            SWA_WIDTH, SWA_KV_HEADS * HEAD_DIM, SWA_KV_HEADS * HEAD_DIM,
            DN_QKV, DN_WIDTH, 2 * DN_HEADS, 2 * DN_HEADS)
D_IN = sum(IN_SIZES)

VMEM_LIMIT_BYTES = 56 * 1024 * 1024
ROW_TILE = 512
IN_ROW_TILE = 1024
FF_CHUNK = 1024


def _rms(xf, g):
    return xf * lax.rsqrt(jnp.mean(xf * xf, axis=-1, keepdims=True) + EPS) * g


def _const_spec(shape):
    return pl.BlockSpec(shape, lambda i: (0,) * len(shape), pipeline_mode=pl.Buffered(1))


def _layer_spec(layer, shape):
    return pl.BlockSpec((None,) + shape, lambda i: (layer,) + (0,) * len(shape), pipeline_mode=pl.Buffered(1))


_O = {}
_acc = 0
for _name, _width in (("dn_qkv", DN_QKV), ("small", 128), ("swa_q", SWA_WIDTH), ("swa_kv", 4 * 2 * HEAD_DIM),
                      ("gla_qk", 2 * GLA_HEADS * GLA_DK), ("gla_v", GLA_WIDTH), ("gla_g", GLA_WIDTH),
                      ("dn_z", DN_WIDTH)):
    _O[_name] = (_acc, _width)
    _acc += _width
D_IN_PACKED = _acc
D_IN_LEAD = _O["small"][0] + _O["small"][1]
_OUT_DTYPES = {"swa_q": jnp.bfloat16, "swa_kv": jnp.bfloat16}


PACK_ROWS = 256


def _pack_w_in_kernel(w_ref, o_ref):
    w = w_ref[0]
    offs, acc = [], 0
    for s in IN_SIZES:
        offs.append(acc)
        acc += s
    gq, gk, gv, gg, glr, sq, sk, sv, dqkv, dz, dbeta, da = [w[:, o:o + s] for o, s in zip(offs, IN_SIZES)]
    hd = HEAD_DIM
    kv = [sk[:, 0:hd], sk[:, 0:hd], sk[:, hd:], sk[:, hd:], sv[:, 0:hd], sv[:, 0:hd], sv[:, hd:], sv[:, hd:]]
    used = glr.shape[1] + dbeta.shape[1] + da.shape[1]
    small = jnp.concatenate([glr, dbeta, da, jnp.zeros((w.shape[0], 128 - used), w.dtype)], axis=1)
    pieces = dict(dn_qkv=dqkv, small=small, swa_q=sq, swa_kv=jnp.concatenate(kv, axis=1),
                  gla_qk=jnp.concatenate([gq, gk], axis=1), gla_v=gv, gla_g=gg, dn_z=dz)
    for name, (off, width) in _O.items():
        o_ref[0, :, off:off + width] = pieces[name].astype(o_ref.dtype)


def pack_w_in(w_in):
    depth, d, d_in = w_in.shape
    return pl.pallas_call(
        _pack_w_in_kernel,
        grid=(depth, d // PACK_ROWS),
        in_specs=[pl.BlockSpec((1, PACK_ROWS, d_in), lambda l, i: (l, i, 0))],
        out_specs=pl.BlockSpec((1, PACK_ROWS, D_IN_PACKED), lambda l, i: (l, i, 0)),
        out_shape=jax.ShapeDtypeStruct((depth, d, D_IN_PACKED), jnp.bfloat16),
        compiler_params=pltpu.CompilerParams(dimension_semantics=("arbitrary", "arbitrary"),
                                             vmem_limit_bytes=VMEM_LIMIT_BYTES),
        name="pack_w_in",
    )(w_in)


def _in_proj_kernel(tiles_per_seq, xp_ref, xc_ref, xn_ref, g_ref, w_ref, cw_ref, na_ref, dt_ref, *refs):
    o_refs, gate_ref, ext_ref = refs[:len(_O)], refs[len(_O)], refs[len(_O) + 1]
    tm, H = xc_ref.shape[0], DN_HALO
    i = pl.program_id(0)
    first = (i % tiles_per_seq) == 0
    last = (i % tiles_per_seq) == tiles_per_seq - 1
    x = jnp.concatenate([xp_ref[...], xc_ref[...], xn_ref[...]], axis=0)
    h = _rms(x, g_ref[...])
    h_tile = h[H:H + tm].astype(jnp.bfloat16)
    out = dict(zip(_O, o_refs))
    y = jnp.dot(h.astype(jnp.bfloat16), w_ref[:, :D_IN_LEAD], preferred_element_type=jnp.float32)
    off, width = _O["dn_qkv"]
    ext_ref[0:H, :] = jnp.where(first, 0.0, y[0:H, off:off + width])
    ext_ref[H:H + tm, :] = y[H:H + tm, off:off + width]
    ext_ref[H + tm:, :] = jnp.where(last, 0.0, y[H + tm:, off:off + width])
    half = DN_CONV // 2
    w = DN_HEADS * DN_DK
    ind = _group_indicator(w, DN_DK)
    sumsq = lambda a: jnp.dot((a * a).astype(jnp.bfloat16), ind, preferred_element_type=jnp.float32)

    def project(names):
        lo, hi = _O[names[0]][0], _O[names[-1]][0] + _O[names[-1]][1]
        part = jnp.dot(h_tile, w_ref[:, lo:hi], preferred_element_type=jnp.float32)
        for name in names:
            o, wd = _O[name]
            out[name][...] = part[:, o - lo:o - lo + wd].astype(out[name].dtype)

    def deltanet_rows(r0, r1):
        acc = ext_ref[H - half + r0:H - half + r1, :] * cw_ref[0:1, :]
        for j in range(1, DN_CONV):
            acc = acc + ext_ref[H - half + j + r0:H - half + j + r1, :] * cw_ref[j:j + 1, :]
        c = acc * (1.0 / (1.0 + jnp.exp(-acc)))
        q, k = c[:, :w], c[:, w:2 * w]
        out["dn_qkv"][r0:r1, :w] = q * lax.rsqrt(sumsq(q) + EPS) * (DN_DK ** -0.5)
        out["dn_qkv"][r0:r1, w:2 * w] = k * lax.rsqrt(sumsq(k) + EPS)
        out["dn_qkv"][r0:r1, 2 * w:] = c[:, 2 * w:]

    groups = (["swa_q", "swa_kv"], ["gla_qk", "gla_v", "gla_g", "dn_z"])
    for n, names in enumerate(groups):
        project(names)
        deltanet_rows(n * tm // len(groups), (n + 1) * tm // len(groups))
    off, width = _O["small"]
    s = y[H:H + tm, off:off + width]
    out["small"][...] = s
    lane = lax.broadcasted_iota(jnp.int32, s.shape, 1)
    beta = 1.0 / (1.0 + jnp.exp(-s))
    a = s + dt_ref[...]
    gdec = na_ref[...] * (jnp.maximum(a, 0.0) + jnp.log1p(jnp.exp(-jnp.abs(a))))
    gate_ref[...] = jnp.where((lane >= DN_BETA_LANE) & (lane < DN_A_LANE), beta,
                              jnp.where((lane >= DN_A_LANE) & (lane < DN_A_LANE + 2 * DN_HEADS), gdec, 0.0))


def in_proj(x2d, g, w_packed, layer, conv_w, a_log, dt_bias, seq):
    n, d = x2d.shape
    tm = min(IN_ROW_TILE, seq)
    hb = tm // DN_HALO
    nh = n // DN_HALO
    lanes = lambda v: jnp.zeros((1, 128), jnp.float32).at[0, DN_A_LANE:DN_A_LANE + 2 * DN_HEADS].set(v.reshape(-1))
    row = lambda w: pl.BlockSpec((tm, w), lambda i: (i, 0))
    return pl.pallas_call(
        functools.partial(_in_proj_kernel, seq // tm),
        grid=(n // tm,),
        in_specs=[pl.BlockSpec((DN_HALO, d), lambda i: (jnp.maximum(i * hb - 1, 0), 0)),
                  row(d),
                  pl.BlockSpec((DN_HALO, d), lambda i: (jnp.minimum((i + 1) * hb, nh - 1), 0)),
                  _const_spec((1, d)),
                  _layer_spec(layer, (d, D_IN_PACKED)),
                  _const_spec((DN_CONV, DN_QKV)), _const_spec((1, 128)), _const_spec((1, 128))],
        out_specs=[row(w) for _, w in _O.values()] + [row(128)],
        out_shape=[jax.ShapeDtypeStruct((n, w), _OUT_DTYPES.get(name, jnp.float32))
                   for name, (_, w) in _O.items()] + [jax.ShapeDtypeStruct((n, 128), jnp.float32)],
        scratch_shapes=[pltpu.VMEM((tm + 2 * DN_HALO, DN_QKV), jnp.float32)],
        compiler_params=pltpu.CompilerParams(dimension_semantics=("arbitrary",),
                                             vmem_limit_bytes=VMEM_LIMIT_BYTES),
        name="in_proj",
    )(x2d, x2d, x2d, g.reshape(1, d), w_packed, conv_w.astype(jnp.float32),
      lanes(-jnp.exp(a_log.astype(jnp.float32))), lanes(dt_bias.astype(jnp.float32)))


SWA_Q_TILE = 1024
SWA_GROUP = SWA_HEADS // SWA_KV_HEADS


def _t5_bucket(rel):
    nb = NUM_BUCKETS // 2
    max_exact = nb // 2
    base = jnp.where(rel > 0, nb, 0)
    n = jnp.abs(rel)
    nf = jnp.maximum(n, 1).astype(jnp.float32)
    large = max_exact + (jnp.log(nf / max_exact) / math.log(MAX_DISTANCE / max_exact)
                         * (nb - max_exact)).astype(jnp.int32)
    large = jnp.minimum(large, nb - 1)
    return base + jnp.where(n < max_exact, n, large)


def _band_bias_kernel(tab_ref, idx_ref, o_ref):
    W = SWA_BLOCK
    idx = idx_ref[...]
    i = lax.broadcasted_iota(jnp.int32, idx.shape, 0)
    j = lax.broadcasted_iota(jnp.int32, idx.shape, 1)
    in_band = jnp.abs(j - W - i) <= SWA_WINDOW
    for h in range(SWA_HEADS):
        acc = jnp.zeros(idx.shape, jnp.float32)
        for b in range(NUM_BUCKETS):
            acc = jnp.where(idx == b, tab_ref[b, h], acc)
        o_ref[h] = jnp.where(in_band, acc, -jnp.inf)


def band_bias(table):
    W = SWA_BLOCK
    rel = jnp.arange(3 * W)[None, :] - W - jnp.arange(W)[:, None]
    idx = _t5_bucket(rel).astype(jnp.int32)
    return pl.pallas_call(
        _band_bias_kernel,
        in_specs=[pl.BlockSpec(memory_space=pltpu.SMEM), pl.BlockSpec(memory_space=pltpu.VMEM)],
        out_specs=pl.BlockSpec(memory_space=pltpu.VMEM),
        out_shape=jax.ShapeDtypeStruct((SWA_HEADS, W, 3 * W), jnp.float32),
        name="band_bias",
    )(table.astype(jnp.float32), idx)


def _swa_kernel(sink_ref, q_ref, kvp_ref, kvc_ref, kvn_ref, bias_ref, o_ref, kv_buf):
    W = SWA_BLOCK
    n_sub = q_ref.shape[0] // W
    n = pl.program_id(1)
    last = pl.num_programs(1) - 1
    kv_buf[0:W, :] = kvp_ref[...]
    kv_buf[W:W + n_sub * W, :] = kvc_ref[...]
    kv_buf[W + n_sub * W:, :] = kvn_ref[...]
    lane = lax.broadcasted_iota(jnp.int32, (W, 2 * HEAD_DIM), 1)
    lo = lane < HEAD_DIM
    lo3 = (lax.broadcasted_iota(jnp.int32, (3 * W, 2 * HEAD_DIM), 1) < HEAD_DIM).astype(jnp.float32).astype(jnp.bfloat16)
    hi3 = 1 - lo3
    key_blk = lax.broadcasted_iota(jnp.int32, (1, 3 * W), 1) // W
    neg = jnp.float32(-jnp.inf)
    for s in range(n_sub):
        edge = jnp.zeros((1, 3 * W), jnp.float32)
        if s == 0:
            edge = jnp.where((key_blk == 0) & (n == 0), neg, edge)
        if s == n_sub - 1:
            edge = jnp.where((key_blk == 2) & (n == last), neg, edge)
        rows = slice(s * W, (s + 1) * W)
        for g in range(SWA_KV_HEADS):
            kd = kv_buf[s * W:(s + 3) * W, g * 128:(g + 1) * 128]
            vd = kv_buf[s * W:(s + 3) * W, 256 + g * 128:256 + (g + 1) * 128]
            k2 = jnp.concatenate([kd * lo3, kd * hi3], axis=0)
            v2 = jnp.concatenate([jnp.concatenate([vd * lo3, lo3], axis=1),
                                  jnp.concatenate([vd * hi3, hi3], axis=1)], axis=0)
            for pair in range(SWA_GROUP // 2):
                col = (g * SWA_GROUP // 2 + pair) * 128
                qp = q_ref[rows, col:col + 128] * (HEAD_DIM ** -0.5)
                s2 = lax.dot_general(qp, k2, (((1,), (1,)), ((), ())),
                                     preferred_element_type=jnp.float32)
                ps, sink_terms = [], []
                for par in range(2):
                    h = g * SWA_GROUP + 2 * pair + par
                    sc = s2[:, par * 3 * W:(par + 1) * 3 * W] + bias_ref[h]
                    if s == 0 or s == n_sub - 1:
                        sc = sc + edge
                    sink = sink_ref[h]
                    m = jnp.maximum(jnp.max(sc, axis=-1, keepdims=True), sink)
                    ps.append(jnp.exp((sc - m).astype(jnp.bfloat16)))
                    sink_terms.append(jnp.exp(sink - m))
                o2 = jnp.dot(jnp.concatenate(ps, axis=1), v2, preferred_element_type=jnp.float32)
                denom = o2[:, 128:] + jnp.where(lo, sink_terms[0], sink_terms[1])
                o_ref[rows, col:col + 128] = (o2[:, :128] * (1.0 / denom)).astype(o_ref.dtype)


def swa_attention(q, kv, sink, bias, batch, seq):
    W = SWA_BLOCK
    tq = min(SWA_Q_TILE, seq)
    n_sub = tq // W
    nq = seq // tq
    nb = seq // W
    return pl.pallas_call(
        _swa_kernel,
        grid=(batch, nq),
        in_specs=[pl.BlockSpec(memory_space=pltpu.SMEM),
                  pl.BlockSpec((tq, SWA_WIDTH), lambda b, i: (b * nq + i, 0)),
                  pl.BlockSpec((W, 512), lambda b, i: (b * nb + jnp.maximum(i * n_sub - 1, 0), 0)),
                  pl.BlockSpec((tq, 512), lambda b, i: (b * nq + i, 0)),
                  pl.BlockSpec((W, 512), lambda b, i: (b * nb + jnp.minimum((i + 1) * n_sub, nb - 1), 0)),
                  pl.BlockSpec((SWA_HEADS, W, 3 * W), lambda b, i: (0, 0, 0), pipeline_mode=pl.Buffered(1))],
        out_specs=pl.BlockSpec((tq, SWA_WIDTH), lambda b, i: (b * nq + i, 0)),
        out_shape=jax.ShapeDtypeStruct((batch * seq, SWA_WIDTH), jnp.bfloat16),
        scratch_shapes=[pltpu.VMEM((tq + 2 * W, 512), jnp.bfloat16)],
        compiler_params=pltpu.CompilerParams(dimension_semantics=("arbitrary", "arbitrary"),
                                             vmem_limit_bytes=VMEM_LIMIT_BYTES),
        name="swa",
    )(sink.astype(jnp.float32), q, kv, kv, kv, bias)


GLA_TILE = 512
GLA_SUB = 16
GLA_CUMSUM_ROWS = 128
_NT = (((1,), (1,)), ((), ()))
_TN = (((0,), (0,)), ((), ()))


def _seg_cumsum(x, seg, reverse):
    n = x.shape[0]
    rowmod = lax.broadcasted_iota(jnp.int32, x.shape, 0) % seg
    sh = 1
    while sh < seg:
        if reverse:
            x = x + jnp.where(rowmod < seg - sh, pltpu.roll(x, n - sh, 0), 0.0)
        else:
            x = x + jnp.where(rowmod >= sh, pltpu.roll(x, sh, 0), 0.0)
        sh *= 2
    return x


def _log_sigmoid(x):
    return jnp.minimum(x, 0.0) - jnp.log1p(jnp.exp(-jnp.abs(x)))


def _gla_constants():
    S, hk = GLA_SUB, GLA_HEADS * GLA_DK
    r = np.arange(GLA_CUMSUM_ROWS)
    same = (r[:, None] // S) == (r[None, :] // S)
    tri = np.stack([same & (r[None, :] <= r[:, None]), same & (r[None, :] >= r[:, None])])
    src = np.arange(S * hk)
    d, h = src // hk, (src % hk) // GLA_DK
    lane = np.arange(128)
    place = np.stack([lane[None, :] == ((S * h - d) % 128)[:, None], lane[None, :] == (S * h + d)[:, None]])
    return jnp.asarray(tri, jnp.bfloat16), jnp.asarray(place, jnp.bfloat16)


def _gla_direction(qk_ref, v_ref, sm_ref, wg_ref, bg_ref, tri_ref, place_ref, st_ref, o_ref, reverse):
    tb, S = qk_ref.shape[0], GLA_SUB
    hk = GLA_HEADS * GLA_DK
    logits = jnp.dot(sm_ref[...].astype(jnp.bfloat16), wg_ref[...], preferred_element_type=jnp.float32)
    g = _log_sigmoid(logits + bg_ref[...]) * (1.0 / GLA_GATE_NORMALIZER)
    g_hi = g.astype(jnp.bfloat16)
    g_lo = (g - g_hi.astype(jnp.float32)).astype(jnp.bfloat16)
    R = GLA_CUMSUM_ROWS
    cs = jnp.concatenate([jnp.dot(tri_ref[...], g_hi[r:r + R], preferred_element_type=jnp.float32) +
                          jnp.dot(tri_ref[...], g_lo[r:r + R], preferred_element_type=jnp.float32)
                          for r in range(0, tb, R)], axis=0)
    yield
    a = jnp.exp(g)
    q = qk_ref[:, :hk] * (GLA_DK ** -0.5)
    k = qk_ref[:, hk:]
    v = v_ref[...]
    rowmod = lax.broadcasted_iota(jnp.int32, (tb, hk), 0) % S
    has_prev = (rowmod < S - 1) if reverse else (rowmod >= 1)
    kd = k
    ps = [(q * kd).astype(jnp.bfloat16)]
    for d in range(1, S):
        kd = a * jnp.where(has_prev, pltpu.roll(kd, tb - 1 if reverse else 1, 0), 0.0)
        ps.append((q * kd).astype(jnp.bfloat16))
    scores = jnp.dot(jnp.concatenate(ps, axis=1), place_ref[...], preferred_element_type=jnp.float32)
    yield
    head_of_lane = lax.broadcasted_iota(jnp.int32, (S, GLA_WIDTH), 1) // GLA_DV
    acc = {}
    for s in range(tb // S):
        rows = slice(s * S, (s + 1) * S)
        band = pltpu.roll(scores[rows], 0, 1, stride=1, stride_axis=0)[:, :GLA_HEADS * S].astype(jnp.bfloat16)
        v_blk = v[rows]
        v_heads = jnp.concatenate([jnp.where(head_of_lane == h, v_blk, 0.0) for h in range(GLA_HEADS)], axis=0)
        acc[s] = jnp.dot(band, v_heads.astype(jnp.bfloat16), preferred_element_type=jnp.float32)
        if s % 4 == 3:
            yield
    qd = (q * jnp.exp(cs)).astype(jnp.bfloat16)
    blockdiag = (lax.broadcasted_iota(jnp.int32, (hk, GLA_WIDTH), 0) // GLA_DK ==
                 lax.broadcasted_iota(jnp.int32, (hk, GLA_WIDTH), 1) // GLA_DV)
    n_blk = tb // S
    order = range(n_blk - 1, -1, -1) if reverse else range(n_blk)
    edge_row = 0 if reverse else S - 1
    edges = jnp.concatenate([cs[s * S + edge_row:s * S + edge_row + 1] for s in range(n_blk)], axis=0)
    dec_cols = jnp.exp(edges).T
    upd = {}
    for n, s in enumerate(order):
        rows = slice(s * S, (s + 1) * S)
        kt = (k[rows] * jnp.exp(edges[s:s + 1] - cs[rows])).astype(jnp.bfloat16)
        upd[s] = jnp.where(blockdiag, lax.dot_general(kt, v[rows].astype(jnp.bfloat16), _TN,
                                                      preferred_element_type=jnp.float32), 0.0)
        if n % 4 == 3:
            yield
    st = st_ref[...]
    for n, s in enumerate(order):
        rows = slice(s * S, (s + 1) * S)
        o_ref[rows, :] = acc[s] + jnp.dot(qd[rows], st.astype(jnp.bfloat16), preferred_element_type=jnp.float32)
        st = st * dec_cols[:, s:s + 1] + upd[s]
        if n % 2 == 1:
            yield
    st_ref[...] = st


DN_HALO = 8
DN_BETA_LANE = 2 * GLA_LOWRANK
DN_A_LANE = DN_BETA_LANE + 2 * DN_HEADS


def _group_indicator(n, group):
    r = lax.broadcasted_iota(jnp.int32, (n, n), 0) // group
    c = lax.broadcasted_iota(jnp.int32, (n, n), 1) // group
    return (r == c).astype(jnp.bfloat16)


def _mm(a, b, dims=None):
    a, b = a.astype(jnp.bfloat16), b.astype(jnp.bfloat16)
    if dims is None:
        return jnp.dot(a, b, preferred_element_type=jnp.float32)
    return lax.dot_general(a, b, dims, preferred_element_type=jnp.float32)


def _head_blockdiag(x):
    c, hw = x.shape
    w = hw // DN_HEADS
    stacked = jnp.concatenate([x] * DN_HEADS, axis=0)
    r = lax.broadcasted_iota(jnp.int32, stacked.shape, 0) // c
    l = lax.broadcasted_iota(jnp.int32, stacked.shape, 1) // w
    return jnp.where(r == l, stacked, 0.0)


def _dn_chunk_prepare(qkv, gate, direction):
    C, W = DN_CHUNK, DN_HEADS * DN_DK
    reverse = direction == 1
    q, k, v = qkv[:, :W], qkv[:, W:2 * W], qkv[:, 2 * W:]
    g_lo = DN_A_LANE + direction * DN_HEADS
    b_lo = DN_BETA_LANE + direction * DN_HEADS
    cs = _seg_cumsum(gate, C, reverse)
    beta = gate
    head_of_lane = lax.broadcasted_iota(jnp.int32, (C, W), 1) // DN_DK

    def expand(x, lo):
        out = jnp.broadcast_to(x[:, lo:lo + 1], (C, W))
        for h in range(1, DN_HEADS):
            out = jnp.where(head_of_lane == h, jnp.broadcast_to(x[:, lo + h:lo + h + 1], (C, W)), out)
        return out

    bcol, bexp = expand(cs, g_lo), expand(beta, b_lo)
    yield
    row = lax.broadcasted_iota(jnp.int32, (C, W), 0)
    col = lax.broadcasted_iota(jnp.int32, (C, W), 1) % C
    cs_t = jnp.concatenate([cs, jnp.zeros_like(cs)], axis=0).T
    lane_t = lax.broadcasted_iota(jnp.int32, (1, 128), 1)
    halves = []
    for pair in range(DN_HEADS // 2):
        even = cs_t[g_lo + 2 * pair:g_lo + 2 * pair + 1, :]
        odd = pltpu.roll(cs_t[g_lo + 2 * pair + 1:g_lo + 2 * pair + 2, :], C, 1)
        halves.append(jnp.where(lane_t < C, even, odd))
    brow = jnp.broadcast_to(jnp.concatenate(halves, axis=1), (C, W))
    incl = (col >= row) if reverse else (col <= row)
    strict = (col > row) if reverse else (col < row)
    lmat = jnp.exp(jnp.where(incl, bcol - brow, -jnp.inf))
    yield
    kb = k * bexp
    kk = _mm(jnp.concatenate([kb, q], axis=0), _head_blockdiag(k), _NT)
    m = jnp.where(strict, kk[:C] * lmat, 0.0)
    aqk = jnp.where(incl, kk[C:] * lmat, 0.0)
    yield
    t = jnp.where(row == col, 1.0, 0.0)
    s = 1
    while s < C:
        same = (row // (2 * s)) == (col // (2 * s))
        r_hi, c_hi = (row % (2 * s)) >= s, (col % (2 * s)) >= s
        off = same & ((c_hi & ~r_hi) if reverse else (r_hi & ~c_hi))
        m_off = jnp.where(off, m, 0.0)
        if s == 1:
            t = t - m_off
        else:
            x = _mm(m_off, _head_blockdiag(t))
            yield
            t = t - _mm(t, _head_blockdiag(x))
            yield
        s *= 2
    ecs = jnp.exp(bcol)
    u = _mm(t, _head_blockdiag(v * bexp))
    w = _mm(t, _head_blockdiag(kb * ecs))
    yield
    edge = bcol[0:1] if reverse else bcol[C - 1:C]
    return dict(u=u, wq=jnp.concatenate([w, q * ecs], axis=0), aqk=aqk, k_dec=k * jnp.exp(edge - bcol),
                dl=jnp.exp(edge))


def _lockstep_stages(generators):
    results = [None] * len(generators)
    live = list(range(len(generators)))
    while live:
        still = []
        for i in live:
            try:
                next(generators[i])
                still.append(i)
            except StopIteration as stop:
                results[i] = stop.value
        live = still
        yield
    return results


def _lockstep(generators):
    stages = _lockstep_stages(generators)
    while True:
        try:
            next(stages)
        except StopIteration as stop:
            return stop.value


def _dn_state_chain(prepared, s_ref, o_ref, reverse):
    C, W = DN_CHUNK, DN_HEADS * DN_DK
    n_chunks = len(prepared)
    bd = (lax.broadcasted_iota(jnp.int32, (W, W), 0) // DN_DK ==
          lax.broadcasted_iota(jnp.int32, (W, W), 1) // DN_DV)
    s = s_ref[...]
    for c in (range(n_chunks - 1, -1, -1) if reverse else range(n_chunks)):
        pc = prepared[c]
        both = _mm(pc["wq"], s)
        v_new = pc["u"] - both[:C]
        yield
        o_ref[c * C:(c + 1) * C, :] = both[C:] + _mm(pc["aqk"], _head_blockdiag(v_new))
        s = s * pc["dl"] + jnp.where(bd, _mm(pc["k_dec"], v_new, _TN), 0.0)
        yield
    s_ref[...] = s


def _dn_tile(qkvf_ref, gf_ref, qkvb_ref, gb_ref, of_ref, ob_ref, sf_ref, sb_ref):
    C = DN_CHUNK
    n_chunks = qkvf_ref.shape[0] // C
    rows = lambda c: slice(c * C, (c + 1) * C)
    prepared = yield from _lockstep_stages(
        [_dn_chunk_prepare(qkvf_ref[rows(c), :], gf_ref[rows(c), :], 0) for c in range(n_chunks)] +
        [_dn_chunk_prepare(qkvb_ref[rows(c), :], gb_ref[rows(c), :], 1) for c in range(n_chunks)])
    yield from _lockstep_stages([_dn_state_chain(prepared[:n_chunks], sf_ref, of_ref, False),
                                 _dn_state_chain(prepared[n_chunks:], sb_ref, ob_ref, True)])


def _seq_mixers_kernel(qkf_ref, vf_ref, smf_ref, qkb_ref, vb_ref, smb_ref, wg_ref, bg_ref, tri_ref, place_ref,
                       dqkvf_ref, dgf_ref, dqkvb_ref, dgb_ref,
                       gof_ref, gob_ref, dof_ref, dob_ref, stf_ref, stb_ref, sf_ref, sb_ref):
    @pl.when(pl.program_id(1) == 0)
    def _():
        for ref in (stf_ref, stb_ref, sf_ref, sb_ref):
            ref[...] = jnp.zeros_like(ref)

    _lockstep([_dn_tile(dqkvf_ref, dgf_ref, dqkvb_ref, dgb_ref, dof_ref, dob_ref, sf_ref, sb_ref),
               _gla_direction(qkf_ref, vf_ref, smf_ref, wg_ref.at[0], bg_ref.at[0], tri_ref.at[0], place_ref.at[0],
                              stf_ref, gof_ref, False),
               _gla_direction(qkb_ref, vb_ref, smb_ref, wg_ref.at[1], bg_ref.at[1], tri_ref.at[1], place_ref.at[1],
                              stb_ref, gob_ref, True)])


def seq_mixers(gla_qk, gla_v, small, w_lr2, b_lr, dn_qkvn, dn_gates, batch, seq):
    hk = GLA_HEADS * GLA_DK
    W = DN_HEADS * DN_DK
    tb = min(GLA_TILE, seq)
    nt = seq // tb
    wg = jnp.zeros((2, 128, hk), jnp.float32)
    for z in range(2):
        wg = wg.at[z, z * GLA_LOWRANK:(z + 1) * GLA_LOWRANK].set(w_lr2[z])
    tri, place = _gla_constants()
    fwd = lambda w: pl.BlockSpec((tb, w), lambda b, i: (b * nt + i, 0))
    bwd = lambda w: pl.BlockSpec((tb, w), lambda b, i: (b * nt + nt - 1 - i, 0))
    const = lambda a: pl.BlockSpec(a.shape, lambda b, i: (0,) * a.ndim)
    out = jax.ShapeDtypeStruct((batch * seq, GLA_WIDTH), jnp.float32)
    bg = b_lr.reshape(2, 1, hk).astype(jnp.float32)
    wg = wg.astype(jnp.bfloat16)
    return pl.pallas_call(
        _seq_mixers_kernel,
        grid=(batch, nt),
        in_specs=[fwd(2 * hk), fwd(GLA_WIDTH), fwd(128), bwd(2 * hk), bwd(GLA_WIDTH), bwd(128),
                  const(wg), const(bg), const(tri), const(place),
                  fwd(DN_QKV), fwd(128), bwd(DN_QKV), bwd(128)],
        out_specs=[fwd(GLA_WIDTH), bwd(GLA_WIDTH), fwd(DN_WIDTH), bwd(DN_WIDTH)],
        out_shape=[out, out, out, out],
        scratch_shapes=[pltpu.VMEM((hk, GLA_WIDTH), jnp.float32), pltpu.VMEM((hk, GLA_WIDTH), jnp.float32),
                        pltpu.VMEM((W, DN_WIDTH), jnp.float32), pltpu.VMEM((W, DN_WIDTH), jnp.float32)],
        compiler_params=pltpu.CompilerParams(dimension_semantics=("arbitrary", "arbitrary"),
                                             vmem_limit_bytes=VMEM_LIMIT_BYTES),
        name="seq_mixers",
    )(gla_qk, gla_v, small, gla_qk, gla_v, small, wg, bg, tri, place, dn_qkvn, dn_gates, dn_qkvn, dn_gates)


def _gated_group_norm(o, gate, gn, ind):
    ms = jnp.dot((o * o).astype(jnp.bfloat16), ind, preferred_element_type=jnp.float32) * (1.0 / HEAD_DIM)
    return (o * lax.rsqrt(ms + EPS) * gn * (gate * (1.0 / (1.0 + jnp.exp(-gate))))).astype(jnp.bfloat16)


def _out_mlp_kernel(glaf_ref, glab_ref, glag_ref, swa_ref, dnf_ref, dnb_ref, dnz_ref, x_ref, wo_ref,
                    gng_ref, gnd_ref, g1_ref, g2_ref, w1_ref, w2_ref, g3_ref, o_ref):
    ind = _group_indicator(GLA_WIDTH, HEAD_DIM)
    y_gla = _gated_group_norm(glaf_ref[...] + glab_ref[...], glag_ref[...], gng_ref[...], ind)
    y_dn = _gated_group_norm(dnf_ref[...] + dnb_ref[...], dnz_ref[...], gnd_ref[...], ind)
    a, b = GLA_WIDTH, GLA_WIDTH + SWA_WIDTH
    m = (jnp.dot(y_gla, wo_ref[0:a, :], preferred_element_type=jnp.float32) +
         jnp.dot(swa_ref[...], wo_ref[a:b, :], preferred_element_type=jnp.float32) +
         jnp.dot(y_dn, wo_ref[b:, :], preferred_element_type=jnp.float32))
    x1 = x_ref[...] + _rms(m, g1_ref[...])
    h = _rms(x1, g2_ref[...]).astype(jnp.bfloat16)
    acc = jnp.zeros_like(x1)
    for c in range(D_FF // FF_CHUNK):
        f = jnp.dot(h, w1_ref[:, c * FF_CHUNK:(c + 1) * FF_CHUNK], preferred_element_type=jnp.float32)
        f = jnp.square(jnp.maximum(f, 0.0)).astype(jnp.bfloat16)
        acc = acc + jnp.dot(f, w2_ref[c * FF_CHUNK:(c + 1) * FF_CHUNK, :], preferred_element_type=jnp.float32)
    o_ref[...] = x1 + _rms(acc, g3_ref[...])


def out_mlp(gla_f, gla_b, gla_g, swa_o, dn_f, dn_b, dn_z, x2d, layer, wo, gla_norm_g, dn_norm_g, g1, g2, w1, w2, g3):
    n, d = x2d.shape
    tm = min(ROW_TILE, n)
    row = lambda w: pl.BlockSpec((tm, w), lambda i: (i, 0))
    vec = _const_spec((1, d))
    hvec = _const_spec((1, GLA_WIDTH))
    tile_gain = lambda g, h: jnp.tile(g.astype(jnp.float32), h).reshape(1, -1)
    return pl.pallas_call(
        _out_mlp_kernel,
        grid=(n // tm,),
        in_specs=[row(GLA_WIDTH), row(GLA_WIDTH), row(GLA_WIDTH), row(SWA_WIDTH), row(DN_WIDTH), row(DN_WIDTH),
                  row(DN_WIDTH), row(d), _layer_spec(layer, (d, d)), hvec, hvec, vec, vec,
                  _layer_spec(layer, (d, D_FF)), _layer_spec(layer, (D_FF, d)), vec],
        out_specs=row(d),
        out_shape=jax.ShapeDtypeStruct((n, d), jnp.float32),
        compiler_params=pltpu.CompilerParams(dimension_semantics=("arbitrary",),
                                             vmem_limit_bytes=VMEM_LIMIT_BYTES),
        name="out_mlp",
    )(gla_f, gla_b, gla_g, swa_o, dn_f, dn_b, dn_z, x2d, wo, tile_gain(gla_norm_g, GLA_HEADS),
      tile_gain(dn_norm_g, DN_HEADS), g1.reshape(1, d), g2.reshape(1, d), w1, w2, g3.reshape(1, d))


def _trunk(x, p):
    B, T, D = x.shape
    bias = p["band_bias"]
    x2 = x.reshape(B * T, D)
    for l in range(DEPTH):
        dn_qkvn, small, swa_q, swa_kv, gla_qk, gla_v, gla_g, dn_z, dn_gates = in_proj(
            x2, p["mix_pre_g"][l], p["w_in_packed"], l, p["dn_conv_w"][l], p["dn_a_log"][l], p["dn_dt_bias"][l], T)
        o_swa = swa_attention(swa_q, swa_kv, p["swa_sink"][l], bias, B, T)
        gla_f, gla_b, dn_f, dn_b = seq_mixers(gla_qk, gla_v, small, p["gla_w_lr2"][l], p["gla_b_lr"][l],
                                              dn_qkvn, dn_gates, B, T)
        x2 = out_mlp(gla_f, gla_b, gla_g, o_swa, dn_f, dn_b, dn_z, x2, l, p["w_out_bf16"], p["gla_norm_g"][l],
                     p["dn_norm_g"][l], p["mix_post_g"][l], p["mlp_pre_g"][l],
                     p["mlp_w1_bf16"], p["mlp_w2_bf16"], p["mlp_post_g"][l])
    return x2.reshape(B, T, D)


def _prepare_params(rel_bias_table, mix_pre_g, w_in, gla_w_lr2, gla_b_lr, gla_norm_g, swa_sink, dn_conv_w, dn_a_log,
                    dn_dt_bias, dn_norm_g, w_out, mix_post_g, mlp_pre_g, mlp_w1, mlp_w2, mlp_post_g):
    return dict(mix_pre_g=mix_pre_g, gla_w_lr2=gla_w_lr2, gla_b_lr=gla_b_lr,
                gla_norm_g=gla_norm_g, swa_sink=swa_sink, dn_conv_w=dn_conv_w, dn_a_log=dn_a_log,
                dn_dt_bias=dn_dt_bias, dn_norm_g=dn_norm_g, mix_post_g=mix_post_g, mlp_pre_g=mlp_pre_g,
                mlp_post_g=mlp_post_g,
                band_bias=band_bias(rel_bias_table),
                w_in_packed=pack_w_in(w_in), w_out_bf16=w_out.astype(jnp.bfloat16),
                mlp_w1_bf16=mlp_w1.astype(jnp.bfloat16), mlp_w2_bf16=mlp_w2.astype(jnp.bfloat16))


def kernel(x_prompt, x_sample, rel_bias_table, mix_pre_g, w_in, gla_w_lr2, gla_b_lr, gla_norm_g, swa_sink, dn_conv_w, dn_a_log, dn_dt_bias, dn_norm_g, w_out, mix_post_g, mlp_pre_g, mlp_w1, mlp_w2, mlp_post_g):
    p = _prepare_params(rel_bias_table, mix_pre_g, w_in, gla_w_lr2, gla_b_lr, gla_norm_g, swa_sink, dn_conv_w,
                        dn_a_log, dn_dt_bias, dn_norm_g, w_out, mix_post_g, mlp_pre_g, mlp_w1, mlp_w2, mlp_post_g)
    return (_trunk(x_prompt, p), _trunk(x_sample, p))
```

```python
import functools
import math

import numpy as np

import jax
import jax.numpy as jnp
from jax import lax
from jax.experimental import pallas as pl
from jax.experimental.pallas import tpu as pltpu

D_MODEL = 1024
DEPTH = 4
HEAD_DIM = 64
EPS = 1e-6
GLA_HEADS = 4
GLA_DK = 32
GLA_DV = 64
GLA_LOWRANK = 16
GLA_GATE_NORMALIZER = 16.0
SWA_HEADS = 8
SWA_KV_HEADS = 2
SWA_WINDOW = 128
SWA_BLOCK = 128
NUM_BUCKETS = 32
MAX_DISTANCE = 128
DN_HEADS = 4
DN_DK = 64
DN_DV = 64
DN_CONV = 5
DN_CHUNK = 64
GLA_WIDTH = GLA_HEADS * GLA_DV
SWA_WIDTH = SWA_HEADS * HEAD_DIM
DN_WIDTH = DN_HEADS * DN_DV
D_FF = 4 * D_MODEL
DN_QKV = DN_HEADS * (2 * DN_DK + DN_DV)
IN_SIZES = (GLA_HEADS * GLA_DK, GLA_HEADS * GLA_DK, GLA_WIDTH, GLA_WIDTH, 2 * GLA_LOWRANK,
            SWA_WIDTH, SWA_KV_HEADS * HEAD_DIM, SWA_KV_HEADS * HEAD_DIM,
            DN_QKV, DN_WIDTH, 2 * DN_HEADS, 2 * DN_HEADS)
D_IN = sum(IN_SIZES)

VMEM_LIMIT_BYTES = 56 * 1024 * 1024
ROW_TILE = 512
IN_ROW_TILE = 1024
FF_CHUNK = 2048


def _rms(xf, g):
    return xf * lax.rsqrt(jnp.mean(xf * xf, axis=-1, keepdims=True) + EPS) * g


def _const_spec(shape):
    return pl.BlockSpec(shape, lambda i: (0,) * len(shape), pipeline_mode=pl.Buffered(1))


def _layer_spec(layer, shape):
    return pl.BlockSpec((None,) + shape, lambda i: (layer,) + (0,) * len(shape), pipeline_mode=pl.Buffered(1))


_O = {}
_acc = 0
for _name, _width in (("dn_qkv", DN_QKV), ("small", 128), ("swa_q", SWA_WIDTH), ("swa_kv", 4 * 2 * HEAD_DIM),
                      ("gla_qk", 2 * GLA_HEADS * GLA_DK), ("gla_v", GLA_WIDTH), ("gla_g", GLA_WIDTH),
                      ("dn_z", DN_WIDTH)):
    _O[_name] = (_acc, _width)
    _acc += _width
D_IN_PACKED = _acc
D_IN_LEAD = _O["small"][0] + _O["small"][1]
_OUT_DTYPES = {"swa_q": jnp.bfloat16, "swa_kv": jnp.bfloat16}


PACK_ROWS = 256


def _pack_w_in_kernel(w_ref, o_ref):
    w = w_ref[0]
    offs, acc = [], 0
    for s in IN_SIZES:
        offs.append(acc)
        acc += s
    gq, gk, gv, gg, glr, sq, sk, sv, dqkv, dz, dbeta, da = [w[:, o:o + s] for o, s in zip(offs, IN_SIZES)]
    hd = HEAD_DIM
    kv = [sk[:, 0:hd], sk[:, 0:hd], sk[:, hd:], sk[:, hd:], sv[:, 0:hd], sv[:, 0:hd], sv[:, hd:], sv[:, hd:]]
    used = glr.shape[1] + dbeta.shape[1] + da.shape[1]
    small = jnp.concatenate([glr, dbeta, da, jnp.zeros((w.shape[0], 128 - used), w.dtype)], axis=1)
    pieces = dict(dn_qkv=dqkv, small=small, swa_q=sq, swa_kv=jnp.concatenate(kv, axis=1),
                  gla_qk=jnp.concatenate([gq, gk], axis=1), gla_v=gv, gla_g=gg, dn_z=dz)
    for name, (off, width) in _O.items():
        o_ref[0, :, off:off + width] = pieces[name].astype(o_ref.dtype)


def pack_w_in(w_in):
    depth, d, d_in = w_in.shape
    return pl.pallas_call(
        _pack_w_in_kernel,
        grid=(depth, d // PACK_ROWS),
        in_specs=[pl.BlockSpec((1, PACK_ROWS, d_in), lambda l, i: (l, i, 0))],
        out_specs=pl.BlockSpec((1, PACK_ROWS, D_IN_PACKED), lambda l, i: (l, i, 0)),
        out_shape=jax.ShapeDtypeStruct((depth, d, D_IN_PACKED), jnp.bfloat16),
        compiler_params=pltpu.CompilerParams(dimension_semantics=("arbitrary", "arbitrary"),
                                             vmem_limit_bytes=VMEM_LIMIT_BYTES),
        name="pack_w_in",
    )(w_in)


def _in_proj_kernel(tiles_per_seq, xp_ref, xc_ref, xn_ref, g_ref, w_ref, cw_ref, na_ref, dt_ref, *refs):
    o_refs, gate_ref, ext_ref = refs[:len(_O)], refs[len(_O)], refs[len(_O) + 1]
    tm, H = xc_ref.shape[0], DN_HALO
    i = pl.program_id(0)
    first = (i % tiles_per_seq) == 0
    last = (i % tiles_per_seq) == tiles_per_seq - 1
    x = jnp.concatenate([xp_ref[...], xc_ref[...], xn_ref[...]], axis=0)
    h = _rms(x, g_ref[...])
    h_tile = h[H:H + tm].astype(jnp.bfloat16)
    out = dict(zip(_O, o_refs))
    y = jnp.dot(h.astype(jnp.bfloat16), w_ref[:, :D_IN_LEAD], preferred_element_type=jnp.float32)
    off, width = _O["dn_qkv"]
    ext_ref[0:H, :] = jnp.where(first, 0.0, y[0:H, off:off + width])
    ext_ref[H:H + tm, :] = y[H:H + tm, off:off + width]
    ext_ref[H + tm:, :] = jnp.where(last, 0.0, y[H + tm:, off:off + width])
    half = DN_CONV // 2
    w = DN_HEADS * DN_DK
    ind = _group_indicator(w, DN_DK)
    sumsq = lambda a: jnp.dot((a * a).astype(jnp.bfloat16), ind, preferred_element_type=jnp.float32)

    def project(names):
        lo, hi = _O[names[0]][0], _O[names[-1]][0] + _O[names[-1]][1]
        part = jnp.dot(h_tile, w_ref[:, lo:hi], preferred_element_type=jnp.float32)
        for name in names:
            o, wd = _O[name]
            out[name][...] = part[:, o - lo:o - lo + wd].astype(out[name].dtype)

    def deltanet_rows(r0, r1):
        acc = ext_ref[H - half + r0:H - half + r1, :] * cw_ref[0:1, :]
        for j in range(1, DN_CONV):
            acc = acc + ext_ref[H - half + j + r0:H - half + j + r1, :] * cw_ref[j:j + 1, :]
        c = acc * (1.0 / (1.0 + jnp.exp(-acc)))
        q, k = c[:, :w], c[:, w:2 * w]
        out["dn_qkv"][r0:r1, :w] = q * lax.rsqrt(sumsq(q) + EPS) * (DN_DK ** -0.5)
        out["dn_qkv"][r0:r1, w:2 * w] = k * lax.rsqrt(sumsq(k) + EPS)
        out["dn_qkv"][r0:r1, 2 * w:] = c[:, 2 * w:]

    groups = (["swa_q", "swa_kv"], ["gla_qk", "gla_v", "gla_g", "dn_z"])
    for n, names in enumerate(groups):
        project(names)
        deltanet_rows(n * tm // len(groups), (n + 1) * tm // len(groups))
    off, width = _O["small"]
    s = y[H:H + tm, off:off + width]
    out["small"][...] = s
    lane = lax.broadcasted_iota(jnp.int32, s.shape, 1)
    beta = 1.0 / (1.0 + jnp.exp(-s))
    a = s + dt_ref[...]
    gdec = na_ref[...] * (jnp.maximum(a, 0.0) + jnp.log1p(jnp.exp(-jnp.abs(a))))
    gate_ref[...] = jnp.where((lane >= DN_BETA_LANE) & (lane < DN_A_LANE), beta,
                              jnp.where((lane >= DN_A_LANE) & (lane < DN_A_LANE + 2 * DN_HEADS), gdec, 0.0))


def in_proj(x2d, g, w_packed, layer, conv_w, a_log, dt_bias, seq):
    n, d = x2d.shape
    tm = min(IN_ROW_TILE, seq)
    hb = tm // DN_HALO
    nh = n // DN_HALO
    lanes = lambda v: jnp.zeros((1, 128), jnp.float32).at[0, DN_A_LANE:DN_A_LANE + 2 * DN_HEADS].set(v.reshape(-1))
    row = lambda w: pl.BlockSpec((tm, w), lambda i: (i, 0))
    return pl.pallas_call(
        functools.partial(_in_proj_kernel, seq // tm),
        grid=(n // tm,),
        in_specs=[pl.BlockSpec((DN_HALO, d), lambda i: (jnp.maximum(i * hb - 1, 0), 0)),
                  row(d),
                  pl.BlockSpec((DN_HALO, d), lambda i: (jnp.minimum((i + 1) * hb, nh - 1), 0)),
                  _const_spec((1, d)),
                  _layer_spec(layer, (d, D_IN_PACKED)),
                  _const_spec((DN_CONV, DN_QKV)), _const_spec((1, 128)), _const_spec((1, 128))],
        out_specs=[row(w) for _, w in _O.values()] + [row(128)],
        out_shape=[jax.ShapeDtypeStruct((n, w), _OUT_DTYPES.get(name, jnp.float32))
                   for name, (_, w) in _O.items()] + [jax.ShapeDtypeStruct((n, 128), jnp.float32)],
        scratch_shapes=[pltpu.VMEM((tm + 2 * DN_HALO, DN_QKV), jnp.float32)],
        compiler_params=pltpu.CompilerParams(dimension_semantics=("arbitrary",),
                                             vmem_limit_bytes=VMEM_LIMIT_BYTES),
        name="in_proj",
    )(x2d, x2d, x2d, g.reshape(1, d), w_packed, conv_w.astype(jnp.float32),
      lanes(-jnp.exp(a_log.astype(jnp.float32))), lanes(dt_bias.astype(jnp.float32)))


SWA_Q_TILE = 1024
SWA_GROUP = SWA_HEADS // SWA_KV_HEADS


def _t5_bucket(rel):
    nb = NUM_BUCKETS // 2
    max_exact = nb // 2
    base = jnp.where(rel > 0, nb, 0)
    n = jnp.abs(rel)
    nf = jnp.maximum(n, 1).astype(jnp.float32)
    large = max_exact + (jnp.log(nf / max_exact) / math.log(MAX_DISTANCE / max_exact)
                         * (nb - max_exact)).astype(jnp.int32)
    large = jnp.minimum(large, nb - 1)
    return base + jnp.where(n < max_exact, n, large)


def _band_bias_kernel(tab_ref, idx_ref, o_ref):
    W = SWA_BLOCK
    idx = idx_ref[...]
    i = lax.broadcasted_iota(jnp.int32, idx.shape, 0)
    j = lax.broadcasted_iota(jnp.int32, idx.shape, 1)
    in_band = jnp.abs(j - W - i) <= SWA_WINDOW
    for h in range(SWA_HEADS):
        acc = jnp.zeros(idx.shape, jnp.float32)
        for b in range(NUM_BUCKETS):
            acc = jnp.where(idx == b, tab_ref[b, h], acc)
        o_ref[h] = jnp.where(in_band, acc, -jnp.inf)


def band_bias(table):
    W = SWA_BLOCK
    rel = jnp.arange(3 * W)[None, :] - W - jnp.arange(W)[:, None]
    idx = _t5_bucket(rel).astype(jnp.int32)
    return pl.pallas_call(
        _band_bias_kernel,
        in_specs=[pl.BlockSpec(memory_space=pltpu.SMEM), pl.BlockSpec(memory_space=pltpu.VMEM)],
        out_specs=pl.BlockSpec(memory_space=pltpu.VMEM),
        out_shape=jax.ShapeDtypeStruct((SWA_HEADS, W, 3 * W), jnp.float32),
        name="band_bias",
    )(table.astype(jnp.float32), idx)


def _swa_kernel(sink_ref, q_ref, kvp_ref, kvc_ref, kvn_ref, bias_ref, o_ref, kv_buf):
    W = SWA_BLOCK
    n_sub = q_ref.shape[0] // W
    n = pl.program_id(1)
    last = pl.num_programs(1) - 1
    kv_buf[0:W, :] = kvp_ref[...]
    kv_buf[W:W + n_sub * W, :] = kvc_ref[...]
    kv_buf[W + n_sub * W:, :] = kvn_ref[...]
    lane = lax.broadcasted_iota(jnp.int32, (W, 2 * HEAD_DIM), 1)
    lo = lane < HEAD_DIM
    lo3 = (lax.broadcasted_iota(jnp.int32, (3 * W, 2 * HEAD_DIM), 1) < HEAD_DIM).astype(jnp.float32).astype(jnp.bfloat16)
    hi3 = 1 - lo3
    key_blk = lax.broadcasted_iota(jnp.int32, (1, 3 * W), 1) // W
    neg = jnp.float32(-jnp.inf)

    def head_pair(s, g, pair, rows, edge, k2, v2):
        col = (g * SWA_GROUP // 2 + pair) * 128
        qp = q_ref[rows, col:col + 128] * (HEAD_DIM ** -0.5)
        s2 = lax.dot_general(qp, k2, (((1,), (1,)), ((), ())),
                             preferred_element_type=jnp.float32)
        yield
        ps, sink_terms = [], []
        for par in range(2):
            h = g * SWA_GROUP + 2 * pair + par
            sc = s2[:, par * 3 * W:(par + 1) * 3 * W] + bias_ref[h]
            if s == 0 or s == n_sub - 1:
                sc = sc + edge
            sink = sink_ref[h]
            m = jnp.maximum(jnp.max(sc, axis=-1, keepdims=True), sink)
            ps.append(jnp.exp((sc - m).astype(jnp.bfloat16)))
            sink_terms.append(jnp.exp(sink - m))
        o2 = jnp.dot(jnp.concatenate(ps, axis=1), v2, preferred_element_type=jnp.float32)
        denom = o2[:, 128:] + jnp.where(lo, sink_terms[0], sink_terms[1])
        o_ref[rows, col:col + 128] = (o2[:, :128] * (1.0 / denom)).astype(o_ref.dtype)

    pairs = []
    for s in range(n_sub):
        edge = jnp.zeros((1, 3 * W), jnp.float32)
        if s == 0:
            edge = jnp.where((key_blk == 0) & (n == 0), neg, edge)
        if s == n_sub - 1:
            edge = jnp.where((key_blk == 2) & (n == last), neg, edge)
        rows = slice(s * W, (s + 1) * W)
        for g in range(SWA_KV_HEADS):
            kd = kv_buf[s * W:(s + 3) * W, g * 128:(g + 1) * 128]
            vd = kv_buf[s * W:(s + 3) * W, 256 + g * 128:256 + (g + 1) * 128]
            k2 = jnp.concatenate([kd * lo3, kd * hi3], axis=0)
            v2 = jnp.concatenate([jnp.concatenate([vd * lo3, lo3], axis=1),
                                  jnp.concatenate([vd * hi3, hi3], axis=1)], axis=0)
            for pair in range(SWA_GROUP // 2):
                pairs.append(head_pair(s, g, pair, rows, edge, k2, v2))
    next(pairs[0])
    for i, gen in enumerate(pairs):
        if i + 1 < len(pairs):
            next(pairs[i + 1])
        for _ in gen:
            pass


def swa_attention(q, kv, sink, bias, batch, seq):
    W = SWA_BLOCK
    tq = min(SWA_Q_TILE, seq)
    n_sub = tq // W
    nq = seq // tq
    nb = seq // W
    return pl.pallas_call(
        _swa_kernel,
        grid=(batch, nq),
        in_specs=[pl.BlockSpec(memory_space=pltpu.SMEM),
                  pl.BlockSpec((tq, SWA_WIDTH), lambda b, i: (b * nq + i, 0)),
                  pl.BlockSpec((W, 512), lambda b, i: (b * nb + jnp.maximum(i * n_sub - 1, 0), 0)),
                  pl.BlockSpec((tq, 512), lambda b, i: (b * nq + i, 0)),
                  pl.BlockSpec((W, 512), lambda b, i: (b * nb + jnp.minimum((i + 1) * n_sub, nb - 1), 0)),
                  pl.BlockSpec((SWA_HEADS, W, 3 * W), lambda b, i: (0, 0, 0), pipeline_mode=pl.Buffered(1))],
        out_specs=pl.BlockSpec((tq, SWA_WIDTH), lambda b, i: (b * nq + i, 0)),
        out_shape=jax.ShapeDtypeStruct((batch * seq, SWA_WIDTH), jnp.bfloat16),
        scratch_shapes=[pltpu.VMEM((tq + 2 * W, 512), jnp.bfloat16)],
        compiler_params=pltpu.CompilerParams(dimension_semantics=("arbitrary", "arbitrary"),
                                             vmem_limit_bytes=VMEM_LIMIT_BYTES),
        name="swa",
    )(sink.astype(jnp.float32), q, kv, kv, kv, bias)


GLA_TILE = 512
GLA_SUB = 16
GLA_CUMSUM_ROWS = 128
_NT = (((1,), (1,)), ((), ()))
_TN = (((0,), (0,)), ((), ()))


def _seg_cumsum(x, seg, reverse):
    n = x.shape[0]
    rowmod = lax.broadcasted_iota(jnp.int32, x.shape, 0) % seg
    sh = 1
    while sh < seg:
        if reverse:
            x = x + jnp.where(rowmod < seg - sh, pltpu.roll(x, n - sh, 0), 0.0)
        else:
            x = x + jnp.where(rowmod >= sh, pltpu.roll(x, sh, 0), 0.0)
        sh *= 2
    return x


def _log_sigmoid(x):
    return jnp.minimum(x, 0.0) - jnp.log1p(jnp.exp(-jnp.abs(x)))


def _gla_constants():
    S, hk = GLA_SUB, GLA_HEADS * GLA_DK
    r = np.arange(GLA_CUMSUM_ROWS)
    same = (r[:, None] // S) == (r[None, :] // S)
    tri = np.stack([same & (r[None, :] <= r[:, None]), same & (r[None, :] >= r[:, None])])
    src = np.arange(S * hk)
    d, h = src // hk, (src % hk) // GLA_DK
    lane = np.arange(128)
    place = np.stack([lane[None, :] == ((S * h - d) % 128)[:, None], lane[None, :] == (S * h + d)[:, None]])
    return jnp.asarray(tri, jnp.bfloat16), jnp.asarray(place, jnp.bfloat16)


def _gla_direction(qk_ref, v_ref, sm_ref, wg_ref, bg_ref, tri_ref, place_ref, st_ref, o_ref, reverse):
    tb, S = qk_ref.shape[0], GLA_SUB
    hk = GLA_HEADS * GLA_DK
    logits = jnp.dot(sm_ref[...].astype(jnp.bfloat16), wg_ref[...], preferred_element_type=jnp.float32)
    g = _log_sigmoid(logits + bg_ref[...]) * (1.0 / GLA_GATE_NORMALIZER)
    g_hi = g.astype(jnp.bfloat16)
    g_lo = (g - g_hi.astype(jnp.float32)).astype(jnp.bfloat16)
    R = GLA_CUMSUM_ROWS
    cs = jnp.concatenate([jnp.dot(tri_ref[...], g_hi[r:r + R], preferred_element_type=jnp.float32) +
                          jnp.dot(tri_ref[...], g_lo[r:r + R], preferred_element_type=jnp.float32)
                          for r in range(0, tb, R)], axis=0)
    yield
    a = jnp.exp(g)
    q = qk_ref[:, :hk] * (GLA_DK ** -0.5)
    k = qk_ref[:, hk:]
    v = v_ref[...]
    rowmod = lax.broadcasted_iota(jnp.int32, (tb, hk), 0) % S
    has_prev = (rowmod < S - 1) if reverse else (rowmod >= 1)
    kd = k
    ps = [(q * kd).astype(jnp.bfloat16)]
    for d in range(1, S):
        kd = a * jnp.where(has_prev, pltpu.roll(kd, tb - 1 if reverse else 1, 0), 0.0)
        ps.append((q * kd).astype(jnp.bfloat16))
    scores = jnp.dot(jnp.concatenate(ps, axis=1), place_ref[...], preferred_element_type=jnp.float32)
    yield
    head_of_lane = lax.broadcasted_iota(jnp.int32, (S, GLA_WIDTH), 1) // GLA_DV
    acc = {}
    for s in range(tb // S):
        rows = slice(s * S, (s + 1) * S)
        band = pltpu.roll(scores[rows], 0, 1, stride=1, stride_axis=0)[:, :GLA_HEADS * S].astype(jnp.bfloat16)
        v_blk = v[rows]
        v_heads = jnp.concatenate([jnp.where(head_of_lane == h, v_blk, 0.0) for h in range(GLA_HEADS)], axis=0)
        acc[s] = jnp.dot(band, v_heads.astype(jnp.bfloat16), preferred_element_type=jnp.float32)
        if s % 4 == 3:
            yield
    qd = (q * jnp.exp(cs)).astype(jnp.bfloat16)
    blockdiag = (lax.broadcasted_iota(jnp.int32, (hk, GLA_WIDTH), 0) // GLA_DK ==
                 lax.broadcasted_iota(jnp.int32, (hk, GLA_WIDTH), 1) // GLA_DV)
    n_blk = tb // S
    order = range(n_blk - 1, -1, -1) if reverse else range(n_blk)
    edge_row = 0 if reverse else S - 1
    edges = jnp.concatenate([cs[s * S + edge_row:s * S + edge_row + 1] for s in range(n_blk)], axis=0)
    dec_cols = jnp.exp(edges).T
    upd = {}
    for n, s in enumerate(order):
        rows = slice(s * S, (s + 1) * S)
        kt = (k[rows] * jnp.exp(edges[s:s + 1] - cs[rows])).astype(jnp.bfloat16)
        upd[s] = jnp.where(blockdiag, lax.dot_general(kt, v[rows].astype(jnp.bfloat16), _TN,
                                                      preferred_element_type=jnp.float32), 0.0)
        if n % 4 == 3:
            yield
    st = st_ref[...]
    for n, s in enumerate(order):
        rows = slice(s * S, (s + 1) * S)
        o_ref[rows, :] = acc[s] + jnp.dot(qd[rows], st.astype(jnp.bfloat16), preferred_element_type=jnp.float32)
        st = st * dec_cols[:, s:s + 1] + upd[s]
        if n % 4 == 3:
            yield
    st_ref[...] = st


DN_HALO = 8
DN_BETA_LANE = 2 * GLA_LOWRANK
DN_A_LANE = DN_BETA_LANE + 2 * DN_HEADS


def _group_indicator(n, group):
    r = lax.broadcasted_iota(jnp.int32, (n, n), 0) // group
    c = lax.broadcasted_iota(jnp.int32, (n, n), 1) // group
    return (r == c).astype(jnp.bfloat16)


def _mm(a, b, dims=None):
    a, b = a.astype(jnp.bfloat16), b.astype(jnp.bfloat16)
    if dims is None:
        return jnp.dot(a, b, preferred_element_type=jnp.float32)
    return lax.dot_general(a, b, dims, preferred_element_type=jnp.float32)


def _head_blockdiag(x):
    c, hw = x.shape
    w = hw // DN_HEADS
    stacked = jnp.concatenate([x] * DN_HEADS, axis=0)
    r = lax.broadcasted_iota(jnp.int32, stacked.shape, 0) // c
    l = lax.broadcasted_iota(jnp.int32, stacked.shape, 1) // w
    return jnp.where(r == l, stacked, 0.0)


def _dn_chunk_prepare(qkv, gate, direction):
    C, W = DN_CHUNK, DN_HEADS * DN_DK
    reverse = direction == 1
    q, k, v = qkv[:, :W], qkv[:, W:2 * W], qkv[:, 2 * W:]
    g_lo = DN_A_LANE + direction * DN_HEADS
    b_lo = DN_BETA_LANE + direction * DN_HEADS
    cs = _seg_cumsum(gate, C, reverse)
    beta = gate
    head_of_lane = lax.broadcasted_iota(jnp.int32, (C, W), 1) // DN_DK

    def expand(x, lo):
        out = jnp.broadcast_to(x[:, lo:lo + 1], (C, W))
        for h in range(1, DN_HEADS):
            out = jnp.where(head_of_lane == h, jnp.broadcast_to(x[:, lo + h:lo + h + 1], (C, W)), out)
        return out

    bcol, bexp = expand(cs, g_lo), expand(beta, b_lo)
    yield
    row = lax.broadcasted_iota(jnp.int32, (C, W), 0)
    col = lax.broadcasted_iota(jnp.int32, (C, W), 1) % C
    cs_t = jnp.concatenate([cs, jnp.zeros_like(cs)], axis=0).T
    lane_t = lax.broadcasted_iota(jnp.int32, (1, 128), 1)
    halves = []
    for pair in range(DN_HEADS // 2):
        even = cs_t[g_lo + 2 * pair:g_lo + 2 * pair + 1, :]
        odd = pltpu.roll(cs_t[g_lo + 2 * pair + 1:g_lo + 2 * pair + 2, :], C, 1)
        halves.append(jnp.where(lane_t < C, even, odd))
    brow = jnp.broadcast_to(jnp.concatenate(halves, axis=1), (C, W))
    incl = (col >= row) if reverse else (col <= row)
    strict = (col > row) if reverse else (col < row)
    lmat = jnp.exp(jnp.where(incl, bcol - brow, -jnp.inf))
    yield
    kb = k * bexp
    kk = _mm(jnp.concatenate([kb, q], axis=0), _head_blockdiag(k), _NT)
    m = jnp.where(strict, kk[:C] * lmat, 0.0)
    aqk = jnp.where(incl, kk[C:] * lmat, 0.0)
    yield
    t = jnp.where(row == col, 1.0, 0.0)
    s = 1
    while s < C:
        same = (row // (2 * s)) == (col // (2 * s))
        r_hi, c_hi = (row % (2 * s)) >= s, (col % (2 * s)) >= s
        off = same & ((c_hi & ~r_hi) if reverse else (r_hi & ~c_hi))
        m_off = jnp.where(off, m, 0.0)
        if s == 1:
            t = t - m_off
        else:
            x = _mm(m_off, _head_blockdiag(t))
            yield
            t = t - _mm(t, _head_blockdiag(x))
            yield
        s *= 2
    ecs = jnp.exp(bcol)
    u = _mm(t, _head_blockdiag(v * bexp))
    w = _mm(t, _head_blockdiag(kb * ecs))
    yield
    edge = bcol[0:1] if reverse else bcol[C - 1:C]
    return dict(u=u, wq=jnp.concatenate([w, q * ecs], axis=0), aqk=aqk, k_dec=k * jnp.exp(edge - bcol),
                dl=jnp.exp(edge))


def _lockstep_stages(generators):
    results = [None] * len(generators)
    live = list(range(len(generators)))
    while live:
        still = []
        for i in live:
            try:
                next(generators[i])
                still.append(i)
            except StopIteration as stop:
                results[i] = stop.value
        live = still
        yield
    return results


def _lockstep(generators):
    stages = _lockstep_stages(generators)
    while True:
        try:
            next(stages)
        except StopIteration as stop:
            return stop.value


def _dn_state_chain(prepared, s_ref, o_ref, reverse):
    C, W = DN_CHUNK, DN_HEADS * DN_DK
    n_chunks = len(prepared)
    bd = (lax.broadcasted_iota(jnp.int32, (W, W), 0) // DN_DK ==
          lax.broadcasted_iota(jnp.int32, (W, W), 1) // DN_DV)
    s = s_ref[...]
    for c in (range(n_chunks - 1, -1, -1) if reverse else range(n_chunks)):
        pc = prepared[c]
        both = _mm(pc["wq"], s)
        v_new = pc["u"] - both[:C]
        yield
        o_ref[c * C:(c + 1) * C, :] = both[C:] + _mm(pc["aqk"], _head_blockdiag(v_new))
        s = s * pc["dl"] + jnp.where(bd, _mm(pc["k_dec"], v_new, _TN), 0.0)
        yield
    s_ref[...] = s


def _dn_tile(qkvf_ref, gf_ref, qkvb_ref, gb_ref, of_ref, ob_ref, sf_ref, sb_ref):
    C = DN_CHUNK
    n_chunks = qkvf_ref.shape[0] // C
    rows = lambda c: slice(c * C, (c + 1) * C)
    prepared = yield from _lockstep_stages(
        [_dn_chunk_prepare(qkvf_ref[rows(c), :], gf_ref[rows(c), :], 0) for c in range(n_chunks)] +
        [_dn_chunk_prepare(qkvb_ref[rows(c), :], gb_ref[rows(c), :], 1) for c in range(n_chunks)])
    yield from _lockstep_stages([_dn_state_chain(prepared[:n_chunks], sf_ref, of_ref, False),
                                 _dn_state_chain(prepared[n_chunks:], sb_ref, ob_ref, True)])


def _seq_mixers_kernel(qkf_ref, vf_ref, smf_ref, qkb_ref, vb_ref, smb_ref, wg_ref, bg_ref, tri_ref, place_ref,
                       dqkvf_ref, dgf_ref, dqkvb_ref, dgb_ref,
                       gof_ref, gob_ref, dof_ref, dob_ref, stf_ref, stb_ref, sf_ref, sb_ref):
    @pl.when(pl.program_id(1) == 0)
    def _():
        for ref in (stf_ref, stb_ref, sf_ref, sb_ref):
            ref[...] = jnp.zeros_like(ref)

    _lockstep([_dn_tile(dqkvf_ref, dgf_ref, dqkvb_ref, dgb_ref, dof_ref, dob_ref, sf_ref, sb_ref),
               _gla_direction(qkf_ref, vf_ref, smf_ref, wg_ref.at[0], bg_ref.at[0], tri_ref.at[0], place_ref.at[0],
                              stf_ref, gof_ref, False),
               _gla_direction(qkb_ref, vb_ref, smb_ref, wg_ref.at[1], bg_ref.at[1], tri_ref.at[1], place_ref.at[1],
                              stb_ref, gob_ref, True)])


def seq_mixers(gla_qk, gla_v, small, w_lr2, b_lr, dn_qkvn, dn_gates, batch, seq):
    hk = GLA_HEADS * GLA_DK
    W = DN_HEADS * DN_DK
    tb = min(GLA_TILE, seq)
    nt = seq // tb
    wg = jnp.zeros((2, 128, hk), jnp.float32)
    for z in range(2):
        wg = wg.at[z, z * GLA_LOWRANK:(z + 1) * GLA_LOWRANK].set(w_lr2[z])
    tri, place = _gla_constants()
    fwd = lambda w: pl.BlockSpec((tb, w), lambda b, i: (b * nt + i, 0))
    bwd = lambda w: pl.BlockSpec((tb, w), lambda b, i: (b * nt + nt - 1 - i, 0))
    const = lambda a: pl.BlockSpec(a.shape, lambda b, i: (0,) * a.ndim)
    out = jax.ShapeDtypeStruct((batch * seq, GLA_WIDTH), jnp.float32)
    bg = b_lr.reshape(2, 1, hk).astype(jnp.float32)
    wg = wg.astype(jnp.bfloat16)
    return pl.pallas_call(
        _seq_mixers_kernel,
        grid=(batch, nt),
        in_specs=[fwd(2 * hk), fwd(GLA_WIDTH), fwd(128), bwd(2 * hk), bwd(GLA_WIDTH), bwd(128),
                  const(wg), const(bg), const(tri), const(place),
                  fwd(DN_QKV), fwd(128), bwd(DN_QKV), bwd(128)],
        out_specs=[fwd(GLA_WIDTH), bwd(GLA_WIDTH), fwd(DN_WIDTH), bwd(DN_WIDTH)],
        out_shape=[out, out, out, out],
        scratch_shapes=[pltpu.VMEM((hk, GLA_WIDTH), jnp.float32), pltpu.VMEM((hk, GLA_WIDTH), jnp.float32),
                        pltpu.VMEM((W, DN_WIDTH), jnp.float32), pltpu.VMEM((W, DN_WIDTH), jnp.float32)],
        compiler_params=pltpu.CompilerParams(dimension_semantics=("arbitrary", "arbitrary"),
                                             vmem_limit_bytes=VMEM_LIMIT_BYTES),
        name="seq_mixers",
    )(gla_qk, gla_v, small, gla_qk, gla_v, small, wg, bg, tri, place, dn_qkvn, dn_gates, dn_qkvn, dn_gates)


def _gated_group_norm(o, gate, gn, ind):
    ms = jnp.dot((o * o).astype(jnp.bfloat16), ind, preferred_element_type=jnp.float32) * (1.0 / HEAD_DIM)
    return (o * lax.rsqrt(ms + EPS) * gn * (gate * (1.0 / (1.0 + jnp.exp(-gate))))).astype(jnp.bfloat16)


def _out_mlp_kernel(glaf_ref, glab_ref, glag_ref, swa_ref, dnf_ref, dnb_ref, dnz_ref, x_ref, wo_ref,
                    gng_ref, gnd_ref, g1_ref, g2_ref, w1_ref, w2_ref, g3_ref, o_ref):
    ind = _group_indicator(GLA_WIDTH, HEAD_DIM)
    y_gla = _gated_group_norm(glaf_ref[...] + glab_ref[...], glag_ref[...], gng_ref[...], ind)
    y_dn = _gated_group_norm(dnf_ref[...] + dnb_ref[...], dnz_ref[...], gnd_ref[...], ind)
    a, b = GLA_WIDTH, GLA_WIDTH + SWA_WIDTH
    m = (jnp.dot(y_gla, wo_ref[0:a, :], preferred_element_type=jnp.float32) +
         jnp.dot(swa_ref[...], wo_ref[a:b, :], preferred_element_type=jnp.float32) +
         jnp.dot(y_dn, wo_ref[b:, :], preferred_element_type=jnp.float32))
    x1 = x_ref[...] + _rms(m, g1_ref[...])
    h = _rms(x1, g2_ref[...]).astype(jnp.bfloat16)
    acc = jnp.zeros_like(x1)
    for c in range(D_FF // FF_CHUNK):
        f = jnp.dot(h, w1_ref[:, c * FF_CHUNK:(c + 1) * FF_CHUNK], preferred_element_type=jnp.float32)
        f = jnp.square(jnp.maximum(f, 0.0)).astype(jnp.bfloat16)
        acc = acc + jnp.dot(f, w2_ref[c * FF_CHUNK:(c + 1) * FF_CHUNK, :], preferred_element_type=jnp.float32)
    o_ref[...] = x1 + _rms(acc, g3_ref[...])


def out_mlp(gla_f, gla_b, gla_g, swa_o, dn_f, dn_b, dn_z, x2d, layer, wo, gla_norm_g, dn_norm_g, g1, g2, w1, w2, g3):
    n, d = x2d.shape
    tm = min(ROW_TILE, n)
    row = lambda w: pl.BlockSpec((tm, w), lambda i: (i, 0))
    vec = _const_spec((1, d))
    hvec = _const_spec((1, GLA_WIDTH))
    tile_gain = lambda g, h: jnp.tile(g.astype(jnp.float32), h).reshape(1, -1)
    return pl.pallas_call(
        _out_mlp_kernel,
        grid=(n // tm,),
        in_specs=[row(GLA_WIDTH), row(GLA_WIDTH), row(GLA_WIDTH), row(SWA_WIDTH), row(DN_WIDTH), row(DN_WIDTH),
                  row(DN_WIDTH), row(d), _layer_spec(layer, (d, d)), hvec, hvec, vec, vec,
                  _layer_spec(layer, (d, D_FF)), _layer_spec(layer, (D_FF, d)), vec],
        out_specs=row(d),
        out_shape=jax.ShapeDtypeStruct((n, d), jnp.float32),
        compiler_params=pltpu.CompilerParams(dimension_semantics=("arbitrary",),
                                             vmem_limit_bytes=VMEM_LIMIT_BYTES),
        name="out_mlp",
    )(gla_f, gla_b, gla_g, swa_o, dn_f, dn_b, dn_z, x2d, wo, tile_gain(gla_norm_g, GLA_HEADS),
      tile_gain(dn_norm_g, DN_HEADS), g1.reshape(1, d), g2.reshape(1, d), w1, w2, g3.reshape(1, d))


def _trunk(x, p):
    B, T, D = x.shape
    bias = p["band_bias"]
    x2 = x.reshape(B * T, D)
    for l in range(DEPTH):
        dn_qkvn, small, swa_q, swa_kv, gla_qk, gla_v, gla_g, dn_z, dn_gates = in_proj(
            x2, p["mix_pre_g"][l], p["w_in_packed"], l, p["dn_conv_w"][l], p["dn_a_log"][l], p["dn_dt_bias"][l], T)
        o_swa = swa_attention(swa_q, swa_kv, p["swa_sink"][l], bias, B, T)
        gla_f, gla_b, dn_f, dn_b = seq_mixers(gla_qk, gla_v, small, p["gla_w_lr2"][l], p["gla_b_lr"][l],
                                              dn_qkvn, dn_gates, B, T)
        x2 = out_mlp(gla_f, gla_b, gla_g, o_swa, dn_f, dn_b, dn_z, x2, l, p["w_out_bf16"], p["gla_norm_g"][l],
                     p["dn_norm_g"][l], p["mix_post_g"][l], p["mlp_pre_g"][l],
                     p["mlp_w1_bf16"], p["mlp_w2_bf16"], p["mlp_post_g"][l])
    return x2.reshape(B, T, D)


def _prepare_params(rel_bias_table, mix_pre_g, w_in, gla_w_lr2, gla_b_lr, gla_norm_g, swa_sink, dn_conv_w, dn_a_log,
                    dn_dt_bias, dn_norm_g, w_out, mix_post_g, mlp_pre_g, mlp_w1, mlp_w2, mlp_post_g):
    return dict(mix_pre_g=mix_pre_g, gla_w_lr2=gla_w_lr2, gla_b_lr=gla_b_lr,
                gla_norm_g=gla_norm_g, swa_sink=swa_sink, dn_conv_w=dn_conv_w, dn_a_log=dn_a_log,
                dn_dt_bias=dn_dt_bias, dn_norm_g=dn_norm_g, mix_post_g=mix_post_g, mlp_pre_g=mlp_pre_g,
                mlp_post_g=mlp_post_g,
                band_bias=band_bias(rel_bias_table),
                w_in_packed=pack_w_in(w_in), w_out_bf16=w_out.astype(jnp.bfloat16),
                mlp_w1_bf16=mlp_w1.astype(jnp.bfloat16), mlp_w2_bf16=mlp_w2.astype(jnp.bfloat16))


def kernel(x_prompt, x_sample, rel_bias_table, mix_pre_g, w_in, gla_w_lr2, gla_b_lr, gla_norm_g, swa_sink, dn_conv_w, dn_a_log, dn_dt_bias, dn_norm_g, w_out, mix_post_g, mlp_pre_g, mlp_w1, mlp_w2, mlp_post_g):
    p = _prepare_params(rel_bias_table, mix_pre_g, w_in, gla_w_lr2, gla_b_lr, gla_norm_g, swa_sink, dn_conv_w,
                        dn_a_log, dn_dt_bias, dn_norm_g, w_out, mix_post_g, mlp_pre_g, mlp_w1, mlp_w2, mlp_post_g)
    return (_trunk(x_prompt, p), _trunk(x_sample, p))
```
